```python
import math
import jax, jax.numpy as jnp
from jax import lax
import numpy as np

D_MODEL = 1024
BATCH = 32
SEQ = 256
DEPTH = 1
DEC_BATCH = 4
DEC_SEQ = 2048
PAST_LEN = 512

GRID_W = 64
HEAD_DIM = 64
A_HEADS = 8
A_QK = A_HEADS * 2 * HEAD_DIM
A_V = A_HEADS * 2 * HEAD_DIM
B_HEADS = 16
B_KV_HEADS = 4
B_GROUP = B_HEADS // B_KV_HEADS
B_Q = B_HEADS * HEAD_DIM
B_KVW = B_KV_HEADS * HEAD_DIM
WINDOW = 128
BLOCK = 128
N_IN = 2 * A_QK + A_V + B_Q + 2 * B_KVW + 2 * D_MODEL
D_FF = 2816
N_MOD = 6
EPS = 1e-6
ROPE_BASE = 10000.0
NEG = -1e30

kernel_name = "hybrid_diffattn_window_sink_dit_step"


def _rmsnorm(x, g):
    xf = x.astype(jnp.float32)
    y = xf * lax.rsqrt(jnp.mean(xf * xf, axis=-1, keepdims=True) + EPS)
    return (y * g.astype(jnp.float32)).astype(x.dtype)


def _axial_rope(x):
    T = x.shape[1]
    rows = T // GRID_W
    row = jnp.repeat(jnp.arange(rows), GRID_W).astype(jnp.float32)
    col = jnp.tile(jnp.arange(GRID_W), rows).astype(jnp.float32)
    quarter = HEAD_DIM // 4
    freqs = ROPE_BASE ** (-jnp.arange(quarter, dtype=jnp.float32) / quarter)
    bshape = (1, T) + (1,) * (x.ndim - 3) + (quarter,)
    ang_r = (row[:, None] * freqs).reshape(bshape)
    ang_c = (col[:, None] * freqs).reshape(bshape)

    def rot(xh, ang):
        x1, x2 = xh[..., :quarter], xh[..., quarter:]
        cos, sin = jnp.cos(ang), jnp.sin(ang)
        return jnp.concatenate([x1 * cos - x2 * sin, x2 * cos + x1 * sin], axis=-1)

    xf = x.astype(jnp.float32)
    half = HEAD_DIM // 2
    out = jnp.concatenate([rot(xf[..., :half], ang_r), rot(xf[..., half:], ang_c)], axis=-1)
    return out.astype(x.dtype)


def _diff_attend(q, k, v, lam):
    B, T = q.shape[:2]
    nb = T // BLOCK
    qb = q.reshape(B, nb, BLOCK, A_HEADS, 2, HEAD_DIM).swapaxes(0, 1)
    scale = HEAD_DIM ** -0.5

    def one(qi):
        s = jnp.einsum("bqhcd,bkhcd->bhcqk", qi, k).astype(jnp.float32) * scale
        p = jax.nn.softmax(s, axis=-1)
        a = (p[:, :, 0] - lam * p[:, :, 1]).astype(v.dtype)
        return jnp.einsum("bhqk,bkhe->bqhe", a, v)

    out = lax.map(one, qb)
    return out.swapaxes(0, 1).reshape(B, T, A_HEADS, 2 * HEAD_DIM)


def _window_attend(q, k_ctx, v_ctx, sink, k_lat, v_lat):
    B, T = q.shape[:2]
    nb = T // BLOCK
    n_ctx = k_ctx.shape[1]
    scale = HEAD_DIM ** -0.5
    qb = q.reshape(B, nb, BLOCK, B_KV_HEADS, B_GROUP, HEAD_DIM).swapaxes(0, 1)
    sink_col = jnp.broadcast_to(sink.astype(jnp.float32).reshape(1, B_KV_HEADS, B_GROUP, 1, 1),
                                (B, B_KV_HEADS, B_GROUP, BLOCK, 1))
    local = k_lat is not None
    if local:
        pad = ((0, 0), (BLOCK, BLOCK), (0, 0), (0, 0))
        kpad = jnp.pad(k_lat, pad)
        vpad = jnp.pad(v_lat, pad)
        rel = jnp.arange(3 * BLOCK)[None, :] - BLOCK - jnp.arange(BLOCK)[:, None]
        band = jnp.abs(rel) <= WINDOW

    def one(args):
        i, qi = args
        s_ctx = jnp.einsum("bqgrd,bkgd->bgrqk", qi, k_ctx).astype(jnp.float32) * scale
        if local:
            kb = lax.dynamic_slice_in_dim(kpad, i * BLOCK, 3 * BLOCK, axis=1)
            vb = lax.dynamic_slice_in_dim(vpad, i * BLOCK, 3 * BLOCK, axis=1)
            s_loc = jnp.einsum("bqgrd,bkgd->bgrqk", qi, kb).astype(jnp.float32) * scale
            kpos = i * BLOCK - BLOCK + jnp.arange(3 * BLOCK)
            valid = band & ((kpos >= 0) & (kpos < T))[None, :]
            s_loc = jnp.where(valid, s_loc, NEG)
            p = jax.nn.softmax(jnp.concatenate([s_loc, s_ctx, sink_col], axis=-1), axis=-1)
            p = p.astype(v_ctx.dtype)
            n_loc = 3 * BLOCK
            out = (jnp.einsum("bgrqk,bkgd->bqgrd", p[..., :n_loc], vb)
                   + jnp.einsum("bgrqk,bkgd->bqgrd", p[..., n_loc:n_loc + n_ctx], v_ctx))
        else:
            p = jax.nn.softmax(jnp.concatenate([s_ctx, sink_col], axis=-1), axis=-1)
            p = p.astype(v_ctx.dtype)
            out = jnp.einsum("bgrqk,bkgd->bqgrd", p[..., :n_ctx], v_ctx)
        return out

    out = lax.map(one, (jnp.arange(nb), qb))
    return out.swapaxes(0, 1).reshape(B, T, B_HEADS, HEAD_DIM)


def _layer(x, mod, ctx_kv, lambda_init, norm1_g, w_in, qn_a, kn_a, lq1, lk1, lq2, lk2, subln_g,
           qn_b, kn_b, sink, w_oa, w_ob, w_out, norm2_g, w_gate, w_up, w_down):
    B, T = x.shape[:2]
    sh1, sc1, g1, sh2, sc2, g2 = jnp.split(mod, N_MOD, axis=-1)
    h = _rmsnorm(x, norm1_g) * (1 + sc1) + sh1
    proj = h @ w_in
    offs = list(np.cumsum([A_QK, A_QK, A_V, B_Q, B_KVW, B_KVW, D_MODEL]))
    qa, ka, va, qb, kb, vb, ga, gb = jnp.split(proj, offs, axis=-1)
    qa = _rmsnorm(qa.reshape(B, T, A_HEADS, 2, HEAD_DIM), qn_a)
    ka = _rmsnorm(ka.reshape(B, T, A_HEADS, 2, HEAD_DIM), kn_a)
    va = va.reshape(B, T, A_HEADS, 2 * HEAD_DIM)
    qb = _rmsnorm(qb.reshape(B, T, B_HEADS, HEAD_DIM), qn_b)
    kb = _rmsnorm(kb.reshape(B, T, B_KV_HEADS, HEAD_DIM), kn_b)
    vb = vb.reshape(B, T, B_KV_HEADS, HEAD_DIM)
    f32 = lambda a: a.astype(jnp.float32)
    lam = (jnp.exp(jnp.sum(f32(lq1) * f32(lk1))) - jnp.exp(jnp.sum(f32(lq2) * f32(lk2)))
           + lambda_init)
    if ctx_kv is None:
        oa = _diff_attend(qa, ka, va, lam)
        ob = _window_attend(qb, kb, vb, sink, None, None)
        new_kv = (ka, va, kb, vb)
    else:
        cka, cva, ckb, cvb = ctx_kv
        qa, ka, qb, kb = _axial_rope(qa), _axial_rope(ka), _axial_rope(qb), _axial_rope(kb)
        oa = _diff_attend(qa, jnp.concatenate([ka, cka], axis=1),
                          jnp.concatenate([va, cva], axis=1), lam)
        ob = _window_attend(qb, ckb, cvb, sink, kb, vb)
        new_kv = None
    oa = _rmsnorm(oa, subln_g) * (1.0 - lambda_init)
    oa = oa.reshape(B, T, A_V) @ w_oa
    ob = ob.reshape(B, T, B_Q) @ w_ob
    merged = jax.nn.sigmoid(ga) * oa + jax.nn.sigmoid(gb) * ob
    x = x + g1 * (merged @ w_out)
    h2 = _rmsnorm(x, norm2_g) * (1 + sc2) + sh2
    x = x + g2 * ((jax.nn.silu(h2 @ w_gate) * (h2 @ w_up)) @ w_down)
    return x, new_kv


def setup_inputs(seed: int = 0) -> dict:
    key = jax.random.key(seed)
    ks = jax.random.split(key, 32)
    n = lambda k, s: jax.random.normal(k, s, jnp.float32)
    gain = lambda k, s: 1.0 + 0.02 * n(k, s)
    L = DEPTH
    return {
        "x_prompt": n(ks[0], (BATCH, SEQ, D_MODEL)),
        "x_sample": n(ks[1], (DEC_BATCH, DEC_SEQ, D_MODEL)),
        "cache_diff_k": n(ks[2], (DEC_BATCH, L, PAST_LEN, A_HEADS, 2, HEAD_DIM)),
        "cache_diff_v": n(ks[3], (DEC_BATCH, L, PAST_LEN, A_HEADS, 2 * HEAD_DIM)),
        "cache_win_k": n(ks[4], (DEC_BATCH, L, PAST_LEN, B_KV_HEADS, HEAD_DIM)),
        "cache_win_v": n(ks[5], (DEC_BATCH, L, PAST_LEN, B_KV_HEADS, HEAD_DIM)),
        "c": n(ks[6], (DEC_BATCH, D_MODEL)),
        "c_ctx": n(ks[7], (D_MODEL,)),
        "w_ada": 0.3 * D_MODEL ** -0.5 * n(ks[8], (L, D_MODEL, N_MOD * D_MODEL)),
        "b_ada": 0.02 * n(ks[9], (L, N_MOD * D_MODEL)),
        "norm1_g": gain(ks[10], (L, D_MODEL)),
        "w_in": D_MODEL ** -0.5 * n(ks[11], (L, D_MODEL, N_IN)),
        "qn_a": gain(ks[12], (L, HEAD_DIM)),
        "kn_a": gain(ks[13], (L, HEAD_DIM)),
        "lambda_q1": 0.1 * n(ks[14], (L, HEAD_DIM)),
        "lambda_k1": 0.1 * n(ks[15], (L, HEAD_DIM)),
        "lambda_q2": 0.1 * n(ks[16], (L, HEAD_DIM)),
        "lambda_k2": 0.1 * n(ks[17], (L, HEAD_DIM)),
        "subln_g": gain(ks[18], (L, 2 * HEAD_DIM)),
        "qn_b": gain(ks[19], (L, HEAD_DIM)),
        "kn_b": gain(ks[20], (L, HEAD_DIM)),
        "sink": 0.5 * n(ks[21], (L, B_HEADS)),
        "w_oa": A_V ** -0.5 * n(ks[22], (L, A_V, D_MODEL)),
        "w_ob": B_Q ** -0.5 * n(ks[23], (L, B_Q, D_MODEL)),
        "w_out": D_MODEL ** -0.5 * n(ks[24], (L, D_MODEL, D_MODEL)),
        "norm2_g": gain(ks[25], (L, D_MODEL)),
        "w_gate": D_MODEL ** -0.5 * n(ks[26], (L, D_MODEL, D_FF)),
        "w_up": D_MODEL ** -0.5 * n(ks[27], (L, D_MODEL, D_FF)),
        "w_down": D_FF ** -0.5 * n(ks[28], (L, D_FF, D_MODEL)),
    }


def reference(x_prompt, x_sample, cache_diff_k, cache_diff_v, cache_win_k, cache_win_v, c, c_ctx,
              w_ada, b_ada, norm1_g, w_in, qn_a, kn_a, lambda_q1, lambda_k1, lambda_q2, lambda_k2,
              subln_g, qn_b, kn_b, sink, w_oa, w_ob, w_out, norm2_g, w_gate, w_up, w_down):
    xp, xs = x_prompt, x_sample
    dk, dv, wk, wv = [], [], [], []
    for l in range(DEPTH):
        lambda_init = 0.8 - 0.6 * math.exp(-0.3 * l)
        mod_ctx = (jax.nn.silu(c_ctx) @ w_ada[l] + b_ada[l])[None, None, :]
        mod_lat = (jax.nn.silu(c) @ w_ada[l] + b_ada[l])[:, None, :]
        wts = (lambda_init, norm1_g[l], w_in[l], qn_a[l], kn_a[l], lambda_q1[l], lambda_k1[l],
               lambda_q2[l], lambda_k2[l], subln_g[l], qn_b[l], kn_b[l], sink[l], w_oa[l], w_ob[l],
               w_out[l], norm2_g[l], w_gate[l], w_up[l], w_down[l])
        xp, kv = _layer(xp, mod_ctx, None, *wts)
        dk.append(kv[0]); dv.append(kv[1]); wk.append(kv[2]); wv.append(kv[3])
        ctx = (cache_diff_k[:, l], cache_diff_v[:, l], cache_win_k[:, l], cache_win_v[:, l])
        xs, _ = _layer(xs, mod_lat, ctx, *wts)
    new_diff_k = jnp.stack(dk, axis=1)
    new_diff_v = jnp.stack(dv, axis=1)
    new_win_k = jnp.stack(wk, axis=1)
    new_win_v = jnp.stack(wv, axis=1)
    return (xp, xs, new_diff_k, new_diff_v, new_win_k, new_win_v)
```

```python
import functools
import math

import jax
import jax.numpy as jnp
import numpy as np
from jax import lax
from jax.experimental import pallas as pl
from jax.experimental.pallas import tpu as pltpu

D_MODEL = 1024
HEAD_DIM = 64
A_HEADS = 8
B_HEADS = 16
B_KV_HEADS = 4
B_GROUP = B_HEADS // B_KV_HEADS
A_W = A_HEADS * 2 * HEAD_DIM
B_Q = B_HEADS * HEAD_DIM
B_KVW = B_KV_HEADS * HEAD_DIM
N_QKV = 3 * A_W + B_Q + 2 * B_KVW
WINDOW = 128
BLOCK = 128
GRID_W = 64
D_FF = 2816
N_MOD = 6
EPS = 1e-6
ROPE_BASE = 10000.0
NEG = -1e30
LAMBDA_INIT = 0.8 - 0.6 * math.exp(-0.3 * 0)
QK_SCALE = HEAD_DIM ** -0.5

LANES = 128
CHUNK = 256
VMEM_LIMIT = 56 * 1024 * 1024

F32 = jnp.float32
BF16 = jnp.bfloat16
_NT = (((1,), (1,)), ((), ()))


def _params(n_axes):
    return pltpu.CompilerParams(dimension_semantics=("arbitrary",) * n_axes,
                                vmem_limit_bytes=VMEM_LIMIT)


def _const_spec(shape):
    nd = len(shape)
    return pl.BlockSpec(shape, lambda *_: (0,) * nd, pipeline_mode=pl.Buffered(1))


def _dot(a, b):
    return jnp.dot(a, b, preferred_element_type=F32)


def _modulated_norm(x, gain, scale, shift):
    y = x * lax.rsqrt(jnp.mean(x * x, axis=-1, keepdims=True) + EPS) * gain
    return y * (1.0 + scale) + shift


def _mod_kernel(c_ref, w_ref, b_ref, o_ref):
    cc = c_ref[...]
    s = cc * jax.nn.sigmoid(cc)
    o_ref[...] = _dot(s.astype(BF16), w_ref[...].astype(BF16)) + b_ref[...]


def _modulation(cc, w_ada, b_ada):
    rows, n = cc.shape[0], w_ada.shape[1]
    tn = n // 4
    return pl.pallas_call(
        _mod_kernel,
        grid=(n // tn,),
        in_specs=[pl.BlockSpec((rows, D_MODEL), lambda j: (0, 0)),
                  pl.BlockSpec((D_MODEL, tn), lambda j: (0, j)),
                  pl.BlockSpec((1, tn), lambda j: (0, j))],
        out_specs=pl.BlockSpec((rows, tn), lambda j: (0, j)),
        out_shape=jax.ShapeDtypeStruct((rows, n), F32),
        compiler_params=_params(1),
        name="adaln_mod",
    )(cc, w_ada, b_ada)


def _qkv_kernel(*refs, rope, emit_f32):
    x_ref, mod_ref, g1_ref, w_ref, gsum_ref, gains_ref = refs[:6]
    refs = refs[6:]
    if rope:
        cos_ref, sa_ref, sb_ref = refs[:3]
        refs = refs[3:]
    qa_ref, ka_ref, va_ref, qb_ref, kb_ref, vb_ref = refs[:6]
    if emit_f32:
        kaf_ref, vaf_ref, kbf_ref, vbf_ref = refs[6:]

    mod = mod_ref[...]
    h = _modulated_norm(x_ref[...], g1_ref[...], mod[:, D_MODEL:2 * D_MODEL],
                        mod[:, 0:D_MODEL]).astype(BF16)
    gsum = gsum_ref[...]

    def head_norm(p, row):
        ss = _dot((p * p).astype(BF16), gsum)
        return p * lax.rsqrt(ss * (1.0 / HEAD_DIM) + EPS) * gains_ref[row:row + 1, :]

    def rotary(p):
        if not rope:
            return p
        cos, sa, sb = cos_ref[...], sa_ref[...], sb_ref[...]
        halves = []
        for i in range(CHUNK // LANES):
            xh = p[:, i * LANES:(i + 1) * LANES]
            nxt = pltpu.roll(xh, LANES - HEAD_DIM // 4, 1)
            prv = pltpu.roll(xh, HEAD_DIM // 4, 1)
            halves.append(xh * cos + nxt * sa + prv * sb)
        return jnp.concatenate(halves, axis=1)

    def proj(col):
        return _dot(h, w_ref[:, col:col + CHUNK])

    for j in range(A_W // CHUNK):
        sl = slice(j * CHUNK, (j + 1) * CHUNK)
        qa_ref[:, sl] = (rotary(head_norm(proj(j * CHUNK), 0)) * QK_SCALE).astype(BF16)
        ka = head_norm(proj(A_W + j * CHUNK), 1)
        ka_ref[:, sl] = rotary(ka).astype(BF16)
        va = proj(2 * A_W + j * CHUNK)
        va_ref[:, sl] = va.astype(BF16)
        qb_ref[:, sl] = (rotary(head_norm(proj(3 * A_W + j * CHUNK), 2)) * QK_SCALE).astype(BF16)
        if emit_f32:
            kaf_ref[:, sl] = ka
            vaf_ref[:, sl] = va
    kb = head_norm(proj(3 * A_W + B_Q), 3)
    kb_ref[...] = rotary(kb).astype(BF16)
    vb = proj(3 * A_W + B_Q + B_KVW)
    vb_ref[...] = vb.astype(BF16)
    if emit_f32:
        kbf_ref[...] = kb
        vbf_ref[...] = vb


def _qkv_proj(x2d, mod3, mod_row, g1, w_qkv, gsum, gains, rope_tabs, seq_len, tm, emit_f32):
    rows = x2d.shape[0]
    rope = rope_tabs is not None
    tiles_per_seq = seq_len // tm
    in_specs = [pl.BlockSpec((tm, D_MODEL), lambda i: (i, 0)),
                pl.BlockSpec((None, 1, N_MOD * D_MODEL), lambda i: (mod_row(i), 0, 0)),
                _const_spec((1, D_MODEL)),
                _const_spec((D_MODEL, N_QKV)),
                _const_spec((CHUNK, CHUNK)),
                _const_spec((4, CHUNK))]
    args = [x2d, mod3, g1, w_qkv, gsum, gains]
    if rope:
        in_specs += [pl.BlockSpec((tm, LANES), lambda i: (i % tiles_per_seq, 0))] * 3
        args += list(rope_tabs)
    wide = pl.BlockSpec((tm, A_W), lambda i: (i, 0))
    narrow = pl.BlockSpec((tm, B_KVW), lambda i: (i, 0))
    out_specs = [wide, wide, wide, wide, narrow, narrow]
    out_shape = [jax.ShapeDtypeStruct((rows, A_W), BF16)] * 4 + \
                [jax.ShapeDtypeStruct((rows, B_KVW), BF16)] * 2
    if emit_f32:
        out_specs += [wide, wide, narrow, narrow]
        out_shape += [jax.ShapeDtypeStruct((rows, A_W), F32)] * 2 + \
                     [jax.ShapeDtypeStruct((rows, B_KVW), F32)] * 2
    return pl.pallas_call(
        functools.partial(_qkv_kernel, rope=rope, emit_f32=emit_f32),
        grid=(rows // tm,),
        in_specs=in_specs,
        out_specs=out_specs,
        out_shape=out_shape,
        compiler_params=_params(1),
        name="qkv_rope" if rope else "qkv_ctx",
    )(*args)


def _softmax_parts(scores, extra=None):
    m = functools.reduce(jnp.maximum, [jnp.max(s, axis=-1, keepdims=True) for s in scores])
    if extra is not None:
        m = jnp.maximum(m, extra)
    es = [jnp.exp(s - m) for s in scores]
    total = functools.reduce(jnp.add, [jnp.sum(e, axis=-1, keepdims=True) for e in es])
    if extra is not None:
        total = total + jnp.exp(extra - m)
    return es, total


def _lambda(lam_ref):
    lv = lam_ref[...]
    t1 = jnp.sum(lv[0:1] * lv[1:2], axis=-1, keepdims=True)
    t2 = jnp.sum(lv[2:3] * lv[3:4], axis=-1, keepdims=True)
    return jnp.exp(t1) - jnp.exp(t2) + LAMBDA_INIT


def _diff_head(q, ks, vs, lam, subg):
    lane = lax.broadcasted_iota(jnp.int32, q.shape, 1)
    zero = jnp.zeros_like(q)
    q0 = jnp.where(lane < HEAD_DIM, q, zero)
    q1 = jnp.where(lane >= HEAD_DIM, q, zero)
    e0, l0 = _softmax_parts([lax.dot_general(q0, k, _NT, preferred_element_type=F32) for k in ks])
    e1, l1 = _softmax_parts([lax.dot_general(q1, k, _NT, preferred_element_type=F32) for k in ks])
    c0 = 1.0 / l0
    c1 = lam / l1
    o = functools.reduce(jnp.add, [_dot((a * c0 - b * c1).astype(BF16), v)
                                   for a, b, v in zip(e0, e1, vs)])
    o = o * lax.rsqrt(jnp.mean(o * o, axis=-1, keepdims=True) + EPS)
    return o * subg * (1.0 - LAMBDA_INIT)


def _diff_ctx_kernel(q_ref, k_ref, v_ref, lam_ref, subg_ref, o_ref):
    lam = _lambda(lam_ref)
    subg = subg_ref[...]
    for h in range(A_HEADS):
        sl = slice(h * LANES, (h + 1) * LANES)
        o = _diff_head(q_ref[:, sl], [k_ref[:, sl]], [v_ref[:, sl]], lam, subg)
        o_ref[:, sl] = o.astype(BF16)


def _diff_ctx(q, k, v, lam_vecs, subg):
    b, t, _ = q.shape
    seq = pl.BlockSpec((None, t, A_W), lambda i: (i, 0, 0))
    return pl.pallas_call(
        _diff_ctx_kernel,
        grid=(b,),
        in_specs=[seq, seq, seq, _const_spec((4, HEAD_DIM)), _const_spec((1, LANES))],
        out_specs=seq,
        out_shape=jax.ShapeDtypeStruct((b, t, A_W), BF16),
        compiler_params=_params(1),
        name="diff_ctx",
    )(q, k, v, lam_vecs, subg)


def _diff_lat_kernel(q_ref, k_ref, v_ref, ck_ref, cv_ref, lam_ref, subg_ref, o_ref):
    ks = [k_ref[...], ck_ref[...].astype(BF16)]
    vs = [v_ref[...], cv_ref[...].astype(BF16)]
    o = _diff_head(q_ref[...], ks, vs, _lambda(lam_ref), subg_ref[...])
    o_ref[...] = o.astype(BF16)


def _diff_lat(q, k, v, ck, cv, lam_vecs, subg, tq):
    b, t, _ = q.shape
    past = ck.shape[1]
    qspec = pl.BlockSpec((None, tq, LANES), lambda i, h, j: (i, j, h))
    kspec = pl.BlockSpec((None, t, LANES), lambda i, h, j: (i, 0, h))
    cspec = pl.BlockSpec((None, past, LANES), lambda i, h, j: (i, 0, h))
    return pl.pallas_call(
        _diff_lat_kernel,
        grid=(b, A_HEADS, t // tq),
        in_specs=[qspec, kspec, kspec, cspec, cspec,
                  _const_spec((4, HEAD_DIM)), _const_spec((1, LANES))],
        out_specs=qspec,
        out_shape=jax.ShapeDtypeStruct((b, t, A_W), BF16),
        compiler_params=_params(3),
        name="diff_lat",
    )(q, k, v, ck, cv, lam_vecs, subg)


def _expand_group(x, e):
    return _dot(x, e).astype(BF16)


def _gqa_group(qg, k_parts, v_parts, masks, sink_ref, g):
    lane = lax.broadcasted_iota(jnp.int32, qg.shape, 1)
    zero = jnp.zeros_like(qg)
    acc = jnp.zeros(qg.shape, F32)
    for r in range(B_GROUP):
        sel = (lane >= r * HEAD_DIM) & (lane < (r + 1) * HEAD_DIM)
        qr = jnp.where(sel, qg, zero)
        scores = []
        for k, mask in zip(k_parts, masks):
            s = lax.dot_general(qr, k, _NT, preferred_element_type=F32)
            scores.append(s if mask is None else jnp.where(mask, s, NEG))
        head = g * B_GROUP + r
        es, total = _softmax_parts(scores, extra=sink_ref[head:head + 1, 0:1])
        inv = 1.0 / total
        o = functools.reduce(jnp.add, [_dot((e * inv).astype(BF16), v)
                                       for e, v in zip(es, v_parts)])
        acc = jnp.where(sel, o, acc)
    return acc


def _win_ctx_kernel(q_ref, k_ref, v_ref, e_ref, sink_ref, o_ref):
    k, v = k_ref[...], v_ref[...]
    for g in range(B_KV_HEADS):
        e = e_ref[g]
        sl = slice(g * CHUNK, (g + 1) * CHUNK)
        o = _gqa_group(q_ref[:, sl], [_expand_group(k, e)], [_expand_group(v, e)],
                       [None], sink_ref, g)
        o_ref[:, sl] = o.astype(BF16)


def _win_ctx(q, k, v, expand, sink_rows):
    b, t, _ = q.shape
    qspec = pl.BlockSpec((None, t, B_Q), lambda i: (i, 0, 0))
    kspec = pl.BlockSpec((None, t, B_KVW), lambda i: (i, 0, 0))
    return pl.pallas_call(
        _win_ctx_kernel,
        grid=(b,),
        in_specs=[qspec, kspec, kspec, _const_spec((B_KV_HEADS, CHUNK, CHUNK)),
                  _const_spec((B_HEADS, LANES))],
        out_specs=qspec,
        out_shape=jax.ShapeDtypeStruct((b, t, B_Q), BF16),
        compiler_params=_params(1),
        name="win_ctx",
    )(q, k, v, expand, sink_rows)


def _win_lat_kernel(q_ref, k_ref, v_ref, ck_ref, cv_ref, e_ref, sink_ref, o_ref,
                    kl_s, vl_s, kc_s, vc_s, *, seq_len):
    g = pl.program_id(1)
    j = pl.program_id(2)

    @pl.when(j == 0)
    def _():
        e = e_ref[...]
        kl_s[...] = _expand_group(k_ref[...], e)
        vl_s[...] = _expand_group(v_ref[...], e)
        kc_s[...] = _expand_group(ck_ref[...].astype(BF16), e)
        vc_s[...] = _expand_group(cv_ref[...].astype(BF16), e)

    span = 3 * BLOCK
    start = pl.multiple_of(jnp.clip((j - 1) * BLOCK, 0, seq_len - span), BLOCK)
    qpos = j * BLOCK + lax.broadcasted_iota(jnp.int32, (BLOCK, span), 0)
    kpos = start + lax.broadcasted_iota(jnp.int32, (BLOCK, span), 1)
    band = jnp.abs(kpos - qpos) <= WINDOW
    lane = lax.broadcasted_iota(jnp.int32, (BLOCK, CHUNK), 1)
    qg = q_ref[...]
    zero = jnp.zeros_like(qg)
    k_parts = [kl_s[pl.ds(start, span), :], kc_s[...]]
    v_parts = [vl_s[pl.ds(start, span), :], vc_s[...]]
    acc = jnp.zeros((BLOCK, CHUNK), F32)
    for r in range(B_GROUP):
        sel = (lane >= r * HEAD_DIM) & (lane < (r + 1) * HEAD_DIM)
        qr = jnp.where(sel, qg, zero)
        s_loc = lax.dot_general(qr, k_parts[0], _NT, preferred_element_type=F32)
        s_ctx = lax.dot_general(qr, k_parts[1], _NT, preferred_element_type=F32)
        sink = sink_ref[pl.ds(g * B_GROUP + r, 1), 0:1]
        es, total = _softmax_parts([jnp.where(band, s_loc, NEG), s_ctx], extra=sink)
        inv = 1.0 / total
        o = _dot((es[0] * inv).astype(BF16), v_parts[0]) + _dot((es[1] * inv).astype(BF16), v_parts[1])
        acc = jnp.where(sel, o, acc)
    o_ref[...] = acc.astype(BF16)


def _win_lat(q, k, v, ck, cv, expand, sink_rows):
    b, t, _ = q.shape
    past = ck.shape[1]
    qspec = pl.BlockSpec((None, BLOCK, CHUNK), lambda i, g, j: (i, j, g))
    kspec = pl.BlockSpec((None, t, B_KVW), lambda i, g, j: (i, 0, 0))
    cspec = pl.BlockSpec((None, past, B_KVW), lambda i, g, j: (i, 0, 0))
    return pl.pallas_call(
        functools.partial(_win_lat_kernel, seq_len=t),
        grid=(b, B_KV_HEADS, t // BLOCK),
        in_specs=[qspec, kspec, kspec, cspec, cspec,
                  pl.BlockSpec((None, CHUNK, CHUNK), lambda i, g, j: (g, 0, 0)),
                  _const_spec((B_HEADS, LANES))],
        out_specs=qspec,
        out_shape=jax.ShapeDtypeStruct((b, t, B_Q), BF16),
        scratch_shapes=[pltpu.VMEM((t, CHUNK), BF16), pltpu.VMEM((t, CHUNK), BF16),
                        pltpu.VMEM((past, CHUNK), BF16), pltpu.VMEM((past, CHUNK), BF16)],
        compiler_params=_params(3),
        name="win_lat",
    )(q, k, v, ck, cv, expand, sink_rows)


def _merge_kernel(x_ref, oa_ref, ob_ref, mod_ref, g1_ref, wg_ref, woa_ref, wob_ref, wout_ref, o_ref):
    x = x_ref[...]
    mod = mod_ref[...]
    h = _modulated_norm(x, g1_ref[...], mod[:, D_MODEL:2 * D_MODEL], mod[:, 0:D_MODEL]).astype(BF16)
    ga = _dot(h, wg_ref[:, 0:D_MODEL])
    gb = _dot(h, wg_ref[:, D_MODEL:2 * D_MODEL])
    merged = (jax.nn.sigmoid(ga) * _dot(oa_ref[...], woa_ref[...])
              + jax.nn.sigmoid(gb) * _dot(ob_ref[...], wob_ref[...]))
    o_ref[...] = x + mod[:, 2 * D_MODEL:3 * D_MODEL] * _dot(merged.astype(BF16), wout_ref[...])


def _merge(x2d, oa, ob, mod3, mod_row, g1, w_g, w_oa, w_ob, w_out, tm):
    rows = x2d.shape[0]
    tile = pl.BlockSpec((tm, D_MODEL), lambda i: (i, 0))
    sq = _const_spec((D_MODEL, D_MODEL))
    return pl.pallas_call(
        _merge_kernel,
        grid=(rows // tm,),
        in_specs=[tile, tile, tile,
                  pl.BlockSpec((None, 1, N_MOD * D_MODEL), lambda i: (mod_row(i), 0, 0)),
                  _const_spec((1, D_MODEL)), _const_spec((D_MODEL, 2 * D_MODEL)), sq, sq, sq],
        out_specs=tile,
        out_shape=jax.ShapeDtypeStruct((rows, D_MODEL), F32),
        compiler_params=_params(1),
        name="merge_out",
    )(x2d, oa, ob, mod3, g1, w_g, w_oa, w_ob, w_out)


def _ffn_kernel(x_ref, mod_ref, g2_ref, wgate_ref, wup_ref, wdown_ref, o_ref):
    x = x_ref[...]
    mod = mod_ref[...]
    h = _modulated_norm(x, g2_ref[...], mod[:, 4 * D_MODEL:5 * D_MODEL],
                        mod[:, 3 * D_MODEL:4 * D_MODEL]).astype(BF16)
    gate = _dot(h, wgate_ref[...])
    act = (gate * jax.nn.sigmoid(gate) * _dot(h, wup_ref[...])).astype(BF16)
    o_ref[...] = x + mod[:, 5 * D_MODEL:6 * D_MODEL] * _dot(act, wdown_ref[...])


def _ffn(x2d, mod3, mod_row, g2, w_gate, w_up, w_down, tm):
    rows = x2d.shape[0]
    tile = pl.BlockSpec((tm, D_MODEL), lambda i: (i, 0))
    return pl.pallas_call(
        _ffn_kernel,
        grid=(rows // tm,),
        in_specs=[tile,
                  pl.BlockSpec((None, 1, N_MOD * D_MODEL), lambda i: (mod_row(i), 0, 0)),
                  _const_spec((1, D_MODEL)), _const_spec((D_MODEL, D_FF)),
                  _const_spec((D_MODEL, D_FF)), _const_spec((D_FF, D_MODEL))],
        out_specs=tile,
        out_shape=jax.ShapeDtypeStruct((rows, D_MODEL), F32),
        compiler_params=_params(1),
        name="swiglu_ffn",
    )(x2d, mod3, g2, w_gate, w_up, w_down)


def _rope_tables(seq_len):
    quarter = HEAD_DIM // 4
    t = jnp.arange(seq_len)
    row = (t // GRID_W).astype(F32)
    col = (t % GRID_W).astype(F32)
    freqs = ROPE_BASE ** (-jnp.arange(quarter, dtype=F32) / quarter)
    ang_r = row[:, None] * freqs
    ang_c = col[:, None] * freqs
    zeros = jnp.zeros_like(ang_r)
    cos = jnp.concatenate([jnp.cos(ang_r)] * 2 + [jnp.cos(ang_c)] * 2, axis=-1)
    sin_next = jnp.concatenate([-jnp.sin(ang_r), zeros, -jnp.sin(ang_c), zeros], axis=-1)
    sin_prev = jnp.concatenate([zeros, jnp.sin(ang_r), zeros, jnp.sin(ang_c)], axis=-1)
    rep = LANES // HEAD_DIM
    return tuple(jnp.tile(a, (1, rep)) for a in (cos, sin_next, sin_prev))


def _group_sum_matrix():
    idx = np.arange(CHUNK) // HEAD_DIM
    return jnp.asarray(idx[:, None] == idx[None, :], dtype=BF16)


def _expand_matrices():
    src = np.arange(CHUNK)[:, None]
    dst = np.arange(CHUNK)[None, :]
    mats = [(src // HEAD_DIM == g) & (src % HEAD_DIM == dst % HEAD_DIM) for g in range(B_KV_HEADS)]
    return jnp.asarray(np.stack(mats), dtype=BF16)


def kernel(x_prompt, x_sample, cache_diff_k, cache_diff_v, cache_win_k, cache_win_v, c, c_ctx,
           w_ada, b_ada, norm1_g, w_in, qn_a, kn_a, lambda_q1, lambda_k1, lambda_q2, lambda_k2,
           subln_g, qn_b, kn_b, sink, w_oa, w_ob, w_out, norm2_g, w_gate, w_up, w_down):
    batch, seq, _ = x_prompt.shape
    dec_batch, dec_seq, _ = x_sample.shape
    past = cache_diff_k.shape[2]
    l = 0

    cc = jnp.zeros((8, D_MODEL), F32).at[:dec_batch].set(c).at[dec_batch].set(c_ctx)
    mod3 = _modulation(cc, w_ada[l], b_ada[l][None, :]).reshape(8, 1, N_MOD * D_MODEL)

    w_in_b = w_in[l].astype(BF16)
    w_qkv, w_g = w_in_b[:, :N_QKV], w_in_b[:, N_QKV:]
    w_oa_b, w_ob_b, w_out_b = w_oa[l].astype(BF16), w_ob[l].astype(BF16), w_out[l].astype(BF16)
    w_gate_b, w_up_b, w_down_b = w_gate[l].astype(BF16), w_up[l].astype(BF16), w_down[l].astype(BF16)
    g1 = norm1_g[l][None, :]
    g2 = norm2_g[l][None, :]
    rep = CHUNK // HEAD_DIM
    gains = jnp.stack([jnp.tile(v[l], rep) for v in (qn_a, kn_a, qn_b, kn_b)])
    lam_vecs = jnp.stack([lambda_q1[l], lambda_k1[l], lambda_q2[l], lambda_k2[l]])
    subg = subln_g[l][None, :]
    sink_rows = jnp.broadcast_to(sink[l][:, None], (B_HEADS, LANES))
    gsum = _group_sum_matrix()
    expand = _expand_matrices()
    tm = 512

    xp = x_prompt.reshape(batch * seq, D_MODEL)
    ctx_row = lambda i: dec_batch
    qa, ka, va, qb, kb, vb, ka_f, va_f, kb_f, vb_f = _qkv_proj(
        xp, mod3, ctx_row, g1, w_qkv, gsum, gains, None, seq, tm, True)
    r3 = lambda a, b: a.reshape(b, -1, a.shape[-1])
    oa = _diff_ctx(r3(qa, batch), r3(ka, batch), r3(va, batch), lam_vecs, subg)
    ob = _win_ctx(r3(qb, batch), r3(kb, batch), r3(vb, batch), expand, sink_rows)
    xp1 = _merge(xp, oa.reshape(-1, A_W), ob.reshape(-1, B_Q), mod3, ctx_row, g1,
                 w_g, w_oa_b, w_ob_b, w_out_b, tm)
    y_prompt = _ffn(xp1, mod3, ctx_row, g2, w_gate_b, w_up_b, w_down_b, tm)

    xs = x_sample.reshape(dec_batch * dec_seq, D_MODEL)
    lat_row = lambda i: i // (dec_seq // tm)
    qa, ka, va, qb, kb, vb = _qkv_proj(
        xs, mod3, lat_row, g1, w_qkv, gsum, gains, _rope_tables(dec_seq), dec_seq, tm, False)
    ck_a = cache_diff_k[:, l].reshape(dec_batch, past, A_W)
    cv_a = cache_diff_v[:, l].reshape(dec_batch, past, A_W)
    ck_b = cache_win_k[:, l].reshape(dec_batch, past, B_KVW)
    cv_b = cache_win_v[:, l].reshape(dec_batch, past, B_KVW)
    oa = _diff_lat(r3(qa, dec_batch), r3(ka, dec_batch), r3(va, dec_batch), ck_a, cv_a,
                   lam_vecs, subg, 256)
    ob = _win_lat(r3(qb, dec_batch), r3(kb, dec_batch), r3(vb, dec_batch), ck_b, cv_b,
                  expand, sink_rows)
    xs1 = _merge(xs, oa.reshape(-1, A_W), ob.reshape(-1, B_Q), mod3, lat_row, g1,
                 w_g, w_oa_b, w_ob_b, w_out_b, tm)
    y_sample = _ffn(xs1, mod3, lat_row, g2, w_gate_b, w_up_b, w_down_b, tm)

    return (y_prompt.reshape(batch, seq, D_MODEL),
            y_sample.reshape(dec_batch, dec_seq, D_MODEL),
            ka_f.reshape(batch, 1, seq, A_HEADS, 2, HEAD_DIM),
            va_f.reshape(batch, 1, seq, A_HEADS, 2 * HEAD_DIM),
            kb_f.reshape(batch, 1, seq, B_KV_HEADS, HEAD_DIM),
            vb_f.reshape(batch, 1, seq, B_KV_HEADS, HEAD_DIM))
```

```python
import functools
import math

import jax
import jax.numpy as jnp
import numpy as np
from jax import lax
from jax.experimental import pallas as pl
from jax.experimental.pallas import tpu as pltpu

D_MODEL = 1024
HEAD_DIM = 64
A_HEADS = 8
B_HEADS = 16
B_KV_HEADS = 4
B_GROUP = B_HEADS // B_KV_HEADS
A_W = A_HEADS * 2 * HEAD_DIM
B_Q = B_HEADS * HEAD_DIM
B_KVW = B_KV_HEADS * HEAD_DIM
N_QKV = 3 * A_W + B_Q + 2 * B_KVW
WINDOW = 128
BLOCK = 128
GRID_W = 64
D_FF = 2816
N_MOD = 6
EPS = 1e-6
ROPE_BASE = 10000.0
NEG = -1e30
LAMBDA_INIT = 0.8 - 0.6 * math.exp(-0.3 * 0)
LOG2E = math.log2(math.e)
Q_SCALE = HEAD_DIM ** -0.5 * LOG2E

LANES = 128
CHUNK = 256
VMEM_LIMIT = 56 * 1024 * 1024

F32 = jnp.float32
BF16 = jnp.bfloat16
_NT = (((1,), (1,)), ((), ()))


def _params(n_axes):
    return pltpu.CompilerParams(dimension_semantics=("arbitrary",) * n_axes,
                                vmem_limit_bytes=VMEM_LIMIT)


def _const_spec(shape):
    nd = len(shape)
    return pl.BlockSpec(shape, lambda *_: (0,) * nd, pipeline_mode=pl.Buffered(1))


def _dot(a, b):
    return jnp.dot(a, b, preferred_element_type=F32)


def _dot_nt(a, b):
    return lax.dot_general(a, b, _NT, preferred_element_type=F32)


def _modulated_norm(x, gain, scale, shift):
    y = x * lax.rsqrt(jnp.mean(x * x, axis=-1, keepdims=True) + EPS) * gain
    return y * (1.0 + scale) + shift


def _staggered(n, start, finish):
    state = start(0)
    for i in range(n):
        nxt = start(i + 1) if i + 1 < n else None
        finish(i, state)
        state = nxt


def _mod_kernel(c_ref, w_ref, b_ref, o_ref):
    cc = c_ref[...]
    s = cc * jax.nn.sigmoid(cc)
    o_ref[...] = _dot(s.astype(BF16), w_ref[...].astype(BF16)) + b_ref[...]


def _modulation(cc, w_ada, b_ada):
    rows, n = cc.shape[0], w_ada.shape[1]
    tn = n // 4
    return pl.pallas_call(
        _mod_kernel,
        grid=(n // tn,),
        in_specs=[pl.BlockSpec((rows, D_MODEL), lambda j: (0, 0)),
                  pl.BlockSpec((D_MODEL, tn), lambda j: (0, j)),
                  pl.BlockSpec((1, tn), lambda j: (0, j))],
        out_specs=pl.BlockSpec((rows, tn), lambda j: (0, j)),
        out_shape=jax.ShapeDtypeStruct((rows, n), F32),
        compiler_params=_params(1),
        name="adaln_mod",
    )(cc, w_ada, b_ada)


def _qkv_kernel(*refs, rope):
    x_ref, mod_ref, g1_ref, w_ref, gsum_ref, gains_ref = refs[:6]
    refs = refs[6:]
    if rope:
        cos_ref, sa_ref, sb_ref = refs[:3]
        refs = refs[3:]
    qa_ref, qb_ref, kat_ref, va_ref, kbt_ref, vbt_ref = refs

    mod = mod_ref[...]
    h = _modulated_norm(x_ref[...], g1_ref[...], mod[:, D_MODEL:2 * D_MODEL],
                        mod[:, 0:D_MODEL]).astype(BF16)
    gsum = gsum_ref[...]

    def head_norm(p, row):
        ss = _dot((p * p).astype(BF16), gsum)
        return p * lax.rsqrt(ss * (1.0 / HEAD_DIM) + EPS) * gains_ref[row:row + 1, :]

    def rotary(p):
        if not rope:
            return p
        cos, sa, sb = cos_ref[...], sa_ref[...], sb_ref[...]
        halves = []
        for i in range(CHUNK // LANES):
            xh = p[:, i * LANES:(i + 1) * LANES]
            nxt = pltpu.roll(xh, LANES - HEAD_DIM // 4, 1)
            prv = pltpu.roll(xh, HEAD_DIM // 4, 1)
            halves.append(xh * cos + nxt * sa + prv * sb)
        return jnp.concatenate(halves, axis=1)

    def proj(col):
        return _dot(h, w_ref[:, col:col + CHUNK])

    for j in range(A_W // CHUNK):
        sl = slice(j * CHUNK, (j + 1) * CHUNK)
        qa_ref[:, sl] = (rotary(head_norm(proj(j * CHUNK), 0)) * Q_SCALE).astype(BF16)
        kat_ref[sl, :] = rotary(head_norm(proj(A_W + j * CHUNK), 1)).T.astype(kat_ref.dtype)
        va_ref[:, sl] = proj(2 * A_W + j * CHUNK).astype(va_ref.dtype)
        qb_ref[:, sl] = (rotary(head_norm(proj(3 * A_W + j * CHUNK), 2)) * Q_SCALE).astype(BF16)
    kbt_ref[...] = rotary(head_norm(proj(3 * A_W + B_Q), 3)).T.astype(kbt_ref.dtype)
    vbt_ref[...] = proj(3 * A_W + B_Q + B_KVW).T.astype(vbt_ref.dtype)


def _qkv_proj(x3d, mod3, mod_row, g1, w_qkv, gsum, gains, rope_tabs, tm, kv_dtype):
    b, t, _ = x3d.shape
    rows = b * t
    rope = rope_tabs is not None
    tps = t // tm
    in_specs = [pl.BlockSpec((tm, D_MODEL), lambda i: (i, 0)),
                pl.BlockSpec((None, 1, N_MOD * D_MODEL), lambda i: (mod_row(i), 0, 0)),
                _const_spec((1, D_MODEL)),
                _const_spec((D_MODEL, N_QKV)),
                _const_spec((CHUNK, CHUNK)),
                _const_spec((4, CHUNK))]
    args = [x3d.reshape(rows, D_MODEL), mod3, g1, w_qkv, gsum, gains]
    if rope:
        in_specs += [pl.BlockSpec((tm, LANES), lambda i: (i % tps, 0))] * 3
        args += list(rope_tabs)
    wide = pl.BlockSpec((tm, A_W), lambda i: (i, 0))
    out_specs = [wide, wide,
                 pl.BlockSpec((None, A_W, tm), lambda i: (i // tps, 0, i % tps)),
                 wide,
                 pl.BlockSpec((None, B_KVW, tm), lambda i: (i // tps, 0, i % tps)),
                 pl.BlockSpec((None, B_KVW, tm), lambda i: (i // tps, 0, i % tps))]
    out_shape = [jax.ShapeDtypeStruct((rows, A_W), BF16),
                 jax.ShapeDtypeStruct((rows, B_Q), BF16),
                 jax.ShapeDtypeStruct((b, A_W, t), kv_dtype),
                 jax.ShapeDtypeStruct((rows, A_W), kv_dtype),
                 jax.ShapeDtypeStruct((b, B_KVW, t), kv_dtype),
                 jax.ShapeDtypeStruct((b, B_KVW, t), kv_dtype)]
    return pl.pallas_call(
        functools.partial(_qkv_kernel, rope=rope),
        grid=(rows // tm,),
        in_specs=in_specs,
        out_specs=out_specs,
        out_shape=out_shape,
        compiler_params=_params(1),
        name="qkv_rope" if rope else "qkv_ctx",
    )(*args)


def _lambda(lam_ref):
    lv = lam_ref[...]
    t1 = jnp.sum(lv[0:1] * lv[1:2], axis=-1, keepdims=True)
    t2 = jnp.sum(lv[2:3] * lv[3:4], axis=-1, keepdims=True)
    return jnp.exp(t1) - jnp.exp(t2) + LAMBDA_INIT


def _softmax_parts(scores, extra=None):
    m = functools.reduce(jnp.maximum, [jnp.max(s, axis=-1, keepdims=True) for s in scores])
    if extra is not None:
        m = jnp.maximum(m, extra)
    es = [jnp.exp2(s - m) for s in scores]
    total = functools.reduce(jnp.add, [jnp.sum(e, axis=-1, keepdims=True) for e in es])
    if extra is not None:
        total = total + jnp.exp2(extra - m)
    return es, 1.0 / total


def _diff_queries(q):
    lane = lax.broadcasted_iota(jnp.int32, q.shape, 1)
    zero = jnp.zeros_like(q)
    return jnp.concatenate([jnp.where(lane < HEAD_DIM, q, zero),
                            jnp.where(lane >= HEAD_DIM, q, zero)], axis=0)


def _diff_finish(scores, values, lam, subg):
    n = scores[0].shape[0] // 2
    es, inv = _softmax_parts(scores)
    row = lax.broadcasted_iota(jnp.int32, inv.shape, 0)
    coef = jnp.where(row < n, inv, -lam * inv)
    o = None
    for e, v in zip(es, values):
        w = e * coef
        part = _dot((w[:n] + w[n:]).astype(BF16), v)
        o = part if o is None else o + part
    o = o * lax.rsqrt(jnp.mean(o * o, axis=-1, keepdims=True) + EPS)
    return o * subg * (1.0 - LAMBDA_INIT)


def _group_queries(qg):
    lane = lax.broadcasted_iota(jnp.int32, qg.shape, 1)
    zero = jnp.zeros_like(qg)
    return jnp.concatenate(
        [jnp.where((lane >= r * HEAD_DIM) & (lane < (r + 1) * HEAD_DIM), qg, zero)
         for r in range(B_GROUP)], axis=0)


def _tile4(x):
    return jnp.concatenate([x] * B_GROUP, axis=0)


def _sink_column(sink_ref, g, n):
    return jnp.concatenate(
        [jnp.broadcast_to(sink_ref[pl.ds(g * B_GROUP + r, 1), 0:1], (n, 1)) for r in range(B_GROUP)],
        axis=0)


def _group_finish(scores, values_t, sink_col):
    n = scores[0].shape[0] // B_GROUP
    es, inv = _softmax_parts(scores, extra=sink_col)
    o = None
    for e, vt in zip(es, values_t):
        part = _dot_nt((e * inv).astype(BF16), vt)
        o = part if o is None else o + part
    lane = lax.broadcasted_iota(jnp.int32, (n, CHUNK), 1)
    out = jnp.zeros((n, CHUNK), F32)
    for r in range(B_GROUP):
        sel = (lane >= r * HEAD_DIM) & (lane < (r + 1) * HEAD_DIM)
        out = jnp.where(sel, o[r * n:(r + 1) * n], out)
    return out


def _ctx_attn_kernel(qa_ref, qb_ref, kat_ref, va_ref, kbt_ref, vbt_ref, lam_ref, subg_ref, sink_ref,
                     oa_ref, ob_ref):
    lam = _lambda(lam_ref)
    subg = subg_ref[...]
    n = qa_ref.shape[0]

    def start(c):
        if c < A_HEADS:
            sl = slice(c * LANES, (c + 1) * LANES)
            return _dot(_diff_queries(qa_ref[:, sl]), kat_ref[sl, :].astype(BF16))
        g = c - A_HEADS
        kt = kbt_ref[g * HEAD_DIM:(g + 1) * HEAD_DIM, :].astype(BF16)
        return _dot(_group_queries(qb_ref[:, g * CHUNK:(g + 1) * CHUNK]), _tile4(kt))

    def finish(c, s):
        if c < A_HEADS:
            sl = slice(c * LANES, (c + 1) * LANES)
            oa_ref[:, sl] = _diff_finish([s], [va_ref[:, sl].astype(BF16)], lam, subg).astype(BF16)
        else:
            g = c - A_HEADS
            vt = vbt_ref[g * HEAD_DIM:(g + 1) * HEAD_DIM, :].astype(BF16)
            o = _group_finish([s], [_tile4(vt)], _sink_column(sink_ref, g, n))
            ob_ref[:, g * CHUNK:(g + 1) * CHUNK] = o.astype(BF16)

    _staggered(A_HEADS + B_KV_HEADS, start, finish)


def _ctx_attn(qa, qb, kat, va, kbt, vbt, lam_vecs, subg, sink_rows):
    b, _, t = kat.shape
    tok = pl.BlockSpec((t, A_W), lambda i: (i, 0))
    kvt = pl.BlockSpec((None, B_KVW, t), lambda i: (i, 0, 0))
    return pl.pallas_call(
        _ctx_attn_kernel,
        grid=(b,),
        in_specs=[tok, tok, pl.BlockSpec((None, A_W, t), lambda i: (i, 0, 0)), tok, kvt, kvt,
                  _const_spec((4, HEAD_DIM)), _const_spec((1, LANES)), _const_spec((B_HEADS, LANES))],
        out_specs=[tok, tok],
        out_shape=[jax.ShapeDtypeStruct((b * t, A_W), BF16), jax.ShapeDtypeStruct((b * t, B_Q), BF16)],
        compiler_params=_params(1),
        name="ctx_attn",
    )(qa, qb, kat, va, kbt, vbt, lam_vecs, subg, sink_rows)


def _diff_lat_kernel(q_ref, kt_ref, v_ref, ckt_ref, cv_ref, lam_ref, subg_ref, o_ref):
    lam = _lambda(lam_ref)
    subg = subg_ref[...]
    kt = kt_ref[...]
    ckt = ckt_ref[...].astype(BF16)
    values = [v_ref[...], cv_ref[...].astype(BF16)]

    def start(i):
        q = _diff_queries(q_ref[i * BLOCK:(i + 1) * BLOCK, :])
        return [_dot(q, kt), _dot(q, ckt)]

    def finish(i, scores):
        o_ref[i * BLOCK:(i + 1) * BLOCK, :] = _diff_finish(scores, values, lam, subg).astype(BF16)

    _staggered(q_ref.shape[0] // BLOCK, start, finish)


def _diff_lat(q, kt, v, ckt, cv, lam_vecs, subg, tq):
    b, t, _ = q.shape
    past = cv.shape[1]
    qspec = pl.BlockSpec((None, tq, LANES), lambda i, h, j: (i, j, h))
    return pl.pallas_call(
        _diff_lat_kernel,
        grid=(b, A_HEADS, t // tq),
        in_specs=[qspec,
                  pl.BlockSpec((None, LANES, t), lambda i, h, j: (i, h, 0)),
                  pl.BlockSpec((None, t, LANES), lambda i, h, j: (i, 0, h)),
                  pl.BlockSpec((None, LANES, past), lambda i, h, j: (i, h, 0)),
                  pl.BlockSpec((None, past, LANES), lambda i, h, j: (i, 0, h)),
                  _const_spec((4, HEAD_DIM)), _const_spec((1, LANES))],
        out_specs=qspec,
        out_shape=jax.ShapeDtypeStruct((b, t, A_W), BF16),
        compiler_params=_params(3),
        name="diff_lat",
    )(q, kt, v, ckt, cv, lam_vecs, subg)


def _win_lat_kernel(q_ref, kt_ref, vt_ref, ckt_ref, cvt_ref, sink_ref, o_ref):
    g = pl.program_id(1)
    t = kt_ref.shape[1]
    nblk = t // BLOCK
    span = 3 * BLOCK
    kt = _tile4(kt_ref[...])
    vt = _tile4(vt_ref[...])
    ckt = _tile4(ckt_ref[...].astype(BF16))
    cvt = _tile4(cvt_ref[...].astype(BF16))
    sink_col = _sink_column(sink_ref, g, BLOCK)
    rel = (lax.broadcasted_iota(jnp.int32, (B_GROUP * BLOCK, span), 1)
           - lax.broadcasted_iota(jnp.int32, (B_GROUP * BLOCK, span), 0) % BLOCK)

    def window(i):
        return min(max((i - 1) * BLOCK, 0), t - span)

    bands = {}

    def band(offset):
        if offset not in bands:
            bands[offset] = jnp.abs(rel + offset) <= WINDOW
        return bands[offset]

    def start(i):
        q = _group_queries(q_ref[i * BLOCK:(i + 1) * BLOCK, :])
        w = window(i)
        return [_dot(q, kt[:, w:w + span]), _dot(q, ckt)]

    def finish(i, scores):
        w = window(i)
        scores = [jnp.where(band(w - i * BLOCK), scores[0], NEG), scores[1]]
        o = _group_finish(scores, [vt[:, w:w + span], cvt], sink_col)
        o_ref[i * BLOCK:(i + 1) * BLOCK, :] = o.astype(BF16)

    _staggered(nblk, start, finish)


def _win_lat(q, kt, vt, ckt, cvt, sink_rows):
    b, t, _ = q.shape
    past = ckt.shape[2]
    qspec = pl.BlockSpec((None, t, CHUNK), lambda i, g: (i, 0, g))
    kspec = pl.BlockSpec((None, HEAD_DIM, t), lambda i, g: (i, g, 0))
    cspec = pl.BlockSpec((None, HEAD_DIM, past), lambda i, g: (i, g, 0))
    return pl.pallas_call(
        _win_lat_kernel,
        grid=(b, B_KV_HEADS),
        in_specs=[qspec, kspec, kspec, cspec, cspec, _const_spec((B_HEADS, LANES))],
        out_specs=qspec,
        out_shape=jax.ShapeDtypeStruct((b, t, B_Q), BF16),
        compiler_params=_params(2),
        name="win_lat",
    )(q, kt, vt, ckt, cvt, sink_rows)


def _merge_kernel(x_ref, oa_ref, ob_ref, mod_ref, g1_ref, wg_ref, woa_ref, wob_ref, wout_ref, o_ref):
    x = x_ref[...]
    mod = mod_ref[...]
    h = _modulated_norm(x, g1_ref[...], mod[:, D_MODEL:2 * D_MODEL], mod[:, 0:D_MODEL]).astype(BF16)
    ga = _dot(h, wg_ref[:, 0:D_MODEL])
    gb = _dot(h, wg_ref[:, D_MODEL:2 * D_MODEL])
    merged = (jax.nn.sigmoid(ga) * _dot(oa_ref[...], woa_ref[...])
              + jax.nn.sigmoid(gb) * _dot(ob_ref[...], wob_ref[...]))
    o_ref[...] = x + mod[:, 2 * D_MODEL:3 * D_MODEL] * _dot(merged.astype(BF16), wout_ref[...])


def _merge(x2d, oa, ob, mod3, mod_row, g1, w_g, w_oa, w_ob, w_out, tm):
    rows = x2d.shape[0]
    tile = pl.BlockSpec((tm, D_MODEL), lambda i: (i, 0))
    sq = _const_spec((D_MODEL, D_MODEL))
    return pl.pallas_call(
        _merge_kernel,
        grid=(rows // tm,),
        in_specs=[tile, tile, tile,
                  pl.BlockSpec((None, 1, N_MOD * D_MODEL), lambda i: (mod_row(i), 0, 0)),
                  _const_spec((1, D_MODEL)), _const_spec((D_MODEL, 2 * D_MODEL)), sq, sq, sq],
        out_specs=tile,
        out_shape=jax.ShapeDtypeStruct((rows, D_MODEL), F32),
        compiler_params=_params(1),
        name="merge_out",
    )(x2d, oa, ob, mod3, g1, w_g, w_oa, w_ob, w_out)


def _ffn_kernel(x_ref, mod_ref, g2_ref, wgate_ref, wup_ref, wdown_ref, o_ref):
    x = x_ref[...]
    mod = mod_ref[...]
    h = _modulated_norm(x, g2_ref[...], mod[:, 4 * D_MODEL:5 * D_MODEL],
                        mod[:, 3 * D_MODEL:4 * D_MODEL]).astype(BF16)
    gate = _dot(h, wgate_ref[...])
    act = (gate * jax.nn.sigmoid(gate) * _dot(h, wup_ref[...])).astype(BF16)
    o_ref[...] = x + mod[:, 5 * D_MODEL:6 * D_MODEL] * _dot(act, wdown_ref[...])


def _ffn(x2d, mod3, mod_row, g2, w_gate, w_up, w_down, tm):
    rows = x2d.shape[0]
    tile = pl.BlockSpec((tm, D_MODEL), lambda i: (i, 0))
    return pl.pallas_call(
        _ffn_kernel,
        grid=(rows // tm,),
        in_specs=[tile,
                  pl.BlockSpec((None, 1, N_MOD * D_MODEL), lambda i: (mod_row(i), 0, 0)),
                  _const_spec((1, D_MODEL)), _const_spec((D_MODEL, D_FF)),
                  _const_spec((D_MODEL, D_FF)), _const_spec((D_FF, D_MODEL))],
        out_specs=tile,
        out_shape=jax.ShapeDtypeStruct((rows, D_MODEL), F32),
        compiler_params=_params(1),
        name="swiglu_ffn",
    )(x2d, mod3, g2, w_gate, w_up, w_down)


def _rope_tables(seq_len):
    quarter = HEAD_DIM // 4
    t = jnp.arange(seq_len)
    row = (t // GRID_W).astype(F32)
    col = (t % GRID_W).astype(F32)
    freqs = ROPE_BASE ** (-jnp.arange(quarter, dtype=F32) / quarter)
    ang_r = row[:, None] * freqs
    ang_c = col[:, None] * freqs
    zeros = jnp.zeros_like(ang_r)
    cos = jnp.concatenate([jnp.cos(ang_r)] * 2 + [jnp.cos(ang_c)] * 2, axis=-1)
    sin_next = jnp.concatenate([-jnp.sin(ang_r), zeros, -jnp.sin(ang_c), zeros], axis=-1)
    sin_prev = jnp.concatenate([zeros, jnp.sin(ang_r), zeros, jnp.sin(ang_c)], axis=-1)
    rep = LANES // HEAD_DIM
    return tuple(jnp.tile(a, (1, rep)) for a in (cos, sin_next, sin_prev))


def _group_sum_matrix():
    idx = np.arange(CHUNK) // HEAD_DIM
    return jnp.asarray(idx[:, None] == idx[None, :], dtype=BF16)


def kernel(x_prompt, x_sample, cache_diff_k, cache_diff_v, cache_win_k, cache_win_v, c, c_ctx,
           w_ada, b_ada, norm1_g, w_in, qn_a, kn_a, lambda_q1, lambda_k1, lambda_q2, lambda_k2,
           subln_g, qn_b, kn_b, sink, w_oa, w_ob, w_out, norm2_g, w_gate, w_up, w_down):
    batch, seq, _ = x_prompt.shape
    dec_batch, dec_seq, _ = x_sample.shape
    past = cache_diff_k.shape[2]
    l = 0

    cc = jnp.zeros((8, D_MODEL), F32).at[:dec_batch].set(c).at[dec_batch].set(c_ctx)
    mod3 = _modulation(cc, w_ada[l], b_ada[l][None, :]).reshape(8, 1, N_MOD * D_MODEL)

    w_in_b = w_in[l].astype(BF16)
    w_qkv, w_g = w_in_b[:, :N_QKV], w_in_b[:, N_QKV:]
    w_oa_b, w_ob_b, w_out_b = w_oa[l].astype(BF16), w_ob[l].astype(BF16), w_out[l].astype(BF16)
    w_gate_b, w_up_b, w_down_b = w_gate[l].astype(BF16), w_up[l].astype(BF16), w_down[l].astype(BF16)
    g1 = norm1_g[l][None, :]
    g2 = norm2_g[l][None, :]
    rep = CHUNK // HEAD_DIM
    gains = jnp.stack([jnp.tile(v[l], rep) for v in (qn_a, kn_a, qn_b, kn_b)])
    lam_vecs = jnp.stack([lambda_q1[l], lambda_k1[l], lambda_q2[l], lambda_k2[l]])
    subg = subln_g[l][None, :]
    sink_rows = jnp.broadcast_to((sink[l] * LOG2E)[:, None], (B_HEADS, LANES))
    gsum = _group_sum_matrix()
    tm = 512

    xp = x_prompt.reshape(batch * seq, D_MODEL)
    ctx_row = lambda i: dec_batch
    qa, qb, kat, va, kbt, vbt = _qkv_proj(x_prompt, mod3, ctx_row, g1, w_qkv, gsum, gains, None, seq, F32)
    oa, ob = _ctx_attn(qa, qb, kat, va, kbt, vbt, lam_vecs, subg, sink_rows)
    xp1 = _merge(xp, oa, ob, mod3, ctx_row, g1, w_g, w_oa_b, w_ob_b, w_out_b, tm)
    y_prompt = _ffn(xp1, mod3, ctx_row, g2, w_gate_b, w_up_b, w_down_b, tm)
    new_diff_k = kat.reshape(batch, A_HEADS, 2, HEAD_DIM, seq).transpose(0, 4, 1, 2, 3)[:, None]
    new_diff_v = va.reshape(batch, 1, seq, A_HEADS, 2 * HEAD_DIM)
    new_win_k = kbt.reshape(batch, B_KV_HEADS, HEAD_DIM, seq).transpose(0, 3, 1, 2)[:, None]
    new_win_v = vbt.reshape(batch, B_KV_HEADS, HEAD_DIM, seq).transpose(0, 3, 1, 2)[:, None]

    xs = x_sample.reshape(dec_batch * dec_seq, D_MODEL)
    lat_row = lambda i: i // (dec_seq // tm)
    qa, qb, kat, va, kbt, vbt = _qkv_proj(x_sample, mod3, lat_row, g1, w_qkv, gsum, gains,
                                          _rope_tables(dec_seq), tm, BF16)
    ckt_a = cache_diff_k[:, l].transpose(0, 2, 3, 4, 1).reshape(dec_batch, A_W, past)
    cv_a = cache_diff_v[:, l].reshape(dec_batch, past, A_W)
    ckt_b = cache_win_k[:, l].transpose(0, 2, 3, 1).reshape(dec_batch, B_KVW, past)
    cvt_b = cache_win_v[:, l].transpose(0, 2, 3, 1).reshape(dec_batch, B_KVW, past)
    r3 = lambda a: a.reshape(dec_batch, dec_seq, a.shape[-1])
    oa = _diff_lat(r3(qa), kat, r3(va), ckt_a, cv_a, lam_vecs, subg, 512)
    ob = _win_lat(r3(qb), kbt, vbt, ckt_b, cvt_b, sink_rows)
    xs1 = _merge(xs, oa.reshape(-1, A_W), ob.reshape(-1, B_Q), mod3, lat_row, g1,
                 w_g, w_oa_b, w_ob_b, w_out_b, tm)
    y_sample = _ffn(xs1, mod3, lat_row, g2, w_gate_b, w_up_b, w_down_b, tm)

    return (y_prompt.reshape(batch, seq, D_MODEL),
            y_sample.reshape(dec_batch, dec_seq, D_MODEL),
            new_diff_k, new_diff_v, new_win_k, new_win_v)
```

```python
import functools
import math

import jax
import jax.numpy as jnp
import numpy as np
from jax import lax
from jax.experimental import pallas as pl
from jax.experimental.pallas import tpu as pltpu

D_MODEL = 1024
HEAD_DIM = 64
A_HEADS = 8
B_HEADS = 16
B_KV_HEADS = 4
B_GROUP = B_HEADS // B_KV_HEADS
A_W = A_HEADS * 2 * HEAD_DIM
B_Q = B_HEADS * HEAD_DIM
B_KVW = B_KV_HEADS * HEAD_DIM
N_K = A_W + B_KVW
N_T = A_W + B_Q + A_W + B_KVW
WINDOW = 128
BLOCK = 128
GRID_W = 64
D_FF = 2816
N_MOD = 6
EPS = 1e-6
ROPE_BASE = 10000.0
NEG = -1e30
LAMBDA_INIT = 0.8 - 0.6 * math.exp(-0.3 * 0)
LOG2E = math.log2(math.e)
Q_SCALE = HEAD_DIM ** -0.5 * LOG2E

LANES = 128
CHUNK = 256
ONES_ROWS = 16
VMEM_LIMIT = 56 * 1024 * 1024

F32 = jnp.float32
BF16 = jnp.bfloat16


def _params(n_axes):
    return pltpu.CompilerParams(dimension_semantics=("arbitrary",) * n_axes,
                                vmem_limit_bytes=VMEM_LIMIT)


def _const_spec(shape):
    nd = len(shape)
    return pl.BlockSpec(shape, lambda *_: (0,) * nd, pipeline_mode=pl.Buffered(1))


def _dot(a, b):
    return jnp.dot(a, b, preferred_element_type=F32)


def _modulated_norm(x, gain, scale, shift):
    y = x * lax.rsqrt(jnp.mean(x * x, axis=-1, keepdims=True) + EPS) * gain
    return y * (1.0 + scale) + shift


def _software_pipeline(n, stages):
    depth = len(stages)
    state = [None] * n
    for step in range(n + depth - 1):
        for k, stage in enumerate(stages):
            i = step - k
            if 0 <= i < n:
                state[i] = stage(i, state[i])


def _mod_kernel(c_ref, w_ref, b_ref, o_ref):
    cc = c_ref[...]
    s = cc * jax.nn.sigmoid(cc)
    o_ref[...] = _dot(s.astype(BF16), w_ref[...].astype(BF16)) + b_ref[...]


def _modulation(cc, w_ada, b_ada):
    rows, n = cc.shape[0], w_ada.shape[1]
    tn = n // 4
    return pl.pallas_call(
        _mod_kernel,
        grid=(n // tn,),
        in_specs=[pl.BlockSpec((rows, D_MODEL), lambda j: (0, 0)),
                  pl.BlockSpec((D_MODEL, tn), lambda j: (0, j)),
                  pl.BlockSpec((1, tn), lambda j: (0, j))],
        out_specs=pl.BlockSpec((rows, tn), lambda j: (0, j)),
        out_shape=jax.ShapeDtypeStruct((rows, n), F32),
        compiler_params=_params(1),
        name="adaln_mod",
    )(cc, w_ada, b_ada)


def _qkv_kernel(*refs, rope, emit_new_kv):
    x_ref, mod_ref, g1_ref, wk_ref, wt_ref, gsum_ref, kgain_ref, qgain_ref = refs[:8]
    refs = refs[8:]
    if rope:
        cos_ref, sa_ref, sb_ref, cost_ref, sint_ref = refs[:5]
        refs = refs[5:]
    qat_ref, qbt_ref, ka_ref, kb_ref, vat_ref, vbt_ref = refs[:6]
    if emit_new_kv:
        kat_ref, va_ref, kbt_ref = refs[6:]

    mod = mod_ref[...]
    h = _modulated_norm(x_ref[...], g1_ref[...], mod[:, D_MODEL:2 * D_MODEL], mod[:, 0:D_MODEL])
    hb = h.astype(BF16)
    ht = h.T.astype(BF16)
    gsum = gsum_ref[...]

    def key_chunk(col, row):
        p = _dot(hb, wk_ref[:, col:col + CHUNK])
        ss = _dot((p * p).astype(BF16), gsum)
        p = p * lax.rsqrt(ss * (1.0 / HEAD_DIM) + EPS) * kgain_ref[row:row + 1, :]
        if not rope:
            return p
        cos, sa, sb = cos_ref[...], sa_ref[...], sb_ref[...]
        halves = []
        for i in range(CHUNK // LANES):
            xh = p[:, i * LANES:(i + 1) * LANES]
            nxt = pltpu.roll(xh, LANES - HEAD_DIM // 4, 1)
            prv = pltpu.roll(xh, HEAD_DIM // 4, 1)
            halves.append(xh * cos + nxt * sa + prv * sb)
        return jnp.concatenate(halves, axis=1)

    for j in range(A_W // CHUNK):
        sl = slice(j * CHUNK, (j + 1) * CHUNK)
        ka = key_chunk(j * CHUNK, 0)
        ka_ref[:, sl] = ka.astype(BF16)
        if emit_new_kv:
            kat_ref[sl, :] = ka.T
    kb = key_chunk(A_W, 1)
    kb_ref[...] = kb.astype(BF16)
    if emit_new_kv:
        kbt_ref[...] = kb.T

    def feat_chunk(row):
        return _dot(wt_ref[row:row + CHUNK, :], ht)

    def query_chunk(row, which):
        p = feat_chunk(row)
        gain = qgain_ref[which]
        q4 = HEAD_DIM // 4
        heads = []
        for r in range(CHUNK // HEAD_DIM):
            x = p[r * HEAD_DIM:(r + 1) * HEAD_DIM, :]
            x = x * lax.rsqrt(jnp.mean(x * x, axis=0, keepdims=True) + EPS) * gain
            if rope:
                swapped = jnp.concatenate([x[q4:2 * q4], x[0:q4], x[3 * q4:4 * q4], x[2 * q4:3 * q4]],
                                          axis=0)
                x = x * cost_ref[...] + swapped * sint_ref[...]
            heads.append(x)
        return (jnp.concatenate(heads, axis=0) * Q_SCALE).astype(BF16)

    for j in range(A_W // CHUNK):
        sl = slice(j * CHUNK, (j + 1) * CHUNK)
        qat_ref[sl, :] = query_chunk(j * CHUNK, 0)
        qbt_ref[sl, :] = query_chunk(A_W + j * CHUNK, 1)
        vt = feat_chunk(A_W + B_Q + j * CHUNK)
        vat_ref[sl, :] = vt.astype(BF16)
        if emit_new_kv:
            va_ref[:, sl] = vt.T
    vbt_ref[...] = feat_chunk(A_W + B_Q + A_W).astype(vbt_ref.dtype)


def _qkv_proj(x3d, mod3, mod_row, g1, w_k, w_t, gsum, kgains, qgains, rope_tabs, tm, emit_new_kv):
    b, t, _ = x3d.shape
    rows = b * t
    rope = rope_tabs is not None
    tps = t // tm
    in_specs = [pl.BlockSpec((tm, D_MODEL), lambda i: (i, 0)),
                pl.BlockSpec((None, 1, N_MOD * D_MODEL), lambda i: (mod_row(i), 0, 0)),
                _const_spec((1, D_MODEL)),
                _const_spec((D_MODEL, N_K)),
                _const_spec((N_T, D_MODEL)),
                _const_spec((CHUNK, CHUNK)),
                _const_spec((2, CHUNK)),
                _const_spec((2, HEAD_DIM, 1))]
    args = [x3d.reshape(rows, D_MODEL), mod3, g1, w_k, w_t, gsum, kgains, qgains]
    if rope:
        nat, featm = rope_tabs
        in_specs += [pl.BlockSpec((tm, LANES), lambda i: (i % tps, 0))] * 3
        in_specs += [pl.BlockSpec((HEAD_DIM, tm), lambda i: (0, i % tps))] * 2
        args += list(nat) + list(featm)
    tok_wide = pl.BlockSpec((tm, A_W), lambda i: (i, 0))
    tok_narrow = pl.BlockSpec((tm, B_KVW), lambda i: (i, 0))
    feat_wide = pl.BlockSpec((None, A_W, tm), lambda i: (i // tps, 0, i % tps))
    feat_narrow = pl.BlockSpec((None, B_KVW, tm), lambda i: (i // tps, 0, i % tps))
    out_specs = [feat_wide, feat_wide, tok_wide, tok_narrow, feat_wide, feat_narrow]
    out_shape = [jax.ShapeDtypeStruct((b, A_W, t), BF16),
                 jax.ShapeDtypeStruct((b, B_Q, t), BF16),
                 jax.ShapeDtypeStruct((rows, A_W), BF16),
                 jax.ShapeDtypeStruct((rows, B_KVW), BF16),
                 jax.ShapeDtypeStruct((b, A_W, t), BF16),
                 jax.ShapeDtypeStruct((b, B_KVW, t), F32 if emit_new_kv else BF16)]
    if emit_new_kv:
        out_specs += [feat_wide, tok_wide, feat_narrow]
        out_shape += [jax.ShapeDtypeStruct((b, A_W, t), F32),
                      jax.ShapeDtypeStruct((rows, A_W), F32),
                      jax.ShapeDtypeStruct((b, B_KVW, t), F32)]
    return pl.pallas_call(
        functools.partial(_qkv_kernel, rope=rope, emit_new_kv=emit_new_kv),
        grid=(rows // tm,),
        in_specs=in_specs,
        out_specs=out_specs,
        out_shape=out_shape,
        compiler_params=_params(1),
        name="qkv_rope" if rope else "qkv_ctx",
    )(*args)


def _lambda(lam_ref):
    lv = lam_ref[...]
    t1 = jnp.sum(lv[0:1] * lv[1:2], axis=-1, keepdims=True)
    t2 = jnp.sum(lv[2:3] * lv[3:4], axis=-1, keepdims=True)
    return jnp.exp(t1) - jnp.exp(t2) + LAMBDA_INIT


def _with_ones(vt):
    return jnp.concatenate([vt, jnp.ones((ONES_ROWS, vt.shape[1]), vt.dtype)], axis=0)


def _column_softmax(scores, extra=None):
    m = functools.reduce(jnp.maximum, [jnp.max(s, axis=0, keepdims=True) for s in scores])
    if extra is not None:
        m = jnp.maximum(m, extra)
    return [jnp.exp2(s - m).astype(BF16) for s in scores], m


def _diff_query_cols(qt):
    zero = jnp.zeros((HEAD_DIM, qt.shape[1]), qt.dtype)
    return jnp.concatenate([jnp.concatenate([qt[:HEAD_DIM], zero], axis=0),
                            jnp.concatenate([zero, qt[HEAD_DIM:]], axis=0)], axis=1)


def _diff_output(r, lam, subg_col):
    n = r.shape[1] // 2
    dv = 2 * HEAD_DIM
    tot = r[dv:dv + 1, :]
    o = r[:dv, :n] * (1.0 / tot[:, :n]) - r[:dv, n:] * (lam / tot[:, n:])
    o = o * lax.rsqrt(jnp.mean(o * o, axis=0, keepdims=True) + EPS)
    return (o * subg_col * (1.0 - LAMBDA_INIT)).T


def _group_output(r, m, sink_row):
    n = r.shape[1] // B_GROUP
    tot = r[HEAD_DIM:HEAD_DIM + 1, :] + jnp.exp2(sink_row - m)
    o = r[:HEAD_DIM, :] * (1.0 / tot)
    return jnp.concatenate([o[:, i * n:(i + 1) * n] for i in range(B_GROUP)], axis=0).T


def _sink_row(sink_ref, g, n):
    reps = n // LANES
    return jnp.concatenate([sink_ref[pl.ds(g * B_GROUP + r, 1), :] for r in range(B_GROUP)
                            for _ in range(reps)], axis=1)


def _ctx_attn_kernel(qat_ref, qbt_ref, ka_ref, kb_ref, vat_ref, vbt_ref, lam_ref, subg_ref, sink_ref,
                     oa_ref, ob_ref):
    lam = _lambda(lam_ref)
    subg_col = subg_ref[...]
    n = ka_ref.shape[0]
    kb = kb_ref[...]

    def scores(c, _):
        if c < A_HEADS:
            sl = slice(c * LANES, (c + 1) * LANES)
            return _dot(ka_ref[:, sl], _diff_query_cols(qat_ref[sl, :]))
        g = c - A_HEADS
        qs = jnp.concatenate([qbt_ref[(g * B_GROUP + r) * HEAD_DIM:(g * B_GROUP + r + 1) * HEAD_DIM, :]
                              for r in range(B_GROUP)], axis=1)
        zero = jnp.zeros_like(qs)
        w = jnp.concatenate([qs if i == g else zero for i in range(B_KV_HEADS)], axis=0)
        return _dot(kb, w)

    def probs(c, s):
        if c < A_HEADS:
            return _column_softmax([s])
        return _column_softmax([s], extra=_sink_row(sink_ref, c - A_HEADS, n))

    def outputs(c, st):
        es, m = st
        if c < A_HEADS:
            sl = slice(c * LANES, (c + 1) * LANES)
            r = _dot(_with_ones(vat_ref[sl, :]), es[0])
            oa_ref[:, sl] = _diff_output(r, lam, subg_col).astype(BF16)
        else:
            g = c - A_HEADS
            vt = vbt_ref[g * HEAD_DIM:(g + 1) * HEAD_DIM, :].astype(BF16)
            r = _dot(_with_ones(vt), es[0])
            o = _group_output(r, m, _sink_row(sink_ref, g, n))
            ob_ref[:, g * CHUNK:(g + 1) * CHUNK] = o.astype(BF16)

    _software_pipeline(A_HEADS + B_KV_HEADS, [scores, probs, outputs])


def _ctx_attn(qat, qbt, ka, kb, vat, vbt, lam_vecs, subg_col, sink_rows):
    b, _, t = qat.shape
    feat = pl.BlockSpec((None, A_W, t), lambda i: (i, 0, 0))
    tok = pl.BlockSpec((t, A_W), lambda i: (i, 0))
    return pl.pallas_call(
        _ctx_attn_kernel,
        grid=(b,),
        in_specs=[feat, feat, tok, pl.BlockSpec((t, B_KVW), lambda i: (i, 0)), feat,
                  pl.BlockSpec((None, B_KVW, t), lambda i: (i, 0, 0)),
                  _const_spec((4, HEAD_DIM)), _const_spec((2 * HEAD_DIM, 1)),
                  _const_spec((B_HEADS, LANES))],
        out_specs=[tok, tok],
        out_shape=[jax.ShapeDtypeStruct((b * t, A_W), BF16), jax.ShapeDtypeStruct((b * t, B_Q), BF16)],
        compiler_params=_params(1),
        name="ctx_attn",
    )(qat, qbt, ka, kb, vat, vbt, lam_vecs, subg_col, sink_rows)


def _diff_lat_kernel(qt_ref, k_ref, vt_ref, ckt_ref, cv_ref, lam_ref, subg_ref, o_ref):
    lam = _lambda(lam_ref)
    subg_col = subg_ref[...]
    k = k_ref[...]
    ck = ckt_ref[...].T.astype(BF16)
    vt = _with_ones(vt_ref[...])
    cvt = _with_ones(cv_ref[...].T.astype(BF16))

    def scores(i, _):
        q = _diff_query_cols(qt_ref[:, i * BLOCK:(i + 1) * BLOCK])
        return [_dot(k, q), _dot(ck, q)]

    def probs(i, s):
        return _column_softmax(s)[0]

    def outputs(i, es):
        r = _dot(vt, es[0]) + _dot(cvt, es[1])
        o_ref[i * BLOCK:(i + 1) * BLOCK, :] = _diff_output(r, lam, subg_col).astype(BF16)

    _software_pipeline(qt_ref.shape[1] // BLOCK, [scores, probs, outputs])


def _diff_lat(qt, k, vt, ckt, cv, lam_vecs, subg_col, tq):
    b, _, t = qt.shape
    past = cv.shape[1]
    return pl.pallas_call(
        _diff_lat_kernel,
        grid=(b, A_HEADS, t // tq),
        in_specs=[pl.BlockSpec((None, LANES, tq), lambda i, h, j: (i, h, j)),
                  pl.BlockSpec((None, t, LANES), lambda i, h, j: (i, 0, h)),
                  pl.BlockSpec((None, LANES, t), lambda i, h, j: (i, h, 0)),
                  pl.BlockSpec((None, LANES, past), lambda i, h, j: (i, h, 0)),
                  pl.BlockSpec((None, past, LANES), lambda i, h, j: (i, 0, h)),
                  _const_spec((4, HEAD_DIM)), _const_spec((2 * HEAD_DIM, 1))],
        out_specs=pl.BlockSpec((None, tq, LANES), lambda i, h, j: (i, j, h)),
        out_shape=jax.ShapeDtypeStruct((b, t, A_W), BF16),
        compiler_params=_params(3),
        name="diff_lat",
    )(qt, k, vt, ckt, cv, lam_vecs, subg_col)


def _win_lat_kernel(qt_ref, k_ref, vt_ref, ckt_ref, cvt_ref, sink_ref, o_ref):
    g = pl.program_id(1)
    t = k_ref.shape[0]
    span = 3 * BLOCK
    keep = lax.broadcasted_iota(jnp.int32, (1, CHUNK), 1) // HEAD_DIM == g
    k = jnp.where(keep, k_ref[...].astype(F32), 0.0).astype(BF16)
    ck = jnp.where(keep, ckt_ref[...].T, 0.0).astype(BF16)
    vt = vt_ref[...]
    cvt = _with_ones(cvt_ref[...].astype(BF16))
    sink_row = _sink_row(sink_ref, g, BLOCK)
    rel = (lax.broadcasted_iota(jnp.int32, (span, B_GROUP * BLOCK), 0)
           - lax.broadcasted_iota(jnp.int32, (span, B_GROUP * BLOCK), 1) % BLOCK)

    def window(i):
        return min(max((i - 1) * BLOCK, 0), t - span)

    bands = {}

    def band(offset):
        if offset not in bands:
            bands[offset] = jnp.abs(rel + offset) <= WINDOW
        return bands[offset]

    def scores(i, _):
        w = window(i)
        qs = jnp.concatenate([qt_ref[r * HEAD_DIM:(r + 1) * HEAD_DIM, i * BLOCK:(i + 1) * BLOCK]
                              for r in range(B_GROUP)], axis=1)
        wq = jnp.concatenate([qs] * B_KV_HEADS, axis=0)
        return [jnp.where(band(w - i * BLOCK), _dot(k[w:w + span], wq), NEG), _dot(ck, wq)]

    def probs(i, s):
        return _column_softmax(s, extra=sink_row)

    def outputs(i, st):
        es, m = st
        w = window(i)
        r = _dot(_with_ones(vt[:, w:w + span]), es[0]) + _dot(cvt, es[1])
        o_ref[i * BLOCK:(i + 1) * BLOCK, :] = _group_output(r, m, sink_row).astype(BF16)

    _software_pipeline(t // BLOCK, [scores, probs, outputs])


def _win_lat(qt, k, vt, ckt, cvt, sink_rows):
    b, _, t = qt.shape
    past = ckt.shape[2]
    return pl.pallas_call(
        _win_lat_kernel,
        grid=(b, B_KV_HEADS),
        in_specs=[pl.BlockSpec((None, CHUNK, t), lambda i, g: (i, g, 0)),
                  pl.BlockSpec((None, t, B_KVW), lambda i, g: (i, 0, 0)),
                  pl.BlockSpec((None, HEAD_DIM, t), lambda i, g: (i, g, 0)),
                  pl.BlockSpec((None, B_KVW, past), lambda i, g: (i, 0, 0)),
                  pl.BlockSpec((None, HEAD_DIM, past), lambda i, g: (i, g, 0)),
                  _const_spec((B_HEADS, LANES))],
        out_specs=pl.BlockSpec((None, t, CHUNK), lambda i, g: (i, 0, g)),
        out_shape=jax.ShapeDtypeStruct((b, t, B_Q), BF16),
        compiler_params=_params(2),
        name="win_lat",
    )(qt, k, vt, ckt, cvt, sink_rows)


def _merge_kernel(x_ref, oa_ref, ob_ref, mod_ref, g1_ref, wg_ref, woa_ref, wob_ref, wout_ref, o_ref):
    x = x_ref[...]
    mod = mod_ref[...]
    h = _modulated_norm(x, g1_ref[...], mod[:, D_MODEL:2 * D_MODEL], mod[:, 0:D_MODEL]).astype(BF16)
    ga = _dot(h, wg_ref[:, 0:D_MODEL])
    gb = _dot(h, wg_ref[:, D_MODEL:2 * D_MODEL])
    merged = (jax.nn.sigmoid(ga) * _dot(oa_ref[...], woa_ref[...])
              + jax.nn.sigmoid(gb) * _dot(ob_ref[...], wob_ref[...]))
    o_ref[...] = x + mod[:, 2 * D_MODEL:3 * D_MODEL] * _dot(merged.astype(BF16), wout_ref[...])


def _merge(x2d, oa, ob, mod3, mod_row, g1, w_g, w_oa, w_ob, w_out, tm):
    rows = x2d.shape[0]
    tile = pl.BlockSpec((tm, D_MODEL), lambda i: (i, 0))
    sq = _const_spec((D_MODEL, D_MODEL))
    return pl.pallas_call(
        _merge_kernel,
        grid=(rows // tm,),
        in_specs=[tile, tile, tile,
                  pl.BlockSpec((None, 1, N_MOD * D_MODEL), lambda i: (mod_row(i), 0, 0)),
                  _const_spec((1, D_MODEL)), _const_spec((D_MODEL, 2 * D_MODEL)), sq, sq, sq],
        out_specs=tile,
        out_shape=jax.ShapeDtypeStruct((rows, D_MODEL), F32),
        compiler_params=_params(1),
        name="merge_out",
    )(x2d, oa, ob, mod3, g1, w_g, w_oa, w_ob, w_out)


def _ffn_kernel(x_ref, mod_ref, g2_ref, wgate_ref, wup_ref, wdown_ref, o_ref):
    x = x_ref[...]
    mod = mod_ref[...]
    h = _modulated_norm(x, g2_ref[...], mod[:, 4 * D_MODEL:5 * D_MODEL],
                        mod[:, 3 * D_MODEL:4 * D_MODEL]).astype(BF16)
    gate = _dot(h, wgate_ref[...])
    act = (gate * jax.nn.sigmoid(gate) * _dot(h, wup_ref[...])).astype(BF16)
    o_ref[...] = x + mod[:, 5 * D_MODEL:6 * D_MODEL] * _dot(act, wdown_ref[...])


def _ffn(x2d, mod3, mod_row, g2, w_gate, w_up, w_down, tm):
    rows = x2d.shape[0]
    tile = pl.BlockSpec((tm, D_MODEL), lambda i: (i, 0))
    return pl.pallas_call(
        _ffn_kernel,
        grid=(rows // tm,),
        in_specs=[tile,
                  pl.BlockSpec((None, 1, N_MOD * D_MODEL), lambda i: (mod_row(i), 0, 0)),
                  _const_spec((1, D_MODEL)), _const_spec((D_MODEL, D_FF)),
                  _const_spec((D_MODEL, D_FF)), _const_spec((D_FF, D_MODEL))],
        out_specs=tile,
        out_shape=jax.ShapeDtypeStruct((rows, D_MODEL), F32),
        compiler_params=_params(1),
        name="swiglu_ffn",
    )(x2d, mod3, g2, w_gate, w_up, w_down)


def _rope_tables(seq_len):
    quarter = HEAD_DIM // 4
    t = jnp.arange(seq_len)
    row = (t // GRID_W).astype(F32)
    col = (t % GRID_W).astype(F32)
    freqs = ROPE_BASE ** (-jnp.arange(quarter, dtype=F32) / quarter)
    ang_r = row[:, None] * freqs
    ang_c = col[:, None] * freqs
    zeros = jnp.zeros_like(ang_r)
    cos = jnp.concatenate([jnp.cos(ang_r)] * 2 + [jnp.cos(ang_c)] * 2, axis=-1)
    sin_next = jnp.concatenate([-jnp.sin(ang_r), zeros, -jnp.sin(ang_c), zeros], axis=-1)
    sin_prev = jnp.concatenate([zeros, jnp.sin(ang_r), zeros, jnp.sin(ang_c)], axis=-1)
    rep = LANES // HEAD_DIM
    nat = tuple(jnp.tile(a, (1, rep)) for a in (cos, sin_next, sin_prev))
    featm = (cos.T, (sin_next + sin_prev).T)
    return nat, featm


def _group_sum_matrix():
    idx = np.arange(CHUNK) // HEAD_DIM
    return jnp.asarray(idx[:, None] == idx[None, :], dtype=BF16)


def kernel(x_prompt, x_sample, cache_diff_k, cache_diff_v, cache_win_k, cache_win_v, c, c_ctx,
           w_ada, b_ada, norm1_g, w_in, qn_a, kn_a, lambda_q1, lambda_k1, lambda_q2, lambda_k2,
           subln_g, qn_b, kn_b, sink, w_oa, w_ob, w_out, norm2_g, w_gate, w_up, w_down):
    batch, seq, _ = x_prompt.shape
    dec_batch, dec_seq, _ = x_sample.shape
    past = cache_diff_k.shape[2]
    l = 0

    cc = jnp.zeros((8, D_MODEL), F32).at[:dec_batch].set(c).at[dec_batch].set(c_ctx)
    mod3 = _modulation(cc, w_ada[l], b_ada[l][None, :]).reshape(8, 1, N_MOD * D_MODEL)

    w_in_b = w_in[l].astype(BF16)
    o_ka, o_va, o_qb, o_kb, o_vb, o_g = A_W, 2 * A_W, 3 * A_W, 3 * A_W + B_Q, 3 * A_W + B_Q + B_KVW, \
        3 * A_W + B_Q + 2 * B_KVW
    w_k = jnp.concatenate([w_in_b[:, o_ka:o_va], w_in_b[:, o_kb:o_vb]], axis=1)
    w_t = jnp.concatenate([w_in_b[:, :o_ka], w_in_b[:, o_qb:o_kb], w_in_b[:, o_va:o_qb],
                           w_in_b[:, o_vb:o_g]], axis=1).T
    w_g = w_in_b[:, o_g:]
    w_oa_b, w_ob_b, w_out_b = w_oa[l].astype(BF16), w_ob[l].astype(BF16), w_out[l].astype(BF16)
    w_gate_b, w_up_b, w_down_b = w_gate[l].astype(BF16), w_up[l].astype(BF16), w_down[l].astype(BF16)
    g1 = norm1_g[l][None, :]
    g2 = norm2_g[l][None, :]
    rep = CHUNK // HEAD_DIM
    kgains = jnp.stack([jnp.tile(kn_a[l], rep), jnp.tile(kn_b[l], rep)])
    qgains = jnp.stack([qn_a[l], qn_b[l]])[:, :, None]
    lam_vecs = jnp.stack([lambda_q1[l], lambda_k1[l], lambda_q2[l], lambda_k2[l]])
    subg_col = subln_g[l][:, None]
    sink_rows = jnp.broadcast_to((sink[l] * LOG2E)[:, None], (B_HEADS, LANES))
    gsum = _group_sum_matrix()
    tm = 512

    xp = x_prompt.reshape(batch * seq, D_MODEL)
    ctx_row = lambda i: dec_batch
    qat, qbt, ka, kb, vat, vbt, kat_f, va_f, kbt_f = _qkv_proj(
        x_prompt, mod3, ctx_row, g1, w_k, w_t, gsum, kgains, qgains, None, seq, True)
    oa, ob = _ctx_attn(qat, qbt, ka, kb, vat, vbt, lam_vecs, subg_col, sink_rows)
    xp1 = _merge(xp, oa, ob, mod3, ctx_row, g1, w_g, w_oa_b, w_ob_b, w_out_b, tm)
    y_prompt = _ffn(xp1, mod3, ctx_row, g2, w_gate_b, w_up_b, w_down_b, tm)
    new_diff_k = kat_f.reshape(batch, A_HEADS, 2, HEAD_DIM, seq).transpose(0, 4, 1, 2, 3)[:, None]
    new_diff_v = va_f.reshape(batch, 1, seq, A_HEADS, 2 * HEAD_DIM)
    new_win_k = kbt_f.reshape(batch, B_KV_HEADS, HEAD_DIM, seq).transpose(0, 3, 1, 2)[:, None]
    new_win_v = vbt.reshape(batch, B_KV_HEADS, HEAD_DIM, seq).transpose(0, 3, 1, 2)[:, None]

    xs = x_sample.reshape(dec_batch * dec_seq, D_MODEL)
    lat_row = lambda i: i // (dec_seq // tm)
    qat, qbt, ka, kb, vat, vbt = _qkv_proj(
        x_sample, mod3, lat_row, g1, w_k, w_t, gsum, kgains, qgains, _rope_tables(dec_seq), tm, False)
    ckt_a = cache_diff_k[:, l].transpose(0, 2, 3, 4, 1).reshape(dec_batch, A_W, past)
    cv_a = cache_diff_v[:, l].reshape(dec_batch, past, A_W)
    ckt_b = cache_win_k[:, l].transpose(0, 2, 3, 1).reshape(dec_batch, B_KVW, past)
    cvt_b = cache_win_v[:, l].transpose(0, 2, 3, 1).reshape(dec_batch, B_KVW, past)
    oa = _diff_lat(qat, ka.reshape(dec_batch, dec_seq, A_W), vat, ckt_a, cv_a, lam_vecs, subg_col, 512)
    ob = _win_lat(qbt, kb.reshape(dec_batch, dec_seq, B_KVW), vbt, ckt_b, cvt_b, sink_rows)
    xs1 = _merge(xs, oa.reshape(-1, A_W), ob.reshape(-1, B_Q), mod3, lat_row, g1,
                 w_g, w_oa_b, w_ob_b, w_out_b, tm)
    y_sample = _ffn(xs1, mod3, lat_row, g2, w_gate_b, w_up_b, w_down_b, tm)

    return (y_prompt.reshape(batch, seq, D_MODEL),
            y_sample.reshape(dec_batch, dec_seq, D_MODEL),
            new_diff_k, new_diff_v, new_win_k, new_win_v)
```

```python
import functools
import math

import jax
import jax.numpy as jnp
import numpy as np
from jax import lax
from jax.experimental import pallas as pl
from jax.experimental.pallas import tpu as pltpu

D_MODEL = 1024
HEAD_DIM = 64
A_HEADS = 8
B_HEADS = 16
B_KV_HEADS = 4
B_GROUP = B_HEADS // B_KV_HEADS
A_W = A_HEADS * 2 * HEAD_DIM
B_Q = B_HEADS * HEAD_DIM
B_KVW = B_KV_HEADS * HEAD_DIM
N_K = A_W + B_KVW
N_T = A_W + B_Q + A_W + B_KVW
WINDOW = 128
BLOCK = 128
GRID_W = 64
D_FF = 2816
N_MOD = 6
EPS = 1e-6
ROPE_BASE = 10000.0
NEG = -1e30
LAMBDA_INIT = 0.8 - 0.6 * math.exp(-0.3 * 0)
LOG2E = math.log2(math.e)
Q_SCALE = HEAD_DIM ** -0.5 * LOG2E

LANES = 128
CHUNK = 256
ONES_ROWS = 16
VMEM_LIMIT = 56 * 1024 * 1024

F32 = jnp.float32
BF16 = jnp.bfloat16


def _params(n_axes):
    return pltpu.CompilerParams(dimension_semantics=("arbitrary",) * n_axes,
                                vmem_limit_bytes=VMEM_LIMIT)


def _const_spec(shape):
    nd = len(shape)
    return pl.BlockSpec(shape, lambda *_: (0,) * nd, pipeline_mode=pl.Buffered(1))


def _dot(a, b):
    return jnp.dot(a, b, preferred_element_type=F32)


def _modulated_norm(x, gain, scale, shift):
    y = x * lax.rsqrt(jnp.mean(x * x, axis=-1, keepdims=True) + EPS) * gain
    return y * (1.0 + scale) + shift


def _software_pipeline(n, stages):
    depth = len(stages)
    state = [None] * n
    for step in range(n + depth - 1):
        for k, stage in enumerate(stages):
            i = step - k
            if 0 <= i < n:
                state[i] = stage(i, state[i])


def _mod_kernel(c_ref, w_ref, b_ref, o_ref):
    cc = c_ref[...]
    s = cc * jax.nn.sigmoid(cc)
    o_ref[...] = _dot(s.astype(BF16), w_ref[...].astype(BF16)) + b_ref[...]


def _modulation(cc, w_ada, b_ada):
    rows, n = cc.shape[0], w_ada.shape[1]
    tn = n // 4
    return pl.pallas_call(
        _mod_kernel,
        grid=(n // tn,),
        in_specs=[pl.BlockSpec((rows, D_MODEL), lambda j: (0, 0)),
                  pl.BlockSpec((D_MODEL, tn), lambda j: (0, j)),
                  pl.BlockSpec((1, tn), lambda j: (0, j))],
        out_specs=pl.BlockSpec((rows, tn), lambda j: (0, j)),
        out_shape=jax.ShapeDtypeStruct((rows, n), F32),
        compiler_params=_params(1),
        name="adaln_mod",
    )(cc, w_ada, b_ada)


def _qkv_kernel(*refs, rope, emit_new_kv):
    x_ref, mod_ref, g1_ref, wk_ref, wt_ref, gsum_ref, kgain_ref, qgain_ref = refs[:8]
    refs = refs[8:]
    if rope:
        cos_ref, sa_ref, sb_ref, cost_ref, sint_ref = refs[:5]
        refs = refs[5:]
    qat_ref, qbt_ref, ka_ref, kb_ref, vat_ref, vbt_ref = refs[:6]
    if emit_new_kv:
        kat_ref, va_ref, kbt_ref = refs[6:]

    mod = mod_ref[...]
    h = _modulated_norm(x_ref[...], g1_ref[...], mod[:, D_MODEL:2 * D_MODEL], mod[:, 0:D_MODEL])
    hb = h.astype(BF16)
    gsum = gsum_ref[...]

    n_wide = A_W // CHUNK
    key_cols = [j * CHUNK for j in range(n_wide)] + [A_W]

    def key_project(j, _):
        return _dot(hb, wk_ref[:, key_cols[j]:key_cols[j] + CHUNK])

    def key_squares(j, p):
        return p, _dot((p * p).astype(BF16), gsum)

    def key_finish(j, st):
        p, ss = st
        row = 0 if j < n_wide else 1
        p = p * lax.rsqrt(ss * (1.0 / HEAD_DIM) + EPS) * kgain_ref[row:row + 1, :]
        if rope:
            cos, sa, sb = cos_ref[...], sa_ref[...], sb_ref[...]
            halves = []
            for i in range(CHUNK // LANES):
                xh = p[:, i * LANES:(i + 1) * LANES]
                nxt = pltpu.roll(xh, LANES - HEAD_DIM // 4, 1)
                prv = pltpu.roll(xh, HEAD_DIM // 4, 1)
                halves.append(xh * cos + nxt * sa + prv * sb)
            p = jnp.concatenate(halves, axis=1)
        if j < n_wide:
            sl = slice(j * CHUNK, (j + 1) * CHUNK)
            ka_ref[:, sl] = p.astype(BF16)
            if emit_new_kv:
                kat_ref[sl, :] = p.T
        else:
            kb_ref[...] = p.astype(BF16)
            if emit_new_kv:
                kbt_ref[...] = p.T

    _software_pipeline(len(key_cols), [key_project, key_squares, key_finish])

    ht = h.T.astype(BF16)
    n_feat = N_T // CHUNK

    def feat_project(c, _):
        return _dot(wt_ref[c * CHUNK:(c + 1) * CHUNK, :], ht)

    def feat_finish(c, p):
        kind, j = divmod(c, n_wide)
        sl = slice(j * CHUNK, (j + 1) * CHUNK)
        if kind == 2:
            vat_ref[sl, :] = p.astype(BF16)
            if emit_new_kv:
                va_ref[:, sl] = p.T
            return
        if kind == 3:
            vbt_ref[...] = p.astype(vbt_ref.dtype)
            return
        gain = qgain_ref[kind]
        q4 = HEAD_DIM // 4
        heads = []
        for r in range(CHUNK // HEAD_DIM):
            x = p[r * HEAD_DIM:(r + 1) * HEAD_DIM, :]
            x = x * lax.rsqrt(jnp.mean(x * x, axis=0, keepdims=True) + EPS) * gain
            if rope:
                swapped = jnp.concatenate([x[q4:2 * q4], x[0:q4], x[3 * q4:4 * q4], x[2 * q4:3 * q4]],
                                          axis=0)
                x = x * cost_ref[...] + swapped * sint_ref[...]
            heads.append(x)
        out = (jnp.concatenate(heads, axis=0) * Q_SCALE).astype(BF16)
        (qat_ref if kind == 0 else qbt_ref)[sl, :] = out

    _software_pipeline(n_feat, [feat_project, feat_finish])


def _qkv_proj(x3d, mod3, mod_row, g1, w_k, w_t, gsum, kgains, qgains, rope_tabs, tm, emit_new_kv):
    b, t, _ = x3d.shape
    rows = b * t
    rope = rope_tabs is not None
    tps = t // tm
    in_specs = [pl.BlockSpec((tm, D_MODEL), lambda i: (i, 0)),
                pl.BlockSpec((None, 1, N_MOD * D_MODEL), lambda i: (mod_row(i), 0, 0)),
                _const_spec((1, D_MODEL)),
                _const_spec((D_MODEL, N_K)),
                _const_spec((N_T, D_MODEL)),
                _const_spec((CHUNK, CHUNK)),
                _const_spec((2, CHUNK)),
                _const_spec((2, HEAD_DIM, 1))]
    args = [x3d.reshape(rows, D_MODEL), mod3, g1, w_k, w_t, gsum, kgains, qgains]
    if rope:
        nat, featm = rope_tabs
        in_specs += [pl.BlockSpec((tm, LANES), lambda i: (i % tps, 0))] * 3
        in_specs += [pl.BlockSpec((HEAD_DIM, tm), lambda i: (0, i % tps))] * 2
        args += list(nat) + list(featm)
    tok_wide = pl.BlockSpec((tm, A_W), lambda i: (i, 0))
    tok_narrow = pl.BlockSpec((tm, B_KVW), lambda i: (i, 0))
    feat_wide = pl.BlockSpec((None, A_W, tm), lambda i: (i // tps, 0, i % tps))
    feat_narrow = pl.BlockSpec((None, B_KVW, tm), lambda i: (i // tps, 0, i % tps))
    out_specs = [feat_wide, feat_wide, tok_wide, tok_narrow, feat_wide, feat_narrow]
    out_shape = [jax.ShapeDtypeStruct((b, A_W, t), BF16),
                 jax.ShapeDtypeStruct((b, B_Q, t), BF16),
                 jax.ShapeDtypeStruct((rows, A_W), BF16),
                 jax.ShapeDtypeStruct((rows, B_KVW), BF16),
                 jax.ShapeDtypeStruct((b, A_W, t), BF16),
                 jax.ShapeDtypeStruct((b, B_KVW, t), F32 if emit_new_kv else BF16)]
    if emit_new_kv:
        out_specs += [feat_wide, tok_wide, feat_narrow]
        out_shape += [jax.ShapeDtypeStruct((b, A_W, t), F32),
                      jax.ShapeDtypeStruct((rows, A_W), F32),
                      jax.ShapeDtypeStruct((b, B_KVW, t), F32)]
    return pl.pallas_call(
        functools.partial(_qkv_kernel, rope=rope, emit_new_kv=emit_new_kv),
        grid=(rows // tm,),
        in_specs=in_specs,
        out_specs=out_specs,
        out_shape=out_shape,
        compiler_params=_params(1),
        name="qkv_rope" if rope else "qkv_ctx",
    )(*args)


def _lambda(lam_ref):
    lv = lam_ref[...]
    t1 = jnp.sum(lv[0:1] * lv[1:2], axis=-1, keepdims=True)
    t2 = jnp.sum(lv[2:3] * lv[3:4], axis=-1, keepdims=True)
    return jnp.exp(t1) - jnp.exp(t2) + LAMBDA_INIT


def _with_ones(vt):
    return jnp.concatenate([vt, jnp.ones((ONES_ROWS, vt.shape[1]), vt.dtype)], axis=0)


def _column_softmax(scores, extra=None):
    m = functools.reduce(jnp.maximum, [jnp.max(s, axis=0, keepdims=True) for s in scores])
    if extra is not None:
        m = jnp.maximum(m, extra)
    return [jnp.exp2(s - m).astype(BF16) for s in scores], m


def _diff_query_cols(qt):
    zero = jnp.zeros((HEAD_DIM, qt.shape[1]), qt.dtype)
    return jnp.concatenate([jnp.concatenate([qt[:HEAD_DIM], zero], axis=0),
                            jnp.concatenate([zero, qt[HEAD_DIM:]], axis=0)], axis=1)


def _diff_output(r, lam, subg_col):
    n = r.shape[1] // 2
    dv = 2 * HEAD_DIM
    tot = r[dv:dv + 1, :]
    o = r[:dv, :n] * (1.0 / tot[:, :n]) - r[:dv, n:] * (lam / tot[:, n:])
    o = o * lax.rsqrt(jnp.mean(o * o, axis=0, keepdims=True) + EPS)
    return (o * subg_col * (1.0 - LAMBDA_INIT)).T


def _group_output(r, m, sink_row):
    n = r.shape[1] // B_GROUP
    tot = r[HEAD_DIM:HEAD_DIM + 1, :] + jnp.exp2(sink_row - m)
    o = r[:HEAD_DIM, :] * (1.0 / tot)
    return jnp.concatenate([o[:, i * n:(i + 1) * n] for i in range(B_GROUP)], axis=0).T


def _sink_row(sink_ref, g, n):
    reps = n // LANES
    return jnp.concatenate([sink_ref[pl.ds(g * B_GROUP + r, 1), :] for r in range(B_GROUP)
                            for _ in range(reps)], axis=1)


def _ctx_attn_kernel(qat_ref, qbt_ref, ka_ref, kb_ref, vat_ref, vbt_ref, lam_ref, subg_ref, sink_ref,
                     oa_ref, ob_ref):
    lam = _lambda(lam_ref)
    subg_col = subg_ref[...]
    n = ka_ref.shape[0]
    kb = kb_ref[...]

    def scores(c, _):
        if c < A_HEADS:
            sl = slice(c * LANES, (c + 1) * LANES)
            return _dot(ka_ref[:, sl], _diff_query_cols(qat_ref[sl, :]))
        g = c - A_HEADS
        qs = jnp.concatenate([qbt_ref[(g * B_GROUP + r) * HEAD_DIM:(g * B_GROUP + r + 1) * HEAD_DIM, :]
                              for r in range(B_GROUP)], axis=1)
        zero = jnp.zeros_like(qs)
        w = jnp.concatenate([qs if i == g else zero for i in range(B_KV_HEADS)], axis=0)
        return _dot(kb, w)

    def probs(c, s):
        if c < A_HEADS:
            return _column_softmax([s])
        return _column_softmax([s], extra=_sink_row(sink_ref, c - A_HEADS, n))

    def outputs(c, st):
        es, m = st
        if c < A_HEADS:
            sl = slice(c * LANES, (c + 1) * LANES)
            r = _dot(_with_ones(vat_ref[sl, :]), es[0])
            oa_ref[:, sl] = _diff_output(r, lam, subg_col).astype(BF16)
        else:
            g = c - A_HEADS
            vt = vbt_ref[g * HEAD_DIM:(g + 1) * HEAD_DIM, :].astype(BF16)
            r = _dot(_with_ones(vt), es[0])
            o = _group_output(r, m, _sink_row(sink_ref, g, n))
            ob_ref[:, g * CHUNK:(g + 1) * CHUNK] = o.astype(BF16)

    _software_pipeline(A_HEADS + B_KV_HEADS, [scores, probs, outputs])


def _ctx_attn(qat, qbt, ka, kb, vat, vbt, lam_vecs, subg_col, sink_rows):
    b, _, t = qat.shape
    feat = pl.BlockSpec((None, A_W, t), lambda i: (i, 0, 0))
    tok = pl.BlockSpec((t, A_W), lambda i: (i, 0))
    return pl.pallas_call(
        _ctx_attn_kernel,
        grid=(b,),
        in_specs=[feat, feat, tok, pl.BlockSpec((t, B_KVW), lambda i: (i, 0)), feat,
                  pl.BlockSpec((None, B_KVW, t), lambda i: (i, 0, 0)),
                  _const_spec((4, HEAD_DIM)), _const_spec((2 * HEAD_DIM, 1)),
                  _const_spec((B_HEADS, LANES))],
        out_specs=[tok, tok],
        out_shape=[jax.ShapeDtypeStruct((b * t, A_W), BF16), jax.ShapeDtypeStruct((b * t, B_Q), BF16)],
        compiler_params=_params(1),
        name="ctx_attn",
    )(qat, qbt, ka, kb, vat, vbt, lam_vecs, subg_col, sink_rows)


def _diff_lat_kernel(qt_ref, k_ref, vt_ref, ckt_ref, cv_ref, lam_ref, subg_ref, o_ref):
    lam = _lambda(lam_ref)
    subg_col = subg_ref[...]
    k = k_ref[...]
    ck = ckt_ref[...].T.astype(BF16)
    vt = _with_ones(vt_ref[...])
    past = cv_ref.shape[0] // A_HEADS
    cv = cv_ref[pl.ds(pl.program_id(1), past, stride=A_HEADS), :]
    cvt = _with_ones(cv.T.astype(BF16))

    def scores(i, _):
        q = _diff_query_cols(qt_ref[:, i * BLOCK:(i + 1) * BLOCK])
        return [_dot(k, q), _dot(ck, q)]

    def probs(i, s):
        return _column_softmax(s)[0]

    def outputs(i, es):
        r = _dot(vt, es[0]) + _dot(cvt, es[1])
        o_ref[i * BLOCK:(i + 1) * BLOCK, :] = _diff_output(r, lam, subg_col).astype(BF16)

    _software_pipeline(qt_ref.shape[1] // BLOCK, [scores, probs, outputs])


def _diff_lat(qt, k, vt, ckt, cv, lam_vecs, subg_col, tq):
    b, _, t = qt.shape
    past = ckt.shape[2]
    return pl.pallas_call(
        _diff_lat_kernel,
        grid=(b, A_HEADS, t // tq),
        in_specs=[pl.BlockSpec((None, LANES, tq), lambda i, h, j: (i, h, j)),
                  pl.BlockSpec((None, t, LANES), lambda i, h, j: (i, 0, h)),
                  pl.BlockSpec((None, LANES, t), lambda i, h, j: (i, h, 0)),
                  pl.BlockSpec((None, LANES, past), lambda i, h, j: (i, h, 0)),
                  pl.BlockSpec((None, past * A_HEADS, LANES), lambda i, h, j: (i, 0, 0)),
                  _const_spec((4, HEAD_DIM)), _const_spec((2 * HEAD_DIM, 1))],
        out_specs=pl.BlockSpec((None, tq, LANES), lambda i, h, j: (i, j, h)),
        out_shape=jax.ShapeDtypeStruct((b, t, A_W), BF16),
        compiler_params=_params(3),
        name="diff_lat",
    )(qt, k, vt, ckt, cv, lam_vecs, subg_col)


def _win_lat_kernel(qt_ref, k_ref, vt_ref, ckt_ref, cvt_ref, sink_ref, o_ref):
    g = pl.program_id(1)
    t = k_ref.shape[0]
    span = 3 * BLOCK
    keep = lax.broadcasted_iota(jnp.int32, (1, CHUNK), 1) // HEAD_DIM == g
    k = jnp.where(keep, k_ref[...].astype(F32), 0.0).astype(BF16)
    ck = jnp.where(keep, ckt_ref[...].T, 0.0).astype(BF16)
    vt = vt_ref[...]
    cvt = _with_ones(cvt_ref[...].astype(BF16))
    sink_row = _sink_row(sink_ref, g, BLOCK)
    rel = (lax.broadcasted_iota(jnp.int32, (span, B_GROUP * BLOCK), 0)
           - lax.broadcasted_iota(jnp.int32, (span, B_GROUP * BLOCK), 1) % BLOCK)

    def window(i):
        return min(max((i - 1) * BLOCK, 0), t - span)

    bands = {}

    def band(offset):
        if offset not in bands:
            bands[offset] = jnp.abs(rel + offset) <= WINDOW
        return bands[offset]

    def scores(i, _):
        w = window(i)
        qs = jnp.concatenate([qt_ref[r * HEAD_DIM:(r + 1) * HEAD_DIM, i * BLOCK:(i + 1) * BLOCK]
                              for r in range(B_GROUP)], axis=1)
        wq = jnp.concatenate([qs] * B_KV_HEADS, axis=0)
        return [jnp.where(band(w - i * BLOCK), _dot(k[w:w + span], wq), NEG), _dot(ck, wq)]

    def probs(i, s):
        return _column_softmax(s, extra=sink_row)

    def outputs(i, st):
        es, m = st
        w = window(i)
        r = _dot(_with_ones(vt[:, w:w + span]), es[0]) + _dot(cvt, es[1])
        o_ref[i * BLOCK:(i + 1) * BLOCK, :] = _group_output(r, m, sink_row).astype(BF16)

    _software_pipeline(t // BLOCK, [scores, probs, outputs])


def _win_lat(qt, k, vt, ckt, cvt, sink_rows):
    b, _, t = qt.shape
    past = ckt.shape[2]
    return pl.pallas_call(
        _win_lat_kernel,
        grid=(b, B_KV_HEADS),
        in_specs=[pl.BlockSpec((None, CHUNK, t), lambda i, g: (i, g, 0)),
                  pl.BlockSpec((None, t, B_KVW), lambda i, g: (i, 0, 0)),
                  pl.BlockSpec((None, HEAD_DIM, t), lambda i, g: (i, g, 0)),
                  pl.BlockSpec((None, B_KVW, past), lambda i, g: (i, 0, 0)),
                  pl.BlockSpec((None, HEAD_DIM, past), lambda i, g: (i, g, 0)),
                  _const_spec((B_HEADS, LANES))],
        out_specs=pl.BlockSpec((None, t, CHUNK), lambda i, g: (i, 0, g)),
        out_shape=jax.ShapeDtypeStruct((b, t, B_Q), BF16),
        compiler_params=_params(2),
        name="win_lat",
    )(qt, k, vt, ckt, cvt, sink_rows)


def _merge_kernel(x_ref, oa_ref, ob_ref, mod_ref, g1_ref, wg_ref, woa_ref, wob_ref, wout_ref, o_ref):
    x = x_ref[...]
    mod = mod_ref[...]
    h = _modulated_norm(x, g1_ref[...], mod[:, D_MODEL:2 * D_MODEL], mod[:, 0:D_MODEL]).astype(BF16)
    ga = _dot(h, wg_ref[:, 0:D_MODEL])
    gb = _dot(h, wg_ref[:, D_MODEL:2 * D_MODEL])
    merged = (jax.nn.sigmoid(ga) * _dot(oa_ref[...], woa_ref[...])
              + jax.nn.sigmoid(gb) * _dot(ob_ref[...], wob_ref[...]))
    o_ref[...] = x + mod[:, 2 * D_MODEL:3 * D_MODEL] * _dot(merged.astype(BF16), wout_ref[...])


def _merge(x2d, oa, ob, mod3, mod_row, g1, w_g, w_oa, w_ob, w_out, tm):
    rows = x2d.shape[0]
    tile = pl.BlockSpec((tm, D_MODEL), lambda i: (i, 0))
    sq = _const_spec((D_MODEL, D_MODEL))
    return pl.pallas_call(
        _merge_kernel,
        grid=(rows // tm,),
        in_specs=[tile, tile, tile,
                  pl.BlockSpec((None, 1, N_MOD * D_MODEL), lambda i: (mod_row(i), 0, 0)),
                  _const_spec((1, D_MODEL)), _const_spec((D_MODEL, 2 * D_MODEL)), sq, sq, sq],
        out_specs=tile,
        out_shape=jax.ShapeDtypeStruct((rows, D_MODEL), F32),
        compiler_params=_params(1),
        name="merge_out",
    )(x2d, oa, ob, mod3, g1, w_g, w_oa, w_ob, w_out)


def _ffn_kernel(x_ref, mod_ref, g2_ref, wgate_ref, wup_ref, wdown_ref, o_ref):
    x = x_ref[...]
    mod = mod_ref[...]
    h = _modulated_norm(x, g2_ref[...], mod[:, 4 * D_MODEL:5 * D_MODEL],
                        mod[:, 3 * D_MODEL:4 * D_MODEL]).astype(BF16)
    gate = _dot(h, wgate_ref[...])
    act = (gate * jax.nn.sigmoid(gate) * _dot(h, wup_ref[...])).astype(BF16)
    o_ref[...] = x + mod[:, 5 * D_MODEL:6 * D_MODEL] * _dot(act, wdown_ref[...])


def _ffn(x2d, mod3, mod_row, g2, w_gate, w_up, w_down, tm):
    rows = x2d.shape[0]
    tile = pl.BlockSpec((tm, D_MODEL), lambda i: (i, 0))
    return pl.pallas_call(
        _ffn_kernel,
        grid=(rows // tm,),
        in_specs=[tile,
                  pl.BlockSpec((None, 1, N_MOD * D_MODEL), lambda i: (mod_row(i), 0, 0)),
                  _const_spec((1, D_MODEL)), _const_spec((D_MODEL, D_FF)),
                  _const_spec((D_MODEL, D_FF)), _const_spec((D_FF, D_MODEL))],
        out_specs=tile,
        out_shape=jax.ShapeDtypeStruct((rows, D_MODEL), F32),
        compiler_params=_params(1),
        name="swiglu_ffn",
    )(x2d, mod3, g2, w_gate, w_up, w_down)


def _rope_tables(seq_len):
    quarter = HEAD_DIM // 4
    t = jnp.arange(seq_len)
    row = (t // GRID_W).astype(F32)
    col = (t % GRID_W).astype(F32)
    freqs = ROPE_BASE ** (-jnp.arange(quarter, dtype=F32) / quarter)
    ang_r = row[:, None] * freqs
    ang_c = col[:, None] * freqs
    zeros = jnp.zeros_like(ang_r)
    cos = jnp.concatenate([jnp.cos(ang_r)] * 2 + [jnp.cos(ang_c)] * 2, axis=-1)
    sin_next = jnp.concatenate([-jnp.sin(ang_r), zeros, -jnp.sin(ang_c), zeros], axis=-1)
    sin_prev = jnp.concatenate([zeros, jnp.sin(ang_r), zeros, jnp.sin(ang_c)], axis=-1)
    rep = LANES // HEAD_DIM
    nat = tuple(jnp.tile(a, (1, rep)) for a in (cos, sin_next, sin_prev))
    featm = (cos.T, (sin_next + sin_prev).T)
    return nat, featm


def _group_sum_matrix():
    idx = np.arange(CHUNK) // HEAD_DIM
    return jnp.asarray(idx[:, None] == idx[None, :], dtype=BF16)


def kernel(x_prompt, x_sample, cache_diff_k, cache_diff_v, cache_win_k, cache_win_v, c, c_ctx,
           w_ada, b_ada, norm1_g, w_in, qn_a, kn_a, lambda_q1, lambda_k1, lambda_q2, lambda_k2,
           subln_g, qn_b, kn_b, sink, w_oa, w_ob, w_out, norm2_g, w_gate, w_up, w_down):
    batch, seq, _ = x_prompt.shape
    dec_batch, dec_seq, _ = x_sample.shape
    past = cache_diff_k.shape[2]
    l = 0

    cc = jnp.zeros((8, D_MODEL), F32).at[:dec_batch].set(c).at[dec_batch].set(c_ctx)
    mod3 = _modulation(cc, w_ada[l], b_ada[l][None, :]).reshape(8, 1, N_MOD * D_MODEL)

    w_in_b = w_in[l].astype(BF16)
    o_ka, o_va, o_qb, o_kb, o_vb, o_g = A_W, 2 * A_W, 3 * A_W, 3 * A_W + B_Q, 3 * A_W + B_Q + B_KVW, \
        3 * A_W + B_Q + 2 * B_KVW
    w_k = jnp.concatenate([w_in_b[:, o_ka:o_va], w_in_b[:, o_kb:o_vb]], axis=1)
    w_t = jnp.concatenate([w_in_b[:, :o_ka], w_in_b[:, o_qb:o_kb], w_in_b[:, o_va:o_qb],
                           w_in_b[:, o_vb:o_g]], axis=1).T
    w_g = w_in_b[:, o_g:]
    w_oa_b, w_ob_b, w_out_b = w_oa[l].astype(BF16), w_ob[l].astype(BF16), w_out[l].astype(BF16)
    w_gate_b, w_up_b, w_down_b = w_gate[l].astype(BF16), w_up[l].astype(BF16), w_down[l].astype(BF16)
    g1 = norm1_g[l][None, :]
    g2 = norm2_g[l][None, :]
    rep = CHUNK // HEAD_DIM
    kgains = jnp.stack([jnp.tile(kn_a[l], rep), jnp.tile(kn_b[l], rep)])
    qgains = jnp.stack([qn_a[l], qn_b[l]])[:, :, None]
    lam_vecs = jnp.stack([lambda_q1[l], lambda_k1[l], lambda_q2[l], lambda_k2[l]])
    subg_col = subln_g[l][:, None]
    sink_rows = jnp.broadcast_to((sink[l] * LOG2E)[:, None], (B_HEADS, LANES))
    gsum = _group_sum_matrix()
    tm = 512

    xp = x_prompt.reshape(batch * seq, D_MODEL)
    ctx_row = lambda i: dec_batch
    qat, qbt, ka, kb, vat, vbt, kat_f, va_f, kbt_f = _qkv_proj(
        x_prompt, mod3, ctx_row, g1, w_k, w_t, gsum, kgains, qgains, None, seq, True)
    oa, ob = _ctx_attn(qat, qbt, ka, kb, vat, vbt, lam_vecs, subg_col, sink_rows)
    xp1 = _merge(xp, oa, ob, mod3, ctx_row, g1, w_g, w_oa_b, w_ob_b, w_out_b, tm)
    y_prompt = _ffn(xp1, mod3, ctx_row, g2, w_gate_b, w_up_b, w_down_b, tm)
    new_diff_k = kat_f.reshape(batch, A_HEADS, 2, HEAD_DIM, seq).transpose(0, 4, 1, 2, 3)[:, None]
    new_diff_v = va_f.reshape(batch, 1, seq, A_HEADS, 2 * HEAD_DIM)
    new_win_k = kbt_f.reshape(batch, B_KV_HEADS, HEAD_DIM, seq).transpose(0, 3, 1, 2)[:, None]
    new_win_v = vbt.reshape(batch, B_KV_HEADS, HEAD_DIM, seq).transpose(0, 3, 1, 2)[:, None]

    xs = x_sample.reshape(dec_batch * dec_seq, D_MODEL)
    lat_row = lambda i: i // (dec_seq // tm)
    qat, qbt, ka, kb, vat, vbt = _qkv_proj(
        x_sample, mod3, lat_row, g1, w_k, w_t, gsum, kgains, qgains, _rope_tables(dec_seq), tm, False)
    ckt_a = cache_diff_k[:, l].transpose(0, 2, 3, 4, 1).reshape(dec_batch, A_W, past)
    cv_a = cache_diff_v[:, l].reshape(dec_batch, past * A_HEADS, 2 * HEAD_DIM)
    ckt_b = cache_win_k[:, l].transpose(0, 2, 3, 1).reshape(dec_batch, B_KVW, past)
    cvt_b = cache_win_v[:, l].transpose(0, 2, 3, 1).reshape(dec_batch, B_KVW, past)
    oa = _diff_lat(qat, ka.reshape(dec_batch, dec_seq, A_W), vat, ckt_a, cv_a, lam_vecs, subg_col, dec_seq)
    ob = _win_lat(qbt, kb.reshape(dec_batch, dec_seq, B_KVW), vbt, ckt_b, cvt_b, sink_rows)
    xs1 = _merge(xs, oa.reshape(-1, A_W), ob.reshape(-1, B_Q), mod3, lat_row, g1,
                 w_g, w_oa_b, w_ob_b, w_out_b, tm)
    y_sample = _ffn(xs1, mod3, lat_row, g2, w_gate_b, w_up_b, w_down_b, tm)

    return (y_prompt.reshape(batch, seq, D_MODEL),
            y_sample.reshape(dec_batch, dec_seq, D_MODEL),
            new_diff_k, new_diff_v, new_win_k, new_win_v)
```

```python
import functools
import math

import jax
import jax.numpy as jnp
import numpy as np
from jax import lax
from jax.experimental import pallas as pl
from jax.experimental.pallas import tpu as pltpu

D_MODEL = 1024
HEAD_DIM = 64
A_HEADS = 8
B_HEADS = 16
B_KV_HEADS = 4
B_GROUP = B_HEADS // B_KV_HEADS
A_W = A_HEADS * 2 * HEAD_DIM
B_Q = B_HEADS * HEAD_DIM
B_KVW = B_KV_HEADS * HEAD_DIM
N_K = A_W + B_KVW
N_T = A_W + B_Q + A_W + B_KVW
WINDOW = 128
BLOCK = 128
GRID_W = 64
D_FF = 2816
N_MOD = 6
EPS = 1e-6
ROPE_BASE = 10000.0
NEG = -1e30
LAMBDA_INIT = 0.8 - 0.6 * math.exp(-0.3 * 0)
LOG2E = math.log2(math.e)
Q_SCALE = HEAD_DIM ** -0.5 * LOG2E

LANES = 128
CHUNK = 256
ONES_ROWS = 16
KEY_CHUNK = 512
VMEM_LIMIT = 56 * 1024 * 1024

F32 = jnp.float32
BF16 = jnp.bfloat16


def _params(n_axes, flags=None):
    return pltpu.CompilerParams(dimension_semantics=("arbitrary",) * n_axes,
                                vmem_limit_bytes=VMEM_LIMIT, flags=flags)


def _const_spec(shape):
    nd = len(shape)
    return pl.BlockSpec(shape, lambda *_: (0,) * nd, pipeline_mode=pl.Buffered(1))


def _dot(a, b):
    return jnp.dot(a, b, preferred_element_type=F32)


def _modulated_norm(x, gain, scale, shift):
    y = x * lax.rsqrt(jnp.mean(x * x, axis=-1, keepdims=True) + EPS) * gain
    return y * (1.0 + scale) + shift


def _software_pipeline(n, stages):
    depth = len(stages)
    state = [None] * n
    for step in range(n + depth - 1):
        for k, stage in enumerate(stages):
            i = step - k
            if 0 <= i < n:
                state[i] = stage(i, state[i])


def _mod_kernel(c_ref, w_ref, b_ref, o_ref):
    cc = c_ref[...]
    s = cc * jax.nn.sigmoid(cc)
    o_ref[...] = _dot(s.astype(BF16), w_ref[...].astype(BF16)) + b_ref[...]


def _modulation(cc, w_ada, b_ada):
    rows, n = cc.shape[0], w_ada.shape[1]
    tn = n // 4
    return pl.pallas_call(
        _mod_kernel,
        grid=(n // tn,),
        in_specs=[pl.BlockSpec((rows, D_MODEL), lambda j: (0, 0)),
                  pl.BlockSpec((D_MODEL, tn), lambda j: (0, j)),
                  pl.BlockSpec((1, tn), lambda j: (0, j))],
        out_specs=pl.BlockSpec((rows, tn), lambda j: (0, j)),
        out_shape=jax.ShapeDtypeStruct((rows, n), F32),
        compiler_params=_params(1),
        name="adaln_mod",
    )(cc, w_ada, b_ada)


def _qkv_kernel(*refs, rope, emit_new_kv):
    x_ref, mod_ref, g1_ref, wk_ref, wt_ref, gsum_ref, kgain_ref, qgain_ref = refs[:8]
    refs = refs[8:]
    if rope:
        cos_ref, sa_ref, sb_ref, cost_ref, sint_ref = refs[:5]
        refs = refs[5:]
    qat_ref, qbt_ref, ka_ref, kb_ref, vat_ref, vbt_ref = refs[:6]
    if emit_new_kv:
        kat_ref, va_ref, kbt_ref = refs[6:]

    mod = mod_ref[...]
    h = _modulated_norm(x_ref[...], g1_ref[...], mod[:, D_MODEL:2 * D_MODEL], mod[:, 0:D_MODEL])
    hb = h.astype(BF16)
    gsum = gsum_ref[...]

    def put_feat(ref, rows, val):
        n_seq, _, width = ref.shape
        for s in range(n_seq):
            ref[s, rows, :] = val[:, s * width:(s + 1) * width].astype(ref.dtype)

    n_wide = A_W // CHUNK
    key_cols = [j * CHUNK for j in range(n_wide)] + [A_W]

    def key_project(j, _):
        return _dot(hb, wk_ref[:, key_cols[j]:key_cols[j] + CHUNK])

    def key_squares(j, p):
        return p, _dot((p * p).astype(BF16), gsum)

    def key_finish(j, st):
        p, ss = st
        row = 0 if j < n_wide else 1
        p = p * lax.rsqrt(ss * (1.0 / HEAD_DIM) + EPS) * kgain_ref[row:row + 1, :]
        if rope:
            cos, sa, sb = cos_ref[...], sa_ref[...], sb_ref[...]
            halves = []
            for i in range(CHUNK // LANES):
                xh = p[:, i * LANES:(i + 1) * LANES]
                nxt = pltpu.roll(xh, LANES - HEAD_DIM // 4, 1)
                prv = pltpu.roll(xh, HEAD_DIM // 4, 1)
                halves.append(xh * cos + nxt * sa + prv * sb)
            p = jnp.concatenate(halves, axis=1)
        if j < n_wide:
            sl = slice(j * CHUNK, (j + 1) * CHUNK)
            ka_ref[:, sl] = p.astype(BF16)
            if emit_new_kv:
                put_feat(kat_ref, sl, p.T)
        else:
            kb_ref[...] = p.astype(BF16)
            if emit_new_kv:
                put_feat(kbt_ref, slice(None), p.T)

    _software_pipeline(len(key_cols), [key_project, key_squares, key_finish])

    ht = h.T.astype(BF16)
    n_feat = N_T // CHUNK

    def feat_project(c, _):
        return _dot(wt_ref[c * CHUNK:(c + 1) * CHUNK, :], ht)

    def feat_finish(c, p):
        kind, j = divmod(c, n_wide)
        sl = slice(j * CHUNK, (j + 1) * CHUNK)
        if kind == 2:
            put_feat(vat_ref, sl, p)
            if emit_new_kv:
                va_ref[:, sl] = p.T
            return
        if kind == 3:
            put_feat(vbt_ref, slice(None), p)
            return
        gain = qgain_ref[kind]
        q4 = HEAD_DIM // 4
        heads = []
        for r in range(CHUNK // HEAD_DIM):
            x = p[r * HEAD_DIM:(r + 1) * HEAD_DIM, :]
            x = x * lax.rsqrt(jnp.mean(x * x, axis=0, keepdims=True) + EPS) * gain
            if rope:
                swapped = jnp.concatenate([x[q4:2 * q4], x[0:q4], x[3 * q4:4 * q4], x[2 * q4:3 * q4]],
                                          axis=0)
                x = x * cost_ref[...] + swapped * sint_ref[...]
            heads.append(x)
        put_feat(qat_ref if kind == 0 else qbt_ref, sl, jnp.concatenate(heads, axis=0) * Q_SCALE)

    _software_pipeline(n_feat, [feat_project, feat_finish])


def _qkv_proj(x3d, mod3, mod_row, g1, w_k, w_t, gsum, kgains, qgains, rope_tabs, tm, emit_new_kv):
    b, t, _ = x3d.shape
    rows = b * t
    rope = rope_tabs is not None
    tps = max(t // tm, 1)
    spt = max(tm // t, 1)
    tw = tm // spt
    in_specs = [pl.BlockSpec((tm, D_MODEL), lambda i: (i, 0)),
                pl.BlockSpec((None, 1, N_MOD * D_MODEL), lambda i: (mod_row(i), 0, 0)),
                _const_spec((1, D_MODEL)),
                _const_spec((D_MODEL, N_K)),
                _const_spec((N_T, D_MODEL)),
                _const_spec((CHUNK, CHUNK)),
                _const_spec((2, CHUNK)),
                _const_spec((2, HEAD_DIM, 1))]
    args = [x3d.reshape(rows, D_MODEL), mod3, g1, w_k, w_t, gsum, kgains, qgains]
    if rope:
        nat, featm = rope_tabs
        in_specs += [pl.BlockSpec((tm, LANES), lambda i: (i % tps, 0))] * 3
        in_specs += [pl.BlockSpec((HEAD_DIM, tm), lambda i: (0, i % tps))] * 2
        args += list(nat) + list(featm)
    tok_wide = pl.BlockSpec((tm, A_W), lambda i: (i, 0))
    tok_narrow = pl.BlockSpec((tm, B_KVW), lambda i: (i, 0))
    feat_wide = pl.BlockSpec((spt, A_W, tw), lambda i: (i // tps, 0, i % tps))
    feat_narrow = pl.BlockSpec((spt, B_KVW, tw), lambda i: (i // tps, 0, i % tps))
    out_specs = [feat_wide, feat_wide, tok_wide, tok_narrow, feat_wide, feat_narrow]
    out_shape = [jax.ShapeDtypeStruct((b, A_W, t), BF16),
                 jax.ShapeDtypeStruct((b, B_Q, t), BF16),
                 jax.ShapeDtypeStruct((rows, A_W), BF16),
                 jax.ShapeDtypeStruct((rows, B_KVW), BF16),
                 jax.ShapeDtypeStruct((b, A_W, t), BF16),
                 jax.ShapeDtypeStruct((b, B_KVW, t), F32 if emit_new_kv else BF16)]
    if emit_new_kv:
        out_specs += [feat_wide, tok_wide, feat_narrow]
        out_shape += [jax.ShapeDtypeStruct((b, A_W, t), F32),
                      jax.ShapeDtypeStruct((rows, A_W), F32),
                      jax.ShapeDtypeStruct((b, B_KVW, t), F32)]
    return pl.pallas_call(
        functools.partial(_qkv_kernel, rope=rope, emit_new_kv=emit_new_kv),
        grid=(rows // tm,),
        in_specs=in_specs,
        out_specs=out_specs,
        out_shape=out_shape,
        compiler_params=_params(1),
        name="qkv_rope" if rope else "qkv_ctx",
    )(*args)


def _lambda(lam_ref):
    lv = lam_ref[...]
    t1 = jnp.sum(lv[0:1] * lv[1:2], axis=-1, keepdims=True)
    t2 = jnp.sum(lv[2:3] * lv[3:4], axis=-1, keepdims=True)
    return jnp.exp(t1) - jnp.exp(t2) + LAMBDA_INIT


def _with_ones(vt):
    return jnp.concatenate([vt, jnp.ones((ONES_ROWS, vt.shape[1]), vt.dtype)], axis=0)


def _column_softmax(scores, extra=None):
    m = functools.reduce(jnp.maximum, [jnp.max(s, axis=0, keepdims=True) for s in scores])
    if extra is not None:
        m = jnp.maximum(m, extra)
    return [jnp.exp2(s - m).astype(BF16) for s in scores], m


def _diff_query_cols(qt):
    zero = jnp.zeros((HEAD_DIM, qt.shape[1]), qt.dtype)
    return jnp.concatenate([jnp.concatenate([qt[:HEAD_DIM], zero], axis=0),
                            jnp.concatenate([zero, qt[HEAD_DIM:]], axis=0)], axis=1)


def _diff_output(r, lam, subg_col):
    n = r.shape[1] // 2
    dv = 2 * HEAD_DIM
    tot = r[dv:dv + 1, :]
    o = r[:dv, :n] * (1.0 / tot[:, :n]) - r[:dv, n:] * (lam / tot[:, n:])
    o = o * lax.rsqrt(jnp.mean(o * o, axis=0, keepdims=True) + EPS)
    return (o * subg_col * (1.0 - LAMBDA_INIT)).T


def _group_output(r, m, sink_row):
    n = r.shape[1] // B_GROUP
    tot = r[HEAD_DIM:HEAD_DIM + 1, :] + jnp.exp2(sink_row - m)
    o = r[:HEAD_DIM, :] * (1.0 / tot)
    return jnp.concatenate([o[:, i * n:(i + 1) * n] for i in range(B_GROUP)], axis=0).T


def _sink_row(sink_ref, g, n):
    reps = n // LANES
    return jnp.concatenate([sink_ref[pl.ds(g * B_GROUP + r, 1), :] for r in range(B_GROUP)
                            for _ in range(reps)], axis=1)


def _ctx_attn_kernel(qat_ref, qbt_ref, ka_ref, kb_ref, vat_ref, vbt_ref, lam_ref, subg_ref, sink_ref,
                     oa_ref, ob_ref):
    lam = _lambda(lam_ref)
    subg_col = subg_ref[...]
    n_seq, _, n = qat_ref.shape
    chains = A_HEADS + B_KV_HEADS

    def scores(item, _):
        s, c = divmod(item, chains)
        rows = slice(s * n, (s + 1) * n)
        if c < A_HEADS:
            sl = slice(c * LANES, (c + 1) * LANES)
            return _dot(ka_ref[rows, sl], _diff_query_cols(qat_ref[s, sl, :]))
        g = c - A_HEADS
        qs = jnp.concatenate([qbt_ref[s, (g * B_GROUP + r) * HEAD_DIM:(g * B_GROUP + r + 1) * HEAD_DIM, :]
                              for r in range(B_GROUP)], axis=1)
        zero = jnp.zeros_like(qs)
        w = jnp.concatenate([qs if i == g else zero for i in range(B_KV_HEADS)], axis=0)
        return _dot(kb_ref[rows, :], w)

    def probs(item, sc):
        c = item % chains
        if c < A_HEADS:
            return _column_softmax([sc])
        return _column_softmax([sc], extra=_sink_row(sink_ref, c - A_HEADS, n))

    def outputs(item, st):
        s, c = divmod(item, chains)
        rows = slice(s * n, (s + 1) * n)
        es, m = st
        if c < A_HEADS:
            sl = slice(c * LANES, (c + 1) * LANES)
            r = _dot(_with_ones(vat_ref[s, sl, :]), es[0])
            oa_ref[rows, sl] = _diff_output(r, lam, subg_col).astype(BF16)
        else:
            g = c - A_HEADS
            vt = vbt_ref[s, g * HEAD_DIM:(g + 1) * HEAD_DIM, :].astype(BF16)
            r = _dot(_with_ones(vt), es[0])
            o = _group_output(r, m, _sink_row(sink_ref, g, n))
            ob_ref[rows, g * CHUNK:(g + 1) * CHUNK] = o.astype(BF16)

    _software_pipeline(n_seq * chains, [scores, probs, outputs])


def _ctx_attn(qat, qbt, ka, kb, vat, vbt, lam_vecs, subg_col, sink_rows, n_seq):
    b, _, t = qat.shape
    feat = pl.BlockSpec((n_seq, A_W, t), lambda i: (i, 0, 0))
    tok = pl.BlockSpec((n_seq * t, A_W), lambda i: (i, 0))
    return pl.pallas_call(
        _ctx_attn_kernel,
        grid=(b // n_seq,),
        in_specs=[feat, feat, tok, pl.BlockSpec((n_seq * t, B_KVW), lambda i: (i, 0)), feat,
                  pl.BlockSpec((n_seq, B_KVW, t), lambda i: (i, 0, 0)),
                  _const_spec((4, HEAD_DIM)), _const_spec((2 * HEAD_DIM, 1)),
                  _const_spec((B_HEADS, LANES))],
        out_specs=[tok, tok],
        out_shape=[jax.ShapeDtypeStruct((b * t, A_W), BF16), jax.ShapeDtypeStruct((b * t, B_Q), BF16)],
        compiler_params=_params(1),
        name="ctx_attn",
    )(qat, qbt, ka, kb, vat, vbt, lam_vecs, subg_col, sink_rows)


def _diff_lat_kernel(qt_ref, k_ref, vt_ref, ckt_ref, cv_ref, lam_ref, subg_ref, o_ref):
    lam = _lambda(lam_ref)
    subg_col = subg_ref[...]
    k = k_ref[...]
    ck = ckt_ref[...].T.astype(BF16)
    vt = _with_ones(vt_ref[...])
    past = cv_ref.shape[0] // A_HEADS
    cv = cv_ref[pl.ds(pl.program_id(1), past, stride=A_HEADS), :]
    cvt = _with_ones(cv.T.astype(BF16))

    def scores(i, _):
        q = _diff_query_cols(qt_ref[:, i * BLOCK:(i + 1) * BLOCK])
        return [_dot(k, q), _dot(ck, q)]

    def outputs(i, s):
        m = jnp.maximum(jnp.max(s[0], axis=0, keepdims=True), jnp.max(s[1], axis=0, keepdims=True))
        r = None
        for sc, v in ((s[0], vt), (s[1], cvt)):
            for c in range(0, sc.shape[0], KEY_CHUNK):
                part = _dot(v[:, c:c + KEY_CHUNK], jnp.exp2(sc[c:c + KEY_CHUNK] - m).astype(BF16))
                r = part if r is None else r + part
        o_ref[i * BLOCK:(i + 1) * BLOCK, :] = _diff_output(r, lam, subg_col).astype(BF16)

    _software_pipeline(qt_ref.shape[1] // BLOCK, [scores, outputs])


def _diff_lat(qt, k, vt, ckt, cv, lam_vecs, subg_col, tq):
    b, _, t = qt.shape
    past = ckt.shape[2]
    return pl.pallas_call(
        _diff_lat_kernel,
        grid=(b, A_HEADS, t // tq),
        in_specs=[pl.BlockSpec((None, LANES, tq), lambda i, h, j: (i, h, j)),
                  pl.BlockSpec((None, t, LANES), lambda i, h, j: (i, 0, h)),
                  pl.BlockSpec((None, LANES, t), lambda i, h, j: (i, h, 0)),
                  pl.BlockSpec((None, LANES, past), lambda i, h, j: (i, h, 0)),
                  pl.BlockSpec((None, past * A_HEADS, LANES), lambda i, h, j: (i, 0, 0)),
                  _const_spec((4, HEAD_DIM)), _const_spec((2 * HEAD_DIM, 1))],
        out_specs=pl.BlockSpec((None, tq, LANES), lambda i, h, j: (i, j, h)),
        out_shape=jax.ShapeDtypeStruct((b, t, A_W), BF16),
        compiler_params=_params(3),
        name="diff_lat",
    )(qt, k, vt, ckt, cv, lam_vecs, subg_col)


def _win_lat_kernel(qt_ref, k_ref, vt_ref, ckt_ref, cvt_ref, sink_ref, o_ref):
    g = pl.program_id(1)
    t = k_ref.shape[0]
    span = 3 * BLOCK
    keep = lax.broadcasted_iota(jnp.int32, (1, CHUNK), 1) // HEAD_DIM == g
    k = jnp.where(keep, k_ref[...].astype(F32), 0.0).astype(BF16)
    ck = jnp.where(keep, ckt_ref[...].T, 0.0).astype(BF16)
    vt = vt_ref[...]
    cvt = _with_ones(cvt_ref[...].astype(BF16))
    sink_row = _sink_row(sink_ref, g, BLOCK)
    rel = (lax.broadcasted_iota(jnp.int32, (span, B_GROUP * BLOCK), 0)
           - lax.broadcasted_iota(jnp.int32, (span, B_GROUP * BLOCK), 1) % BLOCK)

    def window(i):
        return min(max((i - 1) * BLOCK, 0), t - span)

    bias = {off: jnp.where(jnp.abs(rel + off) <= WINDOW, 0.0, NEG)
            for off in sorted({window(i) - i * BLOCK for i in range(t // BLOCK)})}

    def scores(i, _):
        w = window(i)
        qs = jnp.concatenate([qt_ref[r * HEAD_DIM:(r + 1) * HEAD_DIM, i * BLOCK:(i + 1) * BLOCK]
                              for r in range(B_GROUP)], axis=1)
        wq = jnp.concatenate([qs] * B_KV_HEADS, axis=0)
        return [_dot(k[w:w + span], wq) + bias[w - i * BLOCK], _dot(ck, wq)]

    def probs(i, s):
        return _column_softmax(s, extra=sink_row)

    def outputs(i, st):
        es, m = st
        w = window(i)
        r = _dot(_with_ones(vt[:, w:w + span]), es[0]) + _dot(cvt, es[1])
        o_ref[i * BLOCK:(i + 1) * BLOCK, :] = _group_output(r, m, sink_row).astype(BF16)

    _software_pipeline(t // BLOCK, [scores, probs, outputs])


def _win_lat(qt, k, vt, ckt, cvt, sink_rows):
    b, _, t = qt.shape
    past = ckt.shape[2]
    return pl.pallas_call(
        _win_lat_kernel,
        grid=(b, B_KV_HEADS),
        in_specs=[pl.BlockSpec((None, CHUNK, t), lambda i, g: (i, g, 0)),
                  pl.BlockSpec((None, t, B_KVW), lambda i, g: (i, 0, 0)),
                  pl.BlockSpec((None, HEAD_DIM, t), lambda i, g: (i, g, 0)),
                  pl.BlockSpec((None, B_KVW, past), lambda i, g: (i, 0, 0)),
                  pl.BlockSpec((None, HEAD_DIM, past), lambda i, g: (i, g, 0)),
                  _const_spec((B_HEADS, LANES))],
        out_specs=pl.BlockSpec((None, t, CHUNK), lambda i, g: (i, 0, g)),
        out_shape=jax.ShapeDtypeStruct((b, t, B_Q), BF16),
        compiler_params=_params(2),
        name="win_lat",
    )(qt, k, vt, ckt, cvt, sink_rows)


def _merge_kernel(x_ref, oa_ref, ob_ref, mod_ref, g1_ref, wg_ref, woa_ref, wob_ref, wout_ref, o_ref):
    x = x_ref[...]
    mod = mod_ref[...]
    h = _modulated_norm(x, g1_ref[...], mod[:, D_MODEL:2 * D_MODEL], mod[:, 0:D_MODEL]).astype(BF16)
    ga = _dot(h, wg_ref[:, 0:D_MODEL])
    gb = _dot(h, wg_ref[:, D_MODEL:2 * D_MODEL])
    merged = (jax.nn.sigmoid(ga) * _dot(oa_ref[...], woa_ref[...])
              + jax.nn.sigmoid(gb) * _dot(ob_ref[...], wob_ref[...]))
    o_ref[...] = x + mod[:, 2 * D_MODEL:3 * D_MODEL] * _dot(merged.astype(BF16), wout_ref[...])


def _merge(x2d, oa, ob, mod3, mod_row, g1, w_g, w_oa, w_ob, w_out, tm):
    rows = x2d.shape[0]
    tile = pl.BlockSpec((tm, D_MODEL), lambda i: (i, 0))
    sq = _const_spec((D_MODEL, D_MODEL))
    return pl.pallas_call(
        _merge_kernel,
        grid=(rows // tm,),
        in_specs=[tile, tile, tile,
                  pl.BlockSpec((None, 1, N_MOD * D_MODEL), lambda i: (mod_row(i), 0, 0)),
                  _const_spec((1, D_MODEL)), _const_spec((D_MODEL, 2 * D_MODEL)), sq, sq, sq],
        out_specs=tile,
        out_shape=jax.ShapeDtypeStruct((rows, D_MODEL), F32),
        compiler_params=_params(1),
        name="merge_out",
    )(x2d, oa, ob, mod3, g1, w_g, w_oa, w_ob, w_out)


def _ffn_kernel(x_ref, mod_ref, g2_ref, wgate_ref, wup_ref, wdown_ref, o_ref):
    x = x_ref[...]
    mod = mod_ref[...]
    h = _modulated_norm(x, g2_ref[...], mod[:, 4 * D_MODEL:5 * D_MODEL],
                        mod[:, 3 * D_MODEL:4 * D_MODEL]).astype(BF16)
    gate = _dot(h, wgate_ref[...])
    act = (gate * jax.nn.sigmoid(gate) * _dot(h, wup_ref[...])).astype(BF16)
    o_ref[...] = x + mod[:, 5 * D_MODEL:6 * D_MODEL] * _dot(act, wdown_ref[...])


def _ffn(x2d, mod3, mod_row, g2, w_gate, w_up, w_down, tm):
    rows = x2d.shape[0]
    tile = pl.BlockSpec((tm, D_MODEL), lambda i: (i, 0))
    return pl.pallas_call(
        _ffn_kernel,
        grid=(rows // tm,),
        in_specs=[tile,
                  pl.BlockSpec((None, 1, N_MOD * D_MODEL), lambda i: (mod_row(i), 0, 0)),
                  _const_spec((1, D_MODEL)), _const_spec((D_MODEL, D_FF)),
                  _const_spec((D_MODEL, D_FF)), _const_spec((D_FF, D_MODEL))],
        out_specs=tile,
        out_shape=jax.ShapeDtypeStruct((rows, D_MODEL), F32),
        compiler_params=_params(1),
        name="swiglu_ffn",
    )(x2d, mod3, g2, w_gate, w_up, w_down)


def _rope_tables(seq_len):
    quarter = HEAD_DIM // 4
    t = np.arange(seq_len)
    row = (t // GRID_W).astype(np.float32)
    col = (t % GRID_W).astype(np.float32)
    freqs = (np.float32(ROPE_BASE) ** (-np.arange(quarter, dtype=np.float32) / np.float32(quarter))
             ).astype(np.float32)
    ang_r = row[:, None] * freqs
    ang_c = col[:, None] * freqs
    zeros = np.zeros_like(ang_r)
    cos = np.concatenate([np.cos(ang_r)] * 2 + [np.cos(ang_c)] * 2, axis=-1)
    sin_next = np.concatenate([-np.sin(ang_r), zeros, -np.sin(ang_c), zeros], axis=-1)
    sin_prev = np.concatenate([zeros, np.sin(ang_r), zeros, np.sin(ang_c)], axis=-1)
    rep = LANES // HEAD_DIM
    nat = tuple(jnp.asarray(np.tile(a, (1, rep)), dtype=F32) for a in (cos, sin_next, sin_prev))
    featm = (jnp.asarray(cos.T, dtype=F32), jnp.asarray((sin_next + sin_prev).T, dtype=F32))
    return nat, featm


def _group_sum_matrix():
    idx = np.arange(CHUNK) // HEAD_DIM
    return jnp.asarray(idx[:, None] == idx[None, :], dtype=BF16)


def kernel(x_prompt, x_sample, cache_diff_k, cache_diff_v, cache_win_k, cache_win_v, c, c_ctx,
           w_ada, b_ada, norm1_g, w_in, qn_a, kn_a, lambda_q1, lambda_k1, lambda_q2, lambda_k2,
           subln_g, qn_b, kn_b, sink, w_oa, w_ob, w_out, norm2_g, w_gate, w_up, w_down):
    batch, seq, _ = x_prompt.shape
    dec_batch, dec_seq, _ = x_sample.shape
    past = cache_diff_k.shape[2]
    l = 0

    cc = jnp.zeros((8, D_MODEL), F32).at[:dec_batch].set(c).at[dec_batch].set(c_ctx)
    mod3 = _modulation(cc, w_ada[l], b_ada[l][None, :]).reshape(8, 1, N_MOD * D_MODEL)

    w_in_b = w_in[l].astype(BF16)
    o_ka, o_va, o_qb, o_kb, o_vb, o_g = A_W, 2 * A_W, 3 * A_W, 3 * A_W + B_Q, 3 * A_W + B_Q + B_KVW, \
        3 * A_W + B_Q + 2 * B_KVW
    w_k = jnp.concatenate([w_in_b[:, o_ka:o_va], w_in_b[:, o_kb:o_vb]], axis=1)
    w_t = jnp.concatenate([w_in_b[:, :o_ka], w_in_b[:, o_qb:o_kb], w_in_b[:, o_va:o_qb],
                           w_in_b[:, o_vb:o_g]], axis=1).T
    w_g = w_in_b[:, o_g:]
    w_oa_b, w_ob_b, w_out_b = w_oa[l].astype(BF16), w_ob[l].astype(BF16), w_out[l].astype(BF16)
    w_gate_b, w_up_b, w_down_b = w_gate[l].astype(BF16), w_up[l].astype(BF16), w_down[l].astype(BF16)
    g1 = norm1_g[l][None, :]
    g2 = norm2_g[l][None, :]
    rep = CHUNK // HEAD_DIM
    kgains = jnp.stack([jnp.tile(kn_a[l], rep), jnp.tile(kn_b[l], rep)])
    qgains = jnp.stack([qn_a[l], qn_b[l]])[:, :, None]
    lam_vecs = jnp.stack([lambda_q1[l], lambda_k1[l], lambda_q2[l], lambda_k2[l]])
    subg_col = subln_g[l][:, None]
    sink_rows = jnp.broadcast_to((sink[l] * LOG2E)[:, None], (B_HEADS, LANES))
    gsum = _group_sum_matrix()
    tm = 512

    xp = x_prompt.reshape(batch * seq, D_MODEL)
    ctx_row = lambda i: dec_batch
    qat, qbt, ka, kb, vat, vbt, kat_f, va_f, kbt_f = _qkv_proj(
        x_prompt, mod3, ctx_row, g1, w_k, w_t, gsum, kgains, qgains, None, tm, True)
    oa, ob = _ctx_attn(qat, qbt, ka, kb, vat, vbt, lam_vecs, subg_col, sink_rows, 2)
    xp1 = _merge(xp, oa, ob, mod3, ctx_row, g1, w_g, w_oa_b, w_ob_b, w_out_b, tm)
    y_prompt = _ffn(xp1, mod3, ctx_row, g2, w_gate_b, w_up_b, w_down_b, tm)
    new_diff_k = kat_f.reshape(batch, A_HEADS, 2, HEAD_DIM, seq).transpose(0, 4, 1, 2, 3)[:, None]
    new_diff_v = va_f.reshape(batch, 1, seq, A_HEADS, 2 * HEAD_DIM)
    new_win_k = kbt_f.reshape(batch, B_KV_HEADS, HEAD_DIM, seq).transpose(0, 3, 1, 2)[:, None]
    new_win_v = vbt.reshape(batch, B_KV_HEADS, HEAD_DIM, seq).transpose(0, 3, 1, 2)[:, None]

    xs = x_sample.reshape(dec_batch * dec_seq, D_MODEL)
    lat_row = lambda i: i // (dec_seq // tm)
    qat, qbt, ka, kb, vat, vbt = _qkv_proj(
        x_sample, mod3, lat_row, g1, w_k, w_t, gsum, kgains, qgains, _rope_tables(dec_seq), tm, False)
    ckt_a = cache_diff_k[:, l].transpose(0, 2, 3, 4, 1).reshape(dec_batch, A_W, past)
    cv_a = cache_diff_v[:, l].reshape(dec_batch, past * A_HEADS, 2 * HEAD_DIM)
    ckt_b = cache_win_k[:, l].transpose(0, 2, 3, 1).reshape(dec_batch, B_KVW, past)
    cvt_b = cache_win_v[:, l].transpose(0, 2, 3, 1).reshape(dec_batch, B_KVW, past)
    oa = _diff_lat(qat, ka.reshape(dec_batch, dec_seq, A_W), vat, ckt_a, cv_a, lam_vecs, subg_col, dec_seq)
    ob = _win_lat(qbt, kb.reshape(dec_batch, dec_seq, B_KVW), vbt, ckt_b, cvt_b, sink_rows)
    xs1 = _merge(xs, oa.reshape(-1, A_W), ob.reshape(-1, B_Q), mod3, lat_row, g1,
                 w_g, w_oa_b, w_ob_b, w_out_b, tm)
    y_sample = _ffn(xs1, mod3, lat_row, g2, w_gate_b, w_up_b, w_down_b, tm)

    return (y_prompt.reshape(batch, seq, D_MODEL),
            y_sample.reshape(dec_batch, dec_seq, D_MODEL),
            new_diff_k, new_diff_v, new_win_k, new_win_v)
```

```python
import functools
import math

import jax
import jax.numpy as jnp
import numpy as np
from jax import lax
from jax.experimental import pallas as pl
from jax.experimental.pallas import tpu as pltpu

D_MODEL = 1024
HEAD_DIM = 64
A_HEADS = 8
B_HEADS = 16
B_KV_HEADS = 4
B_GROUP = B_HEADS // B_KV_HEADS
A_W = A_HEADS * 2 * HEAD_DIM
B_Q = B_HEADS * HEAD_DIM
B_KVW = B_KV_HEADS * HEAD_DIM
N_K = A_W + B_KVW
N_T = A_W + B_Q + A_W + B_KVW
WINDOW = 128
BLOCK = 128
GRID_W = 64
D_FF = 2816
N_MOD = 6
EPS = 1e-6
ROPE_BASE = 10000.0
NEG = -1e30
LAMBDA_INIT = 0.8 - 0.6 * math.exp(-0.3 * 0)
LOG2E = math.log2(math.e)
Q_SCALE = HEAD_DIM ** -0.5 * LOG2E

LANES = 128
CHUNK = 256
ONES_ROWS = 16
KEY_CHUNK = 512
SHIFT_FREE_LOG2 = 60.0
VMEM_LIMIT = 56 * 1024 * 1024

F32 = jnp.float32
BF16 = jnp.bfloat16


def _params(n_axes, flags=None):
    return pltpu.CompilerParams(dimension_semantics=("arbitrary",) * n_axes,
                                vmem_limit_bytes=VMEM_LIMIT, flags=flags)


def _const_spec(shape):
    nd = len(shape)
    return pl.BlockSpec(shape, lambda *_: (0,) * nd, pipeline_mode=pl.Buffered(1))


def _dot(a, b):
    return jnp.dot(a, b, preferred_element_type=F32)


def _modulated_norm(x, gain, scale, shift):
    y = x * lax.rsqrt(jnp.mean(x * x, axis=-1, keepdims=True) + EPS) * gain
    return y * (1.0 + scale) + shift


def _software_pipeline(n, stages):
    depth = len(stages)
    state = [None] * n
    for step in range(n + depth - 1):
        for k, stage in enumerate(stages):
            i = step - k
            if 0 <= i < n:
                state[i] = stage(i, state[i])


def _mod_kernel(c_ref, w_ref, b_ref, o_ref):
    cc = c_ref[...]
    s = cc * jax.nn.sigmoid(cc)
    o_ref[...] = _dot(s.astype(BF16), w_ref[...].astype(BF16)) + b_ref[...]


def _modulation(cc, w_ada, b_ada):
    rows, n = cc.shape[0], w_ada.shape[1]
    tn = n // 4
    return pl.pallas_call(
        _mod_kernel,
        grid=(n // tn,),
        in_specs=[pl.BlockSpec((rows, D_MODEL), lambda j: (0, 0)),
                  pl.BlockSpec((D_MODEL, tn), lambda j: (0, j)),
                  pl.BlockSpec((1, tn), lambda j: (0, j))],
        out_specs=pl.BlockSpec((rows, tn), lambda j: (0, j)),
        out_shape=jax.ShapeDtypeStruct((rows, n), F32),
        compiler_params=_params(1),
        name="adaln_mod",
    )(cc, w_ada, b_ada)


def _qkv_kernel(*refs, rope, emit_new_kv):
    x_ref, mod_ref, g1_ref, wk_ref, wt_ref, gsum_ref, kgain_ref, qgain_ref = refs[:8]
    refs = refs[8:]
    if rope:
        cos_ref, sa_ref, sb_ref, cost_ref, sint_ref = refs[:5]
        refs = refs[5:]
    qat_ref, qbt_ref, ka_ref, kb_ref, vat_ref, vbt_ref = refs[:6]
    if emit_new_kv:
        kat_ref, va_ref, kbt_ref = refs[6:]

    mod = mod_ref[...]
    h = _modulated_norm(x_ref[...], g1_ref[...], mod[:, D_MODEL:2 * D_MODEL], mod[:, 0:D_MODEL])
    hb = h.astype(BF16)
    gsum = gsum_ref[...]

    def put_feat(ref, rows, val):
        n_seq, _, width = ref.shape
        for s in range(n_seq):
            ref[s, rows, :] = val[:, s * width:(s + 1) * width].astype(ref.dtype)

    n_wide = A_W // CHUNK
    key_cols = [j * CHUNK for j in range(n_wide)] + [A_W]

    def key_project(j, _):
        return _dot(hb, wk_ref[:, key_cols[j]:key_cols[j] + CHUNK])

    def key_squares(j, p):
        return p, _dot((p * p).astype(BF16), gsum)

    def key_finish(j, st):
        p, ss = st
        row = 0 if j < n_wide else 1
        p = p * lax.rsqrt(ss * (1.0 / HEAD_DIM) + EPS) * kgain_ref[row:row + 1, :]
        if rope:
            cos, sa, sb = cos_ref[...], sa_ref[...], sb_ref[...]
            halves = []
            for i in range(CHUNK // LANES):
                xh = p[:, i * LANES:(i + 1) * LANES]
                nxt = pltpu.roll(xh, LANES - HEAD_DIM // 4, 1)
                prv = pltpu.roll(xh, HEAD_DIM // 4, 1)
                halves.append(xh * cos + nxt * sa + prv * sb)
            p = jnp.concatenate(halves, axis=1)
        if j < n_wide:
            sl = slice(j * CHUNK, (j + 1) * CHUNK)
            ka_ref[:, sl] = p.astype(BF16)
            if emit_new_kv:
                put_feat(kat_ref, sl, p.T)
        else:
            kb_ref[...] = p.astype(BF16)
            if emit_new_kv:
                put_feat(kbt_ref, slice(None), p.T)

    _software_pipeline(len(key_cols), [key_project, key_squares, key_finish])

    ht = h.T.astype(BF16)
    n_feat = N_T // CHUNK

    def feat_project(c, _):
        return _dot(wt_ref[c * CHUNK:(c + 1) * CHUNK, :], ht)

    def feat_finish(c, p):
        kind, j = divmod(c, n_wide)
        sl = slice(j * CHUNK, (j + 1) * CHUNK)
        if kind == 2:
            put_feat(vat_ref, sl, p)
            if emit_new_kv:
                va_ref[:, sl] = p.T
            return
        if kind == 3:
            put_feat(vbt_ref, slice(None), p)
            return
        gain = qgain_ref[kind]
        q4 = HEAD_DIM // 4
        heads = []
        for r in range(CHUNK // HEAD_DIM):
            x = p[r * HEAD_DIM:(r + 1) * HEAD_DIM, :]
            x = x * lax.rsqrt(jnp.mean(x * x, axis=0, keepdims=True) + EPS) * gain
            if rope:
                swapped = jnp.concatenate([x[q4:2 * q4], x[0:q4], x[3 * q4:4 * q4], x[2 * q4:3 * q4]],
                                          axis=0)
                x = x * cost_ref[...] + swapped * sint_ref[...]
            heads.append(x)
        put_feat(qat_ref if kind == 0 else qbt_ref, sl, jnp.concatenate(heads, axis=0) * Q_SCALE)

    _software_pipeline(n_feat, [feat_project, feat_finish])


def _qkv_proj(x3d, mod3, mod_row, g1, w_k, w_t, gsum, kgains, qgains, rope_tabs, tm, emit_new_kv):
    b, t, _ = x3d.shape
    rows = b * t
    rope = rope_tabs is not None
    tps = max(t // tm, 1)
    spt = max(tm // t, 1)
    tw = tm // spt
    in_specs = [pl.BlockSpec((tm, D_MODEL), lambda i: (i, 0)),
                pl.BlockSpec((None, 1, N_MOD * D_MODEL), lambda i: (mod_row(i), 0, 0)),
                _const_spec((1, D_MODEL)),
                _const_spec((D_MODEL, N_K)),
                _const_spec((N_T, D_MODEL)),
                _const_spec((CHUNK, CHUNK)),
                _const_spec((2, CHUNK)),
                _const_spec((2, HEAD_DIM, 1))]
    args = [x3d.reshape(rows, D_MODEL), mod3, g1, w_k, w_t, gsum, kgains, qgains]
    if rope:
        nat, featm = rope_tabs
        in_specs += [pl.BlockSpec((tm, LANES), lambda i: (i % tps, 0))] * 3
        in_specs += [pl.BlockSpec((HEAD_DIM, tm), lambda i: (0, i % tps))] * 2
        args += list(nat) + list(featm)
    tok_wide = pl.BlockSpec((tm, A_W), lambda i: (i, 0))
    tok_narrow = pl.BlockSpec((tm, B_KVW), lambda i: (i, 0))
    feat_wide = pl.BlockSpec((spt, A_W, tw), lambda i: (i // tps, 0, i % tps))
    feat_narrow = pl.BlockSpec((spt, B_KVW, tw), lambda i: (i // tps, 0, i % tps))
    out_specs = [feat_wide, feat_wide, tok_wide, tok_narrow, feat_wide, feat_narrow]
    out_shape = [jax.ShapeDtypeStruct((b, A_W, t), BF16),
                 jax.ShapeDtypeStruct((b, B_Q, t), BF16),
                 jax.ShapeDtypeStruct((rows, A_W), BF16),
                 jax.ShapeDtypeStruct((rows, B_KVW), BF16),
                 jax.ShapeDtypeStruct((b, A_W, t), BF16),
                 jax.ShapeDtypeStruct((b, B_KVW, t), F32 if emit_new_kv else BF16)]
    if emit_new_kv:
        out_specs += [feat_wide, tok_wide, feat_narrow]
        out_shape += [jax.ShapeDtypeStruct((b, A_W, t), F32),
                      jax.ShapeDtypeStruct((rows, A_W), F32),
                      jax.ShapeDtypeStruct((b, B_KVW, t), F32)]
    return pl.pallas_call(
        functools.partial(_qkv_kernel, rope=rope, emit_new_kv=emit_new_kv),
        grid=(rows // tm,),
        in_specs=in_specs,
        out_specs=out_specs,
        out_shape=out_shape,
        compiler_params=_params(1),
        name="qkv_rope" if rope else "qkv_ctx",
    )(*args)


def _lambda(lam_ref):
    lv = lam_ref[...]
    t1 = jnp.sum(lv[0:1] * lv[1:2], axis=-1, keepdims=True)
    t2 = jnp.sum(lv[2:3] * lv[3:4], axis=-1, keepdims=True)
    return jnp.exp(t1) - jnp.exp(t2) + LAMBDA_INIT


def _with_ones(vt):
    return jnp.concatenate([vt, jnp.ones((ONES_ROWS, vt.shape[1]), vt.dtype)], axis=0)


def _column_softmax(scores, extra=None):
    m = functools.reduce(jnp.maximum, [jnp.max(s, axis=0, keepdims=True) for s in scores])
    if extra is not None:
        m = jnp.maximum(m, extra)
    return [jnp.exp2(s - m).astype(BF16) for s in scores], m


def _max_row_sq_norm(x):
    x = x.astype(F32)
    return jnp.max(jnp.sum(x * x, axis=1, keepdims=True), axis=0, keepdims=True)[0, 0]


def _scores_bounded(q_sq, k_sq, sink_abs=0.0):
    limit = SHIFT_FREE_LOG2
    return jnp.logical_and(q_sq * k_sq <= limit * limit, sink_abs <= limit)


def _diff_query_cols(qt):
    zero = jnp.zeros((HEAD_DIM, qt.shape[1]), qt.dtype)
    return jnp.concatenate([jnp.concatenate([qt[:HEAD_DIM], zero], axis=0),
                            jnp.concatenate([zero, qt[HEAD_DIM:]], axis=0)], axis=1)


def _diff_output(r, lam, subg_col):
    n = r.shape[1] // 2
    dv = 2 * HEAD_DIM
    tot = r[dv:dv + 1, :]
    o = r[:dv, :n] * (1.0 / tot[:, :n]) - r[:dv, n:] * (lam / tot[:, n:])
    o = o * lax.rsqrt(jnp.mean(o * o, axis=0, keepdims=True) + EPS)
    return (o * subg_col * (1.0 - LAMBDA_INIT)).T


def _group_output(r, m, sink_row):
    n = r.shape[1] // B_GROUP
    tot = r[HEAD_DIM:HEAD_DIM + 1, :] + jnp.exp2(sink_row - m)
    o = r[:HEAD_DIM, :] * (1.0 / tot)
    return jnp.concatenate([o[:, i * n:(i + 1) * n] for i in range(B_GROUP)], axis=0).T


def _sink_row(sink_ref, g, n):
    reps = n // LANES
    return jnp.concatenate([sink_ref[pl.ds(g * B_GROUP + r, 1), :] for r in range(B_GROUP)
                            for _ in range(reps)], axis=1)


def _ctx_attn_kernel(qat_ref, qbt_ref, ka_ref, kb_ref, vat_ref, vbt_ref, lam_ref, subg_ref, sink_ref,
                     bounds_ref, oa_ref, ob_ref):
    lam = _lambda(lam_ref)
    subg_col = subg_ref[...]
    n_seq, _, n = qat_ref.shape
    chains = A_HEADS + B_KV_HEADS

    def scores(item, _):
        s, c = divmod(item, chains)
        rows = slice(s * n, (s + 1) * n)
        if c < A_HEADS:
            sl = slice(c * LANES, (c + 1) * LANES)
            return _dot(ka_ref[rows, sl], _diff_query_cols(qat_ref[s, sl, :]))
        g = c - A_HEADS
        qs = jnp.concatenate([qbt_ref[s, (g * B_GROUP + r) * HEAD_DIM:(g * B_GROUP + r + 1) * HEAD_DIM, :]
                              for r in range(B_GROUP)], axis=1)
        zero = jnp.zeros_like(qs)
        w = jnp.concatenate([qs if i == g else zero for i in range(B_KV_HEADS)], axis=0)
        return _dot(kb_ref[rows, :], w)

    def probs(shift, item, sc):
        c = item % chains
        if not shift:
            return [jnp.exp2(sc).astype(BF16)], jnp.zeros((1, sc.shape[1]), F32)
        if c < A_HEADS:
            return _column_softmax([sc])
        return _column_softmax([sc], extra=_sink_row(sink_ref, c - A_HEADS, n))

    def outputs(item, st):
        s, c = divmod(item, chains)
        rows = slice(s * n, (s + 1) * n)
        es, m = st
        if c < A_HEADS:
            sl = slice(c * LANES, (c + 1) * LANES)
            r = _dot(_with_ones(vat_ref[s, sl, :]), es[0])
            oa_ref[rows, sl] = _diff_output(r, lam, subg_col).astype(BF16)
        else:
            g = c - A_HEADS
            vt = vbt_ref[s, g * HEAD_DIM:(g + 1) * HEAD_DIM, :].astype(BF16)
            r = _dot(_with_ones(vt), es[0])
            o = _group_output(r, m, _sink_row(sink_ref, g, n))
            ob_ref[rows, g * CHUNK:(g + 1) * CHUNK] = o.astype(BF16)

    bounded = jnp.logical_and(_scores_bounded(bounds_ref[0], bounds_ref[1]),
                              _scores_bounded(bounds_ref[2], bounds_ref[3], bounds_ref[4]))

    @pl.when(bounded)
    def _():
        _software_pipeline(n_seq * chains, [scores, functools.partial(probs, False), outputs])

    @pl.when(jnp.logical_not(bounded))
    def _():
        _software_pipeline(n_seq * chains, [scores, functools.partial(probs, True), outputs])


def _ctx_attn(qat, qbt, ka, kb, vat, vbt, lam_vecs, subg_col, sink_rows, bounds, n_seq):
    b, _, t = qat.shape
    feat = pl.BlockSpec((n_seq, A_W, t), lambda i: (i, 0, 0))
    tok = pl.BlockSpec((n_seq * t, A_W), lambda i: (i, 0))
    return pl.pallas_call(
        _ctx_attn_kernel,
        grid=(b // n_seq,),
        in_specs=[feat, feat, tok, pl.BlockSpec((n_seq * t, B_KVW), lambda i: (i, 0)), feat,
                  pl.BlockSpec((n_seq, B_KVW, t), lambda i: (i, 0, 0)),
                  _const_spec((4, HEAD_DIM)), _const_spec((2 * HEAD_DIM, 1)),
                  _const_spec((B_HEADS, LANES)), pl.BlockSpec(memory_space=pltpu.SMEM)],
        out_specs=[tok, tok],
        out_shape=[jax.ShapeDtypeStruct((b * t, A_W), BF16), jax.ShapeDtypeStruct((b * t, B_Q), BF16)],
        compiler_params=_params(1),
        name="ctx_attn",
    )(qat, qbt, ka, kb, vat, vbt, lam_vecs, subg_col, sink_rows, bounds)


def _diff_lat_kernel(qt_ref, k_ref, vt_ref, ckt_ref, cv_ref, lam_ref, subg_ref, bounds_ref, o_ref):
    lam = _lambda(lam_ref)
    subg_col = subg_ref[...]
    k = k_ref[...]
    ck = ckt_ref[...].T.astype(BF16)
    vt = _with_ones(vt_ref[...])
    past = cv_ref.shape[0] // A_HEADS
    cv = cv_ref[pl.ds(pl.program_id(1), past, stride=A_HEADS), :]
    cvt = _with_ones(cv.T.astype(BF16))

    def scores(i, _):
        q = _diff_query_cols(qt_ref[:, i * BLOCK:(i + 1) * BLOCK])
        return [_dot(k, q), _dot(ck, q)]

    def outputs(shift, i, s):
        if shift:
            m = jnp.maximum(jnp.max(s[0], axis=0, keepdims=True), jnp.max(s[1], axis=0, keepdims=True))
        r = None
        for sc, v in ((s[0], vt), (s[1], cvt)):
            for c in range(0, sc.shape[0], KEY_CHUNK):
                piece = sc[c:c + KEY_CHUNK]
                part = _dot(v[:, c:c + KEY_CHUNK], jnp.exp2(piece - m if shift else piece).astype(BF16))
                r = part if r is None else r + part
        o_ref[i * BLOCK:(i + 1) * BLOCK, :] = _diff_output(r, lam, subg_col).astype(BF16)

    n_sub = qt_ref.shape[1] // BLOCK
    bounded = _scores_bounded(bounds_ref[0], jnp.maximum(bounds_ref[1], _max_row_sq_norm(ck)))

    @pl.when(bounded)
    def _():
        _software_pipeline(n_sub, [scores, functools.partial(outputs, False)])

    @pl.when(jnp.logical_not(bounded))
    def _():
        _software_pipeline(n_sub, [scores, functools.partial(outputs, True)])


def _diff_lat(qt, k, vt, ckt, cv, lam_vecs, subg_col, bounds, tq):
    b, _, t = qt.shape
    past = ckt.shape[2]
    return pl.pallas_call(
        _diff_lat_kernel,
        grid=(b, A_HEADS, t // tq),
        in_specs=[pl.BlockSpec((None, LANES, tq), lambda i, h, j: (i, h, j)),
                  pl.BlockSpec((None, t, LANES), lambda i, h, j: (i, 0, h)),
                  pl.BlockSpec((None, LANES, t), lambda i, h, j: (i, h, 0)),
                  pl.BlockSpec((None, LANES, past), lambda i, h, j: (i, h, 0)),
                  pl.BlockSpec((None, past * A_HEADS, LANES), lambda i, h, j: (i, 0, 0)),
                  _const_spec((4, HEAD_DIM)), _const_spec((2 * HEAD_DIM, 1)),
                  pl.BlockSpec(memory_space=pltpu.SMEM)],
        out_specs=pl.BlockSpec((None, tq, LANES), lambda i, h, j: (i, j, h)),
        out_shape=jax.ShapeDtypeStruct((b, t, A_W), BF16),
        compiler_params=_params(3),
        name="diff_lat",
    )(qt, k, vt, ckt, cv, lam_vecs, subg_col, bounds)


def _win_lat_kernel(qt_ref, k_ref, vt_ref, ckt_ref, cvt_ref, sink_ref, bounds_ref, o_ref):
    g = pl.program_id(1)
    t = k_ref.shape[0]
    span = 3 * BLOCK
    keep = lax.broadcasted_iota(jnp.int32, (1, CHUNK), 1) // HEAD_DIM == g
    k = jnp.where(keep, k_ref[...].astype(F32), 0.0).astype(BF16)
    ck = jnp.where(keep, ckt_ref[...].T, 0.0).astype(BF16)
    vt = vt_ref[...]
    cvt = _with_ones(cvt_ref[...].astype(BF16))
    sink_row = _sink_row(sink_ref, g, BLOCK)
    rel = (lax.broadcasted_iota(jnp.int32, (span, B_GROUP * BLOCK), 0)
           - lax.broadcasted_iota(jnp.int32, (span, B_GROUP * BLOCK), 1) % BLOCK)

    def window(i):
        return min(max((i - 1) * BLOCK, 0), t - span)

    bias = {off: jnp.where(jnp.abs(rel + off) <= WINDOW, 0.0, NEG)
            for off in sorted({window(i) - i * BLOCK for i in range(t // BLOCK)})}

    def scores(i, _):
        w = window(i)
        qs = jnp.concatenate([qt_ref[r * HEAD_DIM:(r + 1) * HEAD_DIM, i * BLOCK:(i + 1) * BLOCK]
                              for r in range(B_GROUP)], axis=1)
        wq = jnp.concatenate([qs] * B_KV_HEADS, axis=0)
        return [_dot(k[w:w + span], wq) + bias[w - i * BLOCK], _dot(ck, wq)]

    def probs(shift, i, s):
        if shift:
            return _column_softmax(s, extra=sink_row)
        return [jnp.exp2(piece).astype(BF16) for piece in s], jnp.zeros_like(sink_row)

    def outputs(i, st):
        es, m = st
        w = window(i)
        r = _dot(_with_ones(vt[:, w:w + span]), es[0]) + _dot(cvt, es[1])
        o_ref[i * BLOCK:(i + 1) * BLOCK, :] = _group_output(r, m, sink_row).astype(BF16)

    bounded = _scores_bounded(bounds_ref[2], jnp.maximum(bounds_ref[3], _max_row_sq_norm(ck)), bounds_ref[4])

    @pl.when(bounded)
    def _():
        _software_pipeline(t // BLOCK, [scores, functools.partial(probs, False), outputs])

    @pl.when(jnp.logical_not(bounded))
    def _():
        _software_pipeline(t // BLOCK, [scores, functools.partial(probs, True), outputs])


def _win_lat(qt, k, vt, ckt, cvt, sink_rows, bounds):
    b, _, t = qt.shape
    past = ckt.shape[2]
    return pl.pallas_call(
        _win_lat_kernel,
        grid=(b, B_KV_HEADS),
        in_specs=[pl.BlockSpec((None, CHUNK, t), lambda i, g: (i, g, 0)),
                  pl.BlockSpec((None, t, B_KVW), lambda i, g: (i, 0, 0)),
                  pl.BlockSpec((None, HEAD_DIM, t), lambda i, g: (i, g, 0)),
                  pl.BlockSpec((None, B_KVW, past), lambda i, g: (i, 0, 0)),
                  pl.BlockSpec((None, HEAD_DIM, past), lambda i, g: (i, g, 0)),
                  _const_spec((B_HEADS, LANES)), pl.BlockSpec(memory_space=pltpu.SMEM)],
        out_specs=pl.BlockSpec((None, t, CHUNK), lambda i, g: (i, 0, g)),
        out_shape=jax.ShapeDtypeStruct((b, t, B_Q), BF16),
        compiler_params=_params(2),
        name="win_lat",
    )(qt, k, vt, ckt, cvt, sink_rows, bounds)


def _merge_kernel(x_ref, oa_ref, ob_ref, mod_ref, g1_ref, wg_ref, woa_ref, wob_ref, wout_ref, o_ref):
    x = x_ref[...]
    mod = mod_ref[...]
    h = _modulated_norm(x, g1_ref[...], mod[:, D_MODEL:2 * D_MODEL], mod[:, 0:D_MODEL]).astype(BF16)
    ga = _dot(h, wg_ref[:, 0:D_MODEL])
    gb = _dot(h, wg_ref[:, D_MODEL:2 * D_MODEL])
    merged = (jax.nn.sigmoid(ga) * _dot(oa_ref[...], woa_ref[...])
              + jax.nn.sigmoid(gb) * _dot(ob_ref[...], wob_ref[...]))
    o_ref[...] = x + mod[:, 2 * D_MODEL:3 * D_MODEL] * _dot(merged.astype(BF16), wout_ref[...])


def _merge(x2d, oa, ob, mod3, mod_row, g1, w_g, w_oa, w_ob, w_out, tm):
    rows = x2d.shape[0]
    tile = pl.BlockSpec((tm, D_MODEL), lambda i: (i, 0))
    sq = _const_spec((D_MODEL, D_MODEL))
    return pl.pallas_call(
        _merge_kernel,
        grid=(rows // tm,),
        in_specs=[tile, tile, tile,
                  pl.BlockSpec((None, 1, N_MOD * D_MODEL), lambda i: (mod_row(i), 0, 0)),
                  _const_spec((1, D_MODEL)), _const_spec((D_MODEL, 2 * D_MODEL)), sq, sq, sq],
        out_specs=tile,
        out_shape=jax.ShapeDtypeStruct((rows, D_MODEL), F32),
        compiler_params=_params(1),
        name="merge_out",
    )(x2d, oa, ob, mod3, g1, w_g, w_oa, w_ob, w_out)


def _ffn_kernel(x_ref, mod_ref, g2_ref, wgate_ref, wup_ref, wdown_ref, o_ref):
    x = x_ref[...]
    mod = mod_ref[...]
    h = _modulated_norm(x, g2_ref[...], mod[:, 4 * D_MODEL:5 * D_MODEL],
                        mod[:, 3 * D_MODEL:4 * D_MODEL]).astype(BF16)
    gate = _dot(h, wgate_ref[...])
    act = (gate * jax.nn.sigmoid(gate) * _dot(h, wup_ref[...])).astype(BF16)
    o_ref[...] = x + mod[:, 5 * D_MODEL:6 * D_MODEL] * _dot(act, wdown_ref[...])


def _ffn(x2d, mod3, mod_row, g2, w_gate, w_up, w_down, tm):
    rows = x2d.shape[0]
    tile = pl.BlockSpec((tm, D_MODEL), lambda i: (i, 0))
    return pl.pallas_call(
        _ffn_kernel,
        grid=(rows // tm,),
        in_specs=[tile,
                  pl.BlockSpec((None, 1, N_MOD * D_MODEL), lambda i: (mod_row(i), 0, 0)),
                  _const_spec((1, D_MODEL)), _const_spec((D_MODEL, D_FF)),
                  _const_spec((D_MODEL, D_FF)), _const_spec((D_FF, D_MODEL))],
        out_specs=tile,
        out_shape=jax.ShapeDtypeStruct((rows, D_MODEL), F32),
        compiler_params=_params(1),
        name="swiglu_ffn",
    )(x2d, mod3, g2, w_gate, w_up, w_down)


def _rope_tables(seq_len):
    quarter = HEAD_DIM // 4
    t = np.arange(seq_len)
    row = (t // GRID_W).astype(np.float32)
    col = (t % GRID_W).astype(np.float32)
    freqs = (np.float32(ROPE_BASE) ** (-np.arange(quarter, dtype=np.float32) / np.float32(quarter))
             ).astype(np.float32)
    ang_r = row[:, None] * freqs
    ang_c = col[:, None] * freqs
    zeros = np.zeros_like(ang_r)
    cos = np.concatenate([np.cos(ang_r)] * 2 + [np.cos(ang_c)] * 2, axis=-1)
    sin_next = np.concatenate([-np.sin(ang_r), zeros, -np.sin(ang_c), zeros], axis=-1)
    sin_prev = np.concatenate([zeros, np.sin(ang_r), zeros, np.sin(ang_c)], axis=-1)
    rep = LANES // HEAD_DIM
    nat = tuple(jnp.asarray(np.tile(a, (1, rep)), dtype=F32) for a in (cos, sin_next, sin_prev))
    featm = (jnp.asarray(cos.T, dtype=F32), jnp.asarray((sin_next + sin_prev).T, dtype=F32))
    return nat, featm


def _group_sum_matrix():
    idx = np.arange(CHUNK) // HEAD_DIM
    return jnp.asarray(idx[:, None] == idx[None, :], dtype=BF16)


def kernel(x_prompt, x_sample, cache_diff_k, cache_diff_v, cache_win_k, cache_win_v, c, c_ctx,
           w_ada, b_ada, norm1_g, w_in, qn_a, kn_a, lambda_q1, lambda_k1, lambda_q2, lambda_k2,
           subln_g, qn_b, kn_b, sink, w_oa, w_ob, w_out, norm2_g, w_gate, w_up, w_down):
    batch, seq, _ = x_prompt.shape
    dec_batch, dec_seq, _ = x_sample.shape
    past = cache_diff_k.shape[2]
    l = 0

    cc = jnp.zeros((8, D_MODEL), F32).at[:dec_batch].set(c).at[dec_batch].set(c_ctx)
    mod3 = _modulation(cc, w_ada[l], b_ada[l][None, :]).reshape(8, 1, N_MOD * D_MODEL)

    w_in_b = w_in[l].astype(BF16)
    o_ka, o_va, o_qb, o_kb, o_vb, o_g = A_W, 2 * A_W, 3 * A_W, 3 * A_W + B_Q, 3 * A_W + B_Q + B_KVW, \
        3 * A_W + B_Q + 2 * B_KVW
    w_k = jnp.concatenate([w_in_b[:, o_ka:o_va], w_in_b[:, o_kb:o_vb]], axis=1)
    w_t = jnp.concatenate([w_in_b[:, :o_ka], w_in_b[:, o_qb:o_kb], w_in_b[:, o_va:o_qb],
                           w_in_b[:, o_vb:o_g]], axis=1).T
    w_g = w_in_b[:, o_g:]
    w_oa_b, w_ob_b, w_out_b = w_oa[l].astype(BF16), w_ob[l].astype(BF16), w_out[l].astype(BF16)
    w_gate_b, w_up_b, w_down_b = w_gate[l].astype(BF16), w_up[l].astype(BF16), w_down[l].astype(BF16)
    g1 = norm1_g[l][None, :]
    g2 = norm2_g[l][None, :]
    rep = CHUNK // HEAD_DIM
    kgains = jnp.stack([jnp.tile(kn_a[l], rep), jnp.tile(kn_b[l], rep)])
    qgains = jnp.stack([qn_a[l], qn_b[l]])[:, :, None]
    lam_vecs = jnp.stack([lambda_q1[l], lambda_k1[l], lambda_q2[l], lambda_k2[l]])
    subg_col = subln_g[l][:, None]
    sink_rows = jnp.broadcast_to((sink[l] * LOG2E)[:, None], (B_HEADS, LANES))
    head_sq = lambda g, scale: 1.02 * HEAD_DIM * jnp.max(jnp.square(g[l])) * scale * scale
    bounds = jnp.stack([head_sq(qn_a, Q_SCALE), head_sq(kn_a, 1.0), head_sq(qn_b, Q_SCALE),
                        head_sq(kn_b, 1.0), jnp.max(jnp.abs(sink[l])) * LOG2E]).astype(F32)
    gsum = _group_sum_matrix()
    tm = 512

    xp = x_prompt.reshape(batch * seq, D_MODEL)
    ctx_row = lambda i: dec_batch
    qat, qbt, ka, kb, vat, vbt, kat_f, va_f, kbt_f = _qkv_proj(
        x_prompt, mod3, ctx_row, g1, w_k, w_t, gsum, kgains, qgains, None, tm, True)
    oa, ob = _ctx_attn(qat, qbt, ka, kb, vat, vbt, lam_vecs, subg_col, sink_rows, bounds, 2)
    xp1 = _merge(xp, oa, ob, mod3, ctx_row, g1, w_g, w_oa_b, w_ob_b, w_out_b, tm)
    y_prompt = _ffn(xp1, mod3, ctx_row, g2, w_gate_b, w_up_b, w_down_b, tm)
    new_diff_k = kat_f.reshape(batch, A_HEADS, 2, HEAD_DIM, seq).transpose(0, 4, 1, 2, 3)[:, None]
    new_diff_v = va_f.reshape(batch, 1, seq, A_HEADS, 2 * HEAD_DIM)
    new_win_k = kbt_f.reshape(batch, B_KV_HEADS, HEAD_DIM, seq).transpose(0, 3, 1, 2)[:, None]
    new_win_v = vbt.reshape(batch, B_KV_HEADS, HEAD_DIM, seq).transpose(0, 3, 1, 2)[:, None]

    xs = x_sample.reshape(dec_batch * dec_seq, D_MODEL)
    lat_row = lambda i: i // (dec_seq // tm)
    qat, qbt, ka, kb, vat, vbt = _qkv_proj(
        x_sample, mod3, lat_row, g1, w_k, w_t, gsum, kgains, qgains, _rope_tables(dec_seq), tm, False)
    ckt_a = cache_diff_k[:, l].transpose(0, 2, 3, 4, 1).reshape(dec_batch, A_W, past)
    cv_a = cache_diff_v[:, l].reshape(dec_batch, past * A_HEADS, 2 * HEAD_DIM)
    ckt_b = cache_win_k[:, l].transpose(0, 2, 3, 1).reshape(dec_batch, B_KVW, past)
    cvt_b = cache_win_v[:, l].transpose(0, 2, 3, 1).reshape(dec_batch, B_KVW, past)
    oa = _diff_lat(qat, ka.reshape(dec_batch, dec_seq, A_W), vat, ckt_a, cv_a, lam_vecs, subg_col, bounds,
                   dec_seq)
    ob = _win_lat(qbt, kb.reshape(dec_batch, dec_seq, B_KVW), vbt, ckt_b, cvt_b, sink_rows, bounds)
    xs1 = _merge(xs, oa.reshape(-1, A_W), ob.reshape(-1, B_Q), mod3, lat_row, g1,
                 w_g, w_oa_b, w_ob_b, w_out_b, tm)
    y_sample = _ffn(xs1, mod3, lat_row, g2, w_gate_b, w_up_b, w_down_b, tm)

    return (y_prompt.reshape(batch, seq, D_MODEL),
            y_sample.reshape(dec_batch, dec_seq, D_MODEL),
            new_diff_k, new_diff_v, new_win_k, new_win_v)
```

```python
import functools
import math

import jax
import jax.numpy as jnp
import numpy as np
from jax import lax
from jax.experimental import pallas as pl
from jax.experimental.pallas import tpu as pltpu

D_MODEL = 1024
HEAD_DIM = 64
A_HEADS = 8
B_HEADS = 16
B_KV_HEADS = 4
B_GROUP = B_HEADS // B_KV_HEADS
A_W = A_HEADS * 2 * HEAD_DIM
B_Q = B_HEADS * HEAD_DIM
B_KVW = B_KV_HEADS * HEAD_DIM
N_K = A_W + B_KVW
N_T = A_W + B_Q + A_W + B_KVW
WINDOW = 128
BLOCK = 128
GRID_W = 64
D_FF = 2816
N_MOD = 6
EPS = 1e-6
ROPE_BASE = 10000.0
NEG = -1e30
LAMBDA_INIT = 0.8 - 0.6 * math.exp(-0.3 * 0)
LOG2E = math.log2(math.e)
Q_SCALE = HEAD_DIM ** -0.5 * LOG2E

LANES = 128
CHUNK = 256
ONES_ROWS = 16
KEY_CHUNK = 512
SHIFT_FREE_LOG2 = 60.0
VMEM_LIMIT = 56 * 1024 * 1024

F32 = jnp.float32
BF16 = jnp.bfloat16


def _params(n_axes, flags=None):
    return pltpu.CompilerParams(dimension_semantics=("arbitrary",) * n_axes,
                                vmem_limit_bytes=VMEM_LIMIT, flags=flags)


def _const_spec(shape):
    nd = len(shape)
    return pl.BlockSpec(shape, lambda *_: (0,) * nd, pipeline_mode=pl.Buffered(1))


def _dot(a, b):
    return jnp.dot(a, b, preferred_element_type=F32)


def _modulated_norm(x, gain, scale, shift):
    y = x * lax.rsqrt(jnp.mean(x * x, axis=-1, keepdims=True) + EPS) * gain
    return y * (1.0 + scale) + shift


def _software_pipeline(n, stages):
    depth = len(stages)
    state = [None] * n
    for step in range(n + depth - 1):
        for k, stage in enumerate(stages):
            i = step - k
            if 0 <= i < n:
                state[i] = stage(i, state[i])


def _mod_kernel(c_ref, w_ref, b_ref, o_ref):
    cc = c_ref[...]
    s = cc * jax.nn.sigmoid(cc)
    o_ref[...] = _dot(s.astype(BF16), w_ref[...].astype(BF16)) + b_ref[...]


def _modulation(cc, w_ada, b_ada):
    rows, n = cc.shape[0], w_ada.shape[1]
    tn = n // 4
    return pl.pallas_call(
        _mod_kernel,
        grid=(n // tn,),
        in_specs=[pl.BlockSpec((rows, D_MODEL), lambda j: (0, 0)),
                  pl.BlockSpec((D_MODEL, tn), lambda j: (0, j)),
                  pl.BlockSpec((1, tn), lambda j: (0, j))],
        out_specs=pl.BlockSpec((rows, tn), lambda j: (0, j)),
        out_shape=jax.ShapeDtypeStruct((rows, n), F32),
        compiler_params=_params(1),
        name="adaln_mod",
    )(cc, w_ada, b_ada)


def _qkv_kernel(*refs, rope, emit_new_kv):
    x_ref, mod_ref, g1_ref, wk_ref, wt_ref, gsum_ref, kgain_ref, qgain_ref = refs[:8]
    refs = refs[8:]
    if rope:
        cos_ref, sa_ref, sb_ref, cost_ref, sint_ref = refs[:5]
        refs = refs[5:]
    qat_ref, qbt_ref, ka_ref, kb_ref, vat_ref, vbt_ref = refs[:6]
    if emit_new_kv:
        kat_ref, va_ref, kbt_ref = refs[6:]

    mod = mod_ref[...]
    h = _modulated_norm(x_ref[...], g1_ref[...], mod[:, D_MODEL:2 * D_MODEL], mod[:, 0:D_MODEL])
    hb = h.astype(BF16)
    gsum = gsum_ref[...]

    def put_feat(ref, rows, val):
        n_seq, _, width = ref.shape
        for s in range(n_seq):
            ref[s, rows, :] = val[:, s * width:(s + 1) * width].astype(ref.dtype)

    n_wide = A_W // CHUNK
    key_cols = [j * CHUNK for j in range(n_wide)] + [A_W]

    def key_project(j, _):
        return _dot(hb, wk_ref[:, key_cols[j]:key_cols[j] + CHUNK])

    def key_squares(j, p):
        return p, _dot((p * p).astype(BF16), gsum)

    def key_finish(j, st):
        p, ss = st
        row = 0 if j < n_wide else 1
        p = p * lax.rsqrt(ss * (1.0 / HEAD_DIM) + EPS) * kgain_ref[row:row + 1, :]
        if rope:
            cos, sa, sb = cos_ref[...], sa_ref[...], sb_ref[...]
            halves = []
            for i in range(CHUNK // LANES):
                xh = p[:, i * LANES:(i + 1) * LANES]
                nxt = pltpu.roll(xh, LANES - HEAD_DIM // 4, 1)
                prv = pltpu.roll(xh, HEAD_DIM // 4, 1)
                halves.append(xh * cos + nxt * sa + prv * sb)
            p = jnp.concatenate(halves, axis=1)
        if j < n_wide:
            sl = slice(j * CHUNK, (j + 1) * CHUNK)
            ka_ref[:, sl] = p.astype(BF16)
            if emit_new_kv:
                put_feat(kat_ref, sl, p.T)
        else:
            kb_ref[...] = p.astype(BF16)
            if emit_new_kv:
                put_feat(kbt_ref, slice(None), p.T)

    _software_pipeline(len(key_cols), [key_project, key_squares, key_finish])

    ht = h.T.astype(BF16)
    n_feat = N_T // CHUNK

    def feat_project(c, _):
        return _dot(wt_ref[c * CHUNK:(c + 1) * CHUNK, :], ht)

    def feat_finish(c, p):
        kind, j = divmod(c, n_wide)
        sl = slice(j * CHUNK, (j + 1) * CHUNK)
        if kind == 2:
            put_feat(vat_ref, sl, p)
            if emit_new_kv:
                va_ref[:, sl] = p.T
            return
        if kind == 3:
            put_feat(vbt_ref, slice(None), p)
            return
        gain = qgain_ref[kind]
        q4 = HEAD_DIM // 4
        heads = []
        for r in range(CHUNK // HEAD_DIM):
            x = p[r * HEAD_DIM:(r + 1) * HEAD_DIM, :]
            x = x * lax.rsqrt(jnp.mean(x * x, axis=0, keepdims=True) + EPS) * gain
            if rope:
                swapped = jnp.concatenate([x[q4:2 * q4], x[0:q4], x[3 * q4:4 * q4], x[2 * q4:3 * q4]],
                                          axis=0)
                x = x * cost_ref[...] + swapped * sint_ref[...]
            heads.append(x)
        put_feat(qat_ref if kind == 0 else qbt_ref, sl, jnp.concatenate(heads, axis=0) * Q_SCALE)

    _software_pipeline(n_feat, [feat_project, feat_finish])


def _qkv_proj(x3d, mod3, mod_row, g1, w_k, w_t, gsum, kgains, qgains, rope_tabs, tm, emit_new_kv):
    b, t, _ = x3d.shape
    rows = b * t
    rope = rope_tabs is not None
    tps = max(t // tm, 1)
    spt = max(tm // t, 1)
    tw = tm // spt
    in_specs = [pl.BlockSpec((tm, D_MODEL), lambda i: (i, 0)),
                pl.BlockSpec((None, 1, N_MOD * D_MODEL), lambda i: (mod_row(i), 0, 0)),
                _const_spec((1, D_MODEL)),
                _const_spec((D_MODEL, N_K)),
                _const_spec((N_T, D_MODEL)),
                _const_spec((CHUNK, CHUNK)),
                _const_spec((2, CHUNK)),
                _const_spec((2, HEAD_DIM, 1))]
    args = [x3d.reshape(rows, D_MODEL), mod3, g1, w_k, w_t, gsum, kgains, qgains]
    if rope:
        nat, featm = rope_tabs
        in_specs += [pl.BlockSpec((tm, LANES), lambda i: (i % tps, 0))] * 3
        in_specs += [pl.BlockSpec((HEAD_DIM, tm), lambda i: (0, i % tps))] * 2
        args += list(nat) + list(featm)
    tok_wide = pl.BlockSpec((tm, A_W), lambda i: (i, 0))
    tok_narrow = pl.BlockSpec((tm, B_KVW), lambda i: (i, 0))
    feat_wide = pl.BlockSpec((spt, A_W, tw), lambda i: (i // tps, 0, i % tps))
    feat_narrow = pl.BlockSpec((spt, B_KVW, tw), lambda i: (i // tps, 0, i % tps))
    out_specs = [feat_wide, feat_wide, tok_wide, tok_narrow, feat_wide, feat_narrow]
    out_shape = [jax.ShapeDtypeStruct((b, A_W, t), BF16),
                 jax.ShapeDtypeStruct((b, B_Q, t), BF16),
                 jax.ShapeDtypeStruct((rows, A_W), BF16),
                 jax.ShapeDtypeStruct((rows, B_KVW), BF16),
                 jax.ShapeDtypeStruct((b, A_W, t), BF16),
                 jax.ShapeDtypeStruct((b, B_KVW, t), F32 if emit_new_kv else BF16)]
    if emit_new_kv:
        out_specs += [feat_wide, tok_wide, feat_narrow]
        out_shape += [jax.ShapeDtypeStruct((b, A_W, t), F32),
                      jax.ShapeDtypeStruct((rows, A_W), F32),
                      jax.ShapeDtypeStruct((b, B_KVW, t), F32)]
    return pl.pallas_call(
        functools.partial(_qkv_kernel, rope=rope, emit_new_kv=emit_new_kv),
        grid=(rows // tm,),
        in_specs=in_specs,
        out_specs=out_specs,
        out_shape=out_shape,
        compiler_params=_params(1),
        name="qkv_rope" if rope else "qkv_ctx",
    )(*args)


def _lambda(lam_ref):
    lv = lam_ref[...]
    t1 = jnp.sum(lv[0:1] * lv[1:2], axis=-1, keepdims=True)
    t2 = jnp.sum(lv[2:3] * lv[3:4], axis=-1, keepdims=True)
    return jnp.exp(t1) - jnp.exp(t2) + LAMBDA_INIT


def _with_ones(vt):
    return jnp.concatenate([vt, jnp.ones((ONES_ROWS, vt.shape[1]), vt.dtype)], axis=0)


def _column_softmax(scores, extra=None):
    m = functools.reduce(jnp.maximum, [jnp.max(s, axis=0, keepdims=True) for s in scores])
    if extra is not None:
        m = jnp.maximum(m, extra)
    return [jnp.exp2(s - m).astype(BF16) for s in scores], m


def _max_row_sq_norm(x):
    x = x.astype(F32)
    return jnp.max(jnp.sum(x * x, axis=1, keepdims=True), axis=0, keepdims=True)[0, 0]


def _scores_bounded(q_sq, k_sq, sink_abs=0.0):
    limit = SHIFT_FREE_LOG2
    return jnp.logical_and(q_sq * k_sq <= limit * limit, sink_abs <= limit)


def _diff_query_cols(qt):
    zero = jnp.zeros((HEAD_DIM, qt.shape[1]), qt.dtype)
    return jnp.concatenate([jnp.concatenate([qt[:HEAD_DIM], zero], axis=0),
                            jnp.concatenate([zero, qt[HEAD_DIM:]], axis=0)], axis=1)


def _diff_output(r, lam, subg_col):
    n = r.shape[1] // 2
    dv = 2 * HEAD_DIM
    tot = r[dv:dv + 1, :]
    o = r[:dv, :n] * (1.0 / tot[:, :n]) - r[:dv, n:] * (lam / tot[:, n:])
    o = o * lax.rsqrt(jnp.mean(o * o, axis=0, keepdims=True) + EPS)
    return (o * subg_col * (1.0 - LAMBDA_INIT)).T


def _group_output(r, m, sink_row):
    n = r.shape[1] // B_GROUP
    tot = r[HEAD_DIM:HEAD_DIM + 1, :] + jnp.exp2(sink_row - m)
    o = r[:HEAD_DIM, :] * (1.0 / tot)
    return jnp.concatenate([o[:, i * n:(i + 1) * n] for i in range(B_GROUP)], axis=0).T


def _sink_row(sink_ref, g, n):
    reps = n // LANES
    return jnp.concatenate([sink_ref[pl.ds(g * B_GROUP + r, 1), :] for r in range(B_GROUP)
                            for _ in range(reps)], axis=1)


def _ctx_attn_kernel(qat_ref, qbt_ref, ka_ref, kb_ref, vat_ref, vbt_ref, lam_ref, subg_ref, sink_ref,
                     bounds_ref, oa_ref, ob_ref):
    lam = _lambda(lam_ref)
    subg_col = subg_ref[...]
    n_seq, _, n = qat_ref.shape
    chains = A_HEADS + B_KV_HEADS

    def scores(item, _):
        s, c = divmod(item, chains)
        rows = slice(s * n, (s + 1) * n)
        if c < A_HEADS:
            sl = slice(c * LANES, (c + 1) * LANES)
            return _dot(ka_ref[rows, sl], _diff_query_cols(qat_ref[s, sl, :]))
        g = c - A_HEADS
        qs = jnp.concatenate([qbt_ref[s, (g * B_GROUP + r) * HEAD_DIM:(g * B_GROUP + r + 1) * HEAD_DIM, :]
                              for r in range(B_GROUP)], axis=1)
        zero = jnp.zeros_like(qs)
        w = jnp.concatenate([qs if i == g else zero for i in range(B_KV_HEADS)], axis=0)
        return _dot(kb_ref[rows, :], w)

    def probs(shift, item, sc):
        c = item % chains
        if not shift:
            return [jnp.exp2(sc).astype(BF16)], jnp.zeros((1, sc.shape[1]), F32)
        if c < A_HEADS:
            return _column_softmax([sc])
        return _column_softmax([sc], extra=_sink_row(sink_ref, c - A_HEADS, n))

    def outputs(item, st):
        s, c = divmod(item, chains)
        rows = slice(s * n, (s + 1) * n)
        es, m = st
        if c < A_HEADS:
            sl = slice(c * LANES, (c + 1) * LANES)
            r = _dot(_with_ones(vat_ref[s, sl, :]), es[0])
            oa_ref[rows, sl] = _diff_output(r, lam, subg_col).astype(BF16)
        else:
            g = c - A_HEADS
            vt = vbt_ref[s, g * HEAD_DIM:(g + 1) * HEAD_DIM, :].astype(BF16)
            r = _dot(_with_ones(vt), es[0])
            o = _group_output(r, m, _sink_row(sink_ref, g, n))
            ob_ref[rows, g * CHUNK:(g + 1) * CHUNK] = o.astype(BF16)

    bounded = jnp.logical_and(_scores_bounded(bounds_ref[0], bounds_ref[1]),
                              _scores_bounded(bounds_ref[2], bounds_ref[3], bounds_ref[4]))

    @pl.when(bounded)
    def _():
        _software_pipeline(n_seq * chains, [scores, functools.partial(probs, False), outputs])

    @pl.when(jnp.logical_not(bounded))
    def _():
        _software_pipeline(n_seq * chains, [scores, functools.partial(probs, True), outputs])


def _ctx_attn(qat, qbt, ka, kb, vat, vbt, lam_vecs, subg_col, sink_rows, bounds, n_seq):
    b, _, t = qat.shape
    feat = pl.BlockSpec((n_seq, A_W, t), lambda i: (i, 0, 0))
    tok = pl.BlockSpec((n_seq * t, A_W), lambda i: (i, 0))
    return pl.pallas_call(
        _ctx_attn_kernel,
        grid=(b // n_seq,),
        in_specs=[feat, feat, tok, pl.BlockSpec((n_seq * t, B_KVW), lambda i: (i, 0)), feat,
                  pl.BlockSpec((n_seq, B_KVW, t), lambda i: (i, 0, 0)),
                  _const_spec((4, HEAD_DIM)), _const_spec((2 * HEAD_DIM, 1)),
                  _const_spec((B_HEADS, LANES)), pl.BlockSpec(memory_space=pltpu.SMEM)],
        out_specs=[tok, tok],
        out_shape=[jax.ShapeDtypeStruct((b * t, A_W), BF16), jax.ShapeDtypeStruct((b * t, B_Q), BF16)],
        compiler_params=_params(1),
        name="ctx_attn",
    )(qat, qbt, ka, kb, vat, vbt, lam_vecs, subg_col, sink_rows, bounds)


def _diff_lat_kernel(qt_ref, k_ref, vt_ref, ckt_ref, cv_ref, lam_ref, subg_ref, bounds_ref, o_ref):
    lam = _lambda(lam_ref)
    subg_col = subg_ref[...]
    k = k_ref[...]
    ck = ckt_ref[...].T.astype(BF16)
    vt = _with_ones(vt_ref[...])
    past = cv_ref.shape[0] // A_HEADS
    cv = cv_ref[pl.ds(pl.program_id(1), past, stride=A_HEADS), :]
    cvt = _with_ones(cv.T.astype(BF16))

    def scores(i, _):
        q = _diff_query_cols(qt_ref[:, i * BLOCK:(i + 1) * BLOCK])
        return [_dot(k, q), _dot(ck, q)]

    def outputs(shift, i, s):
        if shift:
            m = jnp.maximum(jnp.max(s[0], axis=0, keepdims=True), jnp.max(s[1], axis=0, keepdims=True))
        r = None
        for sc, v in ((s[0], vt), (s[1], cvt)):
            for c in range(0, sc.shape[0], KEY_CHUNK):
                piece = sc[c:c + KEY_CHUNK]
                part = _dot(v[:, c:c + KEY_CHUNK], jnp.exp2(piece - m if shift else piece).astype(BF16))
                r = part if r is None else r + part
        o_ref[i * BLOCK:(i + 1) * BLOCK, :] = _diff_output(r, lam, subg_col).astype(BF16)

    n_sub = qt_ref.shape[1] // BLOCK
    bounded = _scores_bounded(bounds_ref[0], jnp.maximum(bounds_ref[1], _max_row_sq_norm(ck)))

    @pl.when(bounded)
    def _():
        _software_pipeline(n_sub, [scores, functools.partial(outputs, False)])

    @pl.when(jnp.logical_not(bounded))
    def _():
        _software_pipeline(n_sub, [scores, functools.partial(outputs, True)])


def _diff_lat(qt, k, vt, ckt, cv, lam_vecs, subg_col, bounds, tq):
    b, _, t = qt.shape
    past = ckt.shape[2]
    return pl.pallas_call(
        _diff_lat_kernel,
        grid=(b, A_HEADS, t // tq),
        in_specs=[pl.BlockSpec((None, LANES, tq), lambda i, h, j: (i, h, j)),
                  pl.BlockSpec((None, t, LANES), lambda i, h, j: (i, 0, h)),
                  pl.BlockSpec((None, LANES, t), lambda i, h, j: (i, h, 0)),
                  pl.BlockSpec((None, LANES, past), lambda i, h, j: (i, h, 0)),
                  pl.BlockSpec((None, past * A_HEADS, LANES), lambda i, h, j: (i, 0, 0)),
                  _const_spec((4, HEAD_DIM)), _const_spec((2 * HEAD_DIM, 1)),
                  pl.BlockSpec(memory_space=pltpu.SMEM)],
        out_specs=pl.BlockSpec((None, tq, LANES), lambda i, h, j: (i, j, h)),
        out_shape=jax.ShapeDtypeStruct((b, t, A_W), BF16),
        compiler_params=_params(3),
        name="diff_lat",
    )(qt, k, vt, ckt, cv, lam_vecs, subg_col, bounds)


def _win_lat_kernel(qt_ref, k_ref, vt_ref, ckt_ref, cvt_ref, sink_ref, bounds_ref, o_ref):
    g = pl.program_id(1)
    t = k_ref.shape[0]
    span = 3 * BLOCK
    keep = lax.broadcasted_iota(jnp.int32, (1, CHUNK), 1) // HEAD_DIM == g
    k = jnp.where(keep, k_ref[...].astype(F32), 0.0).astype(BF16)
    ck = jnp.where(keep, ckt_ref[...].T, 0.0).astype(BF16)
    vt = vt_ref[...]
    cvt = _with_ones(cvt_ref[...].astype(BF16))
    sink_row = _sink_row(sink_ref, g, BLOCK)
    rel = (lax.broadcasted_iota(jnp.int32, (span, B_GROUP * BLOCK), 0)
           - lax.broadcasted_iota(jnp.int32, (span, B_GROUP * BLOCK), 1) % BLOCK)

    def window(i):
        return min(max((i - 1) * BLOCK, 0), t - span)

    bias = {off: jnp.where(jnp.abs(rel + off) <= WINDOW, 0.0, NEG)
            for off in sorted({window(i) - i * BLOCK for i in range(t // BLOCK)})}

    def scores(i, _):
        w = window(i)
        qs = jnp.concatenate([qt_ref[r * HEAD_DIM:(r + 1) * HEAD_DIM, i * BLOCK:(i + 1) * BLOCK]
                              for r in range(B_GROUP)], axis=1)
        wq = jnp.concatenate([qs] * B_KV_HEADS, axis=0)
        return [_dot(k[w:w + span], wq) + bias[w - i * BLOCK], _dot(ck, wq)]

    def probs(shift, i, s):
        if shift:
            return _column_softmax(s, extra=sink_row)
        return [jnp.exp2(piece).astype(BF16) for piece in s], jnp.zeros_like(sink_row)

    def outputs(i, st):
        es, m = st
        w = window(i)
        r = _dot(_with_ones(vt[:, w:w + span]), es[0]) + _dot(cvt, es[1])
        o_ref[i * BLOCK:(i + 1) * BLOCK, :] = _group_output(r, m, sink_row).astype(BF16)

    bounded = _scores_bounded(bounds_ref[2], jnp.maximum(bounds_ref[3], _max_row_sq_norm(ck)), bounds_ref[4])

    @pl.when(bounded)
    def _():
        _software_pipeline(t // BLOCK, [scores, functools.partial(probs, False), outputs])

    @pl.when(jnp.logical_not(bounded))
    def _():
        _software_pipeline(t // BLOCK, [scores, functools.partial(probs, True), outputs])


def _win_lat(qt, k, vt, ckt, cvt, sink_rows, bounds):
    b, _, t = qt.shape
    past = ckt.shape[2]
    return pl.pallas_call(
        _win_lat_kernel,
        grid=(b, B_KV_HEADS),
        in_specs=[pl.BlockSpec((None, CHUNK, t), lambda i, g: (i, g, 0)),
                  pl.BlockSpec((None, t, B_KVW), lambda i, g: (i, 0, 0)),
                  pl.BlockSpec((None, HEAD_DIM, t), lambda i, g: (i, g, 0)),
                  pl.BlockSpec((None, B_KVW, past), lambda i, g: (i, 0, 0)),
                  pl.BlockSpec((None, HEAD_DIM, past), lambda i, g: (i, g, 0)),
                  _const_spec((B_HEADS, LANES)), pl.BlockSpec(memory_space=pltpu.SMEM)],
        out_specs=pl.BlockSpec((None, t, CHUNK), lambda i, g: (i, 0, g)),
        out_shape=jax.ShapeDtypeStruct((b, t, B_Q), BF16),
        compiler_params=_params(2),
        name="win_lat",
    )(qt, k, vt, ckt, cvt, sink_rows, bounds)


def _merge_kernel(x1_ref, x2_ref, oa1_ref, oa2_ref, ob1_ref, ob2_ref, mod_ref, g1_ref, wg_ref, woa_ref,
                  wob_ref, wout_ref, o1_ref, o2_ref, *, n_first):
    def tile(x_ref, oa_ref, ob_ref, o_ref):
        x = x_ref[...]
        mod = mod_ref[...]
        h = _modulated_norm(x, g1_ref[...], mod[:, D_MODEL:2 * D_MODEL], mod[:, 0:D_MODEL]).astype(BF16)
        ga = _dot(h, wg_ref[:, 0:D_MODEL])
        gb = _dot(h, wg_ref[:, D_MODEL:2 * D_MODEL])
        merged = (jax.nn.sigmoid(ga) * _dot(oa_ref[...], woa_ref[...])
                  + jax.nn.sigmoid(gb) * _dot(ob_ref[...], wob_ref[...]))
        o_ref[...] = x + mod[:, 2 * D_MODEL:3 * D_MODEL] * _dot(merged.astype(BF16), wout_ref[...])

    _per_token_set(n_first, functools.partial(tile, x1_ref, oa1_ref, ob1_ref, o1_ref),
                   functools.partial(tile, x2_ref, oa2_ref, ob2_ref, o2_ref))


def _per_token_set(n_first, first_body, second_body):
    first = pl.program_id(0) < n_first
    pl.when(first)(first_body)
    pl.when(jnp.logical_not(first))(second_body)


def _split_tiles(n_first, tm, width):
    return (pl.BlockSpec((tm, width), lambda i: (jnp.minimum(i, n_first - 1), 0)),
            pl.BlockSpec((tm, width), lambda i: (jnp.maximum(i - n_first, 0), 0)))


def _merge(x1, x2, oa1, oa2, ob1, ob2, mod3, mod_row, g1, w_g, w_oa, w_ob, w_out, tm):
    n_first, n_second = x1.shape[0] // tm, x2.shape[0] // tm
    t1, t2 = _split_tiles(n_first, tm, D_MODEL)
    sq = _const_spec((D_MODEL, D_MODEL))
    return pl.pallas_call(
        functools.partial(_merge_kernel, n_first=n_first),
        grid=(n_first + n_second,),
        in_specs=[t1, t2, t1, t2, t1, t2,
                  pl.BlockSpec((None, 1, N_MOD * D_MODEL), lambda i: (mod_row(i), 0, 0)),
                  _const_spec((1, D_MODEL)), _const_spec((D_MODEL, 2 * D_MODEL)), sq, sq, sq],
        out_specs=[t1, t2],
        out_shape=[jax.ShapeDtypeStruct(x1.shape, F32), jax.ShapeDtypeStruct(x2.shape, F32)],
        compiler_params=_params(1),
        name="merge_out",
    )(x1, x2, oa1, oa2, ob1, ob2, mod3, g1, w_g, w_oa, w_ob, w_out)


def _ffn_kernel(x1_ref, x2_ref, mod_ref, g2_ref, wgate_ref, wup_ref, wdown_ref, o1_ref, o2_ref, *, n_first):
    def tile(x_ref, o_ref):
        x = x_ref[...]
        mod = mod_ref[...]
        h = _modulated_norm(x, g2_ref[...], mod[:, 4 * D_MODEL:5 * D_MODEL],
                            mod[:, 3 * D_MODEL:4 * D_MODEL]).astype(BF16)
        gate = _dot(h, wgate_ref[...])
        act = (gate * jax.nn.sigmoid(gate) * _dot(h, wup_ref[...])).astype(BF16)
        o_ref[...] = x + mod[:, 5 * D_MODEL:6 * D_MODEL] * _dot(act, wdown_ref[...])

    _per_token_set(n_first, functools.partial(tile, x1_ref, o1_ref), functools.partial(tile, x2_ref, o2_ref))


def _ffn(x1, x2, mod3, mod_row, g2, w_gate, w_up, w_down, tm):
    n_first, n_second = x1.shape[0] // tm, x2.shape[0] // tm
    t1, t2 = _split_tiles(n_first, tm, D_MODEL)
    return pl.pallas_call(
        functools.partial(_ffn_kernel, n_first=n_first),
        grid=(n_first + n_second,),
        in_specs=[t1, t2,
                  pl.BlockSpec((None, 1, N_MOD * D_MODEL), lambda i: (mod_row(i), 0, 0)),
                  _const_spec((1, D_MODEL)), _const_spec((D_MODEL, D_FF)),
                  _const_spec((D_MODEL, D_FF)), _const_spec((D_FF, D_MODEL))],
        out_specs=[t1, t2],
        out_shape=[jax.ShapeDtypeStruct(x1.shape, F32), jax.ShapeDtypeStruct(x2.shape, F32)],
        compiler_params=_params(1),
        name="swiglu_ffn",
    )(x1, x2, mod3, g2, w_gate, w_up, w_down)


def _rope_tables(seq_len):
    quarter = HEAD_DIM // 4
    t = np.arange(seq_len)
    row = (t // GRID_W).astype(np.float32)
    col = (t % GRID_W).astype(np.float32)
    freqs = (np.float32(ROPE_BASE) ** (-np.arange(quarter, dtype=np.float32) / np.float32(quarter))
             ).astype(np.float32)
    ang_r = row[:, None] * freqs
    ang_c = col[:, None] * freqs
    zeros = np.zeros_like(ang_r)
    cos = np.concatenate([np.cos(ang_r)] * 2 + [np.cos(ang_c)] * 2, axis=-1)
    sin_next = np.concatenate([-np.sin(ang_r), zeros, -np.sin(ang_c), zeros], axis=-1)
    sin_prev = np.concatenate([zeros, np.sin(ang_r), zeros, np.sin(ang_c)], axis=-1)
    rep = LANES // HEAD_DIM
    nat = tuple(jnp.asarray(np.tile(a, (1, rep)), dtype=F32) for a in (cos, sin_next, sin_prev))
    featm = (jnp.asarray(cos.T, dtype=F32), jnp.asarray((sin_next + sin_prev).T, dtype=F32))
    return nat, featm


def _group_sum_matrix():
    idx = np.arange(CHUNK) // HEAD_DIM
    return jnp.asarray(idx[:, None] == idx[None, :], dtype=BF16)


def kernel(x_prompt, x_sample, cache_diff_k, cache_diff_v, cache_win_k, cache_win_v, c, c_ctx,
           w_ada, b_ada, norm1_g, w_in, qn_a, kn_a, lambda_q1, lambda_k1, lambda_q2, lambda_k2,
           subln_g, qn_b, kn_b, sink, w_oa, w_ob, w_out, norm2_g, w_gate, w_up, w_down):
    batch, seq, _ = x_prompt.shape
    dec_batch, dec_seq, _ = x_sample.shape
    past = cache_diff_k.shape[2]
    l = 0

    cc = jnp.zeros((8, D_MODEL), F32).at[:dec_batch].set(c).at[dec_batch].set(c_ctx)
    mod3 = _modulation(cc, w_ada[l], b_ada[l][None, :]).reshape(8, 1, N_MOD * D_MODEL)

    w_in_l = w_in[l]
    o_ka, o_va, o_qb, o_kb, o_vb, o_g = A_W, 2 * A_W, 3 * A_W, 3 * A_W + B_Q, 3 * A_W + B_Q + B_KVW, \
        3 * A_W + B_Q + 2 * B_KVW
    w_k = jnp.concatenate([w_in_l[:, o_ka:o_va], w_in_l[:, o_kb:o_vb]], axis=1).astype(BF16)
    w_t = jnp.concatenate([w_in_l[:, :o_ka], w_in_l[:, o_qb:o_kb], w_in_l[:, o_va:o_qb],
                           w_in_l[:, o_vb:o_g]], axis=1).T.astype(BF16)
    w_g = w_in_l[:, o_g:].astype(BF16)
    w_oa_b, w_ob_b, w_out_b = w_oa[l].astype(BF16), w_ob[l].astype(BF16), w_out[l].astype(BF16)
    w_gate_b, w_up_b, w_down_b = w_gate[l].astype(BF16), w_up[l].astype(BF16), w_down[l].astype(BF16)
    g1 = norm1_g[l][None, :]
    g2 = norm2_g[l][None, :]
    rep = CHUNK // HEAD_DIM
    kgains = jnp.stack([jnp.tile(kn_a[l], rep), jnp.tile(kn_b[l], rep)])
    qgains = jnp.stack([qn_a[l], qn_b[l]])[:, :, None]
    lam_vecs = jnp.stack([lambda_q1[l], lambda_k1[l], lambda_q2[l], lambda_k2[l]])
    subg_col = subln_g[l][:, None]
    sink_rows = jnp.broadcast_to((sink[l] * LOG2E)[:, None], (B_HEADS, LANES))
    head_sq = lambda g, scale: 1.02 * HEAD_DIM * jnp.max(jnp.square(g[l])) * scale * scale
    bounds = jnp.stack([head_sq(qn_a, Q_SCALE), head_sq(kn_a, 1.0), head_sq(qn_b, Q_SCALE),
                        head_sq(kn_b, 1.0), jnp.max(jnp.abs(sink[l])) * LOG2E]).astype(F32)
    gsum = _group_sum_matrix()
    tm = 512

    xp = x_prompt.reshape(batch * seq, D_MODEL)
    ctx_row = lambda i: dec_batch
    qat, qbt, ka, kb, vat, vbt, kat_f, va_f, kbt_f = _qkv_proj(
        x_prompt, mod3, ctx_row, g1, w_k, w_t, gsum, kgains, qgains, None, tm, True)
    oa_p, ob_p = _ctx_attn(qat, qbt, ka, kb, vat, vbt, lam_vecs, subg_col, sink_rows, bounds, 2)
    new_diff_k = kat_f.reshape(batch, A_HEADS, 2, HEAD_DIM, seq).transpose(0, 4, 1, 2, 3)[:, None]
    new_diff_v = va_f.reshape(batch, 1, seq, A_HEADS, 2 * HEAD_DIM)
    new_win_k = kbt_f.reshape(batch, B_KV_HEADS, HEAD_DIM, seq).transpose(0, 3, 1, 2)[:, None]
    new_win_v = vbt.reshape(batch, B_KV_HEADS, HEAD_DIM, seq).transpose(0, 3, 1, 2)[:, None]

    xs = x_sample.reshape(dec_batch * dec_seq, D_MODEL)
    lat_row = lambda i: i // (dec_seq // tm)
    qat, qbt, ka, kb, vat, vbt = _qkv_proj(
        x_sample, mod3, lat_row, g1, w_k, w_t, gsum, kgains, qgains, _rope_tables(dec_seq), tm, False)
    ckt_a = cache_diff_k[:, l].transpose(0, 2, 3, 4, 1).reshape(dec_batch, A_W, past)
    cv_a = cache_diff_v[:, l].reshape(dec_batch, past * A_HEADS, 2 * HEAD_DIM)
    ckt_b = cache_win_k[:, l].transpose(0, 2, 3, 1).reshape(dec_batch, B_KVW, past)
    cvt_b = cache_win_v[:, l].transpose(0, 2, 3, 1).reshape(dec_batch, B_KVW, past)
    oa_s = _diff_lat(qat, ka.reshape(dec_batch, dec_seq, A_W), vat, ckt_a, cv_a, lam_vecs, subg_col, bounds,
                     dec_seq)
    ob_s = _win_lat(qbt, kb.reshape(dec_batch, dec_seq, B_KVW), vbt, ckt_b, cvt_b, sink_rows, bounds)

    n_ctx_tiles = batch * seq // tm
    both_row = lambda i: jnp.where(i < n_ctx_tiles, dec_batch, (i - n_ctx_tiles) // (dec_seq // tm))
    xp1, xs1 = _merge(xp, xs, oa_p, oa_s.reshape(-1, A_W), ob_p, ob_s.reshape(-1, B_Q), mod3, both_row, g1,
                      w_g, w_oa_b, w_ob_b, w_out_b, tm)
    y_prompt, y_sample = _ffn(xp1, xs1, mod3, both_row, g2, w_gate_b, w_up_b, w_down_b, tm)

    return (y_prompt.reshape(batch, seq, D_MODEL),
            y_sample.reshape(dec_batch, dec_seq, D_MODEL),
            new_diff_k, new_diff_v, new_win_k, new_win_v)
```

```python
import functools
import math

import jax
import jax.numpy as jnp
import numpy as np
from jax import lax
from jax.experimental import pallas as pl
from jax.experimental.pallas import tpu as pltpu

D_MODEL = 1024
HEAD_DIM = 64
A_HEADS = 8
B_HEADS = 16
B_KV_HEADS = 4
B_GROUP = B_HEADS // B_KV_HEADS
A_W = A_HEADS * 2 * HEAD_DIM
B_Q = B_HEADS * HEAD_DIM
B_KVW = B_KV_HEADS * HEAD_DIM
N_K = A_W + B_KVW
N_T = A_W + B_Q + A_W + B_KVW
GATE_COL = 3 * A_W + B_Q + 2 * B_KVW
WINDOW = 128
BLOCK = 128
GRID_W = 64
D_FF = 2816
N_MOD = 6
EPS = 1e-6
ROPE_BASE = 10000.0
NEG = -1e30
LAMBDA_INIT = 0.8 - 0.6 * math.exp(-0.3 * 0)
LOG2E = math.log2(math.e)
Q_SCALE = HEAD_DIM ** -0.5 * LOG2E

LANES = 128
CHUNK = 256
ONES_ROWS = 16
KEY_CHUNK = 512
ROW_SUBTILE = 256
SHIFT_FREE_LOG2 = 60.0
VMEM_LIMIT = 56 * 1024 * 1024

F32 = jnp.float32
BF16 = jnp.bfloat16


def _params(n_axes, flags=None):
    return pltpu.CompilerParams(dimension_semantics=("arbitrary",) * n_axes,
                                vmem_limit_bytes=VMEM_LIMIT, flags=flags)


def _const_spec(shape):
    nd = len(shape)
    return pl.BlockSpec(shape, lambda *_: (0,) * nd, pipeline_mode=pl.Buffered(1))


def _dot(a, b):
    return jnp.dot(a, b, preferred_element_type=F32)


def _modulated_norm(x, gain, scale, shift):
    y = x * lax.rsqrt(jnp.mean(x * x, axis=-1, keepdims=True) + EPS) * gain
    return y * (1.0 + scale) + shift


def _software_pipeline(n, stages):
    depth = len(stages)
    state = [None] * n
    for step in range(n + depth - 1):
        for k, stage in enumerate(stages):
            i = step - k
            if 0 <= i < n:
                state[i] = stage(i, state[i])


def _mod_kernel(c_ref, w_ref, b_ref, o_ref):
    cc = c_ref[...]
    s = cc * jax.nn.sigmoid(cc)
    o_ref[...] = _dot(s.astype(BF16), w_ref[...].astype(BF16)) + b_ref[...]


def _modulation(cc, w_ada, b_ada):
    rows, n = cc.shape[0], w_ada.shape[1]
    tn = n // 4
    return pl.pallas_call(
        _mod_kernel,
        grid=(n // tn,),
        in_specs=[pl.BlockSpec((rows, D_MODEL), lambda j: (0, 0)),
                  pl.BlockSpec((D_MODEL, tn), lambda j: (0, j)),
                  pl.BlockSpec((1, tn), lambda j: (0, j))],
        out_specs=pl.BlockSpec((rows, tn), lambda j: (0, j)),
        out_shape=jax.ShapeDtypeStruct((rows, n), F32),
        compiler_params=_params(1),
        name="adaln_mod",
    )(cc, w_ada, b_ada)


def _qkv_kernel(*refs, rope, emit_new_kv):
    x_ref, mod_ref, g1_ref, wk_ref, wt_ref, gsum_ref, kgain_ref, qgain_ref = refs[:8]
    refs = refs[8:]
    if rope:
        cos_ref, sa_ref, sb_ref, cost_ref, sint_ref = refs[:5]
        refs = refs[5:]
    qat_ref, qbt_ref, ka_ref, kb_ref, vat_ref, vbt_ref = refs[:6]
    if emit_new_kv:
        kat_ref, va_ref, kbt_ref = refs[6:]

    mod = mod_ref[...]
    h = _modulated_norm(x_ref[...], g1_ref[...], mod[:, D_MODEL:2 * D_MODEL], mod[:, 0:D_MODEL])
    hb = h.astype(BF16)
    gsum = gsum_ref[...]

    def put_feat(ref, rows, val):
        n_seq, _, width = ref.shape
        for s in range(n_seq):
            ref[s, rows, :] = val[:, s * width:(s + 1) * width].astype(ref.dtype)

    n_wide = A_W // CHUNK
    key_cols = [j * CHUNK for j in range(n_wide)] + [A_W]

    def key_project(j, _):
        return _dot(hb, wk_ref[:, key_cols[j]:key_cols[j] + CHUNK])

    def key_squares(j, p):
        return p, _dot((p * p).astype(BF16), gsum)

    def key_finish(j, st):
        p, ss = st
        row = 0 if j < n_wide else 1
        p = p * lax.rsqrt(ss * (1.0 / HEAD_DIM) + EPS) * kgain_ref[row:row + 1, :]
        if rope:
            cos, sa, sb = cos_ref[...], sa_ref[...], sb_ref[...]
            halves = []
            for i in range(CHUNK // LANES):
                xh = p[:, i * LANES:(i + 1) * LANES]
                nxt = pltpu.roll(xh, LANES - HEAD_DIM // 4, 1)
                prv = pltpu.roll(xh, HEAD_DIM // 4, 1)
                halves.append(xh * cos + nxt * sa + prv * sb)
            p = jnp.concatenate(halves, axis=1)
        if j < n_wide:
            sl = slice(j * CHUNK, (j + 1) * CHUNK)
            ka_ref[:, sl] = p.astype(BF16)
            if emit_new_kv:
                put_feat(kat_ref, sl, p.T)
        else:
            kb_ref[...] = p.astype(BF16)
            if emit_new_kv:
                put_feat(kbt_ref, slice(None), p.T)

    _software_pipeline(len(key_cols), [key_project, key_squares, key_finish])

    ht = h.T.astype(BF16)
    n_feat = N_T // CHUNK

    def feat_project(c, _):
        return _dot(wt_ref[c * CHUNK:(c + 1) * CHUNK, :], ht)

    def feat_finish(c, p):
        kind, j = divmod(c, n_wide)
        sl = slice(j * CHUNK, (j + 1) * CHUNK)
        if kind == 2:
            put_feat(vat_ref, sl, p)
            if emit_new_kv:
                va_ref[:, sl] = p.T
            return
        if kind == 3:
            put_feat(vbt_ref, slice(None), p)
            return
        gain = qgain_ref[kind]
        q4 = HEAD_DIM // 4
        heads = []
        for r in range(CHUNK // HEAD_DIM):
            x = p[r * HEAD_DIM:(r + 1) * HEAD_DIM, :]
            x = x * lax.rsqrt(jnp.mean(x * x, axis=0, keepdims=True) + EPS) * gain
            if rope:
                swapped = jnp.concatenate([x[q4:2 * q4], x[0:q4], x[3 * q4:4 * q4], x[2 * q4:3 * q4]],
                                          axis=0)
                x = x * cost_ref[...] + swapped * sint_ref[...]
            heads.append(x)
        put_feat(qat_ref if kind == 0 else qbt_ref, sl, jnp.concatenate(heads, axis=0) * Q_SCALE)

    _software_pipeline(n_feat, [feat_project, feat_finish])


def _qkv_proj(x3d, mod3, mod_row, g1, w_k, w_t, gsum, kgains, qgains, rope_tabs, tm, emit_new_kv):
    b, t, _ = x3d.shape
    rows = b * t
    rope = rope_tabs is not None
    tps = max(t // tm, 1)
    spt = max(tm // t, 1)
    tw = tm // spt
    in_specs = [pl.BlockSpec((tm, D_MODEL), lambda i: (i, 0)),
                pl.BlockSpec((None, 1, N_MOD * D_MODEL), lambda i: (mod_row(i), 0, 0)),
                _const_spec((1, D_MODEL)),
                _const_spec((D_MODEL, N_K)),
                _const_spec((N_T, D_MODEL)),
                _const_spec((CHUNK, CHUNK)),
                _const_spec((2, CHUNK)),
                _const_spec((2, HEAD_DIM, 1))]
    args = [x3d.reshape(rows, D_MODEL), mod3, g1, w_k, w_t, gsum, kgains, qgains]
    if rope:
        nat, featm = rope_tabs
        in_specs += [pl.BlockSpec((tm, LANES), lambda i: (i % tps, 0))] * 3
        in_specs += [pl.BlockSpec((HEAD_DIM, tm), lambda i: (0, i % tps))] * 2
        args += list(nat) + list(featm)
    tok_wide = pl.BlockSpec((tm, A_W), lambda i: (i, 0))
    tok_narrow = pl.BlockSpec((tm, B_KVW), lambda i: (i, 0))
    feat_wide = pl.BlockSpec((spt, A_W, tw), lambda i: (i // tps, 0, i % tps))
    feat_narrow = pl.BlockSpec((spt, B_KVW, tw), lambda i: (i // tps, 0, i % tps))
    out_specs = [feat_wide, feat_wide, tok_wide, tok_narrow, feat_wide, feat_narrow]
    out_shape = [jax.ShapeDtypeStruct((b, A_W, t), BF16),
                 jax.ShapeDtypeStruct((b, B_Q, t), BF16),
                 jax.ShapeDtypeStruct((rows, A_W), BF16),
                 jax.ShapeDtypeStruct((rows, B_KVW), BF16),
                 jax.ShapeDtypeStruct((b, A_W, t), BF16),
                 jax.ShapeDtypeStruct((b, B_KVW, t), F32 if emit_new_kv else BF16)]
    if emit_new_kv:
        out_specs += [feat_wide, tok_wide, feat_narrow]
        out_shape += [jax.ShapeDtypeStruct((b, A_W, t), F32),
                      jax.ShapeDtypeStruct((rows, A_W), F32),
                      jax.ShapeDtypeStruct((b, B_KVW, t), F32)]
    return pl.pallas_call(
        functools.partial(_qkv_kernel, rope=rope, emit_new_kv=emit_new_kv),
        grid=(rows // tm,),
        in_specs=in_specs,
        out_specs=out_specs,
        out_shape=out_shape,
        compiler_params=_params(1),
        name="qkv_rope" if rope else "qkv_ctx",
    )(*args)


def _lambda(lam_ref):
    lv = lam_ref[...]
    t1 = jnp.sum(lv[0:1] * lv[1:2], axis=-1, keepdims=True)
    t2 = jnp.sum(lv[2:3] * lv[3:4], axis=-1, keepdims=True)
    return jnp.exp(t1) - jnp.exp(t2) + LAMBDA_INIT


def _with_ones(vt):
    return jnp.concatenate([vt, jnp.ones((ONES_ROWS, vt.shape[1]), vt.dtype)], axis=0)


def _column_softmax(scores, extra=None):
    m = functools.reduce(jnp.maximum, [jnp.max(s, axis=0, keepdims=True) for s in scores])
    if extra is not None:
        m = jnp.maximum(m, extra)
    return [jnp.exp2(s - m).astype(BF16) for s in scores], m


def _max_row_sq_norm(x):
    x = x.astype(F32)
    return jnp.max(jnp.sum(x * x, axis=1, keepdims=True), axis=0, keepdims=True)[0, 0]


def _scores_bounded(q_sq, k_sq, sink_abs=0.0):
    limit = SHIFT_FREE_LOG2
    return jnp.logical_and(q_sq * k_sq <= limit * limit, sink_abs <= limit)


def _diff_query_cols(qt):
    zero = jnp.zeros((HEAD_DIM, qt.shape[1]), qt.dtype)
    return jnp.concatenate([jnp.concatenate([qt[:HEAD_DIM], zero], axis=0),
                            jnp.concatenate([zero, qt[HEAD_DIM:]], axis=0)], axis=1)


def _diff_output(r, lam, subg_col):
    n = r.shape[1] // 2
    dv = 2 * HEAD_DIM
    tot = r[dv:dv + 1, :]
    o = r[:dv, :n] * (1.0 / tot[:, :n]) - r[:dv, n:] * (lam / tot[:, n:])
    o = o * lax.rsqrt(jnp.mean(o * o, axis=0, keepdims=True) + EPS)
    return (o * subg_col * (1.0 - LAMBDA_INIT)).T


def _group_output(r, m, sink_row):
    n = r.shape[1] // B_GROUP
    tot = r[HEAD_DIM:HEAD_DIM + 1, :] + jnp.exp2(sink_row - m)
    o = r[:HEAD_DIM, :] * (1.0 / tot)
    return jnp.concatenate([o[:, i * n:(i + 1) * n] for i in range(B_GROUP)], axis=0).T


def _sink_row(sink_ref, g, n):
    reps = n // LANES
    return jnp.concatenate([sink_ref[pl.ds(g * B_GROUP + r, 1), :] for r in range(B_GROUP)
                            for _ in range(reps)], axis=1)


def _ctx_attn_kernel(qat_ref, qbt_ref, ka_ref, kb_ref, vat_ref, vbt_ref, lam_ref, subg_ref, sink_ref,
                     bounds_ref, oa_ref, ob_ref):
    lam = _lambda(lam_ref)
    subg_col = subg_ref[...]
    n_seq, _, n = qat_ref.shape
    chains = A_HEADS + B_KV_HEADS

    def scores(item, _):
        s, c = divmod(item, chains)
        rows = slice(s * n, (s + 1) * n)
        if c < A_HEADS:
            sl = slice(c * LANES, (c + 1) * LANES)
            return _dot(ka_ref[rows, sl], _diff_query_cols(qat_ref[s, sl, :]))
        g = c - A_HEADS
        qs = jnp.concatenate([qbt_ref[s, (g * B_GROUP + r) * HEAD_DIM:(g * B_GROUP + r + 1) * HEAD_DIM, :]
                              for r in range(B_GROUP)], axis=1)
        zero = jnp.zeros_like(qs)
        w = jnp.concatenate([qs if i == g else zero for i in range(B_KV_HEADS)], axis=0)
        return _dot(kb_ref[rows, :], w)

    def probs(shift, item, sc):
        c = item % chains
        if not shift:
            return [jnp.exp2(sc).astype(BF16)], jnp.zeros((1, sc.shape[1]), F32)
        if c < A_HEADS:
            return _column_softmax([sc])
        return _column_softmax([sc], extra=_sink_row(sink_ref, c - A_HEADS, n))

    def outputs(item, st):
        s, c = divmod(item, chains)
        rows = slice(s * n, (s + 1) * n)
        es, m = st
        if c < A_HEADS:
            sl = slice(c * LANES, (c + 1) * LANES)
            r = _dot(_with_ones(vat_ref[s, sl, :]), es[0])
            oa_ref[rows, sl] = _diff_output(r, lam, subg_col).astype(BF16)
        else:
            g = c - A_HEADS
            vt = vbt_ref[s, g * HEAD_DIM:(g + 1) * HEAD_DIM, :].astype(BF16)
            r = _dot(_with_ones(vt), es[0])
            o = _group_output(r, m, _sink_row(sink_ref, g, n))
            ob_ref[rows, g * CHUNK:(g + 1) * CHUNK] = o.astype(BF16)

    bounded = jnp.logical_and(_scores_bounded(bounds_ref[0], bounds_ref[1]),
                              _scores_bounded(bounds_ref[2], bounds_ref[3], bounds_ref[4]))

    @pl.when(bounded)
    def _():
        _software_pipeline(n_seq * chains, [scores, functools.partial(probs, False), outputs])

    @pl.when(jnp.logical_not(bounded))
    def _():
        _software_pipeline(n_seq * chains, [scores, functools.partial(probs, True), outputs])


def _ctx_attn(qat, qbt, ka, kb, vat, vbt, lam_vecs, subg_col, sink_rows, bounds, n_seq):
    b, _, t = qat.shape
    feat = pl.BlockSpec((n_seq, A_W, t), lambda i: (i, 0, 0))
    tok = pl.BlockSpec((n_seq * t, A_W), lambda i: (i, 0))
    return pl.pallas_call(
        _ctx_attn_kernel,
        grid=(b // n_seq,),
        in_specs=[feat, feat, tok, pl.BlockSpec((n_seq * t, B_KVW), lambda i: (i, 0)), feat,
                  pl.BlockSpec((n_seq, B_KVW, t), lambda i: (i, 0, 0)),
                  _const_spec((4, HEAD_DIM)), _const_spec((2 * HEAD_DIM, 1)),
                  _const_spec((B_HEADS, LANES)), pl.BlockSpec(memory_space=pltpu.SMEM)],
        out_specs=[tok, tok],
        out_shape=[jax.ShapeDtypeStruct((b * t, A_W), BF16), jax.ShapeDtypeStruct((b * t, B_Q), BF16)],
        compiler_params=_params(1),
        name="ctx_attn",
    )(qat, qbt, ka, kb, vat, vbt, lam_vecs, subg_col, sink_rows, bounds)


def _diff_lat_kernel(qt_ref, k_ref, vt_ref, ckt_ref, cv_ref, lam_ref, subg_ref, bounds_ref, o_ref):
    lam = _lambda(lam_ref)
    subg_col = subg_ref[...]
    k = k_ref[...]
    ck = ckt_ref[...].T.astype(BF16)
    vt = _with_ones(vt_ref[...])
    past = cv_ref.shape[0] // A_HEADS
    cv = cv_ref[pl.ds(pl.program_id(1), past, stride=A_HEADS), :]
    cvt = _with_ones(cv.T.astype(BF16))

    def scores(i, _):
        q = _diff_query_cols(qt_ref[:, i * BLOCK:(i + 1) * BLOCK])
        return [_dot(k, q), _dot(ck, q)]

    def outputs(shift, i, s):
        if shift:
            m = jnp.maximum(jnp.max(s[0], axis=0, keepdims=True), jnp.max(s[1], axis=0, keepdims=True))
        r = None
        for sc, v in ((s[0], vt), (s[1], cvt)):
            for c in range(0, sc.shape[0], KEY_CHUNK):
                piece = sc[c:c + KEY_CHUNK]
                part = _dot(v[:, c:c + KEY_CHUNK], jnp.exp2(piece - m if shift else piece).astype(BF16))
                r = part if r is None else r + part
        o_ref[i * BLOCK:(i + 1) * BLOCK, :] = _diff_output(r, lam, subg_col).astype(BF16)

    n_sub = qt_ref.shape[1] // BLOCK
    bounded = _scores_bounded(bounds_ref[0], jnp.maximum(bounds_ref[1], _max_row_sq_norm(ck)))

    @pl.when(bounded)
    def _():
        _software_pipeline(n_sub, [scores, functools.partial(outputs, False)])

    @pl.when(jnp.logical_not(bounded))
    def _():
        _software_pipeline(n_sub, [scores, functools.partial(outputs, True)])


def _diff_lat(qt, k, vt, ckt, cv, lam_vecs, subg_col, bounds, tq):
    b, _, t = qt.shape
    past = ckt.shape[2]
    return pl.pallas_call(
        _diff_lat_kernel,
        grid=(b, A_HEADS, t // tq),
        in_specs=[pl.BlockSpec((None, LANES, tq), lambda i, h, j: (i, h, j)),
                  pl.BlockSpec((None, t, LANES), lambda i, h, j: (i, 0, h)),
                  pl.BlockSpec((None, LANES, t), lambda i, h, j: (i, h, 0)),
                  pl.BlockSpec((None, LANES, past), lambda i, h, j: (i, h, 0)),
                  pl.BlockSpec((None, past * A_HEADS, LANES), lambda i, h, j: (i, 0, 0)),
                  _const_spec((4, HEAD_DIM)), _const_spec((2 * HEAD_DIM, 1)),
                  pl.BlockSpec(memory_space=pltpu.SMEM)],
        out_specs=pl.BlockSpec((None, tq, LANES), lambda i, h, j: (i, j, h)),
        out_shape=jax.ShapeDtypeStruct((b, t, A_W), BF16),
        compiler_params=_params(3),
        name="diff_lat",
    )(qt, k, vt, ckt, cv, lam_vecs, subg_col, bounds)


def _window_start(i, t):
    return min(max((i - 1) * BLOCK, 0), t - 3 * BLOCK)


def _band_offsets(t):
    return sorted({_window_start(i, t) - i * BLOCK for i in range(t // BLOCK)})


def _band_bias(t):
    key = np.arange(3 * BLOCK)[:, None]
    qry = np.arange(B_GROUP * BLOCK)[None, :] % BLOCK
    return jnp.asarray(np.stack([np.where(np.abs(key + off - qry) <= WINDOW, 0.0, NEG)
                                 for off in _band_offsets(t)]), dtype=F32)


def _win_lat_kernel(qt_ref, k_ref, vt_ref, ckt_ref, cvt_ref, sink_ref, bias_ref, bounds_ref, o_ref):
    g = pl.program_id(1)
    t = k_ref.shape[0]
    span = 3 * BLOCK
    keep = lax.broadcasted_iota(jnp.int32, (1, CHUNK), 1) // HEAD_DIM == g
    k = jnp.where(keep, k_ref[...].astype(F32), 0.0).astype(BF16)
    ck = jnp.where(keep, ckt_ref[...].T, 0.0).astype(BF16)
    vt = vt_ref[...]
    cvt = _with_ones(cvt_ref[...].astype(BF16))
    sink_row = _sink_row(sink_ref, g, BLOCK)
    offsets = _band_offsets(t)

    def scores(i, _):
        w = _window_start(i, t)
        qs = jnp.concatenate([qt_ref[r * HEAD_DIM:(r + 1) * HEAD_DIM, i * BLOCK:(i + 1) * BLOCK]
                              for r in range(B_GROUP)], axis=1)
        wq = jnp.concatenate([qs] * B_KV_HEADS, axis=0)
        return [_dot(k[w:w + span], wq) + bias_ref[offsets.index(w - i * BLOCK)], _dot(ck, wq)]

    def probs(shift, i, s):
        if shift:
            return _column_softmax(s, extra=sink_row)
        return [jnp.exp2(piece).astype(BF16) for piece in s], jnp.zeros_like(sink_row)

    def outputs(i, st):
        es, m = st
        w = _window_start(i, t)
        r = _dot(_with_ones(vt[:, w:w + span]), es[0]) + _dot(cvt, es[1])
        o_ref[i * BLOCK:(i + 1) * BLOCK, :] = _group_output(r, m, sink_row).astype(BF16)

    bounded = _scores_bounded(bounds_ref[2], jnp.maximum(bounds_ref[3], _max_row_sq_norm(ck)), bounds_ref[4])

    @pl.when(bounded)
    def _():
        _software_pipeline(t // BLOCK, [scores, functools.partial(probs, False), outputs])

    @pl.when(jnp.logical_not(bounded))
    def _():
        _software_pipeline(t // BLOCK, [scores, functools.partial(probs, True), outputs])


def _win_lat(qt, k, vt, ckt, cvt, sink_rows, bounds):
    b, _, t = qt.shape
    past = ckt.shape[2]
    bias = _band_bias(t)
    return pl.pallas_call(
        _win_lat_kernel,
        grid=(b, B_KV_HEADS),
        in_specs=[pl.BlockSpec((None, CHUNK, t), lambda i, g: (i, g, 0)),
                  pl.BlockSpec((None, t, B_KVW), lambda i, g: (i, 0, 0)),
                  pl.BlockSpec((None, HEAD_DIM, t), lambda i, g: (i, g, 0)),
                  pl.BlockSpec((None, B_KVW, past), lambda i, g: (i, 0, 0)),
                  pl.BlockSpec((None, HEAD_DIM, past), lambda i, g: (i, g, 0)),
                  _const_spec((B_HEADS, LANES)), _const_spec(bias.shape),
                  pl.BlockSpec(memory_space=pltpu.SMEM)],
        out_specs=pl.BlockSpec((None, t, CHUNK), lambda i, g: (i, 0, g)),
        out_shape=jax.ShapeDtypeStruct((b, t, B_Q), BF16),
        compiler_params=_params(2),
        name="win_lat",
    )(qt, k, vt, ckt, cvt, sink_rows, bias, bounds)


def _merge_kernel(x1_ref, x2_ref, oa1_ref, oa2_ref, ob1_ref, ob2_ref, mod_ref, g1_ref, win_ref, woa_ref,
                  wob_ref, wout_ref, o1_ref, o2_ref, *, n_first):
    def tile(x_ref, oa_ref, ob_ref, o_ref):
        mod = mod_ref[...]
        rows = lambda i: slice(i * ROW_SUBTILE, (i + 1) * ROW_SUBTILE)

        def norm(i, _):
            return _modulated_norm(x_ref[rows(i), :], g1_ref[...], mod[:, D_MODEL:2 * D_MODEL],
                                   mod[:, 0:D_MODEL]).astype(BF16)

        def branches(i, h):
            ga = _dot(h, win_ref[:, GATE_COL:GATE_COL + D_MODEL])
            gb = _dot(h, win_ref[:, GATE_COL + D_MODEL:GATE_COL + 2 * D_MODEL])
            return (jax.nn.sigmoid(ga) * _dot(oa_ref[rows(i), :], woa_ref[...])
                    + jax.nn.sigmoid(gb) * _dot(ob_ref[rows(i), :], wob_ref[...])).astype(BF16)

        def project(i, merged):
            o_ref[rows(i), :] = x_ref[rows(i), :] + mod[:, 2 * D_MODEL:3 * D_MODEL] * _dot(merged, wout_ref[...])

        _software_pipeline(x_ref.shape[0] // ROW_SUBTILE, [norm, branches, project])

    _per_token_set(n_first, functools.partial(tile, x1_ref, oa1_ref, ob1_ref, o1_ref),
                   functools.partial(tile, x2_ref, oa2_ref, ob2_ref, o2_ref))


def _per_token_set(n_first, first_body, second_body):
    first = pl.program_id(0) < n_first
    pl.when(first)(first_body)
    pl.when(jnp.logical_not(first))(second_body)


def _split_tiles(n_first, tm, width):
    return (pl.BlockSpec((tm, width), lambda i: (jnp.minimum(i, n_first - 1), 0)),
            pl.BlockSpec((tm, width), lambda i: (jnp.maximum(i - n_first, 0), 0)))


def _merge(x1, x2, oa1, oa2, ob1, ob2, mod3, mod_row, g1, w_in, w_oa, w_ob, w_out, tm):
    n_first, n_second = x1.shape[0] // tm, x2.shape[0] // tm
    t1, t2 = _split_tiles(n_first, tm, D_MODEL)
    sq = _const_spec((D_MODEL, D_MODEL))
    return pl.pallas_call(
        functools.partial(_merge_kernel, n_first=n_first),
        grid=(n_first + n_second,),
        in_specs=[t1, t2, t1, t2, t1, t2,
                  pl.BlockSpec((None, 1, N_MOD * D_MODEL), lambda i: (mod_row(i), 0, 0)),
                  _const_spec((1, D_MODEL)), _const_spec(w_in.shape), sq, sq, sq],
        out_specs=[t1, t2],
        out_shape=[jax.ShapeDtypeStruct(x1.shape, F32), jax.ShapeDtypeStruct(x2.shape, F32)],
        compiler_params=_params(1),
        name="merge_out",
    )(x1, x2, oa1, oa2, ob1, ob2, mod3, g1, w_in, w_oa, w_ob, w_out)


def _ffn_kernel(x1_ref, x2_ref, mod_ref, g2_ref, wgate_ref, wup_ref, wdown_ref, o1_ref, o2_ref, *, n_first):
    def tile(x_ref, o_ref):
        mod = mod_ref[...]
        rows = lambda i: slice(i * ROW_SUBTILE, (i + 1) * ROW_SUBTILE)

        def norm(i, _):
            return _modulated_norm(x_ref[rows(i), :], g2_ref[...], mod[:, 4 * D_MODEL:5 * D_MODEL],
                                   mod[:, 3 * D_MODEL:4 * D_MODEL]).astype(BF16)

        def hidden(i, h):
            gate = _dot(h, wgate_ref[...])
            return (gate * jax.nn.sigmoid(gate) * _dot(h, wup_ref[...])).astype(BF16)

        def project(i, act):
            o_ref[rows(i), :] = x_ref[rows(i), :] + mod[:, 5 * D_MODEL:6 * D_MODEL] * _dot(act, wdown_ref[...])

        _software_pipeline(x_ref.shape[0] // ROW_SUBTILE, [norm, hidden, project])

    _per_token_set(n_first, functools.partial(tile, x1_ref, o1_ref), functools.partial(tile, x2_ref, o2_ref))


def _ffn(x1, x2, mod3, mod_row, g2, w_gate, w_up, w_down, tm):
    n_first, n_second = x1.shape[0] // tm, x2.shape[0] // tm
    t1, t2 = _split_tiles(n_first, tm, D_MODEL)
    return pl.pallas_call(
        functools.partial(_ffn_kernel, n_first=n_first),
        grid=(n_first + n_second,),
        in_specs=[t1, t2,
                  pl.BlockSpec((None, 1, N_MOD * D_MODEL), lambda i: (mod_row(i), 0, 0)),
                  _const_spec((1, D_MODEL)), _const_spec((D_MODEL, D_FF)),
                  _const_spec((D_MODEL, D_FF)), _const_spec((D_FF, D_MODEL))],
        out_specs=[t1, t2],
        out_shape=[jax.ShapeDtypeStruct(x1.shape, F32), jax.ShapeDtypeStruct(x2.shape, F32)],
        compiler_params=_params(1),
        name="swiglu_ffn",
    )(x1, x2, mod3, g2, w_gate, w_up, w_down)


def _rope_tables(seq_len):
    quarter = HEAD_DIM // 4
    t = np.arange(seq_len)
    row = (t // GRID_W).astype(np.float32)
    col = (t % GRID_W).astype(np.float32)
    freqs = (np.float32(ROPE_BASE) ** (-np.arange(quarter, dtype=np.float32) / np.float32(quarter))
             ).astype(np.float32)
    ang_r = row[:, None] * freqs
    ang_c = col[:, None] * freqs
    zeros = np.zeros_like(ang_r)
    cos = np.concatenate([np.cos(ang_r)] * 2 + [np.cos(ang_c)] * 2, axis=-1)
    sin_next = np.concatenate([-np.sin(ang_r), zeros, -np.sin(ang_c), zeros], axis=-1)
    sin_prev = np.concatenate([zeros, np.sin(ang_r), zeros, np.sin(ang_c)], axis=-1)
    rep = LANES // HEAD_DIM
    nat = tuple(jnp.asarray(np.tile(a, (1, rep)), dtype=F32) for a in (cos, sin_next, sin_prev))
    featm = (jnp.asarray(cos.T, dtype=F32), jnp.asarray((sin_next + sin_prev).T, dtype=F32))
    return nat, featm


def _group_sum_matrix():
    idx = np.arange(CHUNK) // HEAD_DIM
    return jnp.asarray(idx[:, None] == idx[None, :], dtype=BF16)


def kernel(x_prompt, x_sample, cache_diff_k, cache_diff_v, cache_win_k, cache_win_v, c, c_ctx,
           w_ada, b_ada, norm1_g, w_in, qn_a, kn_a, lambda_q1, lambda_k1, lambda_q2, lambda_k2,
           subln_g, qn_b, kn_b, sink, w_oa, w_ob, w_out, norm2_g, w_gate, w_up, w_down):
    batch, seq, _ = x_prompt.shape
    dec_batch, dec_seq, _ = x_sample.shape
    past = cache_diff_k.shape[2]
    l = 0

    cc = jnp.zeros((8, D_MODEL), F32).at[:dec_batch].set(c).at[dec_batch].set(c_ctx)
    mod3 = _modulation(cc, w_ada[l], b_ada[l][None, :]).reshape(8, 1, N_MOD * D_MODEL)

    w_in_l = w_in[l]
    o_ka, o_va, o_qb, o_kb, o_vb, o_g = A_W, 2 * A_W, 3 * A_W, 3 * A_W + B_Q, 3 * A_W + B_Q + B_KVW, \
        3 * A_W + B_Q + 2 * B_KVW
    w_k = jnp.concatenate([w_in_l[:, o_ka:o_va], w_in_l[:, o_kb:o_vb]], axis=1).astype(BF16)
    w_t = jnp.concatenate([w_in_l[:, :o_ka], w_in_l[:, o_qb:o_kb], w_in_l[:, o_va:o_qb],
                           w_in_l[:, o_vb:o_g]], axis=1).T.astype(BF16)
    w_in_b = w_in_l.astype(BF16)
    w_oa_b, w_ob_b, w_out_b = w_oa[l].astype(BF16), w_ob[l].astype(BF16), w_out[l].astype(BF16)
    w_gate_b, w_up_b, w_down_b = w_gate[l].astype(BF16), w_up[l].astype(BF16), w_down[l].astype(BF16)
    g1 = norm1_g[l][None, :]
    g2 = norm2_g[l][None, :]
    rep = CHUNK // HEAD_DIM
    kgains = jnp.stack([jnp.tile(kn_a[l], rep), jnp.tile(kn_b[l], rep)])
    qgains = jnp.stack([qn_a[l], qn_b[l]])[:, :, None]
    lam_vecs = jnp.stack([lambda_q1[l], lambda_k1[l], lambda_q2[l], lambda_k2[l]])
    subg_col = subln_g[l][:, None]
    sink_rows = jnp.broadcast_to((sink[l] * LOG2E)[:, None], (B_HEADS, LANES))
    head_sq = lambda g, scale: 1.02 * HEAD_DIM * jnp.max(jnp.square(g[l])) * scale * scale
    bounds = jnp.stack([head_sq(qn_a, Q_SCALE), head_sq(kn_a, 1.0), head_sq(qn_b, Q_SCALE),
                        head_sq(kn_b, 1.0), jnp.max(jnp.abs(sink[l])) * LOG2E]).astype(F32)
    gsum = _group_sum_matrix()
    tm = 512

    xp = x_prompt.reshape(batch * seq, D_MODEL)
    ctx_row = lambda i: dec_batch
    qat, qbt, ka, kb, vat, vbt, kat_f, va_f, kbt_f = _qkv_proj(
        x_prompt, mod3, ctx_row, g1, w_k, w_t, gsum, kgains, qgains, None, tm, True)
    oa_p, ob_p = _ctx_attn(qat, qbt, ka, kb, vat, vbt, lam_vecs, subg_col, sink_rows, bounds, 2)
    new_diff_k = kat_f.reshape(batch, A_HEADS, 2, HEAD_DIM, seq).transpose(0, 4, 1, 2, 3)[:, None]
    new_diff_v = va_f.reshape(batch, 1, seq, A_HEADS, 2 * HEAD_DIM)
    new_win_k = kbt_f.reshape(batch, B_KV_HEADS, HEAD_DIM, seq).transpose(0, 3, 1, 2)[:, None]
    new_win_v = vbt.reshape(batch, B_KV_HEADS, HEAD_DIM, seq).transpose(0, 3, 1, 2)[:, None]

    xs = x_sample.reshape(dec_batch * dec_seq, D_MODEL)
    lat_row = lambda i: i // (dec_seq // tm)
    qat, qbt, ka, kb, vat, vbt = _qkv_proj(
        x_sample, mod3, lat_row, g1, w_k, w_t, gsum, kgains, qgains, _rope_tables(dec_seq), tm, False)
    ckt_a = cache_diff_k[:, l].transpose(0, 2, 3, 4, 1).reshape(dec_batch, A_W, past)
    cv_a = cache_diff_v[:, l].reshape(dec_batch, past * A_HEADS, 2 * HEAD_DIM)
    ckt_b = cache_win_k[:, l].transpose(0, 2, 3, 1).reshape(dec_batch, B_KVW, past)
    cvt_b = cache_win_v[:, l].transpose(0, 2, 3, 1).reshape(dec_batch, B_KVW, past)
    oa_s = _diff_lat(qat, ka.reshape(dec_batch, dec_seq, A_W), vat, ckt_a, cv_a, lam_vecs, subg_col, bounds,
                     dec_seq)
    ob_s = _win_lat(qbt, kb.reshape(dec_batch, dec_seq, B_KVW), vbt, ckt_b, cvt_b, sink_rows, bounds)

    n_ctx_tiles = batch * seq // tm
    both_row = lambda i: jnp.where(i < n_ctx_tiles, dec_batch, (i - n_ctx_tiles) // (dec_seq // tm))
    xp1, xs1 = _merge(xp, xs, oa_p, oa_s.reshape(-1, A_W), ob_p, ob_s.reshape(-1, B_Q), mod3, both_row, g1,
                      w_in_b, w_oa_b, w_ob_b, w_out_b, tm)
    y_prompt, y_sample = _ffn(xp1, xs1, mod3, both_row, g2, w_gate_b, w_up_b, w_down_b, tm)

    return (y_prompt.reshape(batch, seq, D_MODEL),
            y_sample.reshape(dec_batch, dec_seq, D_MODEL),
            new_diff_k, new_diff_v, new_win_k, new_win_v)
```

```python
import functools
import math

import jax
import jax.numpy as jnp
import numpy as np
from jax import lax
from jax.experimental import pallas as pl
from jax.experimental.pallas import tpu as pltpu

D_MODEL = 1024
HEAD_DIM = 64
A_HEADS = 8
B_HEADS = 16
B_KV_HEADS = 4
B_GROUP = B_HEADS // B_KV_HEADS
A_W = A_HEADS * 2 * HEAD_DIM
B_Q = B_HEADS * HEAD_DIM
B_KVW = B_KV_HEADS * HEAD_DIM
N_K = A_W + B_KVW
N_T = A_W + B_Q + A_W + B_KVW
GATE_COL = 3 * A_W + B_Q + 2 * B_KVW
WINDOW = 128
BLOCK = 128
GRID_W = 64
D_FF = 2816
N_MOD = 6
EPS = 1e-6
ROPE_BASE = 10000.0
NEG = -1e30
LAMBDA_INIT = 0.8 - 0.6 * math.exp(-0.3 * 0)
LOG2E = math.log2(math.e)
Q_SCALE = HEAD_DIM ** -0.5 * LOG2E

LANES = 128
CHUNK = 256
ONES_ROWS = 16
KEY_CHUNK = 512
ROW_SUBTILE = 256
DIFF_QUERIES = 128
SHIFT_FREE_LOG2 = 60.0
VMEM_LIMIT = 56 * 1024 * 1024

F32 = jnp.float32
BF16 = jnp.bfloat16


def _params(n_axes, flags=None):
    return pltpu.CompilerParams(dimension_semantics=("arbitrary",) * n_axes,
                                vmem_limit_bytes=VMEM_LIMIT, flags=flags)


def _const_spec(shape):
    nd = len(shape)
    return pl.BlockSpec(shape, lambda *_: (0,) * nd, pipeline_mode=pl.Buffered(1))


def _dot(a, b):
    return jnp.dot(a, b, preferred_element_type=F32)


def _modulated_norm(x, gain, scale, shift):
    y = x * lax.rsqrt(jnp.mean(x * x, axis=-1, keepdims=True) + EPS) * gain
    return y * (1.0 + scale) + shift


def _software_pipeline(n, stages):
    depth = len(stages)
    state = [None] * n
    for step in range(n + depth - 1):
        for k, stage in enumerate(stages):
            i = step - k
            if 0 <= i < n:
                state[i] = stage(i, state[i])


def _mod_kernel(c_ref, w_ref, b_ref, o_ref):
    cc = c_ref[...]
    s = cc * jax.nn.sigmoid(cc)
    o_ref[...] = _dot(s.astype(BF16), w_ref[...].astype(BF16)) + b_ref[...]


def _modulation(cc, w_ada, b_ada):
    rows, n = cc.shape[0], w_ada.shape[1]
    tn = n // 4
    return pl.pallas_call(
        _mod_kernel,
        grid=(n // tn,),
        in_specs=[pl.BlockSpec((rows, D_MODEL), lambda j: (0, 0)),
                  pl.BlockSpec((D_MODEL, tn), lambda j: (0, j)),
                  pl.BlockSpec((1, tn), lambda j: (0, j))],
        out_specs=pl.BlockSpec((rows, tn), lambda j: (0, j)),
        out_shape=jax.ShapeDtypeStruct((rows, n), F32),
        compiler_params=_params(1),
        name="adaln_mod",
    )(cc, w_ada, b_ada)


def _qkv_kernel(*refs, rope, emit_new_kv):
    x_ref, mod_ref, g1_ref, wk_ref, wt_ref, gsum_ref, kgain_ref, qgain_ref = refs[:8]
    refs = refs[8:]
    if rope:
        cos_ref, sa_ref, sb_ref, cost_ref, sint_ref = refs[:5]
        refs = refs[5:]
    qat_ref, qbt_ref, ka_ref, kb_ref, vat_ref, vbt_ref = refs[:6]
    if emit_new_kv:
        kat_ref, va_ref, kbt_ref = refs[6:]

    mod = mod_ref[...]
    h = _modulated_norm(x_ref[...], g1_ref[...], mod[:, D_MODEL:2 * D_MODEL], mod[:, 0:D_MODEL])
    hb = h.astype(BF16)
    gsum = gsum_ref[...]

    def put_feat(ref, rows, val):
        n_seq, _, width = ref.shape
        for s in range(n_seq):
            ref[s, rows, :] = val[:, s * width:(s + 1) * width].astype(ref.dtype)

    n_wide = A_W // CHUNK
    key_cols = [j * CHUNK for j in range(n_wide)] + [A_W]

    def key_project(j, _):
        return _dot(hb, wk_ref[:, key_cols[j]:key_cols[j] + CHUNK])

    def key_squares(j, p):
        return p, _dot((p * p).astype(BF16), gsum)

    def key_finish(j, st):
        p, ss = st
        row = 0 if j < n_wide else 1
        p = p * lax.rsqrt(ss * (1.0 / HEAD_DIM) + EPS) * kgain_ref[row:row + 1, :]
        if rope:
            cos, sa, sb = cos_ref[...], sa_ref[...], sb_ref[...]
            halves = []
            for i in range(CHUNK // LANES):
                xh = p[:, i * LANES:(i + 1) * LANES]
                nxt = pltpu.roll(xh, LANES - HEAD_DIM // 4, 1)
                prv = pltpu.roll(xh, HEAD_DIM // 4, 1)
                halves.append(xh * cos + nxt * sa + prv * sb)
            p = jnp.concatenate(halves, axis=1)
        if j < n_wide:
            sl = slice(j * CHUNK, (j + 1) * CHUNK)
            ka_ref[:, sl] = p.astype(BF16)
            if emit_new_kv:
                put_feat(kat_ref, sl, p.T)
        else:
            kb_ref[...] = p.astype(BF16)
            if emit_new_kv:
                put_feat(kbt_ref, slice(None), p.T)

    _software_pipeline(len(key_cols), [key_project, key_squares, key_finish])

    ht = h.T.astype(BF16)
    n_feat = N_T // CHUNK

    def feat_project(c, _):
        return _dot(wt_ref[c * CHUNK:(c + 1) * CHUNK, :], ht)

    def feat_finish(c, p):
        kind, j = divmod(c, n_wide)
        sl = slice(j * CHUNK, (j + 1) * CHUNK)
        if kind == 2:
            put_feat(vat_ref, sl, p)
            if emit_new_kv:
                va_ref[:, sl] = p.T
            return
        if kind == 3:
            put_feat(vbt_ref, slice(None), p)
            return
        gain = qgain_ref[kind]
        q4 = HEAD_DIM // 4
        heads = []
        for r in range(CHUNK // HEAD_DIM):
            x = p[r * HEAD_DIM:(r + 1) * HEAD_DIM, :]
            x = x * lax.rsqrt(jnp.mean(x * x, axis=0, keepdims=True) + EPS) * gain
            if rope:
                swapped = jnp.concatenate([x[q4:2 * q4], x[0:q4], x[3 * q4:4 * q4], x[2 * q4:3 * q4]],
                                          axis=0)
                x = x * cost_ref[...] + swapped * sint_ref[...]
            heads.append(x)
        put_feat(qat_ref if kind == 0 else qbt_ref, sl, jnp.concatenate(heads, axis=0) * Q_SCALE)

    _software_pipeline(n_feat, [feat_project, feat_finish])


def _qkv_proj(x3d, mod3, mod_row, g1, w_k, w_t, gsum, kgains, qgains, rope_tabs, tm, emit_new_kv):
    b, t, _ = x3d.shape
    rows = b * t
    rope = rope_tabs is not None
    tps = max(t // tm, 1)
    spt = max(tm // t, 1)
    tw = tm // spt
    in_specs = [pl.BlockSpec((tm, D_MODEL), lambda i: (i, 0)),
                pl.BlockSpec((None, 1, N_MOD * D_MODEL), lambda i: (mod_row(i), 0, 0)),
                _const_spec((1, D_MODEL)),
                _const_spec((D_MODEL, N_K)),
                _const_spec((N_T, D_MODEL)),
                _const_spec((CHUNK, CHUNK)),
                _const_spec((2, CHUNK)),
                _const_spec((2, HEAD_DIM, 1))]
    args = [x3d.reshape(rows, D_MODEL), mod3, g1, w_k, w_t, gsum, kgains, qgains]
    if rope:
        nat, featm = rope_tabs
        in_specs += [pl.BlockSpec((tm, LANES), lambda i: (i % tps, 0))] * 3
        in_specs += [pl.BlockSpec((HEAD_DIM, tm), lambda i: (0, i % tps))] * 2
        args += list(nat) + list(featm)
    tok_wide = pl.BlockSpec((tm, A_W), lambda i: (i, 0))
    tok_narrow = pl.BlockSpec((tm, B_KVW), lambda i: (i, 0))
    feat_wide = pl.BlockSpec((spt, A_W, tw), lambda i: (i // tps, 0, i % tps))
    feat_narrow = pl.BlockSpec((spt, B_KVW, tw), lambda i: (i // tps, 0, i % tps))
    out_specs = [feat_wide, feat_wide, tok_wide, tok_narrow, feat_wide, feat_narrow]
    out_shape = [jax.ShapeDtypeStruct((b, A_W, t), BF16),
                 jax.ShapeDtypeStruct((b, B_Q, t), BF16),
                 jax.ShapeDtypeStruct((rows, A_W), BF16),
                 jax.ShapeDtypeStruct((rows, B_KVW), BF16),
                 jax.ShapeDtypeStruct((b, A_W, t), BF16),
                 jax.ShapeDtypeStruct((b, B_KVW, t), F32 if emit_new_kv else BF16)]
    if emit_new_kv:
        out_specs += [feat_wide, tok_wide, feat_narrow]
        out_shape += [jax.ShapeDtypeStruct((b, A_W, t), F32),
                      jax.ShapeDtypeStruct((rows, A_W), F32),
                      jax.ShapeDtypeStruct((b, B_KVW, t), F32)]
    return pl.pallas_call(
        functools.partial(_qkv_kernel, rope=rope, emit_new_kv=emit_new_kv),
        grid=(rows // tm,),
        in_specs=in_specs,
        out_specs=out_specs,
        out_shape=out_shape,
        compiler_params=_params(1),
        name="qkv_rope" if rope else "qkv_ctx",
    )(*args)


def _lambda(lam_ref):
    lv = lam_ref[...]
    t1 = jnp.sum(lv[0:1] * lv[1:2], axis=-1, keepdims=True)
    t2 = jnp.sum(lv[2:3] * lv[3:4], axis=-1, keepdims=True)
    return jnp.exp(t1) - jnp.exp(t2) + LAMBDA_INIT


def _with_ones(vt):
    return jnp.concatenate([vt, jnp.ones((ONES_ROWS, vt.shape[1]), vt.dtype)], axis=0)


def _column_softmax(scores, extra=None):
    m = functools.reduce(jnp.maximum, [jnp.max(s, axis=0, keepdims=True) for s in scores])
    if extra is not None:
        m = jnp.maximum(m, extra)
    return [jnp.exp2(s - m).astype(BF16) for s in scores], m


def _max_row_sq_norm(x):
    x = x.astype(F32)
    return jnp.max(jnp.sum(x * x, axis=1, keepdims=True), axis=0, keepdims=True)[0, 0]


def _scores_bounded(q_sq, k_sq, sink_abs=0.0):
    limit = SHIFT_FREE_LOG2
    return jnp.logical_and(q_sq * k_sq <= limit * limit, sink_abs <= limit)


def _diff_query_cols(qt):
    zero = jnp.zeros((HEAD_DIM, qt.shape[1]), qt.dtype)
    return jnp.concatenate([jnp.concatenate([qt[:HEAD_DIM], zero], axis=0),
                            jnp.concatenate([zero, qt[HEAD_DIM:]], axis=0)], axis=1)


def _diff_output(r, lam, subg_col):
    n = r.shape[1] // 2
    dv = 2 * HEAD_DIM
    tot = r[dv:dv + 1, :]
    o = r[:dv, :n] * (1.0 / tot[:, :n]) - r[:dv, n:] * (lam / tot[:, n:])
    o = o * lax.rsqrt(jnp.mean(o * o, axis=0, keepdims=True) + EPS)
    return (o * subg_col * (1.0 - LAMBDA_INIT)).T


def _group_output(r, m, sink_row):
    n = r.shape[1] // B_GROUP
    tot = r[HEAD_DIM:HEAD_DIM + 1, :] + jnp.exp2(sink_row - m)
    o = r[:HEAD_DIM, :] * (1.0 / tot)
    return jnp.concatenate([o[:, i * n:(i + 1) * n] for i in range(B_GROUP)], axis=0).T


def _sink_row(sink_ref, g, n):
    reps = n // LANES
    return jnp.concatenate([sink_ref[pl.ds(g * B_GROUP + r, 1), :] for r in range(B_GROUP)
                            for _ in range(reps)], axis=1)


def _ctx_attn_kernel(qat_ref, qbt_ref, ka_ref, kb_ref, vat_ref, vbt_ref, lam_ref, subg_ref, sink_ref,
                     bounds_ref, oa_ref, ob_ref):
    lam = _lambda(lam_ref)
    subg_col = subg_ref[...]
    n_seq, _, n = qat_ref.shape
    chains = A_HEADS + B_KV_HEADS

    def scores(item, _):
        s, c = divmod(item, chains)
        rows = slice(s * n, (s + 1) * n)
        if c < A_HEADS:
            sl = slice(c * LANES, (c + 1) * LANES)
            return _dot(ka_ref[rows, sl], _diff_query_cols(qat_ref[s, sl, :]))
        g = c - A_HEADS
        qs = jnp.concatenate([qbt_ref[s, (g * B_GROUP + r) * HEAD_DIM:(g * B_GROUP + r + 1) * HEAD_DIM, :]
                              for r in range(B_GROUP)], axis=1)
        zero = jnp.zeros_like(qs)
        w = jnp.concatenate([qs if i == g else zero for i in range(B_KV_HEADS)], axis=0)
        return _dot(kb_ref[rows, :], w)

    def probs(shift, item, sc):
        c = item % chains
        if not shift:
            return [jnp.exp2(sc).astype(BF16)], jnp.zeros((1, sc.shape[1]), F32)
        if c < A_HEADS:
            return _column_softmax([sc])
        return _column_softmax([sc], extra=_sink_row(sink_ref, c - A_HEADS, n))

    def outputs(item, st):
        s, c = divmod(item, chains)
        rows = slice(s * n, (s + 1) * n)
        es, m = st
        if c < A_HEADS:
            sl = slice(c * LANES, (c + 1) * LANES)
            r = _dot(_with_ones(vat_ref[s, sl, :]), es[0])
            oa_ref[rows, sl] = _diff_output(r, lam, subg_col).astype(BF16)
        else:
            g = c - A_HEADS
            vt = vbt_ref[s, g * HEAD_DIM:(g + 1) * HEAD_DIM, :].astype(BF16)
            r = _dot(_with_ones(vt), es[0])
            o = _group_output(r, m, _sink_row(sink_ref, g, n))
            ob_ref[rows, g * CHUNK:(g + 1) * CHUNK] = o.astype(BF16)

    bounded = jnp.logical_and(_scores_bounded(bounds_ref[0], bounds_ref[1]),
                              _scores_bounded(bounds_ref[2], bounds_ref[3], bounds_ref[4]))

    @pl.when(bounded)
    def _():
        _software_pipeline(n_seq * chains, [scores, functools.partial(probs, False), outputs])

    @pl.when(jnp.logical_not(bounded))
    def _():
        _software_pipeline(n_seq * chains, [scores, functools.partial(probs, True), outputs])


def _ctx_attn(qat, qbt, ka, kb, vat, vbt, lam_vecs, subg_col, sink_rows, bounds, n_seq):
    b, _, t = qat.shape
    feat = pl.BlockSpec((n_seq, A_W, t), lambda i: (i, 0, 0))
    tok = pl.BlockSpec((n_seq * t, A_W), lambda i: (i, 0))
    return pl.pallas_call(
        _ctx_attn_kernel,
        grid=(b // n_seq,),
        in_specs=[feat, feat, tok, pl.BlockSpec((n_seq * t, B_KVW), lambda i: (i, 0)), feat,
                  pl.BlockSpec((n_seq, B_KVW, t), lambda i: (i, 0, 0)),
                  _const_spec((4, HEAD_DIM)), _const_spec((2 * HEAD_DIM, 1)),
                  _const_spec((B_HEADS, LANES)), pl.BlockSpec(memory_space=pltpu.SMEM)],
        out_specs=[tok, tok],
        out_shape=[jax.ShapeDtypeStruct((b * t, A_W), BF16), jax.ShapeDtypeStruct((b * t, B_Q), BF16)],
        compiler_params=_params(1),
        name="ctx_attn",
    )(qat, qbt, ka, kb, vat, vbt, lam_vecs, subg_col, sink_rows, bounds)


def _rider_specs(weights, n_steps, flat_step):
    specs = []
    for w in weights:
        rows, cols = w.shape
        bands = next(n for n in (n_steps, n_steps // 2, n_steps // 4)
                     if rows % n == 0 and (rows // n) % 16 == 0)
        repeat = n_steps // bands
        spec = pl.BlockSpec((rows // bands, cols), lambda *g, repeat=repeat: (flat_step(*g) // repeat, 0))
        specs.append((spec, jax.ShapeDtypeStruct(w.shape, BF16)))
    return specs


def _cast_riders(rider_refs):
    n = len(rider_refs) // 2
    for w_ref, o_ref in zip(rider_refs[:n], rider_refs[n:]):
        o_ref[...] = w_ref[...].astype(BF16)


def _diff_lat_kernel(qt_ref, k_ref, vt_ref, ckt_ref, cv_ref, lam_ref, subg_ref, bounds_ref, *rest, n_riders):
    o_ref = rest[n_riders]
    _cast_riders(rest[:n_riders] + rest[n_riders + 1:])
    lam = _lambda(lam_ref)
    subg_col = subg_ref[...]
    k = k_ref[...]
    ck = ckt_ref[...].T.astype(BF16)
    vt = _with_ones(vt_ref[...])
    past = cv_ref.shape[0] // A_HEADS
    cv = cv_ref[pl.ds(pl.program_id(1), past, stride=A_HEADS), :]
    cvt = _with_ones(cv.T.astype(BF16))

    def scores(i, _):
        q = _diff_query_cols(qt_ref[:, i * DIFF_QUERIES:(i + 1) * DIFF_QUERIES])
        return [_dot(k, q), _dot(ck, q)]

    def outputs(shift, i, s):
        if shift:
            m = jnp.maximum(jnp.max(s[0], axis=0, keepdims=True), jnp.max(s[1], axis=0, keepdims=True))
        r = None
        for sc, v in ((s[0], vt), (s[1], cvt)):
            for c in range(0, sc.shape[0], KEY_CHUNK):
                piece = sc[c:c + KEY_CHUNK]
                part = _dot(v[:, c:c + KEY_CHUNK], jnp.exp2(piece - m if shift else piece).astype(BF16))
                r = part if r is None else r + part
        o_ref[i * DIFF_QUERIES:(i + 1) * DIFF_QUERIES, :] = _diff_output(r, lam, subg_col).astype(BF16)

    n_sub = qt_ref.shape[1] // DIFF_QUERIES
    bounded = _scores_bounded(bounds_ref[0], jnp.maximum(bounds_ref[1], _max_row_sq_norm(ck)))

    @pl.when(bounded)
    def _():
        _software_pipeline(n_sub, [scores, functools.partial(outputs, False)])

    @pl.when(jnp.logical_not(bounded))
    def _():
        _software_pipeline(n_sub, [scores, functools.partial(outputs, True)])


def _diff_lat(qt, k, vt, ckt, cv, lam_vecs, subg_col, bounds, riders):
    b, _, t = qt.shape
    past = ckt.shape[2]
    rider_specs = _rider_specs(riders, b * A_HEADS, lambda i, h: i * A_HEADS + h)
    return pl.pallas_call(
        functools.partial(_diff_lat_kernel, n_riders=len(riders)),
        grid=(b, A_HEADS),
        in_specs=[pl.BlockSpec((None, LANES, t), lambda i, h: (i, h, 0)),
                  pl.BlockSpec((None, t, LANES), lambda i, h: (i, 0, h)),
                  pl.BlockSpec((None, LANES, t), lambda i, h: (i, h, 0)),
                  pl.BlockSpec((None, LANES, past), lambda i, h: (i, h, 0)),
                  pl.BlockSpec((None, past * A_HEADS, LANES), lambda i, h: (i, 0, 0)),
                  _const_spec((4, HEAD_DIM)), _const_spec((2 * HEAD_DIM, 1)),
                  pl.BlockSpec(memory_space=pltpu.SMEM)] + [spec for spec, _ in rider_specs],
        out_specs=[pl.BlockSpec((None, t, LANES), lambda i, h: (i, 0, h))] + [spec for spec, _ in rider_specs],
        out_shape=[jax.ShapeDtypeStruct((b, t, A_W), BF16)] + [shape for _, shape in rider_specs],
        compiler_params=_params(2),
        name="diff_lat",
    )(qt, k, vt, ckt, cv, lam_vecs, subg_col, bounds, *riders)


def _window_start(i, t):
    return min(max((i - 1) * BLOCK, 0), t - 3 * BLOCK)


def _band_offsets(t):
    return sorted({_window_start(i, t) - i * BLOCK for i in range(t // BLOCK)})


def _band_bias(t):
    key = np.arange(3 * BLOCK)[:, None]
    qry = np.arange(B_GROUP * BLOCK)[None, :] % BLOCK
    return jnp.asarray(np.stack([np.where(np.abs(key + off - qry) <= WINDOW, 0.0, NEG)
                                 for off in _band_offsets(t)]), dtype=F32)


def _win_lat_kernel(qt_ref, k_ref, vt_ref, ckt_ref, cvt_ref, sink_ref, bias_ref, bounds_ref, *rest, n_riders):
    o_ref = rest[n_riders]
    _cast_riders(rest[:n_riders] + rest[n_riders + 1:])
    g = pl.program_id(1)
    t = k_ref.shape[0]
    span = 3 * BLOCK
    keep = lax.broadcasted_iota(jnp.int32, (1, CHUNK), 1) // HEAD_DIM == g
    k = jnp.where(keep, k_ref[...].astype(F32), 0.0).astype(BF16)
    ck = jnp.where(keep, ckt_ref[...].T, 0.0).astype(BF16)
    vt = vt_ref[...]
    cvt = _with_ones(cvt_ref[...].astype(BF16))
    sink_row = _sink_row(sink_ref, g, BLOCK)
    offsets = _band_offsets(t)

    def scores(i, _):
        w = _window_start(i, t)
        qs = jnp.concatenate([qt_ref[r * HEAD_DIM:(r + 1) * HEAD_DIM, i * BLOCK:(i + 1) * BLOCK]
                              for r in range(B_GROUP)], axis=1)
        wq = jnp.concatenate([qs] * B_KV_HEADS, axis=0)
        return [_dot(k[w:w + span], wq) + bias_ref[offsets.index(w - i * BLOCK)], _dot(ck, wq)]

    def probs(shift, i, s):
        if shift:
            return _column_softmax(s, extra=sink_row)
        return [jnp.exp2(piece).astype(BF16) for piece in s], jnp.zeros_like(sink_row)

    def outputs(i, st):
        es, m = st
        w = _window_start(i, t)
        r = _dot(_with_ones(vt[:, w:w + span]), es[0]) + _dot(cvt, es[1])
        o_ref[i * BLOCK:(i + 1) * BLOCK, :] = _group_output(r, m, sink_row).astype(BF16)

    bounded = _scores_bounded(bounds_ref[2], jnp.maximum(bounds_ref[3], _max_row_sq_norm(ck)), bounds_ref[4])

    @pl.when(bounded)
    def _():
        _software_pipeline(t // BLOCK, [scores, functools.partial(probs, False), outputs])

    @pl.when(jnp.logical_not(bounded))
    def _():
        _software_pipeline(t // BLOCK, [scores, functools.partial(probs, True), outputs])


def _win_lat(qt, k, vt, ckt, cvt, sink_rows, bounds, riders):
    b, _, t = qt.shape
    past = ckt.shape[2]
    bias = _band_bias(t)
    rider_specs = _rider_specs(riders, b * B_KV_HEADS, lambda i, g: i * B_KV_HEADS + g)
    return pl.pallas_call(
        functools.partial(_win_lat_kernel, n_riders=len(riders)),
        grid=(b, B_KV_HEADS),
        in_specs=[pl.BlockSpec((None, CHUNK, t), lambda i, g: (i, g, 0)),
                  pl.BlockSpec((None, t, B_KVW), lambda i, g: (i, 0, 0)),
                  pl.BlockSpec((None, HEAD_DIM, t), lambda i, g: (i, g, 0)),
                  pl.BlockSpec((None, B_KVW, past), lambda i, g: (i, 0, 0)),
                  pl.BlockSpec((None, HEAD_DIM, past), lambda i, g: (i, g, 0)),
                  _const_spec((B_HEADS, LANES)), _const_spec(bias.shape),
                  pl.BlockSpec(memory_space=pltpu.SMEM)] + [spec for spec, _ in rider_specs],
        out_specs=[pl.BlockSpec((None, t, CHUNK), lambda i, g: (i, 0, g))] + [spec for spec, _ in rider_specs],
        out_shape=[jax.ShapeDtypeStruct((b, t, B_Q), BF16)] + [shape for _, shape in rider_specs],
        compiler_params=_params(2),
        name="win_lat",
    )(qt, k, vt, ckt, cvt, sink_rows, bias, bounds, *riders)


def _merge_kernel(x1_ref, x2_ref, oa1_ref, oa2_ref, ob1_ref, ob2_ref, mod_ref, g1_ref, win_ref, woa_ref,
                  wob_ref, wout_ref, o1_ref, o2_ref, *, n_first):
    def tile(x_ref, oa_ref, ob_ref, o_ref):
        mod = mod_ref[...]
        rows = lambda i: slice(i * ROW_SUBTILE, (i + 1) * ROW_SUBTILE)

        def norm(i, _):
            return _modulated_norm(x_ref[rows(i), :], g1_ref[...], mod[:, D_MODEL:2 * D_MODEL],
                                   mod[:, 0:D_MODEL]).astype(BF16)

        def branches(i, h):
            ga = _dot(h, win_ref[:, GATE_COL:GATE_COL + D_MODEL])
            gb = _dot(h, win_ref[:, GATE_COL + D_MODEL:GATE_COL + 2 * D_MODEL])
            return (jax.nn.sigmoid(ga) * _dot(oa_ref[rows(i), :], woa_ref[...])
                    + jax.nn.sigmoid(gb) * _dot(ob_ref[rows(i), :], wob_ref[...])).astype(BF16)

        def project(i, merged):
            o_ref[rows(i), :] = x_ref[rows(i), :] + mod[:, 2 * D_MODEL:3 * D_MODEL] * _dot(merged, wout_ref[...])

        _software_pipeline(x_ref.shape[0] // ROW_SUBTILE, [norm, branches, project])

    _per_token_set(n_first, functools.partial(tile, x1_ref, oa1_ref, ob1_ref, o1_ref),
                   functools.partial(tile, x2_ref, oa2_ref, ob2_ref, o2_ref))


def _per_token_set(n_first, first_body, second_body):
    first = pl.program_id(0) < n_first
    pl.when(first)(first_body)
    pl.when(jnp.logical_not(first))(second_body)


def _split_tiles(n_first, tm, width):
    return (pl.BlockSpec((tm, width), lambda i: (jnp.minimum(i, n_first - 1), 0)),
            pl.BlockSpec((tm, width), lambda i: (jnp.maximum(i - n_first, 0), 0)))


def _merge(x1, x2, oa1, oa2, ob1, ob2, mod3, mod_row, g1, w_in, w_oa, w_ob, w_out, tm):
    n_first, n_second = x1.shape[0] // tm, x2.shape[0] // tm
    t1, t2 = _split_tiles(n_first, tm, D_MODEL)
    sq = _const_spec((D_MODEL, D_MODEL))
    return pl.pallas_call(
        functools.partial(_merge_kernel, n_first=n_first),
        grid=(n_first + n_second,),
        in_specs=[t1, t2, t1, t2, t1, t2,
                  pl.BlockSpec((None, 1, N_MOD * D_MODEL), lambda i: (mod_row(i), 0, 0)),
                  _const_spec((1, D_MODEL)), _const_spec(w_in.shape), sq, sq, sq],
        out_specs=[t1, t2],
        out_shape=[jax.ShapeDtypeStruct(x1.shape, F32), jax.ShapeDtypeStruct(x2.shape, F32)],
        compiler_params=_params(1),
        name="merge_out",
    )(x1, x2, oa1, oa2, ob1, ob2, mod3, g1, w_in, w_oa, w_ob, w_out)


def _ffn_kernel(x1_ref, x2_ref, mod_ref, g2_ref, wgate_ref, wup_ref, wdown_ref, o1_ref, o2_ref, *, n_first):
    def tile(x_ref, o_ref):
        mod = mod_ref[...]
        rows = lambda i: slice(i * ROW_SUBTILE, (i + 1) * ROW_SUBTILE)

        def norm(i, _):
            return _modulated_norm(x_ref[rows(i), :], g2_ref[...], mod[:, 4 * D_MODEL:5 * D_MODEL],
                                   mod[:, 3 * D_MODEL:4 * D_MODEL]).astype(BF16)

        def hidden(i, h):
            gate = _dot(h, wgate_ref[...])
            return (gate * jax.nn.sigmoid(gate) * _dot(h, wup_ref[...])).astype(BF16)

        def project(i, act):
            o_ref[rows(i), :] = x_ref[rows(i), :] + mod[:, 5 * D_MODEL:6 * D_MODEL] * _dot(act, wdown_ref[...])

        _software_pipeline(x_ref.shape[0] // ROW_SUBTILE, [norm, hidden, project])

    _per_token_set(n_first, functools.partial(tile, x1_ref, o1_ref), functools.partial(tile, x2_ref, o2_ref))


def _ffn(x1, x2, mod3, mod_row, g2, w_gate, w_up, w_down, tm):
    n_first, n_second = x1.shape[0] // tm, x2.shape[0] // tm
    t1, t2 = _split_tiles(n_first, tm, D_MODEL)
    return pl.pallas_call(
        functools.partial(_ffn_kernel, n_first=n_first),
        grid=(n_first + n_second,),
        in_specs=[t1, t2,
                  pl.BlockSpec((None, 1, N_MOD * D_MODEL), lambda i: (mod_row(i), 0, 0)),
                  _const_spec((1, D_MODEL)), _const_spec((D_MODEL, D_FF)),
                  _const_spec((D_MODEL, D_FF)), _const_spec((D_FF, D_MODEL))],
        out_specs=[t1, t2],
        out_shape=[jax.ShapeDtypeStruct(x1.shape, F32), jax.ShapeDtypeStruct(x2.shape, F32)],
        compiler_params=_params(1),
        name="swiglu_ffn",
    )(x1, x2, mod3, g2, w_gate, w_up, w_down)


def _rope_tables(seq_len):
    quarter = HEAD_DIM // 4
    t = np.arange(seq_len)
    row = (t // GRID_W).astype(np.float32)
    col = (t % GRID_W).astype(np.float32)
    freqs = (np.float32(ROPE_BASE) ** (-np.arange(quarter, dtype=np.float32) / np.float32(quarter))
             ).astype(np.float32)
    ang_r = row[:, None] * freqs
    ang_c = col[:, None] * freqs
    zeros = np.zeros_like(ang_r)
    cos = np.concatenate([np.cos(ang_r)] * 2 + [np.cos(ang_c)] * 2, axis=-1)
    sin_next = np.concatenate([-np.sin(ang_r), zeros, -np.sin(ang_c), zeros], axis=-1)
    sin_prev = np.concatenate([zeros, np.sin(ang_r), zeros, np.sin(ang_c)], axis=-1)
    rep = LANES // HEAD_DIM
    nat = tuple(jnp.asarray(np.tile(a, (1, rep)), dtype=F32) for a in (cos, sin_next, sin_prev))
    featm = (jnp.asarray(cos.T, dtype=F32), jnp.asarray((sin_next + sin_prev).T, dtype=F32))
    return nat, featm


def _group_sum_matrix():
    idx = np.arange(CHUNK) // HEAD_DIM
    return jnp.asarray(idx[:, None] == idx[None, :], dtype=BF16)


def kernel(x_prompt, x_sample, cache_diff_k, cache_diff_v, cache_win_k, cache_win_v, c, c_ctx,
           w_ada, b_ada, norm1_g, w_in, qn_a, kn_a, lambda_q1, lambda_k1, lambda_q2, lambda_k2,
           subln_g, qn_b, kn_b, sink, w_oa, w_ob, w_out, norm2_g, w_gate, w_up, w_down):
    batch, seq, _ = x_prompt.shape
    dec_batch, dec_seq, _ = x_sample.shape
    past = cache_diff_k.shape[2]
    l = 0

    cc = jnp.concatenate([c, c_ctx[None, :], jnp.zeros((8 - dec_batch - 1, D_MODEL), F32)], axis=0)
    mod3 = _modulation(cc, w_ada[l], b_ada[l][None, :]).reshape(8, 1, N_MOD * D_MODEL)

    w_in_l = w_in[l]
    o_ka, o_va, o_qb, o_kb, o_vb, o_g = A_W, 2 * A_W, 3 * A_W, 3 * A_W + B_Q, 3 * A_W + B_Q + B_KVW, \
        3 * A_W + B_Q + 2 * B_KVW
    w_k = jnp.concatenate([w_in_l[:, o_ka:o_va], w_in_l[:, o_kb:o_vb]], axis=1).astype(BF16)
    w_t = jnp.concatenate([w_in_l[:, :o_ka], w_in_l[:, o_qb:o_kb], w_in_l[:, o_va:o_qb],
                           w_in_l[:, o_vb:o_g]], axis=1).T.astype(BF16)
    w_in_b = w_in_l.astype(BF16)
    g1 = norm1_g[l][None, :]
    g2 = norm2_g[l][None, :]
    rep = CHUNK // HEAD_DIM
    kgains = jnp.stack([jnp.tile(kn_a[l], rep), jnp.tile(kn_b[l], rep)])
    qgains = jnp.stack([qn_a[l], qn_b[l]])[:, :, None]
    lam_vecs = jnp.stack([lambda_q1[l], lambda_k1[l], lambda_q2[l], lambda_k2[l]])
    subg_col = subln_g[l][:, None]
    sink_rows = jnp.broadcast_to((sink[l] * LOG2E)[:, None], (B_HEADS, LANES))
    head_sq = lambda g, scale: 1.02 * HEAD_DIM * jnp.max(jnp.square(g[l])) * scale * scale
    bounds = jnp.stack([head_sq(qn_a, Q_SCALE), head_sq(kn_a, 1.0), head_sq(qn_b, Q_SCALE),
                        head_sq(kn_b, 1.0), jnp.max(jnp.abs(sink[l])) * LOG2E]).astype(F32)
    gsum = _group_sum_matrix()
    tm = 512

    xp = x_prompt.reshape(batch * seq, D_MODEL)
    ctx_row = lambda i: dec_batch
    qat, qbt, ka, kb, vat, vbt, kat_f, va_f, kbt_f = _qkv_proj(
        x_prompt, mod3, ctx_row, g1, w_k, w_t, gsum, kgains, qgains, None, tm, True)
    oa_p, ob_p = _ctx_attn(qat, qbt, ka, kb, vat, vbt, lam_vecs, subg_col, sink_rows, bounds, 2)
    new_diff_k = kat_f.reshape(batch, A_HEADS, 2, HEAD_DIM, seq).transpose(0, 4, 1, 2, 3)[:, None]
    new_diff_v = va_f.reshape(batch, 1, seq, A_HEADS, 2 * HEAD_DIM)
    new_win_k = kbt_f.reshape(batch, B_KV_HEADS, HEAD_DIM, seq).transpose(0, 3, 1, 2)[:, None]
    new_win_v = vbt.reshape(batch, B_KV_HEADS, HEAD_DIM, seq).transpose(0, 3, 1, 2)[:, None]

    xs = x_sample.reshape(dec_batch * dec_seq, D_MODEL)
    lat_row = lambda i: i // (dec_seq // tm)
    qat, qbt, ka, kb, vat, vbt = _qkv_proj(
        x_sample, mod3, lat_row, g1, w_k, w_t, gsum, kgains, qgains, _rope_tables(dec_seq), tm, False)
    ckt_a = cache_diff_k[:, l].transpose(0, 2, 3, 4, 1).reshape(dec_batch, A_W, past)
    cv_a = cache_diff_v[:, l].reshape(dec_batch, past * A_HEADS, 2 * HEAD_DIM)
    ckt_b = cache_win_k[:, l].transpose(0, 2, 3, 1).reshape(dec_batch, B_KVW, past)
    cvt_b = cache_win_v[:, l].transpose(0, 2, 3, 1).reshape(dec_batch, B_KVW, past)
    oa_s, w_gate_b, w_up_b, w_down_b = _diff_lat(
        qat, ka.reshape(dec_batch, dec_seq, A_W), vat, ckt_a, cv_a, lam_vecs, subg_col, bounds,
        [w_gate[l], w_up[l], w_down[l]])
    ob_s, w_oa_b, w_ob_b, w_out_b = _win_lat(
        qbt, kb.reshape(dec_batch, dec_seq, B_KVW), vbt, ckt_b, cvt_b, sink_rows, bounds,
        [w_oa[l], w_ob[l], w_out[l]])

    n_ctx_tiles = batch * seq // tm
    both_row = lambda i: jnp.where(i < n_ctx_tiles, dec_batch, (i - n_ctx_tiles) // (dec_seq // tm))
    xp1, xs1 = _merge(xp, xs, oa_p, oa_s.reshape(-1, A_W), ob_p, ob_s.reshape(-1, B_Q), mod3, both_row, g1,
                      w_in_b, w_oa_b, w_ob_b, w_out_b, tm)
    y_prompt, y_sample = _ffn(xp1, xs1, mod3, both_row, g2, w_gate_b, w_up_b, w_down_b, tm)

    return (y_prompt.reshape(batch, seq, D_MODEL),
            y_sample.reshape(dec_batch, dec_seq, D_MODEL),
            new_diff_k, new_diff_v, new_win_k, new_win_v)
```

```python
import functools
import math

import jax
import jax.numpy as jnp
import numpy as np
from jax import lax
from jax.experimental import pallas as pl
from jax.experimental.pallas import tpu as pltpu

D_MODEL = 1024
HEAD_DIM = 64
A_HEADS = 8
B_HEADS = 16
B_KV_HEADS = 4
B_GROUP = B_HEADS // B_KV_HEADS
A_W = A_HEADS * 2 * HEAD_DIM
B_Q = B_HEADS * HEAD_DIM
B_KVW = B_KV_HEADS * HEAD_DIM
N_K = A_W + B_KVW
N_T = A_W + B_Q + A_W + B_KVW
GATE_COL = 3 * A_W + B_Q + 2 * B_KVW
WINDOW = 128
BLOCK = 128
GRID_W = 64
D_FF = 2816
N_MOD = 6
EPS = 1e-6
ROPE_BASE = 10000.0
NEG = -1e30
LAMBDA_INIT = 0.8 - 0.6 * math.exp(-0.3 * 0)
LOG2E = math.log2(math.e)
Q_SCALE = HEAD_DIM ** -0.5 * LOG2E

LANES = 128
CHUNK = 256
ONES_ROWS = 16
KEY_CHUNK = 512
ROW_SUBTILE = 256
DIFF_QUERIES = 128
SHIFT_FREE_LOG2 = 60.0
VMEM_LIMIT = 56 * 1024 * 1024

F32 = jnp.float32
BF16 = jnp.bfloat16


def _params(n_axes, flags=None):
    return pltpu.CompilerParams(dimension_semantics=("arbitrary",) * n_axes,
                                vmem_limit_bytes=VMEM_LIMIT, flags=flags)


def _const_spec(shape):
    nd = len(shape)
    return pl.BlockSpec(shape, lambda *_: (0,) * nd, pipeline_mode=pl.Buffered(1))


def _dot(a, b):
    return jnp.dot(a, b, preferred_element_type=F32)


def _modulated_norm(x, gain, scale, shift):
    y = x * lax.rsqrt(jnp.mean(x * x, axis=-1, keepdims=True) + EPS) * gain
    return y * (1.0 + scale) + shift


def _software_pipeline(n, stages):
    depth = len(stages)
    state = [None] * n
    for step in range(n + depth - 1):
        for k, stage in enumerate(stages):
            i = step - k
            if 0 <= i < n:
                state[i] = stage(i, state[i])


def _mod_kernel(c_ref, w_ref, b_ref, o_ref):
    cc = c_ref[...]
    s = cc * jax.nn.sigmoid(cc)
    o_ref[...] = _dot(s.astype(BF16), w_ref[...].astype(BF16)) + b_ref[...]


def _modulation(cc, w_ada, b_ada):
    rows, n = cc.shape[0], w_ada.shape[1]
    tn = n // 4
    return pl.pallas_call(
        _mod_kernel,
        grid=(n // tn,),
        in_specs=[pl.BlockSpec((rows, D_MODEL), lambda j: (0, 0)),
                  pl.BlockSpec((D_MODEL, tn), lambda j: (0, j)),
                  pl.BlockSpec((1, tn), lambda j: (0, j))],
        out_specs=pl.BlockSpec((rows, tn), lambda j: (0, j)),
        out_shape=jax.ShapeDtypeStruct((rows, n), F32),
        compiler_params=_params(1),
        name="adaln_mod",
    )(cc, w_ada, b_ada)


def _qkv_kernel(*refs, rope, emit_new_kv):
    x_ref, mod_ref, g1_ref, wk_ref, wt_ref, gsum_ref, kgain_ref, qgain_ref = refs[:8]
    refs = refs[8:]
    if rope:
        cos_ref, sa_ref, sb_ref, cost_ref, sint_ref = refs[:5]
        refs = refs[5:]
    qat_ref, qbt_ref, ka_ref, kb_ref, vat_ref, vbt_ref = refs[:6]
    if emit_new_kv:
        kat_ref, va_ref, kbt_ref = refs[6:]

    mod = mod_ref[...]
    h = _modulated_norm(x_ref[...], g1_ref[...], mod[:, D_MODEL:2 * D_MODEL], mod[:, 0:D_MODEL])
    hb = h.astype(BF16)
    gsum = gsum_ref[...]

    def put_feat(ref, rows, val):
        n_seq, _, width = ref.shape
        for s in range(n_seq):
            ref[s, rows, :] = val[:, s * width:(s + 1) * width].astype(ref.dtype)

    n_wide = A_W // CHUNK
    key_cols = [j * CHUNK for j in range(n_wide)] + [A_W]

    def key_project(j, _):
        return _dot(hb, wk_ref[:, key_cols[j]:key_cols[j] + CHUNK])

    def key_squares(j, p):
        return p, _dot((p * p).astype(BF16), gsum)

    def key_finish(j, st):
        p, ss = st
        row = 0 if j < n_wide else 1
        p = p * lax.rsqrt(ss * (1.0 / HEAD_DIM) + EPS) * kgain_ref[row:row + 1, :]
        if rope:
            cos, sa, sb = cos_ref[...], sa_ref[...], sb_ref[...]
            halves = []
            for i in range(CHUNK // LANES):
                xh = p[:, i * LANES:(i + 1) * LANES]
                nxt = pltpu.roll(xh, LANES - HEAD_DIM // 4, 1)
                prv = pltpu.roll(xh, HEAD_DIM // 4, 1)
                halves.append(xh * cos + nxt * sa + prv * sb)
            p = jnp.concatenate(halves, axis=1)
        if j < n_wide:
            sl = slice(j * CHUNK, (j + 1) * CHUNK)
            ka_ref[:, sl] = p.astype(BF16)
            if emit_new_kv:
                put_feat(kat_ref, sl, p.T)
        else:
            kb_ref[...] = p.astype(BF16)
            if emit_new_kv:
                put_feat(kbt_ref, slice(None), p.T)

    _software_pipeline(len(key_cols), [key_project, key_squares, key_finish])

    ht = h.T.astype(BF16)
    n_feat = N_T // CHUNK

    def feat_project(c, _):
        return _dot(wt_ref[c * CHUNK:(c + 1) * CHUNK, :], ht)

    def feat_finish(c, p):
        kind, j = divmod(c, n_wide)
        sl = slice(j * CHUNK, (j + 1) * CHUNK)
        if kind == 2:
            put_feat(vat_ref, sl, p)
            if emit_new_kv:
                va_ref[:, sl] = p.T
            return
        if kind == 3:
            put_feat(vbt_ref, slice(None), p)
            return
        gain = qgain_ref[kind]
        q4 = HEAD_DIM // 4
        heads = []
        for r in range(CHUNK // HEAD_DIM):
            x = p[r * HEAD_DIM:(r + 1) * HEAD_DIM, :]
            x = x * lax.rsqrt(jnp.mean(x * x, axis=0, keepdims=True) + EPS) * gain
            if rope:
                swapped = jnp.concatenate([x[q4:2 * q4], x[0:q4], x[3 * q4:4 * q4], x[2 * q4:3 * q4]],
                                          axis=0)
                x = x * cost_ref[...] + swapped * sint_ref[...]
            heads.append(x)
        put_feat(qat_ref if kind == 0 else qbt_ref, sl, jnp.concatenate(heads, axis=0) * Q_SCALE)

    _software_pipeline(n_feat, [feat_project, feat_finish])


def _qkv_proj(x3d, mod3, mod_row, g1, w_k, w_t, gsum, kgains, qgains, rope_tabs, tm, emit_new_kv):
    b, t, _ = x3d.shape
    rows = b * t
    rope = rope_tabs is not None
    tps = max(t // tm, 1)
    spt = max(tm // t, 1)
    tw = tm // spt
    in_specs = [pl.BlockSpec((tm, D_MODEL), lambda i: (i, 0)),
                pl.BlockSpec((None, 1, N_MOD * D_MODEL), lambda i: (mod_row(i), 0, 0)),
                _const_spec((1, D_MODEL)),
                _const_spec((D_MODEL, N_K)),
                _const_spec((N_T, D_MODEL)),
                _const_spec((CHUNK, CHUNK)),
                _const_spec((2, CHUNK)),
                _const_spec((2, HEAD_DIM, 1))]
    args = [x3d.reshape(rows, D_MODEL), mod3, g1, w_k, w_t, gsum, kgains, qgains]
    if rope:
        nat, featm = rope_tabs
        in_specs += [pl.BlockSpec((tm, LANES), lambda i: (i % tps, 0))] * 3
        in_specs += [pl.BlockSpec((HEAD_DIM, tm), lambda i: (0, i % tps))] * 2
        args += list(nat) + list(featm)
    tok_wide = pl.BlockSpec((tm, A_W), lambda i: (i, 0))
    tok_narrow = pl.BlockSpec((tm, B_KVW), lambda i: (i, 0))
    feat_wide = pl.BlockSpec((spt, A_W, tw), lambda i: (i // tps, 0, i % tps))
    feat_narrow = pl.BlockSpec((spt, B_KVW, tw), lambda i: (i // tps, 0, i % tps))
    out_specs = [feat_wide, feat_wide, tok_wide, tok_narrow, feat_wide, feat_narrow]
    out_shape = [jax.ShapeDtypeStruct((b, A_W, t), BF16),
                 jax.ShapeDtypeStruct((b, B_Q, t), BF16),
                 jax.ShapeDtypeStruct((rows, A_W), BF16),
                 jax.ShapeDtypeStruct((rows, B_KVW), BF16),
                 jax.ShapeDtypeStruct((b, A_W, t), BF16),
                 jax.ShapeDtypeStruct((b, B_KVW, t), F32 if emit_new_kv else BF16)]
    if emit_new_kv:
        out_specs += [feat_wide, tok_wide, feat_narrow]
        out_shape += [jax.ShapeDtypeStruct((b, A_W, t), F32),
                      jax.ShapeDtypeStruct((rows, A_W), F32),
                      jax.ShapeDtypeStruct((b, B_KVW, t), F32)]
    return pl.pallas_call(
        functools.partial(_qkv_kernel, rope=rope, emit_new_kv=emit_new_kv),
        grid=(rows // tm,),
        in_specs=in_specs,
        out_specs=out_specs,
        out_shape=out_shape,
        compiler_params=_params(1),
        name="qkv_rope" if rope else "qkv_ctx",
    )(*args)


def _lambda(lam_ref):
    lv = lam_ref[...]
    t1 = jnp.sum(lv[0:1] * lv[1:2], axis=-1, keepdims=True)
    t2 = jnp.sum(lv[2:3] * lv[3:4], axis=-1, keepdims=True)
    return jnp.exp(t1) - jnp.exp(t2) + LAMBDA_INIT


def _with_ones(vt):
    return jnp.concatenate([vt, jnp.ones((ONES_ROWS, vt.shape[1]), vt.dtype)], axis=0)


def _column_softmax(scores, extra=None):
    m = functools.reduce(jnp.maximum, [jnp.max(s, axis=0, keepdims=True) for s in scores])
    if extra is not None:
        m = jnp.maximum(m, extra)
    return [jnp.exp2(s - m).astype(BF16) for s in scores], m


def _max_row_sq_norm(x):
    x = x.astype(F32)
    return jnp.max(jnp.sum(x * x, axis=1, keepdims=True), axis=0, keepdims=True)[0, 0]


def _scores_bounded(q_sq, k_sq, sink_abs=0.0):
    limit = SHIFT_FREE_LOG2
    return jnp.logical_and(q_sq * k_sq <= limit * limit, sink_abs <= limit)


def _diff_query_cols(qt):
    zero = jnp.zeros((HEAD_DIM, qt.shape[1]), qt.dtype)
    return jnp.concatenate([jnp.concatenate([qt[:HEAD_DIM], zero], axis=0),
                            jnp.concatenate([zero, qt[HEAD_DIM:]], axis=0)], axis=1)


def _diff_output(r, lam, subg_col):
    n = r.shape[1] // 2
    dv = 2 * HEAD_DIM
    tot = r[dv:dv + 1, :]
    o = r[:dv, :n] * (1.0 / tot[:, :n]) - r[:dv, n:] * (lam / tot[:, n:])
    o = o * lax.rsqrt(jnp.mean(o * o, axis=0, keepdims=True) + EPS)
    return (o * subg_col * (1.0 - LAMBDA_INIT)).T


def _group_output(r, m, sink_row):
    n = r.shape[1] // B_GROUP
    tot = r[HEAD_DIM:HEAD_DIM + 1, :] + jnp.exp2(sink_row - m)
    o = r[:HEAD_DIM, :] * (1.0 / tot)
    return jnp.concatenate([o[:, i * n:(i + 1) * n] for i in range(B_GROUP)], axis=0).T


def _sink_row(sink_ref, g, n):
    reps = n // LANES
    return jnp.concatenate([sink_ref[pl.ds(g * B_GROUP + r, 1), :] for r in range(B_GROUP)
                            for _ in range(reps)], axis=1)


def _ctx_attn_kernel(qat_ref, qbt_ref, ka_ref, kb_ref, vat_ref, vbt_ref, lam_ref, subg_ref, sink_ref,
                     bounds_ref, oa_ref, ob_ref):
    lam = _lambda(lam_ref)
    subg_col = subg_ref[...]
    n_seq, _, n = qat_ref.shape
    chains = A_HEADS + B_KV_HEADS

    def scores(item, _):
        s, c = divmod(item, chains)
        rows = slice(s * n, (s + 1) * n)
        if c < A_HEADS:
            sl = slice(c * LANES, (c + 1) * LANES)
            return _dot(ka_ref[rows, sl], _diff_query_cols(qat_ref[s, sl, :]))
        g = c - A_HEADS
        qs = jnp.concatenate([qbt_ref[s, (g * B_GROUP + r) * HEAD_DIM:(g * B_GROUP + r + 1) * HEAD_DIM, :]
                              for r in range(B_GROUP)], axis=1)
        zero = jnp.zeros_like(qs)
        w = jnp.concatenate([qs if i == g else zero for i in range(B_KV_HEADS)], axis=0)
        return _dot(kb_ref[rows, :], w)

    def probs(shift, item, sc):
        c = item % chains
        if not shift:
            return [jnp.exp2(sc).astype(BF16)], jnp.zeros((1, sc.shape[1]), F32)
        if c < A_HEADS:
            return _column_softmax([sc])
        return _column_softmax([sc], extra=_sink_row(sink_ref, c - A_HEADS, n))

    def outputs(item, st):
        s, c = divmod(item, chains)
        rows = slice(s * n, (s + 1) * n)
        es, m = st
        if c < A_HEADS:
            sl = slice(c * LANES, (c + 1) * LANES)
            r = _dot(_with_ones(vat_ref[s, sl, :]), es[0])
            oa_ref[rows, sl] = _diff_output(r, lam, subg_col).astype(BF16)
        else:
            g = c - A_HEADS
            vt = vbt_ref[s, g * HEAD_DIM:(g + 1) * HEAD_DIM, :].astype(BF16)
            r = _dot(_with_ones(vt), es[0])
            o = _group_output(r, m, _sink_row(sink_ref, g, n))
            ob_ref[rows, g * CHUNK:(g + 1) * CHUNK] = o.astype(BF16)

    bounded = jnp.logical_and(_scores_bounded(bounds_ref[0], bounds_ref[1]),
                              _scores_bounded(bounds_ref[2], bounds_ref[3], bounds_ref[4]))

    @pl.when(bounded)
    def _():
        _software_pipeline(n_seq * chains, [scores, functools.partial(probs, False), outputs])

    @pl.when(jnp.logical_not(bounded))
    def _():
        _software_pipeline(n_seq * chains, [scores, functools.partial(probs, True), outputs])


def _ctx_front_kernel(x_ref, mod_ref, g1_ref, wk_ref, wt_ref, gsum_ref, kgain_ref, qgain_ref,
                      lam_ref, subg_ref, sink_ref, bounds_ref,
                      oa_ref, ob_ref, vbt_ref, kat_ref, va_ref, kbt_ref,
                      qat_s, qbt_s, ka_s, kb_s, vat_s):
    _qkv_kernel(x_ref, mod_ref, g1_ref, wk_ref, wt_ref, gsum_ref, kgain_ref, qgain_ref,
                qat_s, qbt_s, ka_s, kb_s, vat_s, vbt_ref, kat_ref, va_ref, kbt_ref,
                rope=False, emit_new_kv=True)
    _ctx_attn_kernel(qat_s, qbt_s, ka_s, kb_s, vat_s, vbt_ref, lam_ref, subg_ref, sink_ref, bounds_ref,
                     oa_ref, ob_ref)


def _ctx_front(x3d, mod3, mod_row, g1, w_k, w_t, gsum, kgains, qgains, lam_vecs, subg_col, sink_rows, bounds,
               n_seq):
    b, t, _ = x3d.shape
    rows, tm = b * t, n_seq * t
    tok = pl.BlockSpec((tm, A_W), lambda i: (i, 0))
    feat_wide = pl.BlockSpec((n_seq, A_W, t), lambda i: (i, 0, 0))
    feat_narrow = pl.BlockSpec((n_seq, B_KVW, t), lambda i: (i, 0, 0))
    return pl.pallas_call(
        _ctx_front_kernel,
        grid=(b // n_seq,),
        in_specs=[pl.BlockSpec((tm, D_MODEL), lambda i: (i, 0)),
                  pl.BlockSpec((None, 1, N_MOD * D_MODEL), lambda i: (mod_row(i), 0, 0)),
                  _const_spec((1, D_MODEL)), _const_spec((D_MODEL, N_K)), _const_spec((N_T, D_MODEL)),
                  _const_spec((CHUNK, CHUNK)), _const_spec((2, CHUNK)), _const_spec((2, HEAD_DIM, 1)),
                  _const_spec((4, HEAD_DIM)), _const_spec((2 * HEAD_DIM, 1)),
                  _const_spec((B_HEADS, LANES)), pl.BlockSpec(memory_space=pltpu.SMEM)],
        out_specs=[tok, tok, feat_narrow, feat_wide, tok, feat_narrow],
        out_shape=[jax.ShapeDtypeStruct((rows, A_W), BF16), jax.ShapeDtypeStruct((rows, B_Q), BF16),
                   jax.ShapeDtypeStruct((b, B_KVW, t), F32), jax.ShapeDtypeStruct((b, A_W, t), F32),
                   jax.ShapeDtypeStruct((rows, A_W), F32), jax.ShapeDtypeStruct((b, B_KVW, t), F32)],
        scratch_shapes=[pltpu.VMEM((n_seq, A_W, t), BF16), pltpu.VMEM((n_seq, B_Q, t), BF16),
                        pltpu.VMEM((tm, A_W), BF16), pltpu.VMEM((tm, B_KVW), BF16),
                        pltpu.VMEM((n_seq, A_W, t), BF16)],
        compiler_params=_params(1),
        name="ctx_front",
    )(x3d.reshape(rows, D_MODEL), mod3, g1, w_k, w_t, gsum, kgains, qgains, lam_vecs, subg_col, sink_rows,
      bounds)


def _rider_specs(weights, n_steps, flat_step):
    specs = []
    for w in weights:
        rows, cols = w.shape
        bands = next(n for n in (n_steps, n_steps // 2, n_steps // 4)
                     if rows % n == 0 and (rows // n) % 16 == 0)
        repeat = n_steps // bands
        spec = pl.BlockSpec((rows // bands, cols), lambda *g, repeat=repeat: (flat_step(*g) // repeat, 0))
        specs.append((spec, jax.ShapeDtypeStruct(w.shape, BF16)))
    return specs


def _cast_riders(rider_refs):
    n = len(rider_refs) // 2
    for w_ref, o_ref in zip(rider_refs[:n], rider_refs[n:]):
        o_ref[...] = w_ref[...].astype(BF16)


def _diff_lat_kernel(qt_ref, k_ref, vt_ref, ckt_ref, cv_ref, lam_ref, subg_ref, bounds_ref, *rest, n_riders):
    o_ref = rest[n_riders]
    _cast_riders(rest[:n_riders] + rest[n_riders + 1:])
    lam = _lambda(lam_ref)
    subg_col = subg_ref[...]
    k = k_ref[...]
    ck = ckt_ref[...].T.astype(BF16)
    vt = _with_ones(vt_ref[...])
    past = cv_ref.shape[0] // A_HEADS
    cv = cv_ref[pl.ds(pl.program_id(1), past, stride=A_HEADS), :]
    cvt = _with_ones(cv.T.astype(BF16))

    def scores(i, _):
        q = _diff_query_cols(qt_ref[:, i * DIFF_QUERIES:(i + 1) * DIFF_QUERIES])
        return [_dot(k, q), _dot(ck, q)]

    def outputs(shift, i, s):
        if shift:
            m = jnp.maximum(jnp.max(s[0], axis=0, keepdims=True), jnp.max(s[1], axis=0, keepdims=True))
        r = None
        for sc, v in ((s[0], vt), (s[1], cvt)):
            for c in range(0, sc.shape[0], KEY_CHUNK):
                piece = sc[c:c + KEY_CHUNK]
                part = _dot(v[:, c:c + KEY_CHUNK], jnp.exp2(piece - m if shift else piece).astype(BF16))
                r = part if r is None else r + part
        o_ref[i * DIFF_QUERIES:(i + 1) * DIFF_QUERIES, :] = _diff_output(r, lam, subg_col).astype(BF16)

    n_sub = qt_ref.shape[1] // DIFF_QUERIES
    bounded = _scores_bounded(bounds_ref[0], jnp.maximum(bounds_ref[1], _max_row_sq_norm(ck)))

    @pl.when(bounded)
    def _():
        _software_pipeline(n_sub, [scores, functools.partial(outputs, False)])

    @pl.when(jnp.logical_not(bounded))
    def _():
        _software_pipeline(n_sub, [scores, functools.partial(outputs, True)])


def _diff_lat(qt, k, vt, ckt, cv, lam_vecs, subg_col, bounds, riders):
    b, _, t = qt.shape
    past = ckt.shape[2]
    rider_specs = _rider_specs(riders, b * A_HEADS, lambda i, h: i * A_HEADS + h)
    return pl.pallas_call(
        functools.partial(_diff_lat_kernel, n_riders=len(riders)),
        grid=(b, A_HEADS),
        in_specs=[pl.BlockSpec((None, LANES, t), lambda i, h: (i, h, 0)),
                  pl.BlockSpec((None, t, LANES), lambda i, h: (i, 0, h)),
                  pl.BlockSpec((None, LANES, t), lambda i, h: (i, h, 0)),
                  pl.BlockSpec((None, LANES, past), lambda i, h: (i, h, 0)),
                  pl.BlockSpec((None, past * A_HEADS, LANES), lambda i, h: (i, 0, 0)),
                  _const_spec((4, HEAD_DIM)), _const_spec((2 * HEAD_DIM, 1)),
                  pl.BlockSpec(memory_space=pltpu.SMEM)] + [spec for spec, _ in rider_specs],
        out_specs=[pl.BlockSpec((None, t, LANES), lambda i, h: (i, 0, h))] + [spec for spec, _ in rider_specs],
        out_shape=[jax.ShapeDtypeStruct((b, t, A_W), BF16)] + [shape for _, shape in rider_specs],
        compiler_params=_params(2),
        name="diff_lat",
    )(qt, k, vt, ckt, cv, lam_vecs, subg_col, bounds, *riders)


def _window_start(i, t):
    return min(max((i - 1) * BLOCK, 0), t - 3 * BLOCK)


def _band_offsets(t):
    return sorted({_window_start(i, t) - i * BLOCK for i in range(t // BLOCK)})


def _band_bias(t):
    key = np.arange(3 * BLOCK)[:, None]
    qry = np.arange(B_GROUP * BLOCK)[None, :] % BLOCK
    return jnp.asarray(np.stack([np.where(np.abs(key + off - qry) <= WINDOW, 0.0, NEG)
                                 for off in _band_offsets(t)]), dtype=F32)


def _win_lat_kernel(qt_ref, k_ref, vt_ref, ckt_ref, cvt_ref, sink_ref, bias_ref, bounds_ref, *rest, n_riders):
    o_ref = rest[n_riders]
    _cast_riders(rest[:n_riders] + rest[n_riders + 1:])
    g = pl.program_id(1)
    t = k_ref.shape[0]
    span = 3 * BLOCK
    keep = lax.broadcasted_iota(jnp.int32, (1, CHUNK), 1) // HEAD_DIM == g
    k = jnp.where(keep, k_ref[...].astype(F32), 0.0).astype(BF16)
    ck = jnp.where(keep, ckt_ref[...].T, 0.0).astype(BF16)
    vt = vt_ref[...]
    cvt = _with_ones(cvt_ref[...].astype(BF16))
    sink_row = _sink_row(sink_ref, g, BLOCK)
    offsets = _band_offsets(t)

    def scores(i, _):
        w = _window_start(i, t)
        qs = jnp.concatenate([qt_ref[r * HEAD_DIM:(r + 1) * HEAD_DIM, i * BLOCK:(i + 1) * BLOCK]
                              for r in range(B_GROUP)], axis=1)
        wq = jnp.concatenate([qs] * B_KV_HEADS, axis=0)
        return [_dot(k[w:w + span], wq) + bias_ref[offsets.index(w - i * BLOCK)], _dot(ck, wq)]

    def probs(shift, i, s):
        if shift:
            return _column_softmax(s, extra=sink_row)
        return [jnp.exp2(piece).astype(BF16) for piece in s], jnp.zeros_like(sink_row)

    def outputs(i, st):
        es, m = st
        w = _window_start(i, t)
        r = _dot(_with_ones(vt[:, w:w + span]), es[0]) + _dot(cvt, es[1])
        o_ref[i * BLOCK:(i + 1) * BLOCK, :] = _group_output(r, m, sink_row).astype(BF16)

    bounded = _scores_bounded(bounds_ref[2], jnp.maximum(bounds_ref[3], _max_row_sq_norm(ck)), bounds_ref[4])

    @pl.when(bounded)
    def _():
        _software_pipeline(t // BLOCK, [scores, functools.partial(probs, False), outputs])

    @pl.when(jnp.logical_not(bounded))
    def _():
        _software_pipeline(t // BLOCK, [scores, functools.partial(probs, True), outputs])


def _win_lat(qt, k, vt, ckt, cvt, sink_rows, bounds, riders):
    b, _, t = qt.shape
    past = ckt.shape[2]
    bias = _band_bias(t)
    rider_specs = _rider_specs(riders, b * B_KV_HEADS, lambda i, g: i * B_KV_HEADS + g)
    return pl.pallas_call(
        functools.partial(_win_lat_kernel, n_riders=len(riders)),
        grid=(b, B_KV_HEADS),
        in_specs=[pl.BlockSpec((None, CHUNK, t), lambda i, g: (i, g, 0)),
                  pl.BlockSpec((None, t, B_KVW), lambda i, g: (i, 0, 0)),
                  pl.BlockSpec((None, HEAD_DIM, t), lambda i, g: (i, g, 0)),
                  pl.BlockSpec((None, B_KVW, past), lambda i, g: (i, 0, 0)),
                  pl.BlockSpec((None, HEAD_DIM, past), lambda i, g: (i, g, 0)),
                  _const_spec((B_HEADS, LANES)), _const_spec(bias.shape),
                  pl.BlockSpec(memory_space=pltpu.SMEM)] + [spec for spec, _ in rider_specs],
        out_specs=[pl.BlockSpec((None, t, CHUNK), lambda i, g: (i, 0, g))] + [spec for spec, _ in rider_specs],
        out_shape=[jax.ShapeDtypeStruct((b, t, B_Q), BF16)] + [shape for _, shape in rider_specs],
        compiler_params=_params(2),
        name="win_lat",
    )(qt, k, vt, ckt, cvt, sink_rows, bias, bounds, *riders)


def _merge_kernel(x1_ref, x2_ref, oa1_ref, oa2_ref, ob1_ref, ob2_ref, mod_ref, g1_ref, win_ref, woa_ref,
                  wob_ref, wout_ref, o1_ref, o2_ref, *, n_first):
    def tile(x_ref, oa_ref, ob_ref, o_ref):
        mod = mod_ref[...]
        rows = lambda i: slice(i * ROW_SUBTILE, (i + 1) * ROW_SUBTILE)

        def norm(i, _):
            return _modulated_norm(x_ref[rows(i), :], g1_ref[...], mod[:, D_MODEL:2 * D_MODEL],
                                   mod[:, 0:D_MODEL]).astype(BF16)

        def branches(i, h):
            ga = _dot(h, win_ref[:, GATE_COL:GATE_COL + D_MODEL])
            gb = _dot(h, win_ref[:, GATE_COL + D_MODEL:GATE_COL + 2 * D_MODEL])
            return (jax.nn.sigmoid(ga) * _dot(oa_ref[rows(i), :], woa_ref[...])
                    + jax.nn.sigmoid(gb) * _dot(ob_ref[rows(i), :], wob_ref[...])).astype(BF16)

        def project(i, merged):
            o_ref[rows(i), :] = x_ref[rows(i), :] + mod[:, 2 * D_MODEL:3 * D_MODEL] * _dot(merged, wout_ref[...])

        _software_pipeline(x_ref.shape[0] // ROW_SUBTILE, [norm, branches, project])

    _per_token_set(n_first, functools.partial(tile, x1_ref, oa1_ref, ob1_ref, o1_ref),
                   functools.partial(tile, x2_ref, oa2_ref, ob2_ref, o2_ref))


def _per_token_set(n_first, first_body, second_body):
    first = pl.program_id(0) < n_first
    pl.when(first)(first_body)
    pl.when(jnp.logical_not(first))(second_body)


def _split_tiles(n_first, tm, width):
    return (pl.BlockSpec((tm, width), lambda i: (jnp.minimum(i, n_first - 1), 0)),
            pl.BlockSpec((tm, width), lambda i: (jnp.maximum(i - n_first, 0), 0)))


def _merge(x1, x2, oa1, oa2, ob1, ob2, mod3, mod_row, g1, w_in, w_oa, w_ob, w_out, tm):
    n_first, n_second = x1.shape[0] // tm, x2.shape[0] // tm
    t1, t2 = _split_tiles(n_first, tm, D_MODEL)
    sq = _const_spec((D_MODEL, D_MODEL))
    return pl.pallas_call(
        functools.partial(_merge_kernel, n_first=n_first),
        grid=(n_first + n_second,),
        in_specs=[t1, t2, t1, t2, t1, t2,
                  pl.BlockSpec((None, 1, N_MOD * D_MODEL), lambda i: (mod_row(i), 0, 0)),
                  _const_spec((1, D_MODEL)), _const_spec(w_in.shape), sq, sq, sq],
        out_specs=[t1, t2],
        out_shape=[jax.ShapeDtypeStruct(x1.shape, F32), jax.ShapeDtypeStruct(x2.shape, F32)],
        compiler_params=_params(1),
        name="merge_out",
    )(x1, x2, oa1, oa2, ob1, ob2, mod3, g1, w_in, w_oa, w_ob, w_out)


def _ffn_kernel(x1_ref, x2_ref, mod_ref, g2_ref, wgate_ref, wup_ref, wdown_ref, o1_ref, o2_ref, *, n_first):
    def tile(x_ref, o_ref):
        mod = mod_ref[...]
        rows = lambda i: slice(i * ROW_SUBTILE, (i + 1) * ROW_SUBTILE)

        def norm(i, _):
            return _modulated_norm(x_ref[rows(i), :], g2_ref[...], mod[:, 4 * D_MODEL:5 * D_MODEL],
                                   mod[:, 3 * D_MODEL:4 * D_MODEL]).astype(BF16)

        def hidden(i, h):
            gate = _dot(h, wgate_ref[...])
            return (gate * jax.nn.sigmoid(gate) * _dot(h, wup_ref[...])).astype(BF16)

        def project(i, act):
            o_ref[rows(i), :] = x_ref[rows(i), :] + mod[:, 5 * D_MODEL:6 * D_MODEL] * _dot(act, wdown_ref[...])

        _software_pipeline(x_ref.shape[0] // ROW_SUBTILE, [norm, hidden, project])

    _per_token_set(n_first, functools.partial(tile, x1_ref, o1_ref), functools.partial(tile, x2_ref, o2_ref))


def _ffn(x1, x2, mod3, mod_row, g2, w_gate, w_up, w_down, tm):
    n_first, n_second = x1.shape[0] // tm, x2.shape[0] // tm
    t1, t2 = _split_tiles(n_first, tm, D_MODEL)
    return pl.pallas_call(
        functools.partial(_ffn_kernel, n_first=n_first),
        grid=(n_first + n_second,),
        in_specs=[t1, t2,
                  pl.BlockSpec((None, 1, N_MOD * D_MODEL), lambda i: (mod_row(i), 0, 0)),
                  _const_spec((1, D_MODEL)), _const_spec((D_MODEL, D_FF)),
                  _const_spec((D_MODEL, D_FF)), _const_spec((D_FF, D_MODEL))],
        out_specs=[t1, t2],
        out_shape=[jax.ShapeDtypeStruct(x1.shape, F32), jax.ShapeDtypeStruct(x2.shape, F32)],
        compiler_params=_params(1),
        name="swiglu_ffn",
    )(x1, x2, mod3, g2, w_gate, w_up, w_down)


def _rope_tables(seq_len):
    quarter = HEAD_DIM // 4
    t = np.arange(seq_len)
    row = (t // GRID_W).astype(np.float32)
    col = (t % GRID_W).astype(np.float32)
    freqs = (np.float32(ROPE_BASE) ** (-np.arange(quarter, dtype=np.float32) / np.float32(quarter))
             ).astype(np.float32)
    ang_r = row[:, None] * freqs
    ang_c = col[:, None] * freqs
    zeros = np.zeros_like(ang_r)
    cos = np.concatenate([np.cos(ang_r)] * 2 + [np.cos(ang_c)] * 2, axis=-1)
    sin_next = np.concatenate([-np.sin(ang_r), zeros, -np.sin(ang_c), zeros], axis=-1)
    sin_prev = np.concatenate([zeros, np.sin(ang_r), zeros, np.sin(ang_c)], axis=-1)
    rep = LANES // HEAD_DIM
    nat = tuple(jnp.asarray(np.tile(a, (1, rep)), dtype=F32) for a in (cos, sin_next, sin_prev))
    featm = (jnp.asarray(cos.T, dtype=F32), jnp.asarray((sin_next + sin_prev).T, dtype=F32))
    return nat, featm


def _group_sum_matrix():
    idx = np.arange(CHUNK) // HEAD_DIM
    return jnp.asarray(idx[:, None] == idx[None, :], dtype=BF16)


def kernel(x_prompt, x_sample, cache_diff_k, cache_diff_v, cache_win_k, cache_win_v, c, c_ctx,
           w_ada, b_ada, norm1_g, w_in, qn_a, kn_a, lambda_q1, lambda_k1, lambda_q2, lambda_k2,
           subln_g, qn_b, kn_b, sink, w_oa, w_ob, w_out, norm2_g, w_gate, w_up, w_down):
    batch, seq, _ = x_prompt.shape
    dec_batch, dec_seq, _ = x_sample.shape
    past = cache_diff_k.shape[2]
    l = 0

    cc = jnp.concatenate([c, c_ctx[None, :], jnp.zeros((8 - dec_batch - 1, D_MODEL), F32)], axis=0)
    mod3 = _modulation(cc, w_ada[l], b_ada[l][None, :]).reshape(8, 1, N_MOD * D_MODEL)

    w_in_l = w_in[l]
    o_ka, o_va, o_qb, o_kb, o_vb, o_g = A_W, 2 * A_W, 3 * A_W, 3 * A_W + B_Q, 3 * A_W + B_Q + B_KVW, \
        3 * A_W + B_Q + 2 * B_KVW
    w_k = jnp.concatenate([w_in_l[:, o_ka:o_va], w_in_l[:, o_kb:o_vb]], axis=1).astype(BF16)
    w_t = jnp.concatenate([w_in_l[:, :o_ka], w_in_l[:, o_qb:o_kb], w_in_l[:, o_va:o_qb],
                           w_in_l[:, o_vb:o_g]], axis=1).T.astype(BF16)
    w_in_b = w_in_l.astype(BF16)
    g1 = norm1_g[l][None, :]
    g2 = norm2_g[l][None, :]
    rep = CHUNK // HEAD_DIM
    kgains = jnp.stack([jnp.tile(kn_a[l], rep), jnp.tile(kn_b[l], rep)])
    qgains = jnp.stack([qn_a[l], qn_b[l]])[:, :, None]
    lam_vecs = jnp.stack([lambda_q1[l], lambda_k1[l], lambda_q2[l], lambda_k2[l]])
    subg_col = subln_g[l][:, None]
    sink_rows = jnp.broadcast_to((sink[l] * LOG2E)[:, None], (B_HEADS, LANES))
    head_sq = lambda g, scale: 1.02 * HEAD_DIM * jnp.max(jnp.square(g[l])) * scale * scale
    bounds = jnp.stack([head_sq(qn_a, Q_SCALE), head_sq(kn_a, 1.0), head_sq(qn_b, Q_SCALE),
                        head_sq(kn_b, 1.0), jnp.max(jnp.abs(sink[l])) * LOG2E]).astype(F32)
    gsum = _group_sum_matrix()
    tm = 512

    xp = x_prompt.reshape(batch * seq, D_MODEL)
    ctx_row = lambda i: dec_batch
    oa_p, ob_p, vbt, kat_f, va_f, kbt_f = _ctx_front(
        x_prompt, mod3, ctx_row, g1, w_k, w_t, gsum, kgains, qgains, lam_vecs, subg_col, sink_rows, bounds,
        tm // seq)
    new_diff_k = kat_f.reshape(batch, A_HEADS, 2, HEAD_DIM, seq).transpose(0, 4, 1, 2, 3)[:, None]
    new_diff_v = va_f.reshape(batch, 1, seq, A_HEADS, 2 * HEAD_DIM)
    new_win_k = kbt_f.reshape(batch, B_KV_HEADS, HEAD_DIM, seq).transpose(0, 3, 1, 2)[:, None]
    new_win_v = vbt.reshape(batch, B_KV_HEADS, HEAD_DIM, seq).transpose(0, 3, 1, 2)[:, None]

    xs = x_sample.reshape(dec_batch * dec_seq, D_MODEL)
    lat_row = lambda i: i // (dec_seq // tm)
    qat, qbt, ka, kb, vat, vbt = _qkv_proj(
        x_sample, mod3, lat_row, g1, w_k, w_t, gsum, kgains, qgains, _rope_tables(dec_seq), tm, False)
    ckt_a = cache_diff_k[:, l].transpose(0, 2, 3, 4, 1).reshape(dec_batch, A_W, past)
    cv_a = cache_diff_v[:, l].reshape(dec_batch, past * A_HEADS, 2 * HEAD_DIM)
    ckt_b = cache_win_k[:, l].transpose(0, 2, 3, 1).reshape(dec_batch, B_KVW, past)
    cvt_b = cache_win_v[:, l].transpose(0, 2, 3, 1).reshape(dec_batch, B_KVW, past)
    oa_s, w_gate_b, w_up_b, w_down_b = _diff_lat(
        qat, ka.reshape(dec_batch, dec_seq, A_W), vat, ckt_a, cv_a, lam_vecs, subg_col, bounds,
        [w_gate[l], w_up[l], w_down[l]])
    ob_s, w_oa_b, w_ob_b, w_out_b = _win_lat(
        qbt, kb.reshape(dec_batch, dec_seq, B_KVW), vbt, ckt_b, cvt_b, sink_rows, bounds,
        [w_oa[l], w_ob[l], w_out[l]])

    n_ctx_tiles = batch * seq // tm
    both_row = lambda i: jnp.where(i < n_ctx_tiles, dec_batch, (i - n_ctx_tiles) // (dec_seq // tm))
    xp1, xs1 = _merge(xp, xs, oa_p, oa_s.reshape(-1, A_W), ob_p, ob_s.reshape(-1, B_Q), mod3, both_row, g1,
                      w_in_b, w_oa_b, w_ob_b, w_out_b, tm)
    y_prompt, y_sample = _ffn(xp1, xs1, mod3, both_row, g2, w_gate_b, w_up_b, w_down_b, tm)

    return (y_prompt.reshape(batch, seq, D_MODEL),
            y_sample.reshape(dec_batch, dec_seq, D_MODEL),
            new_diff_k, new_diff_v, new_win_k, new_win_v)
```

```python
import functools
import math

import jax
import jax.numpy as jnp
import numpy as np
from jax import lax
from jax.experimental import pallas as pl
from jax.experimental.pallas import tpu as pltpu

D_MODEL = 1024
HEAD_DIM = 64
A_HEADS = 8
B_HEADS = 16
B_KV_HEADS = 4
B_GROUP = B_HEADS // B_KV_HEADS
A_W = A_HEADS * 2 * HEAD_DIM
B_Q = B_HEADS * HEAD_DIM
B_KVW = B_KV_HEADS * HEAD_DIM
N_K = A_W + B_KVW
N_T = A_W + B_Q + A_W + B_KVW
GATE_COL = 3 * A_W + B_Q + 2 * B_KVW
WINDOW = 128
BLOCK = 128
GRID_W = 64
D_FF = 2816
N_MOD = 6
EPS = 1e-6
ROPE_BASE = 10000.0
NEG = -1e30
LAMBDA_INIT = 0.8 - 0.6 * math.exp(-0.3 * 0)
LOG2E = math.log2(math.e)
Q_SCALE = HEAD_DIM ** -0.5 * LOG2E

LANES = 128
CHUNK = 256
ONES_ROWS = 16
KEY_CHUNK = 512
QKV_SUBTILE = 512
ROW_SUBTILE = 256
FFN_SUBTILE = 128
DIFF_QUERIES = 128
SHIFT_FREE_LOG2 = 60.0
VMEM_LIMIT = 56 * 1024 * 1024

F32 = jnp.float32
BF16 = jnp.bfloat16


def _params(n_axes, flags=None):
    return pltpu.CompilerParams(dimension_semantics=("arbitrary",) * n_axes,
                                vmem_limit_bytes=VMEM_LIMIT, flags=flags)


def _const_spec(shape):
    nd = len(shape)
    return pl.BlockSpec(shape, lambda *_: (0,) * nd, pipeline_mode=pl.Buffered(1))


def _dot(a, b):
    return jnp.dot(a, b, preferred_element_type=F32)


def _modulated_norm(x, gain, scale, shift):
    y = x * lax.rsqrt(jnp.mean(x * x, axis=-1, keepdims=True) + EPS) * gain
    return y * (1.0 + scale) + shift


def _software_pipeline(n, stages):
    depth = len(stages)
    state = [None] * n
    for step in range(n + depth - 1):
        for k, stage in enumerate(stages):
            i = step - k
            if 0 <= i < n:
                state[i] = stage(i, state[i])


def _mod_kernel(c_ref, w_ref, b_ref, o_ref):
    cc = c_ref[...]
    s = cc * jax.nn.sigmoid(cc)
    mod = _dot(s.astype(BF16), w_ref[...].astype(BF16)) + b_ref[...]
    for r in range(mod.shape[0]):
        o_ref[r] = mod[r:r + 1, :]


def _modulation(cc, w_ada, b_ada):
    rows, n = cc.shape[0], w_ada.shape[1]
    tn = n // 4
    return pl.pallas_call(
        _mod_kernel,
        grid=(n // tn,),
        in_specs=[pl.BlockSpec((rows, D_MODEL), lambda j: (0, 0)),
                  pl.BlockSpec((D_MODEL, tn), lambda j: (0, j)),
                  pl.BlockSpec((1, tn), lambda j: (0, j))],
        out_specs=pl.BlockSpec((rows, 1, tn), lambda j: (0, 0, j)),
        out_shape=jax.ShapeDtypeStruct((rows, 1, n), F32),
        compiler_params=_params(1),
        name="adaln_mod",
    )(cc, w_ada, b_ada)


def _qkv_kernel(*refs, rope, emit_new_kv):
    x_ref, mod_ref, g1_ref, wk_ref, wt_ref, gsum_ref, kgain_ref, qgain_ref = refs[:8]
    refs = refs[8:]
    if rope:
        cos_ref, sa_ref, sb_ref, cost_ref, sint_ref = refs[:5]
        refs = refs[5:]
    qat_ref, qbt_ref, ka_ref, kb_ref, vat_ref, vbt_ref = refs[:6]
    if emit_new_kv:
        kat_ref, va_ref, kbt_ref = refs[6:]

    mod = mod_ref[...]
    gsum = gsum_ref[...]
    n_wide = A_W // CHUNK
    key_cols = [j * CHUNK for j in range(n_wide)] + [A_W]
    n_feat = N_T // CHUNK
    sub = min(QKV_SUBTILE, x_ref.shape[0])

    def normed(u):
        h = _modulated_norm(x_ref[u * sub:(u + 1) * sub, :], g1_ref[...], mod[:, D_MODEL:2 * D_MODEL],
                            mod[:, 0:D_MODEL])
        return h.astype(BF16), h.T.astype(BF16)

    def project(u, hb, ht):
        tok = slice(u * sub, (u + 1) * sub)

        def put_feat(ref, rows, val):
            width = ref.shape[2]
            step = min(sub, width)
            for off in range(0, sub, step):
                s, col = divmod(u * sub + off, width)
                ref[s, rows, col:col + step] = val[:, off:off + step].astype(ref.dtype)

        def key_project(j, _):
            return _dot(hb, wk_ref[:, key_cols[j]:key_cols[j] + CHUNK])

        def key_squares(j, p):
            return p, _dot((p * p).astype(BF16), gsum)

        def key_finish(j, st):
            p, ss = st
            row = 0 if j < n_wide else 1
            p = p * lax.rsqrt(ss * (1.0 / HEAD_DIM) + EPS) * kgain_ref[row:row + 1, :]
            if rope:
                cos, sa, sb = cos_ref[tok, :], sa_ref[tok, :], sb_ref[tok, :]
                halves = []
                for i in range(CHUNK // LANES):
                    xh = p[:, i * LANES:(i + 1) * LANES]
                    nxt = pltpu.roll(xh, LANES - HEAD_DIM // 4, 1)
                    prv = pltpu.roll(xh, HEAD_DIM // 4, 1)
                    halves.append(xh * cos + nxt * sa + prv * sb)
                p = jnp.concatenate(halves, axis=1)
            if j < n_wide:
                sl = slice(j * CHUNK, (j + 1) * CHUNK)
                ka_ref[tok, sl] = p.astype(BF16)
                if emit_new_kv:
                    put_feat(kat_ref, sl, p.T)
            else:
                kb_ref[tok, :] = p.astype(BF16)
                if emit_new_kv:
                    put_feat(kbt_ref, slice(None), p.T)

        _software_pipeline(len(key_cols), [key_project, key_squares, key_finish])

        def feat_project(c, _):
            return _dot(wt_ref[c * CHUNK:(c + 1) * CHUNK, :], ht)

        def feat_finish(c, p):
            kind, j = divmod(c, n_wide)
            sl = slice(j * CHUNK, (j + 1) * CHUNK)
            if kind == 2:
                put_feat(vat_ref, sl, p)
                if emit_new_kv:
                    va_ref[tok, sl] = p.T
                return
            if kind == 3:
                put_feat(vbt_ref, slice(None), p)
                return
            gain = qgain_ref[kind]
            q4 = HEAD_DIM // 4
            heads = []
            for r in range(CHUNK // HEAD_DIM):
                x = p[r * HEAD_DIM:(r + 1) * HEAD_DIM, :]
                x = x * lax.rsqrt(jnp.mean(x * x, axis=0, keepdims=True) + EPS) * gain
                if rope:
                    swapped = jnp.concatenate([x[q4:2 * q4], x[0:q4], x[3 * q4:4 * q4], x[2 * q4:3 * q4]],
                                              axis=0)
                    x = x * cost_ref[:, tok] + swapped * sint_ref[:, tok]
                heads.append(x)
            put_feat(qat_ref if kind == 0 else qbt_ref, sl, jnp.concatenate(heads, axis=0) * Q_SCALE)

        _software_pipeline(n_feat, [feat_project, feat_finish])

    n_sub = x_ref.shape[0] // sub
    nxt = normed(0)
    for u in range(n_sub):
        cur, nxt = nxt, (normed(u + 1) if u + 1 < n_sub else None)
        project(u, *cur)


def _qkv_proj(x3d, mod3, mod_row, g1, w_k, w_t, gsum, kgains, qgains, rope_tabs, tm, emit_new_kv):
    b, t, _ = x3d.shape
    rows = b * t
    rope = rope_tabs is not None
    tps = max(t // tm, 1)
    spt = max(tm // t, 1)
    tw = tm // spt
    in_specs = [pl.BlockSpec((tm, D_MODEL), lambda i: (i, 0)),
                pl.BlockSpec((None, 1, N_MOD * D_MODEL), lambda i: (mod_row(i), 0, 0)),
                _const_spec((1, D_MODEL)),
                _const_spec((D_MODEL, N_K)),
                _const_spec((N_T, D_MODEL)),
                _const_spec((CHUNK, CHUNK)),
                _const_spec((2, CHUNK)),
                _const_spec((2, HEAD_DIM, 1))]
    args = [x3d.reshape(rows, D_MODEL), mod3, g1, w_k, w_t, gsum, kgains, qgains]
    if rope:
        nat, featm = rope_tabs
        in_specs += [pl.BlockSpec((tm, LANES), lambda i: (i % tps, 0))] * 3
        in_specs += [pl.BlockSpec((HEAD_DIM, tm), lambda i: (0, i % tps))] * 2
        args += list(nat) + list(featm)
    tok_wide = pl.BlockSpec((tm, A_W), lambda i: (i, 0))
    tok_narrow = pl.BlockSpec((tm, B_KVW), lambda i: (i, 0))
    feat_wide = pl.BlockSpec((spt, A_W, tw), lambda i: (i // tps, 0, i % tps))
    feat_narrow = pl.BlockSpec((spt, B_KVW, tw), lambda i: (i // tps, 0, i % tps))
    out_specs = [feat_wide, feat_wide, tok_wide, tok_narrow, feat_wide, feat_narrow]
    out_shape = [jax.ShapeDtypeStruct((b, A_W, t), BF16),
                 jax.ShapeDtypeStruct((b, B_Q, t), BF16),
                 jax.ShapeDtypeStruct((rows, A_W), BF16),
                 jax.ShapeDtypeStruct((rows, B_KVW), BF16),
                 jax.ShapeDtypeStruct((b, A_W, t), BF16),
                 jax.ShapeDtypeStruct((b, B_KVW, t), F32 if emit_new_kv else BF16)]
    if emit_new_kv:
        out_specs += [feat_wide, tok_wide, feat_narrow]
        out_shape += [jax.ShapeDtypeStruct((b, A_W, t), F32),
                      jax.ShapeDtypeStruct((rows, A_W), F32),
                      jax.ShapeDtypeStruct((b, B_KVW, t), F32)]
    return pl.pallas_call(
        functools.partial(_qkv_kernel, rope=rope, emit_new_kv=emit_new_kv),
        grid=(rows // tm,),
        in_specs=in_specs,
        out_specs=out_specs,
        out_shape=out_shape,
        compiler_params=_params(1),
        name="qkv_rope" if rope else "qkv_ctx",
    )(*args)


def _lambda(lam_ref):
    lv = lam_ref[...]
    t1 = jnp.sum(lv[0:1] * lv[1:2], axis=-1, keepdims=True)
    t2 = jnp.sum(lv[2:3] * lv[3:4], axis=-1, keepdims=True)
    return jnp.exp(t1) - jnp.exp(t2) + LAMBDA_INIT


def _with_ones(vt):
    return jnp.concatenate([vt, jnp.ones((ONES_ROWS, vt.shape[1]), vt.dtype)], axis=0)


def _column_softmax(scores, extra=None):
    m = functools.reduce(jnp.maximum, [jnp.max(s, axis=0, keepdims=True) for s in scores])
    if extra is not None:
        m = jnp.maximum(m, extra)
    return [jnp.exp2(s - m).astype(BF16) for s in scores], m


def _max_row_sq_norm(x):
    x = x.astype(F32)
    return jnp.max(jnp.sum(x * x, axis=1, keepdims=True), axis=0, keepdims=True)[0, 0]


def _scores_bounded(q_sq, k_sq, sink_abs=0.0):
    limit = SHIFT_FREE_LOG2
    return jnp.logical_and(q_sq * k_sq <= limit * limit, sink_abs <= limit)


def _diff_query_cols(qt):
    zero = jnp.zeros((HEAD_DIM, qt.shape[1]), qt.dtype)
    return jnp.concatenate([jnp.concatenate([qt[:HEAD_DIM], zero], axis=0),
                            jnp.concatenate([zero, qt[HEAD_DIM:]], axis=0)], axis=1)


def _diff_output(r, lam, subg_col):
    n = r.shape[1] // 2
    dv = 2 * HEAD_DIM
    tot = r[dv:dv + 1, :]
    o = r[:dv, :n] * (1.0 / tot[:, :n]) - r[:dv, n:] * (lam / tot[:, n:])
    o = o * lax.rsqrt(jnp.mean(o * o, axis=0, keepdims=True) + EPS)
    return (o * subg_col * (1.0 - LAMBDA_INIT)).T


def _group_output(r, m, sink_row):
    n = r.shape[1] // B_GROUP
    tot = r[HEAD_DIM:HEAD_DIM + 1, :] + jnp.exp2(sink_row - m)
    o = r[:HEAD_DIM, :] * (1.0 / tot)
    return jnp.concatenate([o[:, i * n:(i + 1) * n] for i in range(B_GROUP)], axis=0).T


def _sink_row(sink_ref, g, n):
    reps = n // LANES
    return jnp.concatenate([sink_ref[pl.ds(g * B_GROUP + r, 1), :] for r in range(B_GROUP)
                            for _ in range(reps)], axis=1)


def _ctx_attn_kernel(qat_ref, qbt_ref, ka_ref, kb_ref, vat_ref, vbt_ref, lam_ref, subg_ref, sink_ref,
                     bounds_ref, oa_ref, ob_ref):
    lam = _lambda(lam_ref)
    subg_col = subg_ref[...]
    n_seq, _, n = qat_ref.shape
    chains = A_HEADS + B_KV_HEADS

    def scores(item, _):
        s, c = divmod(item, chains)
        rows = slice(s * n, (s + 1) * n)
        if c < A_HEADS:
            sl = slice(c * LANES, (c + 1) * LANES)
            return _dot(ka_ref[rows, sl], _diff_query_cols(qat_ref[s, sl, :]))
        g = c - A_HEADS
        qs = jnp.concatenate([qbt_ref[s, (g * B_GROUP + r) * HEAD_DIM:(g * B_GROUP + r + 1) * HEAD_DIM, :]
                              for r in range(B_GROUP)], axis=1)
        zero = jnp.zeros_like(qs)
        w = jnp.concatenate([qs if i == g else zero for i in range(B_KV_HEADS)], axis=0)
        return _dot(kb_ref[rows, :], w)

    def probs(shift, item, sc):
        c = item % chains
        if not shift:
            return [jnp.exp2(sc).astype(BF16)], jnp.zeros((1, sc.shape[1]), F32)
        if c < A_HEADS:
            return _column_softmax([sc])
        return _column_softmax([sc], extra=_sink_row(sink_ref, c - A_HEADS, n))

    def outputs(item, st):
        s, c = divmod(item, chains)
        rows = slice(s * n, (s + 1) * n)
        es, m = st
        if c < A_HEADS:
            sl = slice(c * LANES, (c + 1) * LANES)
            r = _dot(_with_ones(vat_ref[s, sl, :]), es[0])
            oa_ref[rows, sl] = _diff_output(r, lam, subg_col).astype(BF16)
        else:
            g = c - A_HEADS
            vt = vbt_ref[s, g * HEAD_DIM:(g + 1) * HEAD_DIM, :].astype(BF16)
            r = _dot(_with_ones(vt), es[0])
            o = _group_output(r, m, _sink_row(sink_ref, g, n))
            ob_ref[rows, g * CHUNK:(g + 1) * CHUNK] = o.astype(BF16)

    bounded = jnp.logical_and(_scores_bounded(bounds_ref[0], bounds_ref[1]),
                              _scores_bounded(bounds_ref[2], bounds_ref[3], bounds_ref[4]))

    @pl.when(bounded)
    def _():
        _software_pipeline(n_seq * chains, [scores, functools.partial(probs, False), outputs])

    @pl.when(jnp.logical_not(bounded))
    def _():
        _software_pipeline(n_seq * chains, [scores, functools.partial(probs, True), outputs])


def _ctx_front_kernel(x_ref, mod_ref, g1_ref, wk_ref, wt_ref, gsum_ref, kgain_ref, qgain_ref,
                      lam_ref, subg_ref, sink_ref, bounds_ref,
                      oa_ref, ob_ref, vbt_ref, kat_ref, va_ref, kbt_ref,
                      qat_s, qbt_s, ka_s, kb_s, vat_s):
    _qkv_kernel(x_ref, mod_ref, g1_ref, wk_ref, wt_ref, gsum_ref, kgain_ref, qgain_ref,
                qat_s, qbt_s, ka_s, kb_s, vat_s, vbt_ref, kat_ref, va_ref, kbt_ref,
                rope=False, emit_new_kv=True)
    _ctx_attn_kernel(qat_s, qbt_s, ka_s, kb_s, vat_s, vbt_ref, lam_ref, subg_ref, sink_ref, bounds_ref,
                     oa_ref, ob_ref)


def _ctx_front(x3d, mod3, mod_row, g1, w_k, w_t, gsum, kgains, qgains, lam_vecs, subg_col, sink_rows, bounds,
               n_seq):
    b, t, _ = x3d.shape
    rows, tm = b * t, n_seq * t
    tok = pl.BlockSpec((tm, A_W), lambda i: (i, 0))
    feat_wide = pl.BlockSpec((n_seq, A_W, t), lambda i: (i, 0, 0))
    feat_narrow = pl.BlockSpec((n_seq, B_KVW, t), lambda i: (i, 0, 0))
    return pl.pallas_call(
        _ctx_front_kernel,
        grid=(b // n_seq,),
        in_specs=[pl.BlockSpec((tm, D_MODEL), lambda i: (i, 0)),
                  pl.BlockSpec((None, 1, N_MOD * D_MODEL), lambda i: (mod_row(i), 0, 0)),
                  _const_spec((1, D_MODEL)), _const_spec((D_MODEL, N_K)), _const_spec((N_T, D_MODEL)),
                  _const_spec((CHUNK, CHUNK)), _const_spec((2, CHUNK)), _const_spec((2, HEAD_DIM, 1)),
                  _const_spec((4, HEAD_DIM)), _const_spec((2 * HEAD_DIM, 1)),
                  _const_spec((B_HEADS, LANES)), pl.BlockSpec(memory_space=pltpu.SMEM)],
        out_specs=[tok, tok, feat_narrow, feat_wide, tok, feat_narrow],
        out_shape=[jax.ShapeDtypeStruct((rows, A_W), BF16), jax.ShapeDtypeStruct((rows, B_Q), BF16),
                   jax.ShapeDtypeStruct((b, B_KVW, t), F32), jax.ShapeDtypeStruct((b, A_W, t), F32),
                   jax.ShapeDtypeStruct((rows, A_W), F32), jax.ShapeDtypeStruct((b, B_KVW, t), F32)],
        scratch_shapes=[pltpu.VMEM((n_seq, A_W, t), BF16), pltpu.VMEM((n_seq, B_Q, t), BF16),
                        pltpu.VMEM((tm, A_W), BF16), pltpu.VMEM((tm, B_KVW), BF16),
                        pltpu.VMEM((n_seq, A_W, t), BF16)],
        compiler_params=_params(1),
        name="ctx_front",
    )(x3d.reshape(rows, D_MODEL), mod3, g1, w_k, w_t, gsum, kgains, qgains, lam_vecs, subg_col, sink_rows,
      bounds)


def _rider_specs(weights, n_steps, flat_step):
    specs = []
    for w in weights:
        rows, cols = w.shape
        bands = next(n for n in (n_steps, n_steps // 2, n_steps // 4)
                     if rows % n == 0 and (rows // n) % 16 == 0)
        repeat = n_steps // bands
        spec = pl.BlockSpec((rows // bands, cols), lambda *g, repeat=repeat: (flat_step(*g) // repeat, 0))
        specs.append((spec, jax.ShapeDtypeStruct(w.shape, BF16)))
    return specs


def _cast_riders(rider_refs):
    n = len(rider_refs) // 2
    for w_ref, o_ref in zip(rider_refs[:n], rider_refs[n:]):
        o_ref[...] = w_ref[...].astype(BF16)


def _diff_lat_kernel(qt_ref, k_ref, vt_ref, ckt_ref, cv_ref, lam_ref, subg_ref, bounds_ref, *rest, n_riders):
    o_ref = rest[n_riders]
    _cast_riders(rest[:n_riders] + rest[n_riders + 1:])
    lam = _lambda(lam_ref)
    subg_col = subg_ref[...]
    k = k_ref[...]
    ck = ckt_ref[...].T.astype(BF16)
    vt = _with_ones(vt_ref[...])
    past = cv_ref.shape[0] // A_HEADS
    cv = cv_ref[pl.ds(pl.program_id(1), past, stride=A_HEADS), :]
    cvt = _with_ones(cv.T.astype(BF16))

    def scores(i, _):
        q = _diff_query_cols(qt_ref[:, i * DIFF_QUERIES:(i + 1) * DIFF_QUERIES])
        return [_dot(k, q), _dot(ck, q)]

    def outputs(shift, i, s):
        if shift:
            m = jnp.maximum(jnp.max(s[0], axis=0, keepdims=True), jnp.max(s[1], axis=0, keepdims=True))
        r = None
        for sc, v in ((s[0], vt), (s[1], cvt)):
            for c in range(0, sc.shape[0], KEY_CHUNK):
                piece = sc[c:c + KEY_CHUNK]
                part = _dot(v[:, c:c + KEY_CHUNK], jnp.exp2(piece - m if shift else piece).astype(BF16))
                r = part if r is None else r + part
        o_ref[i * DIFF_QUERIES:(i + 1) * DIFF_QUERIES, :] = _diff_output(r, lam, subg_col).astype(BF16)

    n_sub = qt_ref.shape[1] // DIFF_QUERIES
    bounded = _scores_bounded(bounds_ref[0], jnp.maximum(bounds_ref[1], _max_row_sq_norm(ck)))

    @pl.when(bounded)
    def _():
        _software_pipeline(n_sub, [scores, functools.partial(outputs, False)])

    @pl.when(jnp.logical_not(bounded))
    def _():
        _software_pipeline(n_sub, [scores, functools.partial(outputs, True)])


def _diff_lat(qt, k, vt, ckt, cv, lam_vecs, subg_col, bounds, riders):
    b, _, t = qt.shape
    past = ckt.shape[2]
    rider_specs = _rider_specs(riders, b * A_HEADS, lambda i, h: i * A_HEADS + h)
    return pl.pallas_call(
        functools.partial(_diff_lat_kernel, n_riders=len(riders)),
        grid=(b, A_HEADS),
        in_specs=[pl.BlockSpec((None, LANES, t), lambda i, h: (i, h, 0)),
                  pl.BlockSpec((None, t, LANES), lambda i, h: (i, 0, h)),
                  pl.BlockSpec((None, LANES, t), lambda i, h: (i, h, 0)),
                  pl.BlockSpec((None, LANES, past), lambda i, h: (i, h, 0)),
                  pl.BlockSpec((None, past * A_HEADS, LANES), lambda i, h: (i, 0, 0)),
                  _const_spec((4, HEAD_DIM)), _const_spec((2 * HEAD_DIM, 1)),
                  pl.BlockSpec(memory_space=pltpu.SMEM)] + [spec for spec, _ in rider_specs],
        out_specs=[pl.BlockSpec((None, t, LANES), lambda i, h: (i, 0, h))] + [spec for spec, _ in rider_specs],
        out_shape=[jax.ShapeDtypeStruct((b, t, A_W), BF16)] + [shape for _, shape in rider_specs],
        compiler_params=_params(2),
        name="diff_lat",
    )(qt, k, vt, ckt, cv, lam_vecs, subg_col, bounds, *riders)


def _window_start(i, t):
    return min(max((i - 1) * BLOCK, 0), t - 3 * BLOCK)


def _band_offsets(t):
    return sorted({_window_start(i, t) - i * BLOCK for i in range(t // BLOCK)})


def _band_bias(t):
    key = np.arange(3 * BLOCK)[:, None]
    qry = np.arange(B_GROUP * BLOCK)[None, :] % BLOCK
    return jnp.asarray(np.stack([np.where(np.abs(key + off - qry) <= WINDOW, 0.0, NEG)
                                 for off in _band_offsets(t)]), dtype=F32)


def _win_lat_kernel(qt_ref, k_ref, vt_ref, ckt_ref, cvt_ref, sink_ref, bias_ref, bounds_ref, *rest, n_riders):
    o_ref = rest[n_riders]
    _cast_riders(rest[:n_riders] + rest[n_riders + 1:])
    g = pl.program_id(1)
    t = k_ref.shape[0]
    span = 3 * BLOCK
    keep = lax.broadcasted_iota(jnp.int32, (1, CHUNK), 1) // HEAD_DIM == g
    k = jnp.where(keep, k_ref[...].astype(F32), 0.0).astype(BF16)
    ck = jnp.where(keep, ckt_ref[...].T, 0.0).astype(BF16)
    vt = vt_ref[...]
    cvt = _with_ones(cvt_ref[...].astype(BF16))
    sink_row = _sink_row(sink_ref, g, BLOCK)
    offsets = _band_offsets(t)

    def scores(i, _):
        w = _window_start(i, t)
        qs = jnp.concatenate([qt_ref[r * HEAD_DIM:(r + 1) * HEAD_DIM, i * BLOCK:(i + 1) * BLOCK]
                              for r in range(B_GROUP)], axis=1)
        wq = jnp.concatenate([qs] * B_KV_HEADS, axis=0)
        return [_dot(k[w:w + span], wq) + bias_ref[offsets.index(w - i * BLOCK)], _dot(ck, wq)]

    def probs(shift, i, s):
        if shift:
            return _column_softmax(s, extra=sink_row)
        return [jnp.exp2(piece).astype(BF16) for piece in s], jnp.zeros_like(sink_row)

    def outputs(i, st):
        es, m = st
        w = _window_start(i, t)
        r = _dot(_with_ones(vt[:, w:w + span]), es[0]) + _dot(cvt, es[1])
        o_ref[i * BLOCK:(i + 1) * BLOCK, :] = _group_output(r, m, sink_row).astype(BF16)

    bounded = _scores_bounded(bounds_ref[2], jnp.maximum(bounds_ref[3], _max_row_sq_norm(ck)), bounds_ref[4])

    @pl.when(bounded)
    def _():
        _software_pipeline(t // BLOCK, [scores, functools.partial(probs, False), outputs])

    @pl.when(jnp.logical_not(bounded))
    def _():
        _software_pipeline(t // BLOCK, [scores, functools.partial(probs, True), outputs])


def _win_lat(qt, k, vt, ckt, cvt, sink_rows, bounds, riders):
    b, _, t = qt.shape
    past = ckt.shape[2]
    bias = _band_bias(t)
    rider_specs = _rider_specs(riders, b * B_KV_HEADS, lambda i, g: i * B_KV_HEADS + g)
    return pl.pallas_call(
        functools.partial(_win_lat_kernel, n_riders=len(riders)),
        grid=(b, B_KV_HEADS),
        in_specs=[pl.BlockSpec((None, CHUNK, t), lambda i, g: (i, g, 0)),
                  pl.BlockSpec((None, t, B_KVW), lambda i, g: (i, 0, 0)),
                  pl.BlockSpec((None, HEAD_DIM, t), lambda i, g: (i, g, 0)),
                  pl.BlockSpec((None, B_KVW, past), lambda i, g: (i, 0, 0)),
                  pl.BlockSpec((None, HEAD_DIM, past), lambda i, g: (i, g, 0)),
                  _const_spec((B_HEADS, LANES)), _const_spec(bias.shape),
                  pl.BlockSpec(memory_space=pltpu.SMEM)] + [spec for spec, _ in rider_specs],
        out_specs=[pl.BlockSpec((None, t, CHUNK), lambda i, g: (i, 0, g))] + [spec for spec, _ in rider_specs],
        out_shape=[jax.ShapeDtypeStruct((b, t, B_Q), BF16)] + [shape for _, shape in rider_specs],
        compiler_params=_params(2),
        name="win_lat",
    )(qt, k, vt, ckt, cvt, sink_rows, bias, bounds, *riders)


def _merge_kernel(x1_ref, x2_ref, oa1_ref, oa2_ref, ob1_ref, ob2_ref, mod_ref, g1_ref, win_ref, woa_ref,
                  wob_ref, wout_ref, o1_ref, o2_ref, *, n_first):
    def tile(x_ref, oa_ref, ob_ref, o_ref):
        mod = mod_ref[...]
        rows = lambda i: slice(i * ROW_SUBTILE, (i + 1) * ROW_SUBTILE)

        def norm(i, _):
            return _modulated_norm(x_ref[rows(i), :], g1_ref[...], mod[:, D_MODEL:2 * D_MODEL],
                                   mod[:, 0:D_MODEL]).astype(BF16)

        def branches(i, h):
            ga = _dot(h, win_ref[:, GATE_COL:GATE_COL + D_MODEL])
            gb = _dot(h, win_ref[:, GATE_COL + D_MODEL:GATE_COL + 2 * D_MODEL])
            return (jax.nn.sigmoid(ga) * _dot(oa_ref[rows(i), :], woa_ref[...])
                    + jax.nn.sigmoid(gb) * _dot(ob_ref[rows(i), :], wob_ref[...])).astype(BF16)

        def project(i, merged):
            o_ref[rows(i), :] = x_ref[rows(i), :] + mod[:, 2 * D_MODEL:3 * D_MODEL] * _dot(merged, wout_ref[...])

        _software_pipeline(x_ref.shape[0] // ROW_SUBTILE, [norm, branches, project])

    _per_token_set(n_first, functools.partial(tile, x1_ref, oa1_ref, ob1_ref, o1_ref),
                   functools.partial(tile, x2_ref, oa2_ref, ob2_ref, o2_ref))


def _per_token_set(n_first, first_body, second_body):
    first = pl.program_id(0) < n_first
    pl.when(first)(first_body)
    pl.when(jnp.logical_not(first))(second_body)


def _split_tiles(n_first, tm, width):
    return (pl.BlockSpec((tm, width), lambda i: (jnp.minimum(i, n_first - 1), 0)),
            pl.BlockSpec((tm, width), lambda i: (jnp.maximum(i - n_first, 0), 0)))


def _merge(x1, x2, oa1, oa2, ob1, ob2, mod3, mod_row, g1, w_in, w_oa, w_ob, w_out, tm):
    n_first, n_second = x1.shape[0] // tm, x2.shape[0] // tm
    t1, t2 = _split_tiles(n_first, tm, D_MODEL)
    sq = _const_spec((D_MODEL, D_MODEL))
    return pl.pallas_call(
        functools.partial(_merge_kernel, n_first=n_first),
        grid=(n_first + n_second,),
        in_specs=[t1, t2, t1, t2, t1, t2,
                  pl.BlockSpec((None, 1, N_MOD * D_MODEL), lambda i: (mod_row(i), 0, 0)),
                  _const_spec((1, D_MODEL)), _const_spec(w_in.shape), sq, sq, sq],
        out_specs=[t1, t2],
        out_shape=[jax.ShapeDtypeStruct(x1.shape, F32), jax.ShapeDtypeStruct(x2.shape, F32)],
        compiler_params=_params(1),
        name="merge_out",
    )(x1, x2, oa1, oa2, ob1, ob2, mod3, g1, w_in, w_oa, w_ob, w_out)


def _ffn_kernel(x1_ref, x2_ref, mod_ref, g2_ref, wgate_ref, wup_ref, wdown_ref, o1_ref, o2_ref, *, n_first):
    def tile(x_ref, o_ref):
        mod = mod_ref[...]
        rows = lambda i: slice(i * FFN_SUBTILE, (i + 1) * FFN_SUBTILE)

        def norm(i, _):
            return _modulated_norm(x_ref[rows(i), :], g2_ref[...], mod[:, 4 * D_MODEL:5 * D_MODEL],
                                   mod[:, 3 * D_MODEL:4 * D_MODEL]).astype(BF16)

        def hidden(i, h):
            gate = _dot(h, wgate_ref[...])
            return (gate * jax.nn.sigmoid(gate) * _dot(h, wup_ref[...])).astype(BF16)

        def project(i, act):
            o_ref[rows(i), :] = x_ref[rows(i), :] + mod[:, 5 * D_MODEL:6 * D_MODEL] * _dot(act, wdown_ref[...])

        _software_pipeline(x_ref.shape[0] // FFN_SUBTILE, [norm, hidden, project])

    _per_token_set(n_first, functools.partial(tile, x1_ref, o1_ref), functools.partial(tile, x2_ref, o2_ref))


def _ffn(x1, x2, mod3, mod_row, g2, w_gate, w_up, w_down, tm):
    n_first, n_second = x1.shape[0] // tm, x2.shape[0] // tm
    t1, t2 = _split_tiles(n_first, tm, D_MODEL)
    return pl.pallas_call(
        functools.partial(_ffn_kernel, n_first=n_first),
        grid=(n_first + n_second,),
        in_specs=[t1, t2,
                  pl.BlockSpec((None, 1, N_MOD * D_MODEL), lambda i: (mod_row(i), 0, 0)),
                  _const_spec((1, D_MODEL)), _const_spec((D_MODEL, D_FF)),
                  _const_spec((D_MODEL, D_FF)), _const_spec((D_FF, D_MODEL))],
        out_specs=[t1, t2],
        out_shape=[jax.ShapeDtypeStruct(x1.shape, F32), jax.ShapeDtypeStruct(x2.shape, F32)],
        compiler_params=_params(1),
        name="swiglu_ffn",
    )(x1, x2, mod3, g2, w_gate, w_up, w_down)


def _rope_tables(seq_len):
    quarter = HEAD_DIM // 4
    t = np.arange(seq_len)
    row = (t // GRID_W).astype(np.float32)
    col = (t % GRID_W).astype(np.float32)
    freqs = (np.float32(ROPE_BASE) ** (-np.arange(quarter, dtype=np.float32) / np.float32(quarter))
             ).astype(np.float32)
    ang_r = row[:, None] * freqs
    ang_c = col[:, None] * freqs
    zeros = np.zeros_like(ang_r)
    cos = np.concatenate([np.cos(ang_r)] * 2 + [np.cos(ang_c)] * 2, axis=-1)
    sin_next = np.concatenate([-np.sin(ang_r), zeros, -np.sin(ang_c), zeros], axis=-1)
    sin_prev = np.concatenate([zeros, np.sin(ang_r), zeros, np.sin(ang_c)], axis=-1)
    rep = LANES // HEAD_DIM
    nat = tuple(jnp.asarray(np.tile(a, (1, rep)), dtype=F32) for a in (cos, sin_next, sin_prev))
    featm = (jnp.asarray(cos.T, dtype=F32), jnp.asarray((sin_next + sin_prev).T, dtype=F32))
    return nat, featm


def _group_sum_matrix():
    idx = np.arange(CHUNK) // HEAD_DIM
    return jnp.asarray(idx[:, None] == idx[None, :], dtype=BF16)


def kernel(x_prompt, x_sample, cache_diff_k, cache_diff_v, cache_win_k, cache_win_v, c, c_ctx,
           w_ada, b_ada, norm1_g, w_in, qn_a, kn_a, lambda_q1, lambda_k1, lambda_q2, lambda_k2,
           subln_g, qn_b, kn_b, sink, w_oa, w_ob, w_out, norm2_g, w_gate, w_up, w_down):
    batch, seq, _ = x_prompt.shape
    dec_batch, dec_seq, _ = x_sample.shape
    past = cache_diff_k.shape[2]
    l = 0

    cc = jnp.concatenate([c, c_ctx[None, :], jnp.zeros((8 - dec_batch - 1, D_MODEL), F32)], axis=0)
    mod3 = _modulation(cc, w_ada[l], b_ada[l][None, :])

    w_in_l = w_in[l]
    o_ka, o_va, o_qb, o_kb, o_vb, o_g = A_W, 2 * A_W, 3 * A_W, 3 * A_W + B_Q, 3 * A_W + B_Q + B_KVW, \
        3 * A_W + B_Q + 2 * B_KVW
    w_k = jnp.concatenate([w_in_l[:, o_ka:o_va], w_in_l[:, o_kb:o_vb]], axis=1).astype(BF16)
    w_t = jnp.concatenate([w_in_l[:, :o_ka], w_in_l[:, o_qb:o_kb], w_in_l[:, o_va:o_qb],
                           w_in_l[:, o_vb:o_g]], axis=1).T.astype(BF16)
    w_in_b = w_in_l.astype(BF16)
    g1 = norm1_g[l][None, :]
    g2 = norm2_g[l][None, :]
    rep = CHUNK // HEAD_DIM
    kgains = jnp.stack([jnp.tile(kn_a[l], rep), jnp.tile(kn_b[l], rep)])
    qgains = jnp.stack([qn_a[l], qn_b[l]])[:, :, None]
    lam_vecs = jnp.stack([lambda_q1[l], lambda_k1[l], lambda_q2[l], lambda_k2[l]])
    subg_col = subln_g[l][:, None]
    sink_rows = jnp.broadcast_to((sink[l] * LOG2E)[:, None], (B_HEADS, LANES))
    head_gains = jnp.stack([qn_a[l], kn_a[l], qn_b[l], kn_b[l]])
    head_scale = jnp.asarray([Q_SCALE, 1.0, Q_SCALE, 1.0], F32)
    bounds = jnp.concatenate([1.02 * HEAD_DIM * jnp.max(jnp.square(head_gains), axis=1) * jnp.square(head_scale),
                              jnp.max(jnp.abs(sink[l]), keepdims=True) * LOG2E]).astype(F32)
    gsum = _group_sum_matrix()
    tm = 512

    xp = x_prompt.reshape(batch * seq, D_MODEL)
    ctx_row = lambda i: dec_batch
    oa_p, ob_p, vbt, kat_f, va_f, kbt_f = _ctx_front(
        x_prompt, mod3, ctx_row, g1, w_k, w_t, gsum, kgains, qgains, lam_vecs, subg_col, sink_rows, bounds,
        tm // seq)
    new_diff_k = kat_f.reshape(batch, A_HEADS, 2, HEAD_DIM, seq).transpose(0, 4, 1, 2, 3)[:, None]
    new_diff_v = va_f.reshape(batch, 1, seq, A_HEADS, 2 * HEAD_DIM)
    new_win_k = kbt_f.reshape(batch, B_KV_HEADS, HEAD_DIM, seq).transpose(0, 3, 1, 2)[:, None]
    new_win_v = vbt.reshape(batch, B_KV_HEADS, HEAD_DIM, seq).transpose(0, 3, 1, 2)[:, None]

    xs = x_sample.reshape(dec_batch * dec_seq, D_MODEL)
    lat_row = lambda i: i // (dec_seq // tm)
    qat, qbt, ka, kb, vat, vbt = _qkv_proj(
        x_sample, mod3, lat_row, g1, w_k, w_t, gsum, kgains, qgains, _rope_tables(dec_seq), tm, False)
    ckt_a = cache_diff_k[:, l].transpose(0, 2, 3, 4, 1).reshape(dec_batch, A_W, past)
    cv_a = cache_diff_v[:, l].reshape(dec_batch, past * A_HEADS, 2 * HEAD_DIM)
    ckt_b = cache_win_k[:, l].transpose(0, 2, 3, 1).reshape(dec_batch, B_KVW, past)
    cvt_b = cache_win_v[:, l].transpose(0, 2, 3, 1).reshape(dec_batch, B_KVW, past)
    oa_s, w_gate_b, w_up_b, w_down_b = _diff_lat(
        qat, ka.reshape(dec_batch, dec_seq, A_W), vat, ckt_a, cv_a, lam_vecs, subg_col, bounds,
        [w_gate[l], w_up[l], w_down[l]])
    ob_s, w_oa_b, w_ob_b, w_out_b = _win_lat(
        qbt, kb.reshape(dec_batch, dec_seq, B_KVW), vbt, ckt_b, cvt_b, sink_rows, bounds,
        [w_oa[l], w_ob[l], w_out[l]])

    n_ctx_tiles = batch * seq // tm
    both_row = lambda i: jnp.where(i < n_ctx_tiles, dec_batch, (i - n_ctx_tiles) // (dec_seq // tm))
    xp1, xs1 = _merge(xp, xs, oa_p, oa_s.reshape(-1, A_W), ob_p, ob_s.reshape(-1, B_Q), mod3, both_row, g1,
                      w_in_b, w_oa_b, w_ob_b, w_out_b, tm)
    y_prompt, y_sample = _ffn(xp1, xs1, mod3, both_row, g2, w_gate_b, w_up_b, w_down_b, tm)

    return (y_prompt.reshape(batch, seq, D_MODEL),
            y_sample.reshape(dec_batch, dec_seq, D_MODEL),
            new_diff_k, new_diff_v, new_win_k, new_win_v)
```

```python
import functools
import math

import jax
import jax.numpy as jnp
import numpy as np
from jax import lax
from jax.experimental import pallas as pl
from jax.experimental.pallas import tpu as pltpu

D_MODEL = 1024
HEAD_DIM = 64
A_HEADS = 8
B_HEADS = 16
B_KV_HEADS = 4
B_GROUP = B_HEADS // B_KV_HEADS
A_W = A_HEADS * 2 * HEAD_DIM
B_Q = B_HEADS * HEAD_DIM
B_KVW = B_KV_HEADS * HEAD_DIM
N_K = A_W + B_KVW
N_T = A_W + B_Q + A_W + B_KVW
GATE_COL = 3 * A_W + B_Q + 2 * B_KVW
WINDOW = 128
BLOCK = 128
GRID_W = 64
D_FF = 2816
N_MOD = 6
EPS = 1e-6
ROPE_BASE = 10000.0
NEG = -1e30
LAMBDA_INIT = 0.8 - 0.6 * math.exp(-0.3 * 0)
LOG2E = math.log2(math.e)
Q_SCALE = HEAD_DIM ** -0.5 * LOG2E

LANES = 128
CHUNK = 256
ONES_ROWS = 16
KEY_CHUNK = 512
QKV_SUBTILE = 512
ROW_SUBTILE = 256
FFN_SUBTILE = 128
DIFF_QUERIES = 128
SHIFT_FREE_LOG2 = 60.0
VMEM_LIMIT = 56 * 1024 * 1024

F32 = jnp.float32
BF16 = jnp.bfloat16


def _params(n_axes, flags=None):
    return pltpu.CompilerParams(dimension_semantics=("arbitrary",) * n_axes,
                                vmem_limit_bytes=VMEM_LIMIT, flags=flags)


def _const_spec(shape):
    nd = len(shape)
    return pl.BlockSpec(shape, lambda *_: (0,) * nd, pipeline_mode=pl.Buffered(1))


def _dot(a, b):
    return jnp.dot(a, b, preferred_element_type=F32)


def _modulated_norm(x, gain, scale, shift):
    y = x * lax.rsqrt(jnp.mean(x * x, axis=-1, keepdims=True) + EPS) * gain
    return y * (1.0 + scale) + shift


def _software_pipeline(n, stages):
    depth = len(stages)
    state = [None] * n
    for step in range(n + depth - 1):
        for k, stage in enumerate(stages):
            i = step - k
            if 0 <= i < n:
                state[i] = stage(i, state[i])


def _mod_kernel(c_ref, w_ref, b_ref, o_ref):
    cc = c_ref[...]
    s = cc * jax.nn.sigmoid(cc)
    mod = _dot(s.astype(BF16), w_ref[...].astype(BF16)) + b_ref[...]
    for r in range(mod.shape[0]):
        o_ref[r] = mod[r:r + 1, :]


def _modulation(cc, w_ada, b_ada):
    rows, n = cc.shape[0], w_ada.shape[1]
    tn = n // 4
    return pl.pallas_call(
        _mod_kernel,
        grid=(n // tn,),
        in_specs=[pl.BlockSpec((rows, D_MODEL), lambda j: (0, 0)),
                  pl.BlockSpec((D_MODEL, tn), lambda j: (0, j)),
                  pl.BlockSpec((1, tn), lambda j: (0, j))],
        out_specs=pl.BlockSpec((rows, 1, tn), lambda j: (0, 0, j)),
        out_shape=jax.ShapeDtypeStruct((rows, 1, n), F32),
        compiler_params=_params(1),
        name="adaln_mod",
    )(cc, w_ada, b_ada)


def _qkv_kernel(*refs, rope, emit_new_kv):
    x_ref, mod_ref, g1_ref, wk_ref, wt_ref, gsum_ref, kgain_ref, qgain_ref = refs[:8]
    refs = refs[8:]
    if rope:
        cos_ref, sa_ref, sb_ref, cost_ref, sint_ref = refs[:5]
        refs = refs[5:]
    qat_ref, qbt_ref, ka_ref, kb_ref, vat_ref, vbt_ref = refs[:6]
    if emit_new_kv:
        kat_ref, va_ref, kbt_ref = refs[6:]

    mod = mod_ref[...]
    gsum = gsum_ref[...]
    n_wide = A_W // CHUNK
    key_cols = [j * CHUNK for j in range(n_wide)] + [A_W]
    n_feat = N_T // CHUNK
    sub = min(QKV_SUBTILE, x_ref.shape[0])

    def normed(u):
        h = _modulated_norm(x_ref[u * sub:(u + 1) * sub, :], g1_ref[...], mod[:, D_MODEL:2 * D_MODEL],
                            mod[:, 0:D_MODEL])
        return h.astype(BF16), h.T.astype(BF16)

    def project(u, hb, ht):
        tok = slice(u * sub, (u + 1) * sub)

        def put_feat(ref, rows, val):
            width = ref.shape[2]
            step = min(sub, width)
            for off in range(0, sub, step):
                s, col = divmod(u * sub + off, width)
                ref[s, rows, col:col + step] = val[:, off:off + step].astype(ref.dtype)

        def key_project(j, _):
            return _dot(hb, wk_ref[:, key_cols[j]:key_cols[j] + CHUNK])

        def key_squares(j, p):
            return p, _dot((p * p).astype(BF16), gsum)

        def key_finish(j, st):
            p, ss = st
            row = 0 if j < n_wide else 1
            p = p * lax.rsqrt(ss * (1.0 / HEAD_DIM) + EPS) * kgain_ref[row:row + 1, :]
            if rope:
                cos, sa, sb = cos_ref[tok, :], sa_ref[tok, :], sb_ref[tok, :]
                halves = []
                for i in range(CHUNK // LANES):
                    xh = p[:, i * LANES:(i + 1) * LANES]
                    nxt = pltpu.roll(xh, LANES - HEAD_DIM // 4, 1)
                    prv = pltpu.roll(xh, HEAD_DIM // 4, 1)
                    halves.append(xh * cos + nxt * sa + prv * sb)
                p = jnp.concatenate(halves, axis=1)
            if j < n_wide:
                sl = slice(j * CHUNK, (j + 1) * CHUNK)
                ka_ref[tok, sl] = p.astype(BF16)
                if emit_new_kv:
                    put_feat(kat_ref, sl, p.T)
            else:
                kb_ref[tok, :] = p.astype(BF16)
                if emit_new_kv:
                    put_feat(kbt_ref, slice(None), p.T)

        _software_pipeline(len(key_cols), [key_project, key_squares, key_finish])

        def feat_project(c, _):
            return _dot(wt_ref[c * CHUNK:(c + 1) * CHUNK, :], ht)

        def feat_finish(c, p):
            kind, j = divmod(c, n_wide)
            sl = slice(j * CHUNK, (j + 1) * CHUNK)
            if kind == 2:
                put_feat(vat_ref, sl, p)
                if emit_new_kv:
                    va_ref[tok, sl] = p.T
                return
            if kind == 3:
                put_feat(vbt_ref, slice(None), p)
                return
            gain = qgain_ref[kind]
            q4 = HEAD_DIM // 4
            heads = []
            for r in range(CHUNK // HEAD_DIM):
                x = p[r * HEAD_DIM:(r + 1) * HEAD_DIM, :]
                x = x * lax.rsqrt(jnp.mean(x * x, axis=0, keepdims=True) + EPS) * gain
                if rope:
                    swapped = jnp.concatenate([x[q4:2 * q4], x[0:q4], x[3 * q4:4 * q4], x[2 * q4:3 * q4]],
                                              axis=0)
                    x = x * cost_ref[:, tok] + swapped * sint_ref[:, tok]
                heads.append(x)
            put_feat(qat_ref if kind == 0 else qbt_ref, sl, jnp.concatenate(heads, axis=0) * Q_SCALE)

        _software_pipeline(n_feat, [feat_project, feat_finish])

    n_sub = x_ref.shape[0] // sub
    nxt = normed(0)
    for u in range(n_sub):
        cur, nxt = nxt, (normed(u + 1) if u + 1 < n_sub else None)
        project(u, *cur)


def _qkv_proj(x3d, mod3, mod_row, g1, w_k, w_t, gsum, kgains, qgains, rope_tabs, tm, emit_new_kv):
    b, t, _ = x3d.shape
    rows = b * t
    rope = rope_tabs is not None
    tps = max(t // tm, 1)
    spt = max(tm // t, 1)
    tw = tm // spt
    in_specs = [pl.BlockSpec((tm, D_MODEL), lambda i: (i, 0)),
                pl.BlockSpec((None, 1, N_MOD * D_MODEL), lambda i: (mod_row(i), 0, 0)),
                _const_spec((1, D_MODEL)),
                _const_spec((D_MODEL, N_K)),
                _const_spec((N_T, D_MODEL)),
                _const_spec((CHUNK, CHUNK)),
                _const_spec((2, CHUNK)),
                _const_spec((2, HEAD_DIM, 1))]
    args = [x3d.reshape(rows, D_MODEL), mod3, g1, w_k, w_t, gsum, kgains, qgains]
    if rope:
        nat, featm = rope_tabs
        in_specs += [pl.BlockSpec((tm, LANES), lambda i: (i % tps, 0))] * 3
        in_specs += [pl.BlockSpec((HEAD_DIM, tm), lambda i: (0, i % tps))] * 2
        args += list(nat) + list(featm)
    tok_wide = pl.BlockSpec((tm, A_W), lambda i: (i, 0))
    tok_narrow = pl.BlockSpec((tm, B_KVW), lambda i: (i, 0))
    feat_wide = pl.BlockSpec((spt, A_W, tw), lambda i: (i // tps, 0, i % tps))
    feat_narrow = pl.BlockSpec((spt, B_KVW, tw), lambda i: (i // tps, 0, i % tps))
    out_specs = [feat_wide, feat_wide, tok_wide, tok_narrow, feat_wide, feat_narrow]
    out_shape = [jax.ShapeDtypeStruct((b, A_W, t), BF16),
                 jax.ShapeDtypeStruct((b, B_Q, t), BF16),
                 jax.ShapeDtypeStruct((rows, A_W), BF16),
                 jax.ShapeDtypeStruct((rows, B_KVW), BF16),
                 jax.ShapeDtypeStruct((b, A_W, t), BF16),
                 jax.ShapeDtypeStruct((b, B_KVW, t), F32 if emit_new_kv else BF16)]
    if emit_new_kv:
        out_specs += [feat_wide, tok_wide, feat_narrow]
        out_shape += [jax.ShapeDtypeStruct((b, A_W, t), F32),
                      jax.ShapeDtypeStruct((rows, A_W), F32),
                      jax.ShapeDtypeStruct((b, B_KVW, t), F32)]
    return pl.pallas_call(
        functools.partial(_qkv_kernel, rope=rope, emit_new_kv=emit_new_kv),
        grid=(rows // tm,),
        in_specs=in_specs,
        out_specs=out_specs,
        out_shape=out_shape,
        compiler_params=_params(1),
        name="qkv_rope" if rope else "qkv_ctx",
    )(*args)


def _lambda(lam_ref):
    lv = lam_ref[...]
    t1 = jnp.sum(lv[0:1] * lv[1:2], axis=-1, keepdims=True)
    t2 = jnp.sum(lv[2:3] * lv[3:4], axis=-1, keepdims=True)
    return jnp.exp(t1) - jnp.exp(t2) + LAMBDA_INIT


def _with_ones(vt):
    return jnp.concatenate([vt, jnp.ones((ONES_ROWS, vt.shape[1]), vt.dtype)], axis=0)


def _column_softmax(scores, extra=None):
    m = functools.reduce(jnp.maximum, [jnp.max(s, axis=0, keepdims=True) for s in scores])
    if extra is not None:
        m = jnp.maximum(m, extra)
    return [jnp.exp2(s - m).astype(BF16) for s in scores], m


def _max_row_sq_norm(x):
    x = x.astype(F32)
    return jnp.max(jnp.sum(x * x, axis=1, keepdims=True), axis=0, keepdims=True)[0, 0]


def _scores_bounded(q_sq, k_sq, sink_abs=0.0):
    limit = SHIFT_FREE_LOG2
    return jnp.logical_and(q_sq * k_sq <= limit * limit, sink_abs <= limit)


def _diff_query_cols(qt):
    zero = jnp.zeros((HEAD_DIM, qt.shape[1]), qt.dtype)
    return jnp.concatenate([jnp.concatenate([qt[:HEAD_DIM], zero], axis=0),
                            jnp.concatenate([zero, qt[HEAD_DIM:]], axis=0)], axis=1)


def _diff_output(r, lam, subg_col):
    n = r.shape[1] // 2
    dv = 2 * HEAD_DIM
    tot = r[dv:dv + 1, :]
    o = r[:dv, :n] * (1.0 / tot[:, :n]) - r[:dv, n:] * (lam / tot[:, n:])
    o = o * lax.rsqrt(jnp.mean(o * o, axis=0, keepdims=True) + EPS)
    return (o * subg_col * (1.0 - LAMBDA_INIT)).T


def _group_output(r, m, sink_row):
    n = r.shape[1] // B_GROUP
    tot = r[HEAD_DIM:HEAD_DIM + 1, :] + jnp.exp2(sink_row - m)
    o = r[:HEAD_DIM, :] * (1.0 / tot)
    return jnp.concatenate([o[:, i * n:(i + 1) * n] for i in range(B_GROUP)], axis=0).T


def _sink_row(sink_ref, g, n):
    reps = n // LANES
    return jnp.concatenate([sink_ref[pl.ds(g * B_GROUP + r, 1), :] for r in range(B_GROUP)
                            for _ in range(reps)], axis=1)


def _ctx_attn_kernel(qat_ref, qbt_ref, ka_ref, kb_ref, vat_ref, vbt_ref, lam_ref, subg_ref, sink_ref,
                     bounds_ref, oa_ref, ob_ref):
    lam = _lambda(lam_ref)
    subg_col = subg_ref[...]
    n_seq, _, n = qat_ref.shape
    chains = A_HEADS + B_KV_HEADS

    def scores(item, _):
        s, c = divmod(item, chains)
        rows = slice(s * n, (s + 1) * n)
        if c < A_HEADS:
            sl = slice(c * LANES, (c + 1) * LANES)
            return _dot(ka_ref[rows, sl], _diff_query_cols(qat_ref[s, sl, :]))
        g = c - A_HEADS
        qs = jnp.concatenate([qbt_ref[s, (g * B_GROUP + r) * HEAD_DIM:(g * B_GROUP + r + 1) * HEAD_DIM, :]
                              for r in range(B_GROUP)], axis=1)
        zero = jnp.zeros_like(qs)
        w = jnp.concatenate([qs if i == g else zero for i in range(B_KV_HEADS)], axis=0)
        return _dot(kb_ref[rows, :], w)

    def probs(shift, item, sc):
        c = item % chains
        if not shift:
            return [jnp.exp2(sc).astype(BF16)], jnp.zeros((1, sc.shape[1]), F32)
        if c < A_HEADS:
            return _column_softmax([sc])
        return _column_softmax([sc], extra=_sink_row(sink_ref, c - A_HEADS, n))

    def outputs(item, st):
        s, c = divmod(item, chains)
        rows = slice(s * n, (s + 1) * n)
        es, m = st
        if c < A_HEADS:
            sl = slice(c * LANES, (c + 1) * LANES)
            r = _dot(_with_ones(vat_ref[s, sl, :]), es[0])
            oa_ref[rows, sl] = _diff_output(r, lam, subg_col).astype(BF16)
        else:
            g = c - A_HEADS
            vt = vbt_ref[s, g * HEAD_DIM:(g + 1) * HEAD_DIM, :].astype(BF16)
            r = _dot(_with_ones(vt), es[0])
            o = _group_output(r, m, _sink_row(sink_ref, g, n))
            ob_ref[rows, g * CHUNK:(g + 1) * CHUNK] = o.astype(BF16)

    bounded = jnp.logical_and(_scores_bounded(bounds_ref[0], bounds_ref[1]),
                              _scores_bounded(bounds_ref[2], bounds_ref[3], bounds_ref[4]))

    @pl.when(bounded)
    def _():
        _software_pipeline(n_seq * chains, [scores, functools.partial(probs, False), outputs])

    @pl.when(jnp.logical_not(bounded))
    def _():
        _software_pipeline(n_seq * chains, [scores, functools.partial(probs, True), outputs])


def _ctx_front_kernel(x_ref, mod_ref, g1_ref, wk_ref, wt_ref, gsum_ref, kgain_ref, qgain_ref,
                      lam_ref, subg_ref, sink_ref, bounds_ref,
                      oa_ref, ob_ref, vbt_ref, kat_ref, va_ref, kbt_ref,
                      qat_s, qbt_s, ka_s, kb_s, vat_s):
    _qkv_kernel(x_ref, mod_ref, g1_ref, wk_ref, wt_ref, gsum_ref, kgain_ref, qgain_ref,
                qat_s, qbt_s, ka_s, kb_s, vat_s, vbt_ref, kat_ref, va_ref, kbt_ref,
                rope=False, emit_new_kv=True)
    _ctx_attn_kernel(qat_s, qbt_s, ka_s, kb_s, vat_s, vbt_ref, lam_ref, subg_ref, sink_ref, bounds_ref,
                     oa_ref, ob_ref)


def _ctx_front(x3d, mod3, mod_row, g1, w_k, w_t, gsum, kgains, qgains, lam_vecs, subg_col, sink_rows, bounds,
               n_seq):
    b, t, _ = x3d.shape
    rows, tm = b * t, n_seq * t
    tok = pl.BlockSpec((tm, A_W), lambda i: (i, 0))
    feat_wide = pl.BlockSpec((n_seq, A_W, t), lambda i: (i, 0, 0))
    feat_narrow = pl.BlockSpec((n_seq, B_KVW, t), lambda i: (i, 0, 0))
    return pl.pallas_call(
        _ctx_front_kernel,
        grid=(b // n_seq,),
        in_specs=[pl.BlockSpec((tm, D_MODEL), lambda i: (i, 0)),
                  pl.BlockSpec((None, 1, N_MOD * D_MODEL), lambda i: (mod_row(i), 0, 0)),
                  _const_spec((1, D_MODEL)), _const_spec((D_MODEL, N_K)), _const_spec((N_T, D_MODEL)),
                  _const_spec((CHUNK, CHUNK)), _const_spec((2, CHUNK)), _const_spec((2, HEAD_DIM, 1)),
                  _const_spec((4, HEAD_DIM)), _const_spec((2 * HEAD_DIM, 1)),
                  _const_spec((B_HEADS, LANES)), pl.BlockSpec(memory_space=pltpu.SMEM)],
        out_specs=[tok, tok, feat_narrow, feat_wide, tok, feat_narrow],
        out_shape=[jax.ShapeDtypeStruct((rows, A_W), BF16), jax.ShapeDtypeStruct((rows, B_Q), BF16),
                   jax.ShapeDtypeStruct((b, B_KVW, t), F32), jax.ShapeDtypeStruct((b, A_W, t), F32),
                   jax.ShapeDtypeStruct((rows, A_W), F32), jax.ShapeDtypeStruct((b, B_KVW, t), F32)],
        scratch_shapes=[pltpu.VMEM((n_seq, A_W, t), BF16), pltpu.VMEM((n_seq, B_Q, t), BF16),
                        pltpu.VMEM((tm, A_W), BF16), pltpu.VMEM((tm, B_KVW), BF16),
                        pltpu.VMEM((n_seq, A_W, t), BF16)],
        compiler_params=_params(1),
        name="ctx_front",
    )(x3d.reshape(rows, D_MODEL), mod3, g1, w_k, w_t, gsum, kgains, qgains, lam_vecs, subg_col, sink_rows,
      bounds)


def _rider_specs(weights, n_steps, flat_step):
    specs = []
    for w in weights:
        rows, cols = w.shape
        bands = next(n for n in (n_steps, n_steps // 2, n_steps // 4)
                     if rows % n == 0 and (rows // n) % 16 == 0)
        repeat = n_steps // bands
        spec = pl.BlockSpec((rows // bands, cols), lambda *g, repeat=repeat: (flat_step(*g) // repeat, 0))
        specs.append((spec, jax.ShapeDtypeStruct(w.shape, BF16)))
    return specs


def _cast_riders(rider_refs):
    n = len(rider_refs) // 2
    for w_ref, o_ref in zip(rider_refs[:n], rider_refs[n:]):
        o_ref[...] = w_ref[...].astype(BF16)


def _diff_lat_kernel(qt_ref, k_ref, vt_ref, ckt_ref, cv_ref, lam_ref, subg_ref, bounds_ref, *rest, n_riders):
    o_ref = rest[n_riders]
    _cast_riders(rest[:n_riders] + rest[n_riders + 1:])
    lam = _lambda(lam_ref)
    subg_col = subg_ref[...]
    k = k_ref[...]
    ck = ckt_ref[...].T.astype(BF16)
    vt = _with_ones(vt_ref[...])
    past = cv_ref.shape[0] // A_HEADS
    cv = cv_ref[pl.ds(pl.program_id(1), past, stride=A_HEADS), :]
    cvt = _with_ones(cv.T.astype(BF16))

    def scores(i, _):
        q = _diff_query_cols(qt_ref[:, i * DIFF_QUERIES:(i + 1) * DIFF_QUERIES])
        return [_dot(k, q), _dot(ck, q)]

    def outputs(shift, i, s):
        if shift:
            m = jnp.maximum(jnp.max(s[0], axis=0, keepdims=True), jnp.max(s[1], axis=0, keepdims=True))
        r = None
        for sc, v in ((s[0], vt), (s[1], cvt)):
            for c in range(0, sc.shape[0], KEY_CHUNK):
                piece = sc[c:c + KEY_CHUNK]
                part = _dot(v[:, c:c + KEY_CHUNK], jnp.exp2(piece - m if shift else piece).astype(BF16))
                r = part if r is None else r + part
        o_ref[i * DIFF_QUERIES:(i + 1) * DIFF_QUERIES, :] = _diff_output(r, lam, subg_col).astype(BF16)

    n_sub = qt_ref.shape[1] // DIFF_QUERIES
    bounded = _scores_bounded(bounds_ref[0], jnp.maximum(bounds_ref[1], _max_row_sq_norm(ck)))

    @pl.when(bounded)
    def _():
        _software_pipeline(n_sub, [scores, functools.partial(outputs, False)])

    @pl.when(jnp.logical_not(bounded))
    def _():
        _software_pipeline(n_sub, [scores, functools.partial(outputs, True)])


def _diff_lat(qt, k, vt, ckt, cv, lam_vecs, subg_col, bounds, riders):
    b, _, t = qt.shape
    past = ckt.shape[2]
    rider_specs = _rider_specs(riders, b * A_HEADS, lambda i, h: i * A_HEADS + h)
    return pl.pallas_call(
        functools.partial(_diff_lat_kernel, n_riders=len(riders)),
        grid=(b, A_HEADS),
        in_specs=[pl.BlockSpec((None, LANES, t), lambda i, h: (i, h, 0)),
                  pl.BlockSpec((None, t, LANES), lambda i, h: (i, 0, h)),
                  pl.BlockSpec((None, LANES, t), lambda i, h: (i, h, 0)),
                  pl.BlockSpec((None, LANES, past), lambda i, h: (i, h, 0)),
                  pl.BlockSpec((None, past * A_HEADS, LANES), lambda i, h: (i, 0, 0)),
                  _const_spec((4, HEAD_DIM)), _const_spec((2 * HEAD_DIM, 1)),
                  pl.BlockSpec(memory_space=pltpu.SMEM)] + [spec for spec, _ in rider_specs],
        out_specs=[pl.BlockSpec((None, t, LANES), lambda i, h: (i, 0, h))] + [spec for spec, _ in rider_specs],
        out_shape=[jax.ShapeDtypeStruct((b, t, A_W), BF16)] + [shape for _, shape in rider_specs],
        compiler_params=_params(2),
        name="diff_lat",
    )(qt, k, vt, ckt, cv, lam_vecs, subg_col, bounds, *riders)


def _window_start(i, t):
    return min(max((i - 1) * BLOCK, 0), t - 3 * BLOCK)


def _band_offsets(t):
    return sorted({_window_start(i, t) - i * BLOCK for i in range(t // BLOCK)})


def _band_bias(t):
    key = np.arange(3 * BLOCK)[:, None]
    qry = np.arange(B_GROUP * BLOCK)[None, :] % BLOCK
    return jnp.asarray(np.stack([np.where(np.abs(key + off - qry) <= WINDOW, 0.0, NEG)
                                 for off in _band_offsets(t)]), dtype=F32)


def _win_lat_kernel(qt_ref, k_ref, vt_ref, ckt_ref, cvt_ref, sink_ref, bias_ref, bounds_ref, *rest, n_riders):
    o_ref = rest[n_riders]
    _cast_riders(rest[:n_riders] + rest[n_riders + 1:])
    g = pl.program_id(1)
    t = k_ref.shape[0]
    span = 3 * BLOCK
    keep = lax.broadcasted_iota(jnp.int32, (1, CHUNK), 1) // HEAD_DIM == g
    k = jnp.where(keep, k_ref[...].astype(F32), 0.0).astype(BF16)
    ck = jnp.where(keep, ckt_ref[...].T, 0.0).astype(BF16)
    vt = vt_ref[...]
    cvt = _with_ones(cvt_ref[...].astype(BF16))
    sink_row = _sink_row(sink_ref, g, BLOCK)
    offsets = _band_offsets(t)

    def scores(i, _):
        w = _window_start(i, t)
        qs = jnp.concatenate([qt_ref[r * HEAD_DIM:(r + 1) * HEAD_DIM, i * BLOCK:(i + 1) * BLOCK]
                              for r in range(B_GROUP)], axis=1)
        wq = jnp.concatenate([qs] * B_KV_HEADS, axis=0)
        return [_dot(k[w:w + span], wq) + bias_ref[offsets.index(w - i * BLOCK)], _dot(ck, wq)]

    def probs(shift, i, s):
        if shift:
            return _column_softmax(s, extra=sink_row)
        return [jnp.exp2(piece).astype(BF16) for piece in s], jnp.zeros_like(sink_row)

    def outputs(i, st):
        es, m = st
        w = _window_start(i, t)
        r = _dot(_with_ones(vt[:, w:w + span]), es[0]) + _dot(cvt, es[1])
        o_ref[i * BLOCK:(i + 1) * BLOCK, :] = _group_output(r, m, sink_row).astype(BF16)

    bounded = _scores_bounded(bounds_ref[2], jnp.maximum(bounds_ref[3], _max_row_sq_norm(ck)), bounds_ref[4])

    @pl.when(bounded)
    def _():
        _software_pipeline(t // BLOCK, [scores, functools.partial(probs, False), outputs])

    @pl.when(jnp.logical_not(bounded))
    def _():
        _software_pipeline(t // BLOCK, [scores, functools.partial(probs, True), outputs])


def _win_lat(qt, k, vt, ckt, cvt, sink_rows, bounds, riders):
    b, _, t = qt.shape
    past = ckt.shape[2]
    bias = _band_bias(t)
    rider_specs = _rider_specs(riders, b * B_KV_HEADS, lambda i, g: i * B_KV_HEADS + g)
    return pl.pallas_call(
        functools.partial(_win_lat_kernel, n_riders=len(riders)),
        grid=(b, B_KV_HEADS),
        in_specs=[pl.BlockSpec((None, CHUNK, t), lambda i, g: (i, g, 0)),
                  pl.BlockSpec((None, t, B_KVW), lambda i, g: (i, 0, 0)),
                  pl.BlockSpec((None, HEAD_DIM, t), lambda i, g: (i, g, 0)),
                  pl.BlockSpec((None, B_KVW, past), lambda i, g: (i, 0, 0)),
                  pl.BlockSpec((None, HEAD_DIM, past), lambda i, g: (i, g, 0)),
                  _const_spec((B_HEADS, LANES)), _const_spec(bias.shape),
                  pl.BlockSpec(memory_space=pltpu.SMEM)] + [spec for spec, _ in rider_specs],
        out_specs=[pl.BlockSpec((None, t, CHUNK), lambda i, g: (i, 0, g))] + [spec for spec, _ in rider_specs],
        out_shape=[jax.ShapeDtypeStruct((b, t, B_Q), BF16)] + [shape for _, shape in rider_specs],
        compiler_params=_params(2),
        name="win_lat",
    )(qt, k, vt, ckt, cvt, sink_rows, bias, bounds, *riders)


def _merge_kernel(x1_ref, x2_ref, oa1_ref, oa2_ref, ob1_ref, ob2_ref, mod_ref, g1_ref, win_ref, woa_ref,
                  wob_ref, wout_ref, o1_ref, o2_ref, *, n_first):
    def tile(x_ref, oa_ref, ob_ref, o_ref):
        mod = mod_ref[...]
        rows = lambda i: slice(i * ROW_SUBTILE, (i + 1) * ROW_SUBTILE)

        def norm(i, _):
            return _modulated_norm(x_ref[rows(i), :], g1_ref[...], mod[:, D_MODEL:2 * D_MODEL],
                                   mod[:, 0:D_MODEL]).astype(BF16)

        def branches(i, h):
            ga = _dot(h, win_ref[:, GATE_COL:GATE_COL + D_MODEL])
            gb = _dot(h, win_ref[:, GATE_COL + D_MODEL:GATE_COL + 2 * D_MODEL])
            return (jax.nn.sigmoid(ga) * _dot(oa_ref[rows(i), :], woa_ref[...])
                    + jax.nn.sigmoid(gb) * _dot(ob_ref[rows(i), :], wob_ref[...])).astype(BF16)

        def project(i, merged):
            o_ref[rows(i), :] = x_ref[rows(i), :] + mod[:, 2 * D_MODEL:3 * D_MODEL] * _dot(merged, wout_ref[...])

        _software_pipeline(x_ref.shape[0] // ROW_SUBTILE, [norm, branches, project])

    _per_token_set(n_first, functools.partial(tile, x1_ref, oa1_ref, ob1_ref, o1_ref),
                   functools.partial(tile, x2_ref, oa2_ref, ob2_ref, o2_ref))


def _per_token_set(n_first, first_body, second_body):
    first = pl.program_id(0) < n_first
    pl.when(first)(first_body)
    pl.when(jnp.logical_not(first))(second_body)


def _split_tiles(n_first, tm, width):
    return (pl.BlockSpec((tm, width), lambda i: (jnp.minimum(i, n_first - 1), 0)),
            pl.BlockSpec((tm, width), lambda i: (jnp.maximum(i - n_first, 0), 0)))


def _merge(x1, x2, oa1, oa2, ob1, ob2, mod3, mod_row, g1, w_in, w_oa, w_ob, w_out, tm):
    n_first, n_second = x1.shape[0] // tm, x2.shape[0] // tm
    t1, t2 = _split_tiles(n_first, tm, D_MODEL)
    sq = _const_spec((D_MODEL, D_MODEL))
    return pl.pallas_call(
        functools.partial(_merge_kernel, n_first=n_first),
        grid=(n_first + n_second,),
        in_specs=[t1, t2, t1, t2, t1, t2,
                  pl.BlockSpec((None, 1, N_MOD * D_MODEL), lambda i: (mod_row(i), 0, 0)),
                  _const_spec((1, D_MODEL)), _const_spec(w_in.shape), sq, sq, sq],
        out_specs=[t1, t2],
        out_shape=[jax.ShapeDtypeStruct(x1.shape, F32), jax.ShapeDtypeStruct(x2.shape, F32)],
        compiler_params=_params(1),
        name="merge_out",
    )(x1, x2, oa1, oa2, ob1, ob2, mod3, g1, w_in, w_oa, w_ob, w_out)


def _ffn_kernel(x1_ref, x2_ref, mod_ref, g2_ref, wgate_ref, wup_ref, wdown_ref, o1_ref, o2_ref, *, n_first):
    def tile(x_ref, o_ref):
        mod = mod_ref[...]
        rows = lambda i: slice(i * FFN_SUBTILE, (i + 1) * FFN_SUBTILE)

        def norm(i, _):
            return _modulated_norm(x_ref[rows(i), :], g2_ref[...], mod[:, 4 * D_MODEL:5 * D_MODEL],
                                   mod[:, 3 * D_MODEL:4 * D_MODEL]).astype(BF16)

        def hidden(i, h):
            gate = _dot(h, wgate_ref[...])
            return (gate * jax.nn.sigmoid(gate) * _dot(h, wup_ref[...])).astype(BF16)

        def project(i, act):
            o_ref[rows(i), :] = x_ref[rows(i), :] + mod[:, 5 * D_MODEL:6 * D_MODEL] * _dot(act, wdown_ref[...])

        _software_pipeline(x_ref.shape[0] // FFN_SUBTILE, [norm, hidden, project])

    _per_token_set(n_first, functools.partial(tile, x1_ref, o1_ref), functools.partial(tile, x2_ref, o2_ref))


def _ffn(x1, x2, mod3, mod_row, g2, w_gate, w_up, w_down, tm):
    n_first, n_second = x1.shape[0] // tm, x2.shape[0] // tm
    t1, t2 = _split_tiles(n_first, tm, D_MODEL)
    return pl.pallas_call(
        functools.partial(_ffn_kernel, n_first=n_first),
        grid=(n_first + n_second,),
        in_specs=[t1, t2,
                  pl.BlockSpec((None, 1, N_MOD * D_MODEL), lambda i: (mod_row(i), 0, 0)),
                  _const_spec((1, D_MODEL)), _const_spec((D_MODEL, D_FF)),
                  _const_spec((D_MODEL, D_FF)), _const_spec((D_FF, D_MODEL))],
        out_specs=[t1, t2],
        out_shape=[jax.ShapeDtypeStruct(x1.shape, F32), jax.ShapeDtypeStruct(x2.shape, F32)],
        compiler_params=_params(1),
        name="swiglu_ffn",
    )(x1, x2, mod3, g2, w_gate, w_up, w_down)


def _rope_tables(seq_len):
    quarter = HEAD_DIM // 4
    t = np.arange(seq_len)
    row = (t // GRID_W).astype(np.float32)
    col = (t % GRID_W).astype(np.float32)
    freqs = (np.float32(ROPE_BASE) ** (-np.arange(quarter, dtype=np.float32) / np.float32(quarter))
             ).astype(np.float32)
    ang_r = row[:, None] * freqs
    ang_c = col[:, None] * freqs
    zeros = np.zeros_like(ang_r)
    cos = np.concatenate([np.cos(ang_r)] * 2 + [np.cos(ang_c)] * 2, axis=-1)
    sin_next = np.concatenate([-np.sin(ang_r), zeros, -np.sin(ang_c), zeros], axis=-1)
    sin_prev = np.concatenate([zeros, np.sin(ang_r), zeros, np.sin(ang_c)], axis=-1)
    rep = LANES // HEAD_DIM
    nat = tuple(jnp.asarray(np.tile(a, (1, rep)), dtype=F32) for a in (cos, sin_next, sin_prev))
    featm = (jnp.asarray(cos.T, dtype=F32), jnp.asarray((sin_next + sin_prev).T, dtype=F32))
    return nat, featm


def _group_sum_matrix():
    idx = np.arange(CHUNK) // HEAD_DIM
    return jnp.asarray(idx[:, None] == idx[None, :], dtype=BF16)


def kernel(x_prompt, x_sample, cache_diff_k, cache_diff_v, cache_win_k, cache_win_v, c, c_ctx,
           w_ada, b_ada, norm1_g, w_in, qn_a, kn_a, lambda_q1, lambda_k1, lambda_q2, lambda_k2,
           subln_g, qn_b, kn_b, sink, w_oa, w_ob, w_out, norm2_g, w_gate, w_up, w_down):
    batch, seq, _ = x_prompt.shape
    dec_batch, dec_seq, _ = x_sample.shape
    past = cache_diff_k.shape[2]
    l = 0

    cc = jnp.concatenate([c, c_ctx[None, :], jnp.zeros((8 - dec_batch - 1, D_MODEL), F32)], axis=0)
    mod3 = _modulation(cc, w_ada[l], b_ada[l][None, :])

    w_in_l = w_in[l]
    o_ka, o_va, o_qb, o_kb, o_vb, o_g = A_W, 2 * A_W, 3 * A_W, 3 * A_W + B_Q, 3 * A_W + B_Q + B_KVW, \
        3 * A_W + B_Q + 2 * B_KVW
    w_k = jnp.concatenate([w_in_l[:, o_ka:o_va], w_in_l[:, o_kb:o_vb]], axis=1).astype(BF16)
    w_t = jnp.concatenate([w_in_l[:, :o_ka], w_in_l[:, o_qb:o_kb], w_in_l[:, o_va:o_qb],
                           w_in_l[:, o_vb:o_g]], axis=1).T.astype(BF16)
    g1 = norm1_g[l][None, :]
    g2 = norm2_g[l][None, :]
    rep = CHUNK // HEAD_DIM
    kgains = jnp.stack([jnp.tile(kn_a[l], rep), jnp.tile(kn_b[l], rep)])
    qgains = jnp.stack([qn_a[l], qn_b[l]])[:, :, None]
    lam_vecs = jnp.stack([lambda_q1[l], lambda_k1[l], lambda_q2[l], lambda_k2[l]])
    subg_col = subln_g[l][:, None]
    sink_rows = jnp.broadcast_to((sink[l] * LOG2E)[:, None], (B_HEADS, LANES))
    head_gains = jnp.stack([qn_a[l], kn_a[l], qn_b[l], kn_b[l]])
    head_scale = jnp.asarray([Q_SCALE, 1.0, Q_SCALE, 1.0], F32)
    bounds = jnp.concatenate([1.02 * HEAD_DIM * jnp.max(jnp.square(head_gains), axis=1) * jnp.square(head_scale),
                              jnp.max(jnp.abs(sink[l]), keepdims=True) * LOG2E]).astype(F32)
    gsum = _group_sum_matrix()
    tm = 512

    xp = x_prompt.reshape(batch * seq, D_MODEL)
    ctx_row = lambda i: dec_batch
    oa_p, ob_p, vbt, kat_f, va_f, kbt_f = _ctx_front(
        x_prompt, mod3, ctx_row, g1, w_k, w_t, gsum, kgains, qgains, lam_vecs, subg_col, sink_rows, bounds,
        tm // seq)
    new_diff_k = kat_f.reshape(batch, A_HEADS, 2, HEAD_DIM, seq).transpose(0, 4, 1, 2, 3)[:, None]
    new_diff_v = va_f.reshape(batch, 1, seq, A_HEADS, 2 * HEAD_DIM)
    new_win_k = kbt_f.reshape(batch, B_KV_HEADS, HEAD_DIM, seq).transpose(0, 3, 1, 2)[:, None]
    new_win_v = vbt.reshape(batch, B_KV_HEADS, HEAD_DIM, seq).transpose(0, 3, 1, 2)[:, None]

    xs = x_sample.reshape(dec_batch * dec_seq, D_MODEL)
    lat_row = lambda i: i // (dec_seq // tm)
    qat, qbt, ka, kb, vat, vbt = _qkv_proj(
        x_sample, mod3, lat_row, g1, w_k, w_t, gsum, kgains, qgains, _rope_tables(dec_seq), tm, False)
    ckt_a = cache_diff_k[:, l].transpose(0, 2, 3, 4, 1).reshape(dec_batch, A_W, past)
    cv_a = cache_diff_v[:, l].reshape(dec_batch, past * A_HEADS, 2 * HEAD_DIM)
    ckt_b = cache_win_k[:, l].transpose(0, 2, 3, 1).reshape(dec_batch, B_KVW, past)
    cvt_b = cache_win_v[:, l].transpose(0, 2, 3, 1).reshape(dec_batch, B_KVW, past)
    oa_s, w_gate_b, w_up_b, w_down_b, w_in_b = _diff_lat(
        qat, ka.reshape(dec_batch, dec_seq, A_W), vat, ckt_a, cv_a, lam_vecs, subg_col, bounds,
        [w_gate[l], w_up[l], w_down[l], w_in_l])
    ob_s, w_oa_b, w_ob_b, w_out_b = _win_lat(
        qbt, kb.reshape(dec_batch, dec_seq, B_KVW), vbt, ckt_b, cvt_b, sink_rows, bounds,
        [w_oa[l], w_ob[l], w_out[l]])

    n_ctx_tiles = batch * seq // tm
    both_row = lambda i: jnp.where(i < n_ctx_tiles, dec_batch, (i - n_ctx_tiles) // (dec_seq // tm))
    xp1, xs1 = _merge(xp, xs, oa_p, oa_s.reshape(-1, A_W), ob_p, ob_s.reshape(-1, B_Q), mod3, both_row, g1,
                      w_in_b, w_oa_b, w_ob_b, w_out_b, tm)
    y_prompt, y_sample = _ffn(xp1, xs1, mod3, both_row, g2, w_gate_b, w_up_b, w_down_b, tm)

    return (y_prompt.reshape(batch, seq, D_MODEL),
            y_sample.reshape(dec_batch, dec_seq, D_MODEL),
            new_diff_k, new_diff_v, new_win_k, new_win_v)
```

```python
import functools
import math

import jax
import jax.numpy as jnp
import numpy as np
from jax import lax
from jax.experimental import pallas as pl
from jax.experimental.pallas import tpu as pltpu

D_MODEL = 1024
HEAD_DIM = 64
A_HEADS = 8
B_HEADS = 16
B_KV_HEADS = 4
B_GROUP = B_HEADS // B_KV_HEADS
A_W = A_HEADS * 2 * HEAD_DIM
B_Q = B_HEADS * HEAD_DIM
B_KVW = B_KV_HEADS * HEAD_DIM
N_K = A_W + B_KVW
N_T = A_W + B_Q + A_W + B_KVW
GATE_COL = 3 * A_W + B_Q + 2 * B_KVW
WINDOW = 128
BLOCK = 128
GRID_W = 64
D_FF = 2816
N_MOD = 6
EPS = 1e-6
ROPE_BASE = 10000.0
NEG = -1e30
LAMBDA_INIT = 0.8 - 0.6 * math.exp(-0.3 * 0)
LOG2E = math.log2(math.e)
Q_SCALE = HEAD_DIM ** -0.5 * LOG2E

LANES = 128
CHUNK = 256
ONES_ROWS = 16
KEY_CHUNK = 512
QKV_SUBTILE = 512
ROW_SUBTILE = 256
FFN_SUBTILE = 128
DIFF_QUERIES = 128
SHIFT_FREE_LOG2 = 60.0
VMEM_LIMIT = 56 * 1024 * 1024

F32 = jnp.float32
BF16 = jnp.bfloat16


def _params(n_axes, flags=None):
    return pltpu.CompilerParams(dimension_semantics=("arbitrary",) * n_axes,
                                vmem_limit_bytes=VMEM_LIMIT, flags=flags)


def _const_spec(shape):
    nd = len(shape)
    return pl.BlockSpec(shape, lambda *_: (0,) * nd, pipeline_mode=pl.Buffered(1))


def _dot(a, b):
    return jnp.dot(a, b, preferred_element_type=F32)


def _modulated_norm(x, gain, scale, shift):
    y = x * lax.rsqrt(jnp.mean(x * x, axis=-1, keepdims=True) + EPS) * gain
    return y * (1.0 + scale) + shift


def _software_pipeline(n, stages):
    depth = len(stages)
    state = [None] * n
    for step in range(n + depth - 1):
        for k, stage in enumerate(stages):
            i = step - k
            if 0 <= i < n:
                state[i] = stage(i, state[i])


def _mod_kernel(c_ref, w_ref, b_ref, o_ref):
    cc = c_ref[...]
    s = cc * jax.nn.sigmoid(cc)
    mod = _dot(s.astype(BF16), w_ref[...].astype(BF16)) + b_ref[...]
    for r in range(mod.shape[0]):
        o_ref[r] = mod[r:r + 1, :]


def _modulation(cc, w_ada, b_ada):
    rows, n = cc.shape[0], w_ada.shape[1]
    tn = n // 4
    return pl.pallas_call(
        _mod_kernel,
        grid=(n // tn,),
        in_specs=[pl.BlockSpec((rows, D_MODEL), lambda j: (0, 0)),
                  pl.BlockSpec((D_MODEL, tn), lambda j: (0, j)),
                  pl.BlockSpec((1, tn), lambda j: (0, j))],
        out_specs=pl.BlockSpec((rows, 1, tn), lambda j: (0, 0, j)),
        out_shape=jax.ShapeDtypeStruct((rows, 1, n), F32),
        compiler_params=_params(1),
        name="adaln_mod",
    )(cc, w_ada, b_ada)


_W_T_BLOCKS = (0, 1, 2, 3, 12, 13, 14, 15, 8, 9, 10, 11, 17)
_W_K_BLOCKS = (4, 5, 6, 7, 16)


def _wprep_kernel(src_ref, w_ref, wt_ref, wk_ref):
    del src_ref
    transposed = pl.program_id(0) < len(_W_T_BLOCKS)

    @pl.when(transposed)
    def _():
        wt_ref[...] = w_ref[...].T.astype(BF16)

    @pl.when(jnp.logical_not(transposed))
    def _():
        wk_ref[...] = w_ref[...].astype(BF16)


def _weight_prep(w_in_l):
    n_t, n_k = len(_W_T_BLOCKS), len(_W_K_BLOCKS)
    order = jnp.asarray(_W_T_BLOCKS + _W_K_BLOCKS, jnp.int32)
    return pl.pallas_call(
        _wprep_kernel,
        grid_spec=pltpu.PrefetchScalarGridSpec(
            num_scalar_prefetch=1,
            grid=(n_t + n_k,),
            in_specs=[pl.BlockSpec((D_MODEL, CHUNK), lambda j, src: (0, src[j]))],
            out_specs=[pl.BlockSpec((CHUNK, D_MODEL), lambda j, src: (jnp.minimum(j, n_t - 1), 0)),
                       pl.BlockSpec((D_MODEL, CHUNK), lambda j, src: (0, jnp.maximum(j - n_t, 0)))]),
        out_shape=[jax.ShapeDtypeStruct((N_T, D_MODEL), BF16), jax.ShapeDtypeStruct((D_MODEL, N_K), BF16)],
        compiler_params=_params(1),
        name="qkv_weight_prep",
    )(order, w_in_l)


def _qkv_kernel(*refs, rope, emit_new_kv):
    x_ref, mod_ref, g1_ref, wk_ref, wt_ref, gsum_ref, kgain_ref, qgain_ref = refs[:8]
    refs = refs[8:]
    if rope:
        cos_ref, sa_ref, sb_ref, cost_ref, sint_ref = refs[:5]
        refs = refs[5:]
    qat_ref, qbt_ref, ka_ref, kb_ref, vat_ref, vbt_ref = refs[:6]
    if emit_new_kv:
        kat_ref, va_ref, kbt_ref = refs[6:]

    mod = mod_ref[...]
    gsum = gsum_ref[...]
    n_wide = A_W // CHUNK
    key_cols = [j * CHUNK for j in range(n_wide)] + [A_W]
    n_feat = N_T // CHUNK
    sub = min(QKV_SUBTILE, x_ref.shape[0])

    def normed(u):
        h = _modulated_norm(x_ref[u * sub:(u + 1) * sub, :], g1_ref[...], mod[:, D_MODEL:2 * D_MODEL],
                            mod[:, 0:D_MODEL])
        return h.astype(BF16), h.T.astype(BF16)

    def project(u, hb, ht):
        tok = slice(u * sub, (u + 1) * sub)

        def put_feat(ref, rows, val):
            width = ref.shape[2]
            step = min(sub, width)
            for off in range(0, sub, step):
                s, col = divmod(u * sub + off, width)
                ref[s, rows, col:col + step] = val[:, off:off + step].astype(ref.dtype)

        def key_project(j, _):
            return _dot(hb, wk_ref[:, key_cols[j]:key_cols[j] + CHUNK])

        def key_squares(j, p):
            return p, _dot((p * p).astype(BF16), gsum)

        def key_finish(j, st):
            p, ss = st
            row = 0 if j < n_wide else 1
            p = p * lax.rsqrt(ss * (1.0 / HEAD_DIM) + EPS) * kgain_ref[row:row + 1, :]
            if rope:
                cos, sa, sb = cos_ref[tok, :], sa_ref[tok, :], sb_ref[tok, :]
                halves = []
                for i in range(CHUNK // LANES):
                    xh = p[:, i * LANES:(i + 1) * LANES]
                    nxt = pltpu.roll(xh, LANES - HEAD_DIM // 4, 1)
                    prv = pltpu.roll(xh, HEAD_DIM // 4, 1)
                    halves.append(xh * cos + nxt * sa + prv * sb)
                p = jnp.concatenate(halves, axis=1)
            if j < n_wide:
                sl = slice(j * CHUNK, (j + 1) * CHUNK)
                ka_ref[tok, sl] = p.astype(BF16)
                if emit_new_kv:
                    put_feat(kat_ref, sl, p.T)
            else:
                kb_ref[tok, :] = p.astype(BF16)
                if emit_new_kv:
                    put_feat(kbt_ref, slice(None), p.T)

        _software_pipeline(len(key_cols), [key_project, key_squares, key_finish])

        def feat_project(c, _):
            return _dot(wt_ref[c * CHUNK:(c + 1) * CHUNK, :], ht)

        def feat_finish(c, p):
            kind, j = divmod(c, n_wide)
            sl = slice(j * CHUNK, (j + 1) * CHUNK)
            if kind == 2:
                put_feat(vat_ref, sl, p)
                if emit_new_kv:
                    va_ref[tok, sl] = p.T
                return
            if kind == 3:
                put_feat(vbt_ref, slice(None), p)
                return
            gain = qgain_ref[kind]
            q4 = HEAD_DIM // 4
            heads = []
            for r in range(CHUNK // HEAD_DIM):
                x = p[r * HEAD_DIM:(r + 1) * HEAD_DIM, :]
                x = x * lax.rsqrt(jnp.mean(x * x, axis=0, keepdims=True) + EPS) * gain
                if rope:
                    swapped = jnp.concatenate([x[q4:2 * q4], x[0:q4], x[3 * q4:4 * q4], x[2 * q4:3 * q4]],
                                              axis=0)
                    x = x * cost_ref[:, tok] + swapped * sint_ref[:, tok]
                heads.append(x)
            put_feat(qat_ref if kind == 0 else qbt_ref, sl, jnp.concatenate(heads, axis=0) * Q_SCALE)

        _software_pipeline(n_feat, [feat_project, feat_finish])

    n_sub = x_ref.shape[0] // sub
    nxt = normed(0)
    for u in range(n_sub):
        cur, nxt = nxt, (normed(u + 1) if u + 1 < n_sub else None)
        project(u, *cur)


def _qkv_proj(x3d, mod3, mod_row, g1, w_k, w_t, gsum, kgains, qgains, rope_tabs, tm, emit_new_kv):
    b, t, _ = x3d.shape
    rows = b * t
    rope = rope_tabs is not None
    tps = max(t // tm, 1)
    spt = max(tm // t, 1)
    tw = tm // spt
    in_specs = [pl.BlockSpec((tm, D_MODEL), lambda i: (i, 0)),
                pl.BlockSpec((None, 1, N_MOD * D_MODEL), lambda i: (mod_row(i), 0, 0)),
                _const_spec((1, D_MODEL)),
                _const_spec((D_MODEL, N_K)),
                _const_spec((N_T, D_MODEL)),
                _const_spec((CHUNK, CHUNK)),
                _const_spec((2, CHUNK)),
                _const_spec((2, HEAD_DIM, 1))]
    args = [x3d.reshape(rows, D_MODEL), mod3, g1, w_k, w_t, gsum, kgains, qgains]
    if rope:
        nat, featm = rope_tabs
        in_specs += [pl.BlockSpec((tm, LANES), lambda i: (i % tps, 0))] * 3
        in_specs += [pl.BlockSpec((HEAD_DIM, tm), lambda i: (0, i % tps))] * 2
        args += list(nat) + list(featm)
    tok_wide = pl.BlockSpec((tm, A_W), lambda i: (i, 0))
    tok_narrow = pl.BlockSpec((tm, B_KVW), lambda i: (i, 0))
    feat_wide = pl.BlockSpec((spt, A_W, tw), lambda i: (i // tps, 0, i % tps))
    feat_narrow = pl.BlockSpec((spt, B_KVW, tw), lambda i: (i // tps, 0, i % tps))
    out_specs = [feat_wide, feat_wide, tok_wide, tok_narrow, feat_wide, feat_narrow]
    out_shape = [jax.ShapeDtypeStruct((b, A_W, t), BF16),
                 jax.ShapeDtypeStruct((b, B_Q, t), BF16),
                 jax.ShapeDtypeStruct((rows, A_W), BF16),
                 jax.ShapeDtypeStruct((rows, B_KVW), BF16),
                 jax.ShapeDtypeStruct((b, A_W, t), BF16),
                 jax.ShapeDtypeStruct((b, B_KVW, t), F32 if emit_new_kv else BF16)]
    if emit_new_kv:
        out_specs += [feat_wide, tok_wide, feat_narrow]
        out_shape += [jax.ShapeDtypeStruct((b, A_W, t), F32),
                      jax.ShapeDtypeStruct((rows, A_W), F32),
                      jax.ShapeDtypeStruct((b, B_KVW, t), F32)]
    return pl.pallas_call(
        functools.partial(_qkv_kernel, rope=rope, emit_new_kv=emit_new_kv),
        grid=(rows // tm,),
        in_specs=in_specs,
        out_specs=out_specs,
        out_shape=out_shape,
        compiler_params=_params(1),
        name="qkv_rope" if rope else "qkv_ctx",
    )(*args)


def _lambda(lam_ref):
    lv = lam_ref[...]
    t1 = jnp.sum(lv[0:1] * lv[1:2], axis=-1, keepdims=True)
    t2 = jnp.sum(lv[2:3] * lv[3:4], axis=-1, keepdims=True)
    return jnp.exp(t1) - jnp.exp(t2) + LAMBDA_INIT


def _with_ones(vt):
    return jnp.concatenate([vt, jnp.ones((ONES_ROWS, vt.shape[1]), vt.dtype)], axis=0)


def _column_softmax(scores, extra=None):
    m = functools.reduce(jnp.maximum, [jnp.max(s, axis=0, keepdims=True) for s in scores])
    if extra is not None:
        m = jnp.maximum(m, extra)
    return [jnp.exp2(s - m).astype(BF16) for s in scores], m


def _max_row_sq_norm(x):
    x = x.astype(F32)
    return jnp.max(jnp.sum(x * x, axis=1, keepdims=True), axis=0, keepdims=True)[0, 0]


def _scores_bounded(q_sq, k_sq, sink_abs=0.0):
    limit = SHIFT_FREE_LOG2
    return jnp.logical_and(q_sq * k_sq <= limit * limit, sink_abs <= limit)


def _diff_query_cols(qt):
    zero = jnp.zeros((HEAD_DIM, qt.shape[1]), qt.dtype)
    return jnp.concatenate([jnp.concatenate([qt[:HEAD_DIM], zero], axis=0),
                            jnp.concatenate([zero, qt[HEAD_DIM:]], axis=0)], axis=1)


def _diff_output(r, lam, subg_col):
    n = r.shape[1] // 2
    dv = 2 * HEAD_DIM
    tot = r[dv:dv + 1, :]
    o = r[:dv, :n] * (1.0 / tot[:, :n]) - r[:dv, n:] * (lam / tot[:, n:])
    o = o * lax.rsqrt(jnp.mean(o * o, axis=0, keepdims=True) + EPS)
    return (o * subg_col * (1.0 - LAMBDA_INIT)).T


def _group_output(r, m, sink_row):
    n = r.shape[1] // B_GROUP
    tot = r[HEAD_DIM:HEAD_DIM + 1, :] + jnp.exp2(sink_row - m)
    o = r[:HEAD_DIM, :] * (1.0 / tot)
    return jnp.concatenate([o[:, i * n:(i + 1) * n] for i in range(B_GROUP)], axis=0).T


def _sink_row(sink_ref, g, n):
    reps = n // LANES
    return jnp.concatenate([sink_ref[pl.ds(g * B_GROUP + r, 1), :] for r in range(B_GROUP)
                            for _ in range(reps)], axis=1)


def _ctx_attn_kernel(qat_ref, qbt_ref, ka_ref, kb_ref, vat_ref, vbt_ref, lam_ref, subg_ref, sink_ref,
                     bounds_ref, oa_ref, ob_ref):
    lam = _lambda(lam_ref)
    subg_col = subg_ref[...]
    n_seq, _, n = qat_ref.shape
    chains = A_HEADS + B_KV_HEADS

    def scores(item, _):
        s, c = divmod(item, chains)
        rows = slice(s * n, (s + 1) * n)
        if c < A_HEADS:
            sl = slice(c * LANES, (c + 1) * LANES)
            return _dot(ka_ref[rows, sl], _diff_query_cols(qat_ref[s, sl, :]))
        g = c - A_HEADS
        qs = jnp.concatenate([qbt_ref[s, (g * B_GROUP + r) * HEAD_DIM:(g * B_GROUP + r + 1) * HEAD_DIM, :]
                              for r in range(B_GROUP)], axis=1)
        zero = jnp.zeros_like(qs)
        w = jnp.concatenate([qs if i == g else zero for i in range(B_KV_HEADS)], axis=0)
        return _dot(kb_ref[rows, :], w)

    def probs(shift, item, sc):
        c = item % chains
        if not shift:
            return [jnp.exp2(sc).astype(BF16)], jnp.zeros((1, sc.shape[1]), F32)
        if c < A_HEADS:
            return _column_softmax([sc])
        return _column_softmax([sc], extra=_sink_row(sink_ref, c - A_HEADS, n))

    def outputs(item, st):
        s, c = divmod(item, chains)
        rows = slice(s * n, (s + 1) * n)
        es, m = st
        if c < A_HEADS:
            sl = slice(c * LANES, (c + 1) * LANES)
            r = _dot(_with_ones(vat_ref[s, sl, :]), es[0])
            oa_ref[rows, sl] = _diff_output(r, lam, subg_col).astype(BF16)
        else:
            g = c - A_HEADS
            vt = vbt_ref[s, g * HEAD_DIM:(g + 1) * HEAD_DIM, :].astype(BF16)
            r = _dot(_with_ones(vt), es[0])
            o = _group_output(r, m, _sink_row(sink_ref, g, n))
            ob_ref[rows, g * CHUNK:(g + 1) * CHUNK] = o.astype(BF16)

    bounded = jnp.logical_and(_scores_bounded(bounds_ref[0], bounds_ref[1]),
                              _scores_bounded(bounds_ref[2], bounds_ref[3], bounds_ref[4]))

    @pl.when(bounded)
    def _():
        _software_pipeline(n_seq * chains, [scores, functools.partial(probs, False), outputs])

    @pl.when(jnp.logical_not(bounded))
    def _():
        _software_pipeline(n_seq * chains, [scores, functools.partial(probs, True), outputs])


def _ctx_front_kernel(x_ref, mod_ref, g1_ref, wk_ref, wt_ref, gsum_ref, kgain_ref, qgain_ref,
                      lam_ref, subg_ref, sink_ref, bounds_ref,
                      oa_ref, ob_ref, vbt_ref, kat_ref, va_ref, kbt_ref,
                      qat_s, qbt_s, ka_s, kb_s, vat_s):
    _qkv_kernel(x_ref, mod_ref, g1_ref, wk_ref, wt_ref, gsum_ref, kgain_ref, qgain_ref,
                qat_s, qbt_s, ka_s, kb_s, vat_s, vbt_ref, kat_ref, va_ref, kbt_ref,
                rope=False, emit_new_kv=True)
    _ctx_attn_kernel(qat_s, qbt_s, ka_s, kb_s, vat_s, vbt_ref, lam_ref, subg_ref, sink_ref, bounds_ref,
                     oa_ref, ob_ref)


def _ctx_front(x3d, mod3, mod_row, g1, w_k, w_t, gsum, kgains, qgains, lam_vecs, subg_col, sink_rows, bounds,
               n_seq):
    b, t, _ = x3d.shape
    rows, tm = b * t, n_seq * t
    tok = pl.BlockSpec((tm, A_W), lambda i: (i, 0))
    feat_wide = pl.BlockSpec((n_seq, A_W, t), lambda i: (i, 0, 0))
    feat_narrow = pl.BlockSpec((n_seq, B_KVW, t), lambda i: (i, 0, 0))
    return pl.pallas_call(
        _ctx_front_kernel,
        grid=(b // n_seq,),
        in_specs=[pl.BlockSpec((tm, D_MODEL), lambda i: (i, 0)),
                  pl.BlockSpec((None, 1, N_MOD * D_MODEL), lambda i: (mod_row(i), 0, 0)),
                  _const_spec((1, D_MODEL)), _const_spec((D_MODEL, N_K)), _const_spec((N_T, D_MODEL)),
                  _const_spec((CHUNK, CHUNK)), _const_spec((2, CHUNK)), _const_spec((2, HEAD_DIM, 1)),
                  _const_spec((4, HEAD_DIM)), _const_spec((2 * HEAD_DIM, 1)),
                  _const_spec((B_HEADS, LANES)), pl.BlockSpec(memory_space=pltpu.SMEM)],
        out_specs=[tok, tok, feat_narrow, feat_wide, tok, feat_narrow],
        out_shape=[jax.ShapeDtypeStruct((rows, A_W), BF16), jax.ShapeDtypeStruct((rows, B_Q), BF16),
                   jax.ShapeDtypeStruct((b, B_KVW, t), F32), jax.ShapeDtypeStruct((b, A_W, t), F32),
                   jax.ShapeDtypeStruct((rows, A_W), F32), jax.ShapeDtypeStruct((b, B_KVW, t), F32)],
        scratch_shapes=[pltpu.VMEM((n_seq, A_W, t), BF16), pltpu.VMEM((n_seq, B_Q, t), BF16),
                        pltpu.VMEM((tm, A_W), BF16), pltpu.VMEM((tm, B_KVW), BF16),
                        pltpu.VMEM((n_seq, A_W, t), BF16)],
        compiler_params=_params(1),
        name="ctx_front",
    )(x3d.reshape(rows, D_MODEL), mod3, g1, w_k, w_t, gsum, kgains, qgains, lam_vecs, subg_col, sink_rows,
      bounds)


def _rider_specs(weights, n_steps, flat_step):
    specs = []
    for w in weights:
        rows, cols = w.shape
        bands = next(n for n in (n_steps, n_steps // 2, n_steps // 4)
                     if rows % n == 0 and (rows // n) % 16 == 0)
        repeat = n_steps // bands
        spec = pl.BlockSpec((rows // bands, cols), lambda *g, repeat=repeat: (flat_step(*g) // repeat, 0))
        specs.append((spec, jax.ShapeDtypeStruct(w.shape, BF16)))
    return specs


def _cast_riders(rider_refs):
    n = len(rider_refs) // 2
    for w_ref, o_ref in zip(rider_refs[:n], rider_refs[n:]):
        o_ref[...] = w_ref[...].astype(BF16)


def _diff_lat_kernel(qt_ref, k_ref, vt_ref, ckt_ref, cv_ref, lam_ref, subg_ref, bounds_ref, *rest, n_riders):
    o_ref = rest[n_riders]
    _cast_riders(rest[:n_riders] + rest[n_riders + 1:])
    lam = _lambda(lam_ref)
    subg_col = subg_ref[...]
    k = k_ref[...]
    ck = ckt_ref[...].T.astype(BF16)
    vt = _with_ones(vt_ref[...])
    past = cv_ref.shape[0] // A_HEADS
    cv = cv_ref[pl.ds(pl.program_id(1), past, stride=A_HEADS), :]
    cvt = _with_ones(cv.T.astype(BF16))

    def scores(i, _):
        q = _diff_query_cols(qt_ref[:, i * DIFF_QUERIES:(i + 1) * DIFF_QUERIES])
        return [_dot(k, q), _dot(ck, q)]

    def outputs(shift, i, s):
        if shift:
            m = jnp.maximum(jnp.max(s[0], axis=0, keepdims=True), jnp.max(s[1], axis=0, keepdims=True))
        r = None
        for sc, v in ((s[0], vt), (s[1], cvt)):
            for c in range(0, sc.shape[0], KEY_CHUNK):
                piece = sc[c:c + KEY_CHUNK]
                part = _dot(v[:, c:c + KEY_CHUNK], jnp.exp2(piece - m if shift else piece).astype(BF16))
                r = part if r is None else r + part
        o_ref[i * DIFF_QUERIES:(i + 1) * DIFF_QUERIES, :] = _diff_output(r, lam, subg_col).astype(BF16)

    n_sub = qt_ref.shape[1] // DIFF_QUERIES
    bounded = _scores_bounded(bounds_ref[0], jnp.maximum(bounds_ref[1], _max_row_sq_norm(ck)))

    @pl.when(bounded)
    def _():
        _software_pipeline(n_sub, [scores, functools.partial(outputs, False)])

    @pl.when(jnp.logical_not(bounded))
    def _():
        _software_pipeline(n_sub, [scores, functools.partial(outputs, True)])


def _diff_lat(qt, k, vt, ckt, cv, lam_vecs, subg_col, bounds, riders):
    b, _, t = qt.shape
    past = ckt.shape[2]
    rider_specs = _rider_specs(riders, b * A_HEADS, lambda i, h: i * A_HEADS + h)
    return pl.pallas_call(
        functools.partial(_diff_lat_kernel, n_riders=len(riders)),
        grid=(b, A_HEADS),
        in_specs=[pl.BlockSpec((None, LANES, t), lambda i, h: (i, h, 0)),
                  pl.BlockSpec((None, t, LANES), lambda i, h: (i, 0, h)),
                  pl.BlockSpec((None, LANES, t), lambda i, h: (i, h, 0)),
                  pl.BlockSpec((None, LANES, past), lambda i, h: (i, h, 0)),
                  pl.BlockSpec((None, past * A_HEADS, LANES), lambda i, h: (i, 0, 0)),
                  _const_spec((4, HEAD_DIM)), _const_spec((2 * HEAD_DIM, 1)),
                  pl.BlockSpec(memory_space=pltpu.SMEM)] + [spec for spec, _ in rider_specs],
        out_specs=[pl.BlockSpec((None, t, LANES), lambda i, h: (i, 0, h))] + [spec for spec, _ in rider_specs],
        out_shape=[jax.ShapeDtypeStruct((b, t, A_W), BF16)] + [shape for _, shape in rider_specs],
        compiler_params=_params(2),
        name="diff_lat",
    )(qt, k, vt, ckt, cv, lam_vecs, subg_col, bounds, *riders)


def _window_start(i, t):
    return min(max((i - 1) * BLOCK, 0), t - 3 * BLOCK)


def _band_offsets(t):
    return sorted({_window_start(i, t) - i * BLOCK for i in range(t // BLOCK)})


def _band_bias(t):
    key = np.arange(3 * BLOCK)[:, None]
    qry = np.arange(B_GROUP * BLOCK)[None, :] % BLOCK
    return jnp.asarray(np.stack([np.where(np.abs(key + off - qry) <= WINDOW, 0.0, NEG)
                                 for off in _band_offsets(t)]), dtype=F32)


def _win_lat_kernel(qt_ref, k_ref, vt_ref, ckt_ref, cvt_ref, sink_ref, bias_ref, bounds_ref, *rest, n_riders):
    o_ref = rest[n_riders]
    _cast_riders(rest[:n_riders] + rest[n_riders + 1:])
    g = pl.program_id(1)
    t = k_ref.shape[0]
    span = 3 * BLOCK
    keep = lax.broadcasted_iota(jnp.int32, (1, CHUNK), 1) // HEAD_DIM == g
    k = jnp.where(keep, k_ref[...].astype(F32), 0.0).astype(BF16)
    ck = jnp.where(keep, ckt_ref[...].T, 0.0).astype(BF16)
    vt = vt_ref[...]
    cvt = _with_ones(cvt_ref[...].astype(BF16))
    sink_row = _sink_row(sink_ref, g, BLOCK)
    offsets = _band_offsets(t)

    def scores(i, _):
        w = _window_start(i, t)
        qs = jnp.concatenate([qt_ref[r * HEAD_DIM:(r + 1) * HEAD_DIM, i * BLOCK:(i + 1) * BLOCK]
                              for r in range(B_GROUP)], axis=1)
        wq = jnp.concatenate([qs] * B_KV_HEADS, axis=0)
        return [_dot(k[w:w + span], wq) + bias_ref[offsets.index(w - i * BLOCK)], _dot(ck, wq)]

    def probs(shift, i, s):
        if shift:
            return _column_softmax(s, extra=sink_row)
        return [jnp.exp2(piece).astype(BF16) for piece in s], jnp.zeros_like(sink_row)

    def outputs(i, st):
        es, m = st
        w = _window_start(i, t)
        r = _dot(_with_ones(vt[:, w:w + span]), es[0]) + _dot(cvt, es[1])
        o_ref[i * BLOCK:(i + 1) * BLOCK, :] = _group_output(r, m, sink_row).astype(BF16)

    bounded = _scores_bounded(bounds_ref[2], jnp.maximum(bounds_ref[3], _max_row_sq_norm(ck)), bounds_ref[4])

    @pl.when(bounded)
    def _():
        _software_pipeline(t // BLOCK, [scores, functools.partial(probs, False), outputs])

    @pl.when(jnp.logical_not(bounded))
    def _():
        _software_pipeline(t // BLOCK, [scores, functools.partial(probs, True), outputs])


def _win_lat(qt, k, vt, ckt, cvt, sink_rows, bounds, riders):
    b, _, t = qt.shape
    past = ckt.shape[2]
    bias = _band_bias(t)
    rider_specs = _rider_specs(riders, b * B_KV_HEADS, lambda i, g: i * B_KV_HEADS + g)
    return pl.pallas_call(
        functools.partial(_win_lat_kernel, n_riders=len(riders)),
        grid=(b, B_KV_HEADS),
        in_specs=[pl.BlockSpec((None, CHUNK, t), lambda i, g: (i, g, 0)),
                  pl.BlockSpec((None, t, B_KVW), lambda i, g: (i, 0, 0)),
                  pl.BlockSpec((None, HEAD_DIM, t), lambda i, g: (i, g, 0)),
                  pl.BlockSpec((None, B_KVW, past), lambda i, g: (i, 0, 0)),
                  pl.BlockSpec((None, HEAD_DIM, past), lambda i, g: (i, g, 0)),
                  _const_spec((B_HEADS, LANES)), _const_spec(bias.shape),
                  pl.BlockSpec(memory_space=pltpu.SMEM)] + [spec for spec, _ in rider_specs],
        out_specs=[pl.BlockSpec((None, t, CHUNK), lambda i, g: (i, 0, g))] + [spec for spec, _ in rider_specs],
        out_shape=[jax.ShapeDtypeStruct((b, t, B_Q), BF16)] + [shape for _, shape in rider_specs],
        compiler_params=_params(2),
        name="win_lat",
    )(qt, k, vt, ckt, cvt, sink_rows, bias, bounds, *riders)


def _merge_kernel(x1_ref, x2_ref, oa1_ref, oa2_ref, ob1_ref, ob2_ref, mod_ref, g1_ref, win_ref, woa_ref,
                  wob_ref, wout_ref, o1_ref, o2_ref, *, n_first):
    def tile(x_ref, oa_ref, ob_ref, o_ref):
        mod = mod_ref[...]
        rows = lambda i: slice(i * ROW_SUBTILE, (i + 1) * ROW_SUBTILE)

        def norm(i, _):
            return _modulated_norm(x_ref[rows(i), :], g1_ref[...], mod[:, D_MODEL:2 * D_MODEL],
                                   mod[:, 0:D_MODEL]).astype(BF16)

        def branches(i, h):
            ga = _dot(h, win_ref[:, GATE_COL:GATE_COL + D_MODEL])
            gb = _dot(h, win_ref[:, GATE_COL + D_MODEL:GATE_COL + 2 * D_MODEL])
            return (jax.nn.sigmoid(ga) * _dot(oa_ref[rows(i), :], woa_ref[...])
                    + jax.nn.sigmoid(gb) * _dot(ob_ref[rows(i), :], wob_ref[...])).astype(BF16)

        def project(i, merged):
            o_ref[rows(i), :] = x_ref[rows(i), :] + mod[:, 2 * D_MODEL:3 * D_MODEL] * _dot(merged, wout_ref[...])

        _software_pipeline(x_ref.shape[0] // ROW_SUBTILE, [norm, branches, project])

    _per_token_set(n_first, functools.partial(tile, x1_ref, oa1_ref, ob1_ref, o1_ref),
                   functools.partial(tile, x2_ref, oa2_ref, ob2_ref, o2_ref))


def _per_token_set(n_first, first_body, second_body):
    first = pl.program_id(0) < n_first
    pl.when(first)(first_body)
    pl.when(jnp.logical_not(first))(second_body)


def _split_tiles(n_first, tm, width):
    return (pl.BlockSpec((tm, width), lambda i: (jnp.minimum(i, n_first - 1), 0)),
            pl.BlockSpec((tm, width), lambda i: (jnp.maximum(i - n_first, 0), 0)))


def _merge(x1, x2, oa1, oa2, ob1, ob2, mod3, mod_row, g1, w_in, w_oa, w_ob, w_out, tm):
    n_first, n_second = x1.shape[0] // tm, x2.shape[0] // tm
    t1, t2 = _split_tiles(n_first, tm, D_MODEL)
    sq = _const_spec((D_MODEL, D_MODEL))
    return pl.pallas_call(
        functools.partial(_merge_kernel, n_first=n_first),
        grid=(n_first + n_second,),
        in_specs=[t1, t2, t1, t2, t1, t2,
                  pl.BlockSpec((None, 1, N_MOD * D_MODEL), lambda i: (mod_row(i), 0, 0)),
                  _const_spec((1, D_MODEL)), _const_spec(w_in.shape), sq, sq, sq],
        out_specs=[t1, t2],
        out_shape=[jax.ShapeDtypeStruct(x1.shape, F32), jax.ShapeDtypeStruct(x2.shape, F32)],
        compiler_params=_params(1),
        name="merge_out",
    )(x1, x2, oa1, oa2, ob1, ob2, mod3, g1, w_in, w_oa, w_ob, w_out)


def _ffn_kernel(x1_ref, x2_ref, mod_ref, g2_ref, wgate_ref, wup_ref, wdown_ref, o1_ref, o2_ref, *, n_first):
    def tile(x_ref, o_ref):
        mod = mod_ref[...]
        rows = lambda i: slice(i * FFN_SUBTILE, (i + 1) * FFN_SUBTILE)

        def norm(i, _):
            return _modulated_norm(x_ref[rows(i), :], g2_ref[...], mod[:, 4 * D_MODEL:5 * D_MODEL],
                                   mod[:, 3 * D_MODEL:4 * D_MODEL]).astype(BF16)

        def hidden(i, h):
            gate = _dot(h, wgate_ref[...])
            return (gate * jax.nn.sigmoid(gate) * _dot(h, wup_ref[...])).astype(BF16)

        def project(i, act):
            o_ref[rows(i), :] = x_ref[rows(i), :] + mod[:, 5 * D_MODEL:6 * D_MODEL] * _dot(act, wdown_ref[...])

        _software_pipeline(x_ref.shape[0] // FFN_SUBTILE, [norm, hidden, project])

    _per_token_set(n_first, functools.partial(tile, x1_ref, o1_ref), functools.partial(tile, x2_ref, o2_ref))


def _ffn(x1, x2, mod3, mod_row, g2, w_gate, w_up, w_down, tm):
    n_first, n_second = x1.shape[0] // tm, x2.shape[0] // tm
    t1, t2 = _split_tiles(n_first, tm, D_MODEL)
    return pl.pallas_call(
        functools.partial(_ffn_kernel, n_first=n_first),
        grid=(n_first + n_second,),
        in_specs=[t1, t2,
                  pl.BlockSpec((None, 1, N_MOD * D_MODEL), lambda i: (mod_row(i), 0, 0)),
                  _const_spec((1, D_MODEL)), _const_spec((D_MODEL, D_FF)),
                  _const_spec((D_MODEL, D_FF)), _const_spec((D_FF, D_MODEL))],
        out_specs=[t1, t2],
        out_shape=[jax.ShapeDtypeStruct(x1.shape, F32), jax.ShapeDtypeStruct(x2.shape, F32)],
        compiler_params=_params(1),
        name="swiglu_ffn",
    )(x1, x2, mod3, g2, w_gate, w_up, w_down)


def _rope_tables(seq_len):
    quarter = HEAD_DIM // 4
    t = np.arange(seq_len)
    row = (t // GRID_W).astype(np.float32)
    col = (t % GRID_W).astype(np.float32)
    freqs = (np.float32(ROPE_BASE) ** (-np.arange(quarter, dtype=np.float32) / np.float32(quarter))
             ).astype(np.float32)
    ang_r = row[:, None] * freqs
    ang_c = col[:, None] * freqs
    zeros = np.zeros_like(ang_r)
    cos = np.concatenate([np.cos(ang_r)] * 2 + [np.cos(ang_c)] * 2, axis=-1)
    sin_next = np.concatenate([-np.sin(ang_r), zeros, -np.sin(ang_c), zeros], axis=-1)
    sin_prev = np.concatenate([zeros, np.sin(ang_r), zeros, np.sin(ang_c)], axis=-1)
    rep = LANES // HEAD_DIM
    nat = tuple(jnp.asarray(np.tile(a, (1, rep)), dtype=F32) for a in (cos, sin_next, sin_prev))
    featm = (jnp.asarray(cos.T, dtype=F32), jnp.asarray((sin_next + sin_prev).T, dtype=F32))
    return nat, featm


def _group_sum_matrix():
    idx = np.arange(CHUNK) // HEAD_DIM
    return jnp.asarray(idx[:, None] == idx[None, :], dtype=BF16)


def kernel(x_prompt, x_sample, cache_diff_k, cache_diff_v, cache_win_k, cache_win_v, c, c_ctx,
           w_ada, b_ada, norm1_g, w_in, qn_a, kn_a, lambda_q1, lambda_k1, lambda_q2, lambda_k2,
           subln_g, qn_b, kn_b, sink, w_oa, w_ob, w_out, norm2_g, w_gate, w_up, w_down):
    batch, seq, _ = x_prompt.shape
    dec_batch, dec_seq, _ = x_sample.shape
    past = cache_diff_k.shape[2]
    l = 0

    cc = jnp.concatenate([c, c_ctx[None, :], jnp.zeros((8 - dec_batch - 1, D_MODEL), F32)], axis=0)
    mod3 = _modulation(cc, w_ada[l], b_ada[l][None, :])

    w_in_l = w_in[l]
    w_t, w_k = _weight_prep(w_in_l)
    g1 = norm1_g[l][None, :]
    g2 = norm2_g[l][None, :]
    rep = CHUNK // HEAD_DIM
    kgains = jnp.stack([jnp.tile(kn_a[l], rep), jnp.tile(kn_b[l], rep)])
    qgains = jnp.stack([qn_a[l], qn_b[l]])[:, :, None]
    lam_vecs = jnp.stack([lambda_q1[l], lambda_k1[l], lambda_q2[l], lambda_k2[l]])
    subg_col = subln_g[l][:, None]
    sink_rows = jnp.broadcast_to((sink[l] * LOG2E)[:, None], (B_HEADS, LANES))
    head_gains = jnp.stack([qn_a[l], kn_a[l], qn_b[l], kn_b[l]])
    head_scale = jnp.asarray([Q_SCALE, 1.0, Q_SCALE, 1.0], F32)
    bounds = jnp.concatenate([1.02 * HEAD_DIM * jnp.max(jnp.square(head_gains), axis=1) * jnp.square(head_scale),
                              jnp.max(jnp.abs(sink[l]), keepdims=True) * LOG2E]).astype(F32)
    gsum = _group_sum_matrix()
    tm = 512

    xp = x_prompt.reshape(batch * seq, D_MODEL)
    ctx_row = lambda i: dec_batch
    oa_p, ob_p, vbt, kat_f, va_f, kbt_f = _ctx_front(
        x_prompt, mod3, ctx_row, g1, w_k, w_t, gsum, kgains, qgains, lam_vecs, subg_col, sink_rows, bounds,
        tm // seq)
    new_diff_k = kat_f.reshape(batch, A_HEADS, 2, HEAD_DIM, seq).transpose(0, 4, 1, 2, 3)[:, None]
    new_diff_v = va_f.reshape(batch, 1, seq, A_HEADS, 2 * HEAD_DIM)
    new_win_k = kbt_f.reshape(batch, B_KV_HEADS, HEAD_DIM, seq).transpose(0, 3, 1, 2)[:, None]
    new_win_v = vbt.reshape(batch, B_KV_HEADS, HEAD_DIM, seq).transpose(0, 3, 1, 2)[:, None]

    xs = x_sample.reshape(dec_batch * dec_seq, D_MODEL)
    lat_row = lambda i: i // (dec_seq // tm)
    qat, qbt, ka, kb, vat, vbt = _qkv_proj(
        x_sample, mod3, lat_row, g1, w_k, w_t, gsum, kgains, qgains, _rope_tables(dec_seq), tm, False)
    ckt_a = cache_diff_k[:, l].transpose(0, 2, 3, 4, 1).reshape(dec_batch, A_W, past)
    cv_a = cache_diff_v[:, l].reshape(dec_batch, past * A_HEADS, 2 * HEAD_DIM)
    ckt_b = cache_win_k[:, l].transpose(0, 2, 3, 1).reshape(dec_batch, B_KVW, past)
    cvt_b = cache_win_v[:, l].transpose(0, 2, 3, 1).reshape(dec_batch, B_KVW, past)
    oa_s, w_gate_b, w_up_b, w_down_b, w_in_b = _diff_lat(
        qat, ka.reshape(dec_batch, dec_seq, A_W), vat, ckt_a, cv_a, lam_vecs, subg_col, bounds,
        [w_gate[l], w_up[l], w_down[l], w_in_l])
    ob_s, w_oa_b, w_ob_b, w_out_b = _win_lat(
        qbt, kb.reshape(dec_batch, dec_seq, B_KVW), vbt, ckt_b, cvt_b, sink_rows, bounds,
        [w_oa[l], w_ob[l], w_out[l]])

    n_ctx_tiles = batch * seq // tm
    both_row = lambda i: jnp.where(i < n_ctx_tiles, dec_batch, (i - n_ctx_tiles) // (dec_seq // tm))
    xp1, xs1 = _merge(xp, xs, oa_p, oa_s.reshape(-1, A_W), ob_p, ob_s.reshape(-1, B_Q), mod3, both_row, g1,
                      w_in_b, w_oa_b, w_ob_b, w_out_b, tm)
    y_prompt, y_sample = _ffn(xp1, xs1, mod3, both_row, g2, w_gate_b, w_up_b, w_down_b, tm)

    return (y_prompt.reshape(batch, seq, D_MODEL),
            y_sample.reshape(dec_batch, dec_seq, D_MODEL),
            new_diff_k, new_diff_v, new_win_k, new_win_v)
```

```python
import functools
import math

import jax
import jax.numpy as jnp
import numpy as np
from jax import lax
from jax.experimental import pallas as pl
from jax.experimental.pallas import tpu as pltpu

D_MODEL = 1024
HEAD_DIM = 64
A_HEADS = 8
B_HEADS = 16
B_KV_HEADS = 4
B_GROUP = B_HEADS // B_KV_HEADS
A_W = A_HEADS * 2 * HEAD_DIM
B_Q = B_HEADS * HEAD_DIM
B_KVW = B_KV_HEADS * HEAD_DIM
N_K = A_W + B_KVW
N_T = A_W + B_Q + A_W + B_KVW
GATE_COL = 3 * A_W + B_Q + 2 * B_KVW
WINDOW = 128
BLOCK = 128
GRID_W = 64
D_FF = 2816
N_MOD = 6
EPS = 1e-6
ROPE_BASE = 10000.0
NEG = -1e30
LAMBDA_INIT = 0.8 - 0.6 * math.exp(-0.3 * 0)
LOG2E = math.log2(math.e)
Q_SCALE = HEAD_DIM ** -0.5 * LOG2E

LANES = 128
CHUNK = 256
ONES_ROWS = 16
KEY_CHUNK = 512
QKV_SUBTILE = 512
ROW_SUBTILE = 256
FFN_SUBTILE = 128
DIFF_QUERIES = 128
SHIFT_FREE_LOG2 = 60.0
VMEM_LIMIT = 56 * 1024 * 1024

F32 = jnp.float32
BF16 = jnp.bfloat16


def _params(n_axes, flags=None):
    return pltpu.CompilerParams(dimension_semantics=("arbitrary",) * n_axes,
                                vmem_limit_bytes=VMEM_LIMIT, flags=flags)


def _const_spec(shape):
    nd = len(shape)
    return pl.BlockSpec(shape, lambda *_: (0,) * nd, pipeline_mode=pl.Buffered(1))


def _dot(a, b):
    return jnp.dot(a, b, preferred_element_type=F32)


def _modulated_norm(x, gain, scale, shift):
    y = x * lax.rsqrt(jnp.mean(x * x, axis=-1, keepdims=True) + EPS) * gain
    return y * (1.0 + scale) + shift


def _software_pipeline(n, stages):
    depth = len(stages)
    state = [None] * n
    for step in range(n + depth - 1):
        for k, stage in enumerate(stages):
            i = step - k
            if 0 <= i < n:
                state[i] = stage(i, state[i])


def _mod_kernel(c_ref, w_ref, b_ref, o_ref):
    cc = c_ref[...]
    s = cc * jax.nn.sigmoid(cc)
    mod = _dot(s.astype(BF16), w_ref[...].astype(BF16)) + b_ref[...]
    for r in range(mod.shape[0]):
        o_ref[r] = mod[r:r + 1, :]


def _modulation(cc, w_ada, b_ada):
    rows, n = cc.shape[0], w_ada.shape[1]
    tn = n // 4
    return pl.pallas_call(
        _mod_kernel,
        grid=(n // tn,),
        in_specs=[pl.BlockSpec((rows, D_MODEL), lambda j: (0, 0)),
                  pl.BlockSpec((D_MODEL, tn), lambda j: (0, j)),
                  pl.BlockSpec((1, tn), lambda j: (0, j))],
        out_specs=pl.BlockSpec((rows, 1, tn), lambda j: (0, 0, j)),
        out_shape=jax.ShapeDtypeStruct((rows, 1, n), F32),
        compiler_params=_params(1),
        name="adaln_mod",
    )(cc, w_ada, b_ada)


PREP_BLOCK = CHUNK
_W_T_BLOCKS = (0, 1, 2, 3, 12, 13, 14, 15, 8, 9, 10, 11, 17)
_W_K_BLOCKS = (4, 5, 6, 7, 16)


def _wprep_kernel(src_ref, w_ref, wt_ref, wk_ref):
    del src_ref
    transposed = pl.program_id(0) < len(_W_T_BLOCKS)

    @pl.when(transposed)
    def _():
        wt_ref[...] = w_ref[...].T.astype(BF16)

    @pl.when(jnp.logical_not(transposed))
    def _():
        wk_ref[...] = w_ref[...].astype(BF16)


def _weight_prep(w_in_l):
    n_t, n_k = len(_W_T_BLOCKS), len(_W_K_BLOCKS)
    order = jnp.asarray(_W_T_BLOCKS + _W_K_BLOCKS, jnp.int32)
    return pl.pallas_call(
        _wprep_kernel,
        grid_spec=pltpu.PrefetchScalarGridSpec(
            num_scalar_prefetch=1,
            grid=(n_t + n_k,),
            in_specs=[pl.BlockSpec((D_MODEL, PREP_BLOCK), lambda j, src: (0, src[j]))],
            out_specs=[pl.BlockSpec((PREP_BLOCK, D_MODEL), lambda j, src: (jnp.minimum(j, n_t - 1), 0)),
                       pl.BlockSpec((D_MODEL, PREP_BLOCK), lambda j, src: (0, jnp.maximum(j - n_t, 0)))]),
        out_shape=[jax.ShapeDtypeStruct((N_T, D_MODEL), BF16), jax.ShapeDtypeStruct((D_MODEL, N_K), BF16)],
        compiler_params=_params(1),
        name="qkv_weight_prep",
    )(order, w_in_l)


def _qkv_kernel(*refs, rope, emit_new_kv):
    x_ref, mod_ref, g1_ref, wk_ref, wt_ref, gsum_ref, kgain_ref, qgain_ref = refs[:8]
    refs = refs[8:]
    if rope:
        cos_ref, sa_ref, sb_ref, cost_ref, sint_ref = refs[:5]
        refs = refs[5:]
    qat_ref, qbt_ref, ka_ref, kb_ref, vat_ref, vbt_ref = refs[:6]
    if emit_new_kv:
        kat_ref, va_ref, kbt_ref = refs[6:]

    mod = mod_ref[...]
    gsum = gsum_ref[...]
    n_wide = A_W // CHUNK
    key_cols = [j * CHUNK for j in range(n_wide)] + [A_W]
    n_feat = N_T // CHUNK
    sub = min(QKV_SUBTILE, x_ref.shape[0])

    def normed(u):
        h = _modulated_norm(x_ref[u * sub:(u + 1) * sub, :], g1_ref[...], mod[:, D_MODEL:2 * D_MODEL],
                            mod[:, 0:D_MODEL])
        return h.astype(BF16), h.T.astype(BF16)

    def project(u, hb, ht):
        tok = slice(u * sub, (u + 1) * sub)

        def put_feat(ref, rows, val):
            width = ref.shape[2]
            step = min(sub, width)
            for off in range(0, sub, step):
                s, col = divmod(u * sub + off, width)
                ref[s, rows, col:col + step] = val[:, off:off + step].astype(ref.dtype)

        def key_project(j, _):
            return _dot(hb, wk_ref[:, key_cols[j]:key_cols[j] + CHUNK])

        def key_squares(j, p):
            return p, _dot((p * p).astype(BF16), gsum)

        def key_finish(j, st):
            p, ss = st
            row = 0 if j < n_wide else 1
            p = p * lax.rsqrt(ss * (1.0 / HEAD_DIM) + EPS) * kgain_ref[row:row + 1, :]
            if rope:
                cos, sa, sb = cos_ref[tok, :], sa_ref[tok, :], sb_ref[tok, :]
                halves = []
                for i in range(CHUNK // LANES):
                    xh = p[:, i * LANES:(i + 1) * LANES]
                    nxt = pltpu.roll(xh, LANES - HEAD_DIM // 4, 1)
                    prv = pltpu.roll(xh, HEAD_DIM // 4, 1)
                    halves.append(xh * cos + nxt * sa + prv * sb)
                p = jnp.concatenate(halves, axis=1)
            if j < n_wide:
                sl = slice(j * CHUNK, (j + 1) * CHUNK)
                ka_ref[tok, sl] = p.astype(BF16)
                if emit_new_kv:
                    put_feat(kat_ref, sl, p.T)
            else:
                kb_ref[tok, :] = p.astype(BF16)
                if emit_new_kv:
                    put_feat(kbt_ref, slice(None), p.T)

        _software_pipeline(len(key_cols), [key_project, key_squares, key_finish])

        def feat_project(c, _):
            return _dot(wt_ref[c * CHUNK:(c + 1) * CHUNK, :], ht)

        def feat_finish(c, p):
            kind, j = divmod(c, n_wide)
            sl = slice(j * CHUNK, (j + 1) * CHUNK)
            if kind == 2:
                put_feat(vat_ref, sl, p)
                if emit_new_kv:
                    va_ref[tok, sl] = p.T
                return
            if kind == 3:
                put_feat(vbt_ref, slice(None), p)
                return
            gain = qgain_ref[kind]
            q4 = HEAD_DIM // 4
            heads = []
            for r in range(CHUNK // HEAD_DIM):
                x = p[r * HEAD_DIM:(r + 1) * HEAD_DIM, :]
                x = x * lax.rsqrt(jnp.mean(x * x, axis=0, keepdims=True) + EPS) * gain
                if rope:
                    swapped = jnp.concatenate([x[q4:2 * q4], x[0:q4], x[3 * q4:4 * q4], x[2 * q4:3 * q4]],
                                              axis=0)
                    x = x * cost_ref[:, tok] + swapped * sint_ref[:, tok]
                heads.append(x)
            put_feat(qat_ref if kind == 0 else qbt_ref, sl, jnp.concatenate(heads, axis=0) * Q_SCALE)

        _software_pipeline(n_feat, [feat_project, feat_finish])

    n_sub = x_ref.shape[0] // sub
    nxt = normed(0)
    for u in range(n_sub):
        cur, nxt = nxt, (normed(u + 1) if u + 1 < n_sub else None)
        project(u, *cur)


def _qkv_proj(x3d, mod3, mod_row, g1, w_k, w_t, gsum, kgains, qgains, rope_tabs, tm, emit_new_kv):
    b, t, _ = x3d.shape
    rows = b * t
    rope = rope_tabs is not None
    tps = max(t // tm, 1)
    spt = max(tm // t, 1)
    tw = tm // spt
    in_specs = [pl.BlockSpec((tm, D_MODEL), lambda i: (i, 0)),
                pl.BlockSpec((None, 1, N_MOD * D_MODEL), lambda i: (mod_row(i), 0, 0)),
                _const_spec((1, D_MODEL)),
                _const_spec((D_MODEL, N_K)),
                _const_spec((N_T, D_MODEL)),
                _const_spec((CHUNK, CHUNK)),
                _const_spec((2, CHUNK)),
                _const_spec((2, HEAD_DIM, 1))]
    args = [x3d.reshape(rows, D_MODEL), mod3, g1, w_k, w_t, gsum, kgains, qgains]
    if rope:
        nat, featm = rope_tabs
        in_specs += [pl.BlockSpec((tm, LANES), lambda i: (i % tps, 0))] * 3
        in_specs += [pl.BlockSpec((HEAD_DIM, tm), lambda i: (0, i % tps))] * 2
        args += list(nat) + list(featm)
    tok_wide = pl.BlockSpec((tm, A_W), lambda i: (i, 0))
    tok_narrow = pl.BlockSpec((tm, B_KVW), lambda i: (i, 0))
    feat_wide = pl.BlockSpec((spt, A_W, tw), lambda i: (i // tps, 0, i % tps))
    feat_narrow = pl.BlockSpec((spt, B_KVW, tw), lambda i: (i // tps, 0, i % tps))
    out_specs = [feat_wide, feat_wide, tok_wide, tok_narrow, feat_wide, feat_narrow]
    out_shape = [jax.ShapeDtypeStruct((b, A_W, t), BF16),
                 jax.ShapeDtypeStruct((b, B_Q, t), BF16),
                 jax.ShapeDtypeStruct((rows, A_W), BF16),
                 jax.ShapeDtypeStruct((rows, B_KVW), BF16),
                 jax.ShapeDtypeStruct((b, A_W, t), BF16),
                 jax.ShapeDtypeStruct((b, B_KVW, t), F32 if emit_new_kv else BF16)]
    if emit_new_kv:
        out_specs += [feat_wide, tok_wide, feat_narrow]
        out_shape += [jax.ShapeDtypeStruct((b, A_W, t), F32),
                      jax.ShapeDtypeStruct((rows, A_W), F32),
                      jax.ShapeDtypeStruct((b, B_KVW, t), F32)]
    return pl.pallas_call(
        functools.partial(_qkv_kernel, rope=rope, emit_new_kv=emit_new_kv),
        grid=(rows // tm,),
        in_specs=in_specs,
        out_specs=out_specs,
        out_shape=out_shape,
        compiler_params=_params(1),
        name="qkv_rope" if rope else "qkv_ctx",
    )(*args)


def _lambda(lam_ref):
    lv = lam_ref[...]
    t1 = jnp.sum(lv[0:1] * lv[1:2], axis=-1, keepdims=True)
    t2 = jnp.sum(lv[2:3] * lv[3:4], axis=-1, keepdims=True)
    return jnp.exp(t1) - jnp.exp(t2) + LAMBDA_INIT


def _with_ones(vt):
    return jnp.concatenate([vt, jnp.ones((ONES_ROWS, vt.shape[1]), vt.dtype)], axis=0)


def _column_softmax(scores, extra=None):
    m = functools.reduce(jnp.maximum, [jnp.max(s, axis=0, keepdims=True) for s in scores])
    if extra is not None:
        m = jnp.maximum(m, extra)
    return [jnp.exp2(s - m).astype(BF16) for s in scores], m


def _max_row_sq_norm(x):
    x = x.astype(F32)
    return jnp.max(jnp.sum(x * x, axis=1, keepdims=True), axis=0, keepdims=True)[0, 0]


def _scores_bounded(q_sq, k_sq, sink_abs=0.0):
    limit = SHIFT_FREE_LOG2
    return jnp.logical_and(q_sq * k_sq <= limit * limit, sink_abs <= limit)


def _diff_query_cols(qt):
    zero = jnp.zeros((HEAD_DIM, qt.shape[1]), qt.dtype)
    return jnp.concatenate([jnp.concatenate([qt[:HEAD_DIM], zero], axis=0),
                            jnp.concatenate([zero, qt[HEAD_DIM:]], axis=0)], axis=1)


def _diff_output(r, lam, subg_col):
    n = r.shape[1] // 2
    dv = 2 * HEAD_DIM
    tot = r[dv:dv + 1, :]
    o = r[:dv, :n] * (1.0 / tot[:, :n]) - r[:dv, n:] * (lam / tot[:, n:])
    o = o * lax.rsqrt(jnp.mean(o * o, axis=0, keepdims=True) + EPS)
    return (o * subg_col * (1.0 - LAMBDA_INIT)).T


def _group_output(r, m, sink_row):
    n = r.shape[1] // B_GROUP
    tot = r[HEAD_DIM:HEAD_DIM + 1, :] + jnp.exp2(sink_row - m)
    o = r[:HEAD_DIM, :] * (1.0 / tot)
    return jnp.concatenate([o[:, i * n:(i + 1) * n] for i in range(B_GROUP)], axis=0).T


def _sink_row(sink_ref, g, n):
    reps = n // LANES
    return jnp.concatenate([sink_ref[pl.ds(g * B_GROUP + r, 1), :] for r in range(B_GROUP)
                            for _ in range(reps)], axis=1)


def _ctx_attn_kernel(qat_ref, qbt_ref, ka_ref, kb_ref, vat_ref, vbt_ref, lam_ref, subg_ref, sink_ref,
                     bounds_ref, oa_ref, ob_ref):
    lam = _lambda(lam_ref)
    subg_col = subg_ref[...]
    n_seq, _, n = qat_ref.shape
    chains = A_HEADS + B_KV_HEADS

    def scores(item, _):
        s, c = divmod(item, chains)
        rows = slice(s * n, (s + 1) * n)
        if c < A_HEADS:
            sl = slice(c * LANES, (c + 1) * LANES)
            return _dot(ka_ref[rows, sl], _diff_query_cols(qat_ref[s, sl, :]))
        g = c - A_HEADS
        qs = jnp.concatenate([qbt_ref[s, (g * B_GROUP + r) * HEAD_DIM:(g * B_GROUP + r + 1) * HEAD_DIM, :]
                              for r in range(B_GROUP)], axis=1)
        zero = jnp.zeros_like(qs)
        w = jnp.concatenate([qs if i == g else zero for i in range(B_KV_HEADS)], axis=0)
        return _dot(kb_ref[rows, :], w)

    def probs(shift, item, sc):
        c = item % chains
        if not shift:
            return [jnp.exp2(sc).astype(BF16)], jnp.zeros((1, sc.shape[1]), F32)
        if c < A_HEADS:
            return _column_softmax([sc])
        return _column_softmax([sc], extra=_sink_row(sink_ref, c - A_HEADS, n))

    def outputs(item, st):
        s, c = divmod(item, chains)
        rows = slice(s * n, (s + 1) * n)
        es, m = st
        if c < A_HEADS:
            sl = slice(c * LANES, (c + 1) * LANES)
            r = _dot(_with_ones(vat_ref[s, sl, :]), es[0])
            oa_ref[rows, sl] = _diff_output(r, lam, subg_col).astype(BF16)
        else:
            g = c - A_HEADS
            vt = vbt_ref[s, g * HEAD_DIM:(g + 1) * HEAD_DIM, :].astype(BF16)
            r = _dot(_with_ones(vt), es[0])
            o = _group_output(r, m, _sink_row(sink_ref, g, n))
            ob_ref[rows, g * CHUNK:(g + 1) * CHUNK] = o.astype(BF16)

    bounded = jnp.logical_and(_scores_bounded(bounds_ref[0], bounds_ref[1]),
                              _scores_bounded(bounds_ref[2], bounds_ref[3], bounds_ref[4]))

    @pl.when(bounded)
    def _():
        _software_pipeline(n_seq * chains, [scores, functools.partial(probs, False), outputs])

    @pl.when(jnp.logical_not(bounded))
    def _():
        _software_pipeline(n_seq * chains, [scores, functools.partial(probs, True), outputs])


def _ctx_front_kernel(x_ref, mod_ref, g1_ref, wk_ref, wt_ref, gsum_ref, kgain_ref, qgain_ref,
                      lam_ref, subg_ref, sink_ref, bounds_ref,
                      oa_ref, ob_ref, vbt_ref, kat_ref, va_ref, kbt_ref,
                      qat_s, qbt_s, ka_s, kb_s, vat_s):
    _qkv_kernel(x_ref, mod_ref, g1_ref, wk_ref, wt_ref, gsum_ref, kgain_ref, qgain_ref,
                qat_s, qbt_s, ka_s, kb_s, vat_s, vbt_ref, kat_ref, va_ref, kbt_ref,
                rope=False, emit_new_kv=True)
    _ctx_attn_kernel(qat_s, qbt_s, ka_s, kb_s, vat_s, vbt_ref, lam_ref, subg_ref, sink_ref, bounds_ref,
                     oa_ref, ob_ref)


def _ctx_front(x3d, mod3, mod_row, g1, w_k, w_t, gsum, kgains, qgains, lam_vecs, subg_col, sink_rows, bounds,
               n_seq):
    b, t, _ = x3d.shape
    rows, tm = b * t, n_seq * t
    tok = pl.BlockSpec((tm, A_W), lambda i: (i, 0))
    feat_wide = pl.BlockSpec((n_seq, A_W, t), lambda i: (i, 0, 0))
    feat_narrow = pl.BlockSpec((n_seq, B_KVW, t), lambda i: (i, 0, 0))
    return pl.pallas_call(
        _ctx_front_kernel,
        grid=(b // n_seq,),
        in_specs=[pl.BlockSpec((tm, D_MODEL), lambda i: (i, 0)),
                  pl.BlockSpec((None, 1, N_MOD * D_MODEL), lambda i: (mod_row(i), 0, 0)),
                  _const_spec((1, D_MODEL)), _const_spec((D_MODEL, N_K)), _const_spec((N_T, D_MODEL)),
                  _const_spec((CHUNK, CHUNK)), _const_spec((2, CHUNK)), _const_spec((2, HEAD_DIM, 1)),
                  _const_spec((4, HEAD_DIM)), _const_spec((2 * HEAD_DIM, 1)),
                  _const_spec((B_HEADS, LANES)), pl.BlockSpec(memory_space=pltpu.SMEM)],
        out_specs=[tok, tok, feat_narrow, feat_wide, tok, feat_narrow],
        out_shape=[jax.ShapeDtypeStruct((rows, A_W), BF16), jax.ShapeDtypeStruct((rows, B_Q), BF16),
                   jax.ShapeDtypeStruct((b, B_KVW, t), F32), jax.ShapeDtypeStruct((b, A_W, t), F32),
                   jax.ShapeDtypeStruct((rows, A_W), F32), jax.ShapeDtypeStruct((b, B_KVW, t), F32)],
        scratch_shapes=[pltpu.VMEM((n_seq, A_W, t), BF16), pltpu.VMEM((n_seq, B_Q, t), BF16),
                        pltpu.VMEM((tm, A_W), BF16), pltpu.VMEM((tm, B_KVW), BF16),
                        pltpu.VMEM((n_seq, A_W, t), BF16)],
        compiler_params=_params(1),
        name="ctx_front",
    )(x3d.reshape(rows, D_MODEL), mod3, g1, w_k, w_t, gsum, kgains, qgains, lam_vecs, subg_col, sink_rows,
      bounds)


def _rider_specs(weights, n_steps, flat_step):
    specs = []
    for w in weights:
        rows, cols = w.shape
        bands = next(n for n in (n_steps, n_steps // 2, n_steps // 4)
                     if rows % n == 0 and (rows // n) % 16 == 0)
        repeat = n_steps // bands
        spec = pl.BlockSpec((rows // bands, cols), lambda *g, repeat=repeat: (flat_step(*g) // repeat, 0))
        specs.append((spec, jax.ShapeDtypeStruct(w.shape, BF16)))
    return specs


def _rider_pieces(rider_refs, n_items):
    n = len(rider_refs) // 2
    pieces = [(w_ref, o_ref, r) for w_ref, o_ref in zip(rider_refs[:n], rider_refs[n:])
              for r in range(0, w_ref.shape[0], 16)]

    def run(item):
        for w_ref, o_ref, r in pieces[item::n_items]:
            o_ref[r:r + 16, :] = w_ref[r:r + 16, :].astype(BF16)

    return run


def _diff_lat_kernel(qt_ref, k_ref, vt_ref, ckt_ref, cv_ref, lam_ref, subg_ref, bounds_ref, *rest, n_riders):
    o_ref = rest[n_riders]
    n_sub = qt_ref.shape[1] // DIFF_QUERIES
    cast_piece = _rider_pieces(rest[:n_riders] + rest[n_riders + 1:], n_sub)
    lam = _lambda(lam_ref)
    subg_col = subg_ref[...]
    k = k_ref[...]
    ck = ckt_ref[...].T.astype(BF16)
    vt = _with_ones(vt_ref[...])
    past = cv_ref.shape[0] // A_HEADS
    cv = cv_ref[pl.ds(pl.program_id(1), past, stride=A_HEADS), :]
    cvt = _with_ones(cv.T.astype(BF16))

    def scores(i, _):
        cast_piece(i)
        q = _diff_query_cols(qt_ref[:, i * DIFF_QUERIES:(i + 1) * DIFF_QUERIES])
        return [_dot(k, q), _dot(ck, q)]

    def outputs(shift, i, s):
        if shift:
            m = jnp.maximum(jnp.max(s[0], axis=0, keepdims=True), jnp.max(s[1], axis=0, keepdims=True))
        r = None
        for sc, v in ((s[0], vt), (s[1], cvt)):
            for c in range(0, sc.shape[0], KEY_CHUNK):
                piece = sc[c:c + KEY_CHUNK]
                part = _dot(v[:, c:c + KEY_CHUNK], jnp.exp2(piece - m if shift else piece).astype(BF16))
                r = part if r is None else r + part
        o_ref[i * DIFF_QUERIES:(i + 1) * DIFF_QUERIES, :] = _diff_output(r, lam, subg_col).astype(BF16)

    bounded = _scores_bounded(bounds_ref[0], jnp.maximum(bounds_ref[1], _max_row_sq_norm(ck)))

    @pl.when(bounded)
    def _():
        _software_pipeline(n_sub, [scores, functools.partial(outputs, False)])

    @pl.when(jnp.logical_not(bounded))
    def _():
        _software_pipeline(n_sub, [scores, functools.partial(outputs, True)])


def _diff_lat(qt, k, vt, ckt, cv, lam_vecs, subg_col, bounds, riders):
    b, _, t = qt.shape
    past = ckt.shape[2]
    rider_specs = _rider_specs(riders, b * A_HEADS, lambda i, h: i * A_HEADS + h)
    return pl.pallas_call(
        functools.partial(_diff_lat_kernel, n_riders=len(riders)),
        grid=(b, A_HEADS),
        in_specs=[pl.BlockSpec((None, LANES, t), lambda i, h: (i, h, 0)),
                  pl.BlockSpec((None, t, LANES), lambda i, h: (i, 0, h)),
                  pl.BlockSpec((None, LANES, t), lambda i, h: (i, h, 0)),
                  pl.BlockSpec((None, LANES, past), lambda i, h: (i, h, 0)),
                  pl.BlockSpec((None, past * A_HEADS, LANES), lambda i, h: (i, 0, 0)),
                  _const_spec((4, HEAD_DIM)), _const_spec((2 * HEAD_DIM, 1)),
                  pl.BlockSpec(memory_space=pltpu.SMEM)] + [spec for spec, _ in rider_specs],
        out_specs=[pl.BlockSpec((None, t, LANES), lambda i, h: (i, 0, h))] + [spec for spec, _ in rider_specs],
        out_shape=[jax.ShapeDtypeStruct((b, t, A_W), BF16)] + [shape for _, shape in rider_specs],
        compiler_params=_params(2),
        name="diff_lat",
    )(qt, k, vt, ckt, cv, lam_vecs, subg_col, bounds, *riders)


def _window_start(i, t):
    return min(max((i - 1) * BLOCK, 0), t - 3 * BLOCK)


def _band_offsets(t):
    return sorted({_window_start(i, t) - i * BLOCK for i in range(t // BLOCK)})


def _band_bias(t):
    key = np.arange(3 * BLOCK)[:, None]
    qry = np.arange(B_GROUP * BLOCK)[None, :] % BLOCK
    return jnp.asarray(np.stack([np.where(np.abs(key + off - qry) <= WINDOW, 0.0, NEG)
                                 for off in _band_offsets(t)]), dtype=F32)


def _win_lat_kernel(qt_ref, k_ref, vt_ref, ckt_ref, cvt_ref, sink_ref, bias_ref, bounds_ref, *rest, n_riders):
    o_ref = rest[n_riders]
    g = pl.program_id(1)
    t = k_ref.shape[0]
    span = 3 * BLOCK
    cast_piece = _rider_pieces(rest[:n_riders] + rest[n_riders + 1:], t // BLOCK)
    keep = lax.broadcasted_iota(jnp.int32, (1, CHUNK), 1) // HEAD_DIM == g
    k = jnp.where(keep, k_ref[...].astype(F32), 0.0).astype(BF16)
    ck = jnp.where(keep, ckt_ref[...].T, 0.0).astype(BF16)
    vt = vt_ref[...]
    cvt = _with_ones(cvt_ref[...].astype(BF16))
    sink_row = _sink_row(sink_ref, g, BLOCK)
    offsets = _band_offsets(t)

    def scores(i, _):
        cast_piece(i)
        w = _window_start(i, t)
        qs = jnp.concatenate([qt_ref[r * HEAD_DIM:(r + 1) * HEAD_DIM, i * BLOCK:(i + 1) * BLOCK]
                              for r in range(B_GROUP)], axis=1)
        wq = jnp.concatenate([qs] * B_KV_HEADS, axis=0)
        return [_dot(k[w:w + span], wq) + bias_ref[offsets.index(w - i * BLOCK)], _dot(ck, wq)]

    def probs(shift, i, s):
        if shift:
            return _column_softmax(s, extra=sink_row)
        return [jnp.exp2(piece).astype(BF16) for piece in s], jnp.zeros_like(sink_row)

    def outputs(i, st):
        es, m = st
        w = _window_start(i, t)
        r = _dot(_with_ones(vt[:, w:w + span]), es[0]) + _dot(cvt, es[1])
        o_ref[i * BLOCK:(i + 1) * BLOCK, :] = _group_output(r, m, sink_row).astype(BF16)

    bounded = _scores_bounded(bounds_ref[2], jnp.maximum(bounds_ref[3], _max_row_sq_norm(ck)), bounds_ref[4])

    @pl.when(bounded)
    def _():
        _software_pipeline(t // BLOCK, [scores, functools.partial(probs, False), outputs])

    @pl.when(jnp.logical_not(bounded))
    def _():
        _software_pipeline(t // BLOCK, [scores, functools.partial(probs, True), outputs])


def _win_lat(qt, k, vt, ckt, cvt, sink_rows, bounds, riders):
    b, _, t = qt.shape
    past = ckt.shape[2]
    bias = _band_bias(t)
    rider_specs = _rider_specs(riders, b * B_KV_HEADS, lambda i, g: i * B_KV_HEADS + g)
    return pl.pallas_call(
        functools.partial(_win_lat_kernel, n_riders=len(riders)),
        grid=(b, B_KV_HEADS),
        in_specs=[pl.BlockSpec((None, CHUNK, t), lambda i, g: (i, g, 0)),
                  pl.BlockSpec((None, t, B_KVW), lambda i, g: (i, 0, 0)),
                  pl.BlockSpec((None, HEAD_DIM, t), lambda i, g: (i, g, 0)),
                  pl.BlockSpec((None, B_KVW, past), lambda i, g: (i, 0, 0)),
                  pl.BlockSpec((None, HEAD_DIM, past), lambda i, g: (i, g, 0)),
                  _const_spec((B_HEADS, LANES)), _const_spec(bias.shape),
                  pl.BlockSpec(memory_space=pltpu.SMEM)] + [spec for spec, _ in rider_specs],
        out_specs=[pl.BlockSpec((None, t, CHUNK), lambda i, g: (i, 0, g))] + [spec for spec, _ in rider_specs],
        out_shape=[jax.ShapeDtypeStruct((b, t, B_Q), BF16)] + [shape for _, shape in rider_specs],
        compiler_params=_params(2),
        name="win_lat",
    )(qt, k, vt, ckt, cvt, sink_rows, bias, bounds, *riders)


def _merge_kernel(x1_ref, x2_ref, oa1_ref, oa2_ref, ob1_ref, ob2_ref, mod_ref, g1_ref, win_ref, woa_ref,
                  wob_ref, wout_ref, o1_ref, o2_ref, *, n_first):
    def tile(x_ref, oa_ref, ob_ref, o_ref):
        mod = mod_ref[...]
        rows = lambda i: slice(i * ROW_SUBTILE, (i + 1) * ROW_SUBTILE)

        def norm(i, _):
            return _modulated_norm(x_ref[rows(i), :], g1_ref[...], mod[:, D_MODEL:2 * D_MODEL],
                                   mod[:, 0:D_MODEL]).astype(BF16)

        def branches(i, h):
            ga = _dot(h, win_ref[:, GATE_COL:GATE_COL + D_MODEL])
            gb = _dot(h, win_ref[:, GATE_COL + D_MODEL:GATE_COL + 2 * D_MODEL])
            return (jax.nn.sigmoid(ga) * _dot(oa_ref[rows(i), :], woa_ref[...])
                    + jax.nn.sigmoid(gb) * _dot(ob_ref[rows(i), :], wob_ref[...])).astype(BF16)

        def project(i, merged):
            o_ref[rows(i), :] = x_ref[rows(i), :] + mod[:, 2 * D_MODEL:3 * D_MODEL] * _dot(merged, wout_ref[...])

        _software_pipeline(x_ref.shape[0] // ROW_SUBTILE, [norm, branches, project])

    _per_token_set(n_first, functools.partial(tile, x1_ref, oa1_ref, ob1_ref, o1_ref),
                   functools.partial(tile, x2_ref, oa2_ref, ob2_ref, o2_ref))


def _per_token_set(n_first, first_body, second_body):
    first = pl.program_id(0) < n_first
    pl.when(first)(first_body)
    pl.when(jnp.logical_not(first))(second_body)


def _split_tiles(n_first, tm, width):
    return (pl.BlockSpec((tm, width), lambda i: (jnp.minimum(i, n_first - 1), 0)),
            pl.BlockSpec((tm, width), lambda i: (jnp.maximum(i - n_first, 0), 0)))


def _merge(x1, x2, oa1, oa2, ob1, ob2, mod3, mod_row, g1, w_in, w_oa, w_ob, w_out, tm):
    n_first, n_second = x1.shape[0] // tm, x2.shape[0] // tm
    t1, t2 = _split_tiles(n_first, tm, D_MODEL)
    sq = _const_spec((D_MODEL, D_MODEL))
    return pl.pallas_call(
        functools.partial(_merge_kernel, n_first=n_first),
        grid=(n_first + n_second,),
        in_specs=[t1, t2, t1, t2, t1, t2,
                  pl.BlockSpec((None, 1, N_MOD * D_MODEL), lambda i: (mod_row(i), 0, 0)),
                  _const_spec((1, D_MODEL)), _const_spec(w_in.shape), sq, sq, sq],
        out_specs=[t1, t2],
        out_shape=[jax.ShapeDtypeStruct(x1.shape, F32), jax.ShapeDtypeStruct(x2.shape, F32)],
        compiler_params=_params(1),
        name="merge_out",
    )(x1, x2, oa1, oa2, ob1, ob2, mod3, g1, w_in, w_oa, w_ob, w_out)


def _ffn_kernel(x1_ref, x2_ref, mod_ref, g2_ref, wgate_ref, wup_ref, wdown_ref, o1_ref, o2_ref, *, n_first):
    def tile(x_ref, o_ref):
        mod = mod_ref[...]
        rows = lambda i: slice(i * FFN_SUBTILE, (i + 1) * FFN_SUBTILE)

        def norm(i, _):
            return _modulated_norm(x_ref[rows(i), :], g2_ref[...], mod[:, 4 * D_MODEL:5 * D_MODEL],
                                   mod[:, 3 * D_MODEL:4 * D_MODEL]).astype(BF16)

        def hidden(i, h):
            gate = _dot(h, wgate_ref[...])
            return (gate * jax.nn.sigmoid(gate) * _dot(h, wup_ref[...])).astype(BF16)

        def project(i, act):
            o_ref[rows(i), :] = x_ref[rows(i), :] + mod[:, 5 * D_MODEL:6 * D_MODEL] * _dot(act, wdown_ref[...])

        _software_pipeline(x_ref.shape[0] // FFN_SUBTILE, [norm, hidden, project])

    _per_token_set(n_first, functools.partial(tile, x1_ref, o1_ref), functools.partial(tile, x2_ref, o2_ref))


def _ffn(x1, x2, mod3, mod_row, g2, w_gate, w_up, w_down, tm):
    n_first, n_second = x1.shape[0] // tm, x2.shape[0] // tm
    t1, t2 = _split_tiles(n_first, tm, D_MODEL)
    return pl.pallas_call(
        functools.partial(_ffn_kernel, n_first=n_first),
        grid=(n_first + n_second,),
        in_specs=[t1, t2,
                  pl.BlockSpec((None, 1, N_MOD * D_MODEL), lambda i: (mod_row(i), 0, 0)),
                  _const_spec((1, D_MODEL)), _const_spec((D_MODEL, D_FF)),
                  _const_spec((D_MODEL, D_FF)), _const_spec((D_FF, D_MODEL))],
        out_specs=[t1, t2],
        out_shape=[jax.ShapeDtypeStruct(x1.shape, F32), jax.ShapeDtypeStruct(x2.shape, F32)],
        compiler_params=_params(1),
        name="swiglu_ffn",
    )(x1, x2, mod3, g2, w_gate, w_up, w_down)


def _rope_tables(seq_len):
    quarter = HEAD_DIM // 4
    t = np.arange(seq_len)
    row = (t // GRID_W).astype(np.float32)
    col = (t % GRID_W).astype(np.float32)
    freqs = (np.float32(ROPE_BASE) ** (-np.arange(quarter, dtype=np.float32) / np.float32(quarter))
             ).astype(np.float32)
    ang_r = row[:, None] * freqs
    ang_c = col[:, None] * freqs
    zeros = np.zeros_like(ang_r)
    cos = np.concatenate([np.cos(ang_r)] * 2 + [np.cos(ang_c)] * 2, axis=-1)
    sin_next = np.concatenate([-np.sin(ang_r), zeros, -np.sin(ang_c), zeros], axis=-1)
    sin_prev = np.concatenate([zeros, np.sin(ang_r), zeros, np.sin(ang_c)], axis=-1)
    rep = LANES // HEAD_DIM
    nat = tuple(jnp.asarray(np.tile(a, (1, rep)), dtype=F32) for a in (cos, sin_next, sin_prev))
    featm = (jnp.asarray(cos.T, dtype=F32), jnp.asarray((sin_next + sin_prev).T, dtype=F32))
    return nat, featm


def _group_sum_matrix():
    idx = np.arange(CHUNK) // HEAD_DIM
    return jnp.asarray(idx[:, None] == idx[None, :], dtype=BF16)


def kernel(x_prompt, x_sample, cache_diff_k, cache_diff_v, cache_win_k, cache_win_v, c, c_ctx,
           w_ada, b_ada, norm1_g, w_in, qn_a, kn_a, lambda_q1, lambda_k1, lambda_q2, lambda_k2,
           subln_g, qn_b, kn_b, sink, w_oa, w_ob, w_out, norm2_g, w_gate, w_up, w_down):
    batch, seq, _ = x_prompt.shape
    dec_batch, dec_seq, _ = x_sample.shape
    past = cache_diff_k.shape[2]
    l = 0

    cc = jnp.concatenate([c, c_ctx[None, :], jnp.zeros((8 - dec_batch - 1, D_MODEL), F32)], axis=0)
    mod3 = _modulation(cc, w_ada[l], b_ada[l][None, :])

    w_in_l = w_in[l]
    w_t, w_k = _weight_prep(w_in_l)
    g1 = norm1_g[l][None, :]
    g2 = norm2_g[l][None, :]
    rep = CHUNK // HEAD_DIM
    kgains = jnp.stack([jnp.tile(kn_a[l], rep), jnp.tile(kn_b[l], rep)])
    qgains = jnp.stack([qn_a[l], qn_b[l]])[:, :, None]
    lam_vecs = jnp.stack([lambda_q1[l], lambda_k1[l], lambda_q2[l], lambda_k2[l]])
    subg_col = subln_g[l][:, None]
    sink_rows = jnp.broadcast_to((sink[l] * LOG2E)[:, None], (B_HEADS, LANES))
    head_gains = jnp.stack([qn_a[l], kn_a[l], qn_b[l], kn_b[l]])
    head_scale = jnp.asarray([Q_SCALE, 1.0, Q_SCALE, 1.0], F32)
    bounds = jnp.concatenate([1.02 * HEAD_DIM * jnp.max(jnp.square(head_gains), axis=1) * jnp.square(head_scale),
                              jnp.max(jnp.abs(sink[l]), keepdims=True) * LOG2E]).astype(F32)
    gsum = _group_sum_matrix()
    tm = 512

    xp = x_prompt.reshape(batch * seq, D_MODEL)
    ctx_row = lambda i: dec_batch
    oa_p, ob_p, vbt, kat_f, va_f, kbt_f = _ctx_front(
        x_prompt, mod3, ctx_row, g1, w_k, w_t, gsum, kgains, qgains, lam_vecs, subg_col, sink_rows, bounds,
        tm // seq)
    new_diff_k = kat_f.reshape(batch, A_HEADS, 2, HEAD_DIM, seq).transpose(0, 4, 1, 2, 3)[:, None]
    new_diff_v = va_f.reshape(batch, 1, seq, A_HEADS, 2 * HEAD_DIM)
    new_win_k = kbt_f.reshape(batch, B_KV_HEADS, HEAD_DIM, seq).transpose(0, 3, 1, 2)[:, None]
    new_win_v = vbt.reshape(batch, B_KV_HEADS, HEAD_DIM, seq).transpose(0, 3, 1, 2)[:, None]

    xs = x_sample.reshape(dec_batch * dec_seq, D_MODEL)
    lat_row = lambda i: i // (dec_seq // tm)
    qat, qbt, ka, kb, vat, vbt = _qkv_proj(
        x_sample, mod3, lat_row, g1, w_k, w_t, gsum, kgains, qgains, _rope_tables(dec_seq), tm, False)
    ckt_a = cache_diff_k[:, l].transpose(0, 2, 3, 4, 1).reshape(dec_batch, A_W, past)
    cv_a = cache_diff_v[:, l].reshape(dec_batch, past * A_HEADS, 2 * HEAD_DIM)
    ckt_b = cache_win_k[:, l].transpose(0, 2, 3, 1).reshape(dec_batch, B_KVW, past)
    cvt_b = cache_win_v[:, l].transpose(0, 2, 3, 1).reshape(dec_batch, B_KVW, past)
    oa_s, w_gate_b, w_up_b, w_down_b, w_in_b = _diff_lat(
        qat, ka.reshape(dec_batch, dec_seq, A_W), vat, ckt_a, cv_a, lam_vecs, subg_col, bounds,
        [w_gate[l], w_up[l], w_down[l], w_in_l])
    ob_s, w_oa_b, w_ob_b, w_out_b = _win_lat(
        qbt, kb.reshape(dec_batch, dec_seq, B_KVW), vbt, ckt_b, cvt_b, sink_rows, bounds,
        [w_oa[l], w_ob[l], w_out[l]])

    n_ctx_tiles = batch * seq // tm
    both_row = lambda i: jnp.where(i < n_ctx_tiles, dec_batch, (i - n_ctx_tiles) // (dec_seq // tm))
    xp1, xs1 = _merge(xp, xs, oa_p, oa_s.reshape(-1, A_W), ob_p, ob_s.reshape(-1, B_Q), mod3, both_row, g1,
                      w_in_b, w_oa_b, w_ob_b, w_out_b, tm)
    y_prompt, y_sample = _ffn(xp1, xs1, mod3, both_row, g2, w_gate_b, w_up_b, w_down_b, tm)

    return (y_prompt.reshape(batch, seq, D_MODEL),
            y_sample.reshape(dec_batch, dec_seq, D_MODEL),
            new_diff_k, new_diff_v, new_win_k, new_win_v)
```

```python
import functools
import math

import jax
import jax.numpy as jnp
import numpy as np
from jax import lax
from jax.experimental import pallas as pl
from jax.experimental.pallas import tpu as pltpu

D_MODEL = 1024
HEAD_DIM = 64
A_HEADS = 8
B_HEADS = 16
B_KV_HEADS = 4
B_GROUP = B_HEADS // B_KV_HEADS
A_W = A_HEADS * 2 * HEAD_DIM
B_Q = B_HEADS * HEAD_DIM
B_KVW = B_KV_HEADS * HEAD_DIM
N_K = A_W + B_KVW
N_T = A_W + B_Q + A_W + B_KVW
GATE_COL = 3 * A_W + B_Q + 2 * B_KVW
WINDOW = 128
BLOCK = 128
GRID_W = 64
D_FF = 2816
N_MOD = 6
EPS = 1e-6
ROPE_BASE = 10000.0
NEG = -1e30
LAMBDA_INIT = 0.8 - 0.6 * math.exp(-0.3 * 0)
LOG2E = math.log2(math.e)
Q_SCALE = HEAD_DIM ** -0.5 * LOG2E

LANES = 128
CHUNK = 256
ONES_ROWS = 16
KEY_CHUNK = 512
QKV_SUBTILE = 512
ROW_SUBTILE = 256
FFN_SUBTILE = 128
DIFF_QUERIES = 128
SHIFT_FREE_LOG2 = 60.0
VMEM_LIMIT = 56 * 1024 * 1024

F32 = jnp.float32
BF16 = jnp.bfloat16


def _params(n_axes, flags=None):
    return pltpu.CompilerParams(dimension_semantics=("arbitrary",) * n_axes,
                                vmem_limit_bytes=VMEM_LIMIT, flags=flags)


def _const_spec(shape):
    nd = len(shape)
    return pl.BlockSpec(shape, lambda *_: (0,) * nd, pipeline_mode=pl.Buffered(1))


def _dot(a, b):
    return jnp.dot(a, b, preferred_element_type=F32)


def _modulated_norm(x, gain, scale, shift):
    y = x * lax.rsqrt(jnp.mean(x * x, axis=-1, keepdims=True) + EPS) * gain
    return y * (1.0 + scale) + shift


def _software_pipeline(n, stages):
    depth = len(stages)
    state = [None] * n
    for step in range(n + depth - 1):
        for k, stage in enumerate(stages):
            i = step - k
            if 0 <= i < n:
                state[i] = stage(i, state[i])


def _mod_kernel(c_ref, w_ref, b_ref, o_ref):
    cc = c_ref[...]
    s = cc * jax.nn.sigmoid(cc)
    mod = _dot(s.astype(BF16), w_ref[...].astype(BF16)) + b_ref[...]
    for r in range(mod.shape[0]):
        o_ref[r] = mod[r:r + 1, :]


def _modulation(cc, w_ada, b_ada):
    rows, n = cc.shape[0], w_ada.shape[1]
    tn = n // 4
    return pl.pallas_call(
        _mod_kernel,
        grid=(n // tn,),
        in_specs=[pl.BlockSpec((rows, D_MODEL), lambda j: (0, 0)),
                  pl.BlockSpec((D_MODEL, tn), lambda j: (0, j)),
                  pl.BlockSpec((1, tn), lambda j: (0, j))],
        out_specs=pl.BlockSpec((rows, 1, tn), lambda j: (0, 0, j)),
        out_shape=jax.ShapeDtypeStruct((rows, 1, n), F32),
        compiler_params=_params(1),
        name="adaln_mod",
    )(cc, w_ada, b_ada)


PREP_BLOCK = CHUNK
_W_T_BLOCKS = (0, 1, 2, 3, 12, 13, 14, 15, 8, 9, 10, 11, 17)
_W_K_BLOCKS = (4, 5, 6, 7, 16)


def _wprep_kernel(src_ref, w_ref, wt_ref, wk_ref):
    del src_ref
    transposed = pl.program_id(0) < len(_W_T_BLOCKS)

    @pl.when(transposed)
    def _():
        wt_ref[...] = w_ref[...].T.astype(BF16)

    @pl.when(jnp.logical_not(transposed))
    def _():
        wk_ref[...] = w_ref[...].astype(BF16)


def _weight_prep(w_in_l):
    n_t, n_k = len(_W_T_BLOCKS), len(_W_K_BLOCKS)
    order = jnp.asarray(_W_T_BLOCKS + _W_K_BLOCKS, jnp.int32)
    return pl.pallas_call(
        _wprep_kernel,
        grid_spec=pltpu.PrefetchScalarGridSpec(
            num_scalar_prefetch=1,
            grid=(n_t + n_k,),
            in_specs=[pl.BlockSpec((D_MODEL, PREP_BLOCK), lambda j, src: (0, src[j]))],
            out_specs=[pl.BlockSpec((PREP_BLOCK, D_MODEL), lambda j, src: (jnp.minimum(j, n_t - 1), 0)),
                       pl.BlockSpec((D_MODEL, PREP_BLOCK), lambda j, src: (0, jnp.maximum(j - n_t, 0)))]),
        out_shape=[jax.ShapeDtypeStruct((N_T, D_MODEL), BF16), jax.ShapeDtypeStruct((D_MODEL, N_K), BF16)],
        compiler_params=_params(1),
        name="qkv_weight_prep",
    )(order, w_in_l)


def _qkv_kernel(*refs, rope, emit_new_kv):
    x_ref, mod_ref, g1_ref, wk_ref, wt_ref, gsum_ref, kgain_ref, qgain_ref = refs[:8]
    refs = refs[8:]
    if rope:
        cos_ref, sa_ref, sb_ref, cost_ref, sint_ref = refs[:5]
        refs = refs[5:]
    qat_ref, qbt_ref, ka_ref, kb_ref, vat_ref, vbt_ref = refs[:6]
    if emit_new_kv:
        kat_ref, va_ref, kbt_ref = refs[6:]

    mod = mod_ref[...]
    gsum = gsum_ref[...]
    n_wide = A_W // CHUNK
    key_cols = [j * CHUNK for j in range(n_wide)] + [A_W]
    n_feat = N_T // CHUNK
    sub = min(QKV_SUBTILE, x_ref.shape[0])

    def normed(u):
        h = _modulated_norm(x_ref[u * sub:(u + 1) * sub, :], g1_ref[...], mod[:, D_MODEL:2 * D_MODEL],
                            mod[:, 0:D_MODEL])
        return h.astype(BF16), h.T.astype(BF16)

    def project(u, hb, ht):
        tok = slice(u * sub, (u + 1) * sub)

        def put_feat(ref, rows, val):
            width = ref.shape[2]
            step = min(sub, width)
            for off in range(0, sub, step):
                s, col = divmod(u * sub + off, width)
                ref[s, rows, col:col + step] = val[:, off:off + step].astype(ref.dtype)

        def key_project(j, _):
            return _dot(hb, wk_ref[:, key_cols[j]:key_cols[j] + CHUNK])

        def key_squares(j, p):
            return p, _dot((p * p).astype(BF16), gsum)

        def key_finish(j, st):
            p, ss = st
            row = 0 if j < n_wide else 1
            p = p * lax.rsqrt(ss * (1.0 / HEAD_DIM) + EPS) * kgain_ref[row:row + 1, :]
            if rope:
                cos, sa, sb = cos_ref[tok, :], sa_ref[tok, :], sb_ref[tok, :]
                halves = []
                for i in range(CHUNK // LANES):
                    xh = p[:, i * LANES:(i + 1) * LANES]
                    nxt = pltpu.roll(xh, LANES - HEAD_DIM // 4, 1)
                    prv = pltpu.roll(xh, HEAD_DIM // 4, 1)
                    halves.append(xh * cos + nxt * sa + prv * sb)
                p = jnp.concatenate(halves, axis=1)
            if j < n_wide:
                sl = slice(j * CHUNK, (j + 1) * CHUNK)
                ka_ref[tok, sl] = p.astype(BF16)
                if emit_new_kv:
                    put_feat(kat_ref, sl, p.T)
            else:
                kb_ref[tok, :] = p.astype(BF16)
                if emit_new_kv:
                    put_feat(kbt_ref, slice(None), p.T)

        _software_pipeline(len(key_cols), [key_project, key_squares, key_finish])

        def feat_project(c, _):
            return _dot(wt_ref[c * CHUNK:(c + 1) * CHUNK, :], ht)

        def feat_finish(c, p):
            kind, j = divmod(c, n_wide)
            sl = slice(j * CHUNK, (j + 1) * CHUNK)
            if kind == 2:
                put_feat(vat_ref, sl, p)
                if emit_new_kv:
                    va_ref[tok, sl] = p.T
                return
            if kind == 3:
                put_feat(vbt_ref, slice(None), p)
                return
            gain = qgain_ref[kind]
            q4 = HEAD_DIM // 4
            heads = []
            for r in range(CHUNK // HEAD_DIM):
                x = p[r * HEAD_DIM:(r + 1) * HEAD_DIM, :]
                x = x * lax.rsqrt(jnp.mean(x * x, axis=0, keepdims=True) + EPS) * gain
                if rope:
                    swapped = jnp.concatenate([x[q4:2 * q4], x[0:q4], x[3 * q4:4 * q4], x[2 * q4:3 * q4]],
                                              axis=0)
                    x = x * cost_ref[:, tok] + swapped * sint_ref[:, tok]
                heads.append(x)
            put_feat(qat_ref if kind == 0 else qbt_ref, sl, jnp.concatenate(heads, axis=0) * Q_SCALE)

        _software_pipeline(n_feat, [feat_project, feat_finish])

    n_sub = x_ref.shape[0] // sub
    nxt = normed(0)
    for u in range(n_sub):
        cur, nxt = nxt, (normed(u + 1) if u + 1 < n_sub else None)
        project(u, *cur)


def _qkv_proj(x3d, mod3, mod_row, g1, w_k, w_t, gsum, kgains, qgains, rope_tabs, tm, emit_new_kv):
    b, t, _ = x3d.shape
    rows = b * t
    rope = rope_tabs is not None
    tps = max(t // tm, 1)
    spt = max(tm // t, 1)
    tw = tm // spt
    in_specs = [pl.BlockSpec((tm, D_MODEL), lambda i: (i, 0)),
                pl.BlockSpec((None, 1, N_MOD * D_MODEL), lambda i: (mod_row(i), 0, 0)),
                _const_spec((1, D_MODEL)),
                _const_spec((D_MODEL, N_K)),
                _const_spec((N_T, D_MODEL)),
                _const_spec((CHUNK, CHUNK)),
                _const_spec((2, CHUNK)),
                _const_spec((2, HEAD_DIM, 1))]
    args = [x3d.reshape(rows, D_MODEL), mod3, g1, w_k, w_t, gsum, kgains, qgains]
    if rope:
        nat, featm = rope_tabs
        in_specs += [pl.BlockSpec((tm, LANES), lambda i: (i % tps, 0))] * 3
        in_specs += [pl.BlockSpec((HEAD_DIM, tm), lambda i: (0, i % tps))] * 2
        args += list(nat) + list(featm)
    tok_wide = pl.BlockSpec((tm, A_W), lambda i: (i, 0))
    tok_narrow = pl.BlockSpec((tm, B_KVW), lambda i: (i, 0))
    feat_wide = pl.BlockSpec((spt, A_W, tw), lambda i: (i // tps, 0, i % tps))
    feat_narrow = pl.BlockSpec((spt, B_KVW, tw), lambda i: (i // tps, 0, i % tps))
    out_specs = [feat_wide, feat_wide, tok_wide, tok_narrow, feat_wide, feat_narrow]
    out_shape = [jax.ShapeDtypeStruct((b, A_W, t), BF16),
                 jax.ShapeDtypeStruct((b, B_Q, t), BF16),
                 jax.ShapeDtypeStruct((rows, A_W), BF16),
                 jax.ShapeDtypeStruct((rows, B_KVW), BF16),
                 jax.ShapeDtypeStruct((b, A_W, t), BF16),
                 jax.ShapeDtypeStruct((b, B_KVW, t), F32 if emit_new_kv else BF16)]
    if emit_new_kv:
        out_specs += [feat_wide, tok_wide, feat_narrow]
        out_shape += [jax.ShapeDtypeStruct((b, A_W, t), F32),
                      jax.ShapeDtypeStruct((rows, A_W), F32),
                      jax.ShapeDtypeStruct((b, B_KVW, t), F32)]
    return pl.pallas_call(
        functools.partial(_qkv_kernel, rope=rope, emit_new_kv=emit_new_kv),
        grid=(rows // tm,),
        in_specs=in_specs,
        out_specs=out_specs,
        out_shape=out_shape,
        compiler_params=_params(1),
        name="qkv_rope" if rope else "qkv_ctx",
    )(*args)


def _lambda(lam_ref):
    lv = lam_ref[...]
    t1 = jnp.sum(lv[0:1] * lv[1:2], axis=-1, keepdims=True)
    t2 = jnp.sum(lv[2:3] * lv[3:4], axis=-1, keepdims=True)
    return jnp.exp(t1) - jnp.exp(t2) + LAMBDA_INIT


def _with_ones(vt):
    return jnp.concatenate([vt, jnp.ones((ONES_ROWS, vt.shape[1]), vt.dtype)], axis=0)


def _column_softmax(scores, extra=None):
    m = functools.reduce(jnp.maximum, [jnp.max(s, axis=0, keepdims=True) for s in scores])
    if extra is not None:
        m = jnp.maximum(m, extra)
    return [jnp.exp2(s - m).astype(BF16) for s in scores], m


def _max_row_sq_norm(x):
    x = x.astype(F32)
    return jnp.max(jnp.sum(x * x, axis=1, keepdims=True), axis=0, keepdims=True)[0, 0]


def _scores_bounded(q_sq, k_sq, sink_abs=0.0):
    limit = SHIFT_FREE_LOG2
    return jnp.logical_and(q_sq * k_sq <= limit * limit, sink_abs <= limit)


def _diff_query_cols(qt):
    zero = jnp.zeros((HEAD_DIM, qt.shape[1]), qt.dtype)
    return jnp.concatenate([jnp.concatenate([qt[:HEAD_DIM], zero], axis=0),
                            jnp.concatenate([zero, qt[HEAD_DIM:]], axis=0)], axis=1)


def _diff_output(r, lam, subg_col):
    n = r.shape[1] // 2
    dv = 2 * HEAD_DIM
    tot = r[dv:dv + 1, :]
    o = r[:dv, :n] * (1.0 / tot[:, :n]) - r[:dv, n:] * (lam / tot[:, n:])
    o = o * lax.rsqrt(jnp.mean(o * o, axis=0, keepdims=True) + EPS)
    return (o * subg_col * (1.0 - LAMBDA_INIT)).T


def _group_output(r, m, sink_row):
    n = r.shape[1] // B_GROUP
    tot = r[HEAD_DIM:HEAD_DIM + 1, :] + jnp.exp2(sink_row - m)
    o = r[:HEAD_DIM, :] * (1.0 / tot)
    return jnp.concatenate([o[:, i * n:(i + 1) * n] for i in range(B_GROUP)], axis=0).T


def _sink_row(sink_ref, g, n):
    reps = n // LANES
    return jnp.concatenate([sink_ref[pl.ds(g * B_GROUP + r, 1), :] for r in range(B_GROUP)
                            for _ in range(reps)], axis=1)


def _ctx_attn_kernel(qat_ref, qbt_ref, ka_ref, kb_ref, vat_ref, vbt_ref, lam_ref, subg_ref, sink_ref,
                     bounds_ref, oa_ref, ob_ref):
    lam = _lambda(lam_ref)
    subg_col = subg_ref[...]
    n_seq, _, n = qat_ref.shape
    chains = A_HEADS + B_KV_HEADS

    def scores(item, _):
        s, c = divmod(item, chains)
        rows = slice(s * n, (s + 1) * n)
        if c < A_HEADS:
            sl = slice(c * LANES, (c + 1) * LANES)
            return _dot(ka_ref[rows, sl], _diff_query_cols(qat_ref[s, sl, :]))
        g = c - A_HEADS
        qs = jnp.concatenate([qbt_ref[s, (g * B_GROUP + r) * HEAD_DIM:(g * B_GROUP + r + 1) * HEAD_DIM, :]
                              for r in range(B_GROUP)], axis=1)
        zero = jnp.zeros_like(qs)
        w = jnp.concatenate([qs if i == g else zero for i in range(B_KV_HEADS)], axis=0)
        return _dot(kb_ref[rows, :], w)

    def probs(shift, item, sc):
        c = item % chains
        if not shift:
            return [jnp.exp2(sc).astype(BF16)], jnp.zeros((1, sc.shape[1]), F32)
        if c < A_HEADS:
            return _column_softmax([sc])
        return _column_softmax([sc], extra=_sink_row(sink_ref, c - A_HEADS, n))

    def outputs(item, st):
        s, c = divmod(item, chains)
        rows = slice(s * n, (s + 1) * n)
        es, m = st
        if c < A_HEADS:
            sl = slice(c * LANES, (c + 1) * LANES)
            r = _dot(_with_ones(vat_ref[s, sl, :]), es[0])
            oa_ref[rows, sl] = _diff_output(r, lam, subg_col).astype(BF16)
        else:
            g = c - A_HEADS
            vt = vbt_ref[s, g * HEAD_DIM:(g + 1) * HEAD_DIM, :].astype(BF16)
            r = _dot(_with_ones(vt), es[0])
            o = _group_output(r, m, _sink_row(sink_ref, g, n))
            ob_ref[rows, g * CHUNK:(g + 1) * CHUNK] = o.astype(BF16)

    bounded = jnp.logical_and(_scores_bounded(bounds_ref[0], bounds_ref[1]),
                              _scores_bounded(bounds_ref[2], bounds_ref[3], bounds_ref[4]))

    @pl.when(bounded)
    def _():
        _software_pipeline(n_seq * chains, [scores, functools.partial(probs, False), outputs])

    @pl.when(jnp.logical_not(bounded))
    def _():
        _software_pipeline(n_seq * chains, [scores, functools.partial(probs, True), outputs])


def _ctx_front_kernel(x_ref, mod_ref, g1_ref, wk_ref, wt_ref, gsum_ref, kgain_ref, qgain_ref,
                      lam_ref, subg_ref, sink_ref, bounds_ref,
                      oa_ref, ob_ref, vbt_ref, kat_ref, va_ref, kbt_ref,
                      qat_s, qbt_s, ka_s, kb_s, vat_s):
    _qkv_kernel(x_ref, mod_ref, g1_ref, wk_ref, wt_ref, gsum_ref, kgain_ref, qgain_ref,
                qat_s, qbt_s, ka_s, kb_s, vat_s, vbt_ref, kat_ref, va_ref, kbt_ref,
                rope=False, emit_new_kv=True)
    _ctx_attn_kernel(qat_s, qbt_s, ka_s, kb_s, vat_s, vbt_ref, lam_ref, subg_ref, sink_ref, bounds_ref,
                     oa_ref, ob_ref)


def _ctx_front(x3d, mod3, mod_row, g1, w_k, w_t, gsum, kgains, qgains, lam_vecs, subg_col, sink_rows, bounds,
               n_seq):
    b, t, _ = x3d.shape
    rows, tm = b * t, n_seq * t
    tok = pl.BlockSpec((tm, A_W), lambda i: (i, 0))
    feat_wide = pl.BlockSpec((n_seq, A_W, t), lambda i: (i, 0, 0))
    feat_narrow = pl.BlockSpec((n_seq, B_KVW, t), lambda i: (i, 0, 0))
    return pl.pallas_call(
        _ctx_front_kernel,
        grid=(b // n_seq,),
        in_specs=[pl.BlockSpec((tm, D_MODEL), lambda i: (i, 0)),
                  pl.BlockSpec((None, 1, N_MOD * D_MODEL), lambda i: (mod_row(i), 0, 0)),
                  _const_spec((1, D_MODEL)), _const_spec((D_MODEL, N_K)), _const_spec((N_T, D_MODEL)),
                  _const_spec((CHUNK, CHUNK)), _const_spec((2, CHUNK)), _const_spec((2, HEAD_DIM, 1)),
                  _const_spec((4, HEAD_DIM)), _const_spec((2 * HEAD_DIM, 1)),
                  _const_spec((B_HEADS, LANES)), pl.BlockSpec(memory_space=pltpu.SMEM)],
        out_specs=[tok, tok, feat_narrow, feat_wide, tok, feat_narrow],
        out_shape=[jax.ShapeDtypeStruct((rows, A_W), BF16), jax.ShapeDtypeStruct((rows, B_Q), BF16),
                   jax.ShapeDtypeStruct((b, B_KVW, t), F32), jax.ShapeDtypeStruct((b, A_W, t), F32),
                   jax.ShapeDtypeStruct((rows, A_W), F32), jax.ShapeDtypeStruct((b, B_KVW, t), F32)],
        scratch_shapes=[pltpu.VMEM((n_seq, A_W, t), BF16), pltpu.VMEM((n_seq, B_Q, t), BF16),
                        pltpu.VMEM((tm, A_W), BF16), pltpu.VMEM((tm, B_KVW), BF16),
                        pltpu.VMEM((n_seq, A_W, t), BF16)],
        compiler_params=_params(1),
        name="ctx_front",
    )(x3d.reshape(rows, D_MODEL), mod3, g1, w_k, w_t, gsum, kgains, qgains, lam_vecs, subg_col, sink_rows,
      bounds)


def _rider_specs(weights, n_steps, flat_step):
    specs = []
    for w in weights:
        rows, cols = w.shape
        bands = next(n for n in (n_steps, n_steps // 2, n_steps // 4)
                     if rows % n == 0 and (rows // n) % 16 == 0)
        repeat = n_steps // bands
        spec = pl.BlockSpec((rows // bands, cols), lambda *g, repeat=repeat: (flat_step(*g) // repeat, 0))
        specs.append((spec, jax.ShapeDtypeStruct(w.shape, BF16)))
    return specs


def _rider_pieces(rider_refs, n_items):
    n = len(rider_refs) // 2
    pieces = [(w_ref, o_ref, r) for w_ref, o_ref in zip(rider_refs[:n], rider_refs[n:])
              for r in range(0, w_ref.shape[0], 16)]

    def run(item):
        for w_ref, o_ref, r in pieces[item::n_items]:
            o_ref[r:r + 16, :] = w_ref[r:r + 16, :].astype(BF16)

    return run


def _diff_lat_kernel(qt_ref, k_ref, vt_ref, ckt_ref, cv_ref, lam_ref, subg_ref, bounds_ref, *rest, n_riders):
    o_ref = rest[n_riders]
    n_sub = qt_ref.shape[1] // DIFF_QUERIES
    cast_piece = _rider_pieces(rest[:n_riders] + rest[n_riders + 1:], n_sub)
    lam = _lambda(lam_ref)
    subg_col = subg_ref[...]
    k = k_ref[...]
    ck = ckt_ref[...].T.astype(BF16)
    vt = _with_ones(vt_ref[...])
    past = cv_ref.shape[0] // A_HEADS
    cv = cv_ref[pl.ds(pl.program_id(1), past, stride=A_HEADS), :]
    cvt = _with_ones(cv.T.astype(BF16))

    def scores(i, _):
        cast_piece(i)
        q = _diff_query_cols(qt_ref[:, i * DIFF_QUERIES:(i + 1) * DIFF_QUERIES])
        return [_dot(k, q), _dot(ck, q)]

    def outputs(shift, i, s):
        if shift:
            m = jnp.maximum(jnp.max(s[0], axis=0, keepdims=True), jnp.max(s[1], axis=0, keepdims=True))
        r = None
        for sc, v in ((s[0], vt), (s[1], cvt)):
            for c in range(0, sc.shape[0], KEY_CHUNK):
                piece = sc[c:c + KEY_CHUNK]
                part = _dot(v[:, c:c + KEY_CHUNK], jnp.exp2(piece - m if shift else piece).astype(BF16))
                r = part if r is None else r + part
        o_ref[i * DIFF_QUERIES:(i + 1) * DIFF_QUERIES, :] = _diff_output(r, lam, subg_col).astype(BF16)

    bounded = _scores_bounded(bounds_ref[0], jnp.maximum(bounds_ref[1], _max_row_sq_norm(ck)))

    @pl.when(bounded)
    def _():
        _software_pipeline(n_sub, [scores, functools.partial(outputs, False)])

    @pl.when(jnp.logical_not(bounded))
    def _():
        _software_pipeline(n_sub, [scores, functools.partial(outputs, True)])


def _diff_lat(qt, k, vt, ckt, cv, lam_vecs, subg_col, bounds, riders):
    b, _, t = qt.shape
    past = ckt.shape[2]
    rider_specs = _rider_specs(riders, b * A_HEADS, lambda i, h: i * A_HEADS + h)
    return pl.pallas_call(
        functools.partial(_diff_lat_kernel, n_riders=len(riders)),
        grid=(b, A_HEADS),
        in_specs=[pl.BlockSpec((None, LANES, t), lambda i, h: (i, h, 0)),
                  pl.BlockSpec((None, t, LANES), lambda i, h: (i, 0, h)),
                  pl.BlockSpec((None, LANES, t), lambda i, h: (i, h, 0)),
                  pl.BlockSpec((None, LANES, past), lambda i, h: (i, h, 0)),
                  pl.BlockSpec((None, past * A_HEADS, LANES), lambda i, h: (i, 0, 0)),
                  _const_spec((4, HEAD_DIM)), _const_spec((2 * HEAD_DIM, 1)),
                  pl.BlockSpec(memory_space=pltpu.SMEM)] + [spec for spec, _ in rider_specs],
        out_specs=[pl.BlockSpec((None, t, LANES), lambda i, h: (i, 0, h))] + [spec for spec, _ in rider_specs],
        out_shape=[jax.ShapeDtypeStruct((b, t, A_W), BF16)] + [shape for _, shape in rider_specs],
        compiler_params=_params(2),
        name="diff_lat",
    )(qt, k, vt, ckt, cv, lam_vecs, subg_col, bounds, *riders)


def _window_start(i, t):
    return min(max((i - 1) * BLOCK, 0), t - 3 * BLOCK)


def _band_offsets(t):
    return sorted({_window_start(i, t) - i * BLOCK for i in range(t // BLOCK)})


def _band_bias(t):
    key = np.arange(3 * BLOCK)[:, None]
    qry = np.arange(B_GROUP * BLOCK)[None, :] % BLOCK
    return jnp.asarray(np.stack([np.where(np.abs(key + off - qry) <= WINDOW, 0.0, NEG)
                                 for off in _band_offsets(t)]), dtype=F32)


def _win_lat_kernel(qt_ref, k_ref, vt_ref, ckt_ref, cvt_ref, sink_ref, bias_ref, bounds_ref, *rest, n_riders):
    o_ref = rest[n_riders]
    g = pl.program_id(1)
    t = k_ref.shape[0]
    span = 3 * BLOCK
    cast_piece = _rider_pieces(rest[:n_riders] + rest[n_riders + 1:], t // BLOCK)
    keep = lax.broadcasted_iota(jnp.int32, (1, CHUNK), 1) // HEAD_DIM == g
    k = jnp.where(keep, k_ref[...].astype(F32), 0.0).astype(BF16)
    ck = jnp.where(keep, ckt_ref[...].T, 0.0).astype(BF16)
    vt = vt_ref[...]
    cvt = _with_ones(cvt_ref[...].astype(BF16))
    sink_row = _sink_row(sink_ref, g, BLOCK)
    offsets = _band_offsets(t)

    def scores(i, _):
        cast_piece(i)
        w = _window_start(i, t)
        qs = jnp.concatenate([qt_ref[r * HEAD_DIM:(r + 1) * HEAD_DIM, i * BLOCK:(i + 1) * BLOCK]
                              for r in range(B_GROUP)], axis=1)
        wq = jnp.concatenate([qs] * B_KV_HEADS, axis=0)
        return [_dot(k[w:w + span], wq) + bias_ref[offsets.index(w - i * BLOCK)], _dot(ck, wq)]

    def probs(shift, i, s):
        if shift:
            return _column_softmax(s, extra=sink_row)
        return [jnp.exp2(piece).astype(BF16) for piece in s], jnp.zeros_like(sink_row)

    def outputs(i, st):
        es, m = st
        w = _window_start(i, t)
        r = _dot(_with_ones(vt[:, w:w + span]), es[0]) + _dot(cvt, es[1])
        o_ref[i * BLOCK:(i + 1) * BLOCK, :] = _group_output(r, m, sink_row).astype(BF16)

    bounded = _scores_bounded(bounds_ref[2], jnp.maximum(bounds_ref[3], _max_row_sq_norm(ck)), bounds_ref[4])

    @pl.when(bounded)
    def _():
        _software_pipeline(t // BLOCK, [scores, functools.partial(probs, False), outputs])

    @pl.when(jnp.logical_not(bounded))
    def _():
        _software_pipeline(t // BLOCK, [scores, functools.partial(probs, True), outputs])


def _win_lat(qt, k, vt, ckt, cvt, sink_rows, bounds, riders):
    b, _, t = qt.shape
    past = ckt.shape[2]
    bias = _band_bias(t)
    rider_specs = _rider_specs(riders, b * B_KV_HEADS, lambda i, g: i * B_KV_HEADS + g)
    return pl.pallas_call(
        functools.partial(_win_lat_kernel, n_riders=len(riders)),
        grid=(b, B_KV_HEADS),
        in_specs=[pl.BlockSpec((None, CHUNK, t), lambda i, g: (i, g, 0)),
                  pl.BlockSpec((None, t, B_KVW), lambda i, g: (i, 0, 0)),
                  pl.BlockSpec((None, HEAD_DIM, t), lambda i, g: (i, g, 0)),
                  pl.BlockSpec((None, B_KVW, past), lambda i, g: (i, 0, 0)),
                  pl.BlockSpec((None, HEAD_DIM, past), lambda i, g: (i, g, 0)),
                  _const_spec((B_HEADS, LANES)), _const_spec(bias.shape),
                  pl.BlockSpec(memory_space=pltpu.SMEM)] + [spec for spec, _ in rider_specs],
        out_specs=[pl.BlockSpec((None, t, CHUNK), lambda i, g: (i, 0, g))] + [spec for spec, _ in rider_specs],
        out_shape=[jax.ShapeDtypeStruct((b, t, B_Q), BF16)] + [shape for _, shape in rider_specs],
        compiler_params=_params(2),
        name="win_lat",
    )(qt, k, vt, ckt, cvt, sink_rows, bias, bounds, *riders)


def _merge_kernel(x1_ref, x2_ref, oa1_ref, oa2_ref, ob1_ref, ob2_ref, mod_ref, g1_ref, win_ref, woa_ref,
                  wob_ref, wout_ref, o1_ref, o2_ref, *, n_first):
    def tile(x_ref, oa_ref, ob_ref, o_ref):
        mod = mod_ref[...]
        rows = lambda i: slice(i * ROW_SUBTILE, (i + 1) * ROW_SUBTILE)

        def norm(i, _):
            return _modulated_norm(x_ref[rows(i), :], g1_ref[...], mod[:, D_MODEL:2 * D_MODEL],
                                   mod[:, 0:D_MODEL]).astype(BF16)

        def branches(i, h):
            ga = _dot(h, win_ref[:, GATE_COL:GATE_COL + D_MODEL])
            gb = _dot(h, win_ref[:, GATE_COL + D_MODEL:GATE_COL + 2 * D_MODEL])
            return (jax.nn.sigmoid(ga) * _dot(oa_ref[rows(i), :], woa_ref[...])
                    + jax.nn.sigmoid(gb) * _dot(ob_ref[rows(i), :], wob_ref[...])).astype(BF16)

        def project(i, merged):
            o_ref[rows(i), :] = x_ref[rows(i), :] + mod[:, 2 * D_MODEL:3 * D_MODEL] * _dot(merged, wout_ref[...])

        _software_pipeline(x_ref.shape[0] // ROW_SUBTILE, [norm, branches, project])

    _per_token_set(n_first, functools.partial(tile, x1_ref, oa1_ref, ob1_ref, o1_ref),
                   functools.partial(tile, x2_ref, oa2_ref, ob2_ref, o2_ref))


def _per_token_set(n_first, first_body, second_body):
    first = pl.program_id(0) < n_first
    pl.when(first)(first_body)
    pl.when(jnp.logical_not(first))(second_body)


def _split_tiles(n_first, tm, width):
    return (pl.BlockSpec((tm, width), lambda i: (jnp.minimum(i, n_first - 1), 0)),
            pl.BlockSpec((tm, width), lambda i: (jnp.maximum(i - n_first, 0), 0)))


def _merge(x1, x2, oa1, oa2, ob1, ob2, mod3, mod_row, g1, w_in, w_oa, w_ob, w_out, tm):
    n_first, n_second = x1.shape[0] // tm, x2.shape[0] // tm
    t1, t2 = _split_tiles(n_first, tm, D_MODEL)
    sq = _const_spec((D_MODEL, D_MODEL))
    return pl.pallas_call(
        functools.partial(_merge_kernel, n_first=n_first),
        grid=(n_first + n_second,),
        in_specs=[t1, t2, t1, t2, t1, t2,
                  pl.BlockSpec((None, 1, N_MOD * D_MODEL), lambda i: (mod_row(i), 0, 0)),
                  _const_spec((1, D_MODEL)), _const_spec(w_in.shape), sq, sq, sq],
        out_specs=[t1, t2],
        out_shape=[jax.ShapeDtypeStruct(x1.shape, F32), jax.ShapeDtypeStruct(x2.shape, F32)],
        compiler_params=_params(1),
        name="merge_out",
    )(x1, x2, oa1, oa2, ob1, ob2, mod3, g1, w_in, w_oa, w_ob, w_out)


def _ffn_kernel(x1_ref, x2_ref, mod_ref, g2_ref, wgate_ref, wup_ref, wdown_ref, o1_ref, o2_ref, *, n_first):
    def tile(x_ref, o_ref):
        mod = mod_ref[...]
        rows = lambda i: slice(i * FFN_SUBTILE, (i + 1) * FFN_SUBTILE)

        def norm(i, _):
            return _modulated_norm(x_ref[rows(i), :], g2_ref[...], mod[:, 4 * D_MODEL:5 * D_MODEL],
                                   mod[:, 3 * D_MODEL:4 * D_MODEL]).astype(BF16)

        def hidden(i, h):
            gate = _dot(h, wgate_ref[...])
            return (gate * jax.nn.sigmoid(gate) * _dot(h, wup_ref[...])).astype(BF16)

        def project(i, act):
            o_ref[rows(i), :] = x_ref[rows(i), :] + mod[:, 5 * D_MODEL:6 * D_MODEL] * _dot(act, wdown_ref[...])

        _software_pipeline(x_ref.shape[0] // FFN_SUBTILE, [norm, hidden, project])

    _per_token_set(n_first, functools.partial(tile, x1_ref, o1_ref), functools.partial(tile, x2_ref, o2_ref))


def _ffn(x1, x2, mod3, mod_row, g2, w_gate, w_up, w_down, tm):
    n_first, n_second = x1.shape[0] // tm, x2.shape[0] // tm
    t1, t2 = _split_tiles(n_first, tm, D_MODEL)
    return pl.pallas_call(
        functools.partial(_ffn_kernel, n_first=n_first),
        grid=(n_first + n_second,),
        in_specs=[t1, t2,
                  pl.BlockSpec((None, 1, N_MOD * D_MODEL), lambda i: (mod_row(i), 0, 0)),
                  _const_spec((1, D_MODEL)), _const_spec((D_MODEL, D_FF)),
                  _const_spec((D_MODEL, D_FF)), _const_spec((D_FF, D_MODEL))],
        out_specs=[t1, t2],
        out_shape=[jax.ShapeDtypeStruct(x1.shape, F32), jax.ShapeDtypeStruct(x2.shape, F32)],
        compiler_params=_params(1),
        name="swiglu_ffn",
    )(x1, x2, mod3, g2, w_gate, w_up, w_down)


def _rope_tables(seq_len):
    quarter = HEAD_DIM // 4
    t = np.arange(seq_len)
    row = (t // GRID_W).astype(np.float32)
    col = (t % GRID_W).astype(np.float32)
    freqs = (np.float32(ROPE_BASE) ** (-np.arange(quarter, dtype=np.float32) / np.float32(quarter))
             ).astype(np.float32)
    ang_r = row[:, None] * freqs
    ang_c = col[:, None] * freqs
    zeros = np.zeros_like(ang_r)
    cos = np.concatenate([np.cos(ang_r)] * 2 + [np.cos(ang_c)] * 2, axis=-1)
    sin_next = np.concatenate([-np.sin(ang_r), zeros, -np.sin(ang_c), zeros], axis=-1)
    sin_prev = np.concatenate([zeros, np.sin(ang_r), zeros, np.sin(ang_c)], axis=-1)
    rep = LANES // HEAD_DIM
    nat = tuple(jnp.asarray(np.tile(a, (1, rep)), dtype=F32) for a in (cos, sin_next, sin_prev))
    featm = (jnp.asarray(cos.T, dtype=F32), jnp.asarray((sin_next + sin_prev).T, dtype=F32))
    return nat, featm


def _group_sum_matrix():
    idx = np.arange(CHUNK) // HEAD_DIM
    return jnp.asarray(idx[:, None] == idx[None, :], dtype=BF16)


def kernel(x_prompt, x_sample, cache_diff_k, cache_diff_v, cache_win_k, cache_win_v, c, c_ctx,
           w_ada, b_ada, norm1_g, w_in, qn_a, kn_a, lambda_q1, lambda_k1, lambda_q2, lambda_k2,
           subln_g, qn_b, kn_b, sink, w_oa, w_ob, w_out, norm2_g, w_gate, w_up, w_down):
    batch, seq, _ = x_prompt.shape
    dec_batch, dec_seq, _ = x_sample.shape
    past = cache_diff_k.shape[2]
    l = 0

    cc = jnp.concatenate([c, c_ctx[None, :], jnp.zeros((8 - dec_batch - 1, D_MODEL), F32)], axis=0)
    mod3 = _modulation(cc, w_ada[l], b_ada[l][None, :])

    w_in_l = w_in[l]
    w_t, w_k = _weight_prep(w_in_l)
    g1 = norm1_g[l][None, :]
    g2 = norm2_g[l][None, :]
    rep = CHUNK // HEAD_DIM
    kgains = jnp.stack([jnp.tile(kn_a[l], rep), jnp.tile(kn_b[l], rep)])
    qgains = jnp.stack([qn_a[l], qn_b[l]])[:, :, None]
    lam_vecs = jnp.stack([lambda_q1[l], lambda_k1[l], lambda_q2[l], lambda_k2[l]])
    subg_col = subln_g[l][:, None]
    sink_rows = jnp.broadcast_to((sink[l] * LOG2E)[:, None], (B_HEADS, LANES))
    head_gains = jnp.stack([qn_a[l], kn_a[l], qn_b[l], kn_b[l]])
    head_scale = jnp.asarray([Q_SCALE, 1.0, Q_SCALE, 1.0], F32)
    bounds = jnp.concatenate([1.02 * HEAD_DIM * jnp.max(jnp.square(head_gains), axis=1) * jnp.square(head_scale),
                              jnp.max(jnp.abs(sink[l]), keepdims=True) * LOG2E]).astype(F32)
    gsum = _group_sum_matrix()
    tm = 512

    xp = x_prompt.reshape(batch * seq, D_MODEL)
    ctx_row = lambda i: dec_batch
    oa_p, ob_p, vbt, kat_f, va_f, kbt_f = _ctx_front(
        x_prompt, mod3, ctx_row, g1, w_k, w_t, gsum, kgains, qgains, lam_vecs, subg_col, sink_rows, bounds,
        tm // seq)
    new_diff_k = kat_f.reshape(batch, A_HEADS, 2, HEAD_DIM, seq).transpose(0, 4, 1, 2, 3)[:, None]
    new_diff_v = va_f.reshape(batch, 1, seq, A_HEADS, 2 * HEAD_DIM)
    new_win_k = kbt_f.reshape(batch, B_KV_HEADS, HEAD_DIM, seq).transpose(0, 3, 1, 2)[:, None]
    new_win_v = vbt.reshape(batch, B_KV_HEADS, HEAD_DIM, seq).transpose(0, 3, 1, 2)[:, None]

    xs = x_sample.reshape(dec_batch * dec_seq, D_MODEL)
    lat_row = lambda i: i // (dec_seq // tm)
    qat, qbt, ka, kb, vat, vbt = _qkv_proj(
        x_sample, mod3, lat_row, g1, w_k, w_t, gsum, kgains, qgains, _rope_tables(dec_seq), tm, False)
    ckt_a = cache_diff_k[:, l].transpose(0, 2, 3, 4, 1).reshape(dec_batch, A_W, past)
    cv_a = cache_diff_v[:, l].reshape(dec_batch, past * A_HEADS, 2 * HEAD_DIM)
    ckt_b = cache_win_k[:, l].transpose(0, 2, 3, 1).reshape(dec_batch, B_KVW, past)
    cvt_b = cache_win_v[:, l].transpose(0, 2, 3, 1).reshape(dec_batch, B_KVW, past)
    oa_s, w_gate_b, w_up_b, w_down_b = _diff_lat(
        qat, ka.reshape(dec_batch, dec_seq, A_W), vat, ckt_a, cv_a, lam_vecs, subg_col, bounds,
        [w_gate[l], w_up[l], w_down[l]])
    ob_s, w_oa_b, w_ob_b, w_out_b, w_in_b = _win_lat(
        qbt, kb.reshape(dec_batch, dec_seq, B_KVW), vbt, ckt_b, cvt_b, sink_rows, bounds,
        [w_oa[l], w_ob[l], w_out[l], w_in_l])

    n_ctx_tiles = batch * seq // tm
    both_row = lambda i: jnp.where(i < n_ctx_tiles, dec_batch, (i - n_ctx_tiles) // (dec_seq // tm))
    xp1, xs1 = _merge(xp, xs, oa_p, oa_s.reshape(-1, A_W), ob_p, ob_s.reshape(-1, B_Q), mod3, both_row, g1,
                      w_in_b, w_oa_b, w_ob_b, w_out_b, tm)
    y_prompt, y_sample = _ffn(xp1, xs1, mod3, both_row, g2, w_gate_b, w_up_b, w_down_b, tm)

    return (y_prompt.reshape(batch, seq, D_MODEL),
            y_sample.reshape(dec_batch, dec_seq, D_MODEL),
            new_diff_k, new_diff_v, new_win_k, new_win_v)
```

```python
import functools
import math

import jax
import jax.numpy as jnp
import numpy as np
from jax import lax
from jax.experimental import pallas as pl
from jax.experimental.pallas import tpu as pltpu

D_MODEL = 1024
HEAD_DIM = 64
A_HEADS = 8
B_HEADS = 16
B_KV_HEADS = 4
B_GROUP = B_HEADS // B_KV_HEADS
A_W = A_HEADS * 2 * HEAD_DIM
B_Q = B_HEADS * HEAD_DIM
B_KVW = B_KV_HEADS * HEAD_DIM
N_K = A_W + B_KVW
N_T = A_W + B_Q + A_W + B_KVW
GATE_COL = 3 * A_W + B_Q + 2 * B_KVW
WINDOW = 128
BLOCK = 128
GRID_W = 64
D_FF = 2816
N_MOD = 6
EPS = 1e-6
ROPE_BASE = 10000.0
NEG = -1e30
LAMBDA_INIT = 0.8 - 0.6 * math.exp(-0.3 * 0)
LOG2E = math.log2(math.e)
Q_SCALE = HEAD_DIM ** -0.5 * LOG2E

LANES = 128
CHUNK = 256
ONES_ROWS = 16
KEY_CHUNK = 512
QKV_SUBTILE = 512
ROW_SUBTILE = 256
FFN_SUBTILE = 128
DIFF_QUERIES = 128
SHIFT_FREE_LOG2 = 60.0
VMEM_LIMIT = 56 * 1024 * 1024

F32 = jnp.float32
BF16 = jnp.bfloat16


def _params(n_axes):
    return pltpu.CompilerParams(dimension_semantics=("arbitrary",) * n_axes,
                                vmem_limit_bytes=VMEM_LIMIT)


def _const_spec(shape):
    nd = len(shape)
    return pl.BlockSpec(shape, lambda *_: (0,) * nd, pipeline_mode=pl.Buffered(1))


def _dot(a, b):
    return jnp.dot(a, b, preferred_element_type=F32)


def _modulated_norm(x, gain, scale, shift):
    y = x * lax.rsqrt(jnp.mean(x * x, axis=-1, keepdims=True) + EPS) * gain
    return y * (1.0 + scale) + shift


def _software_pipeline(n, stages):
    depth = len(stages)
    state = [None] * n
    for step in range(n + depth - 1):
        for k, stage in enumerate(stages):
            i = step - k
            if 0 <= i < n:
                state[i] = stage(i, state[i])


def _mod_kernel(c_ref, w_ref, b_ref, o_ref):
    cc = c_ref[...]
    s = cc * jax.nn.sigmoid(cc)
    mod = _dot(s.astype(BF16), w_ref[...].astype(BF16)) + b_ref[...]
    for r in range(mod.shape[0]):
        o_ref[r] = mod[r:r + 1, :]


def _modulation(cc, w_ada, b_ada):
    rows, n = cc.shape[0], w_ada.shape[1]
    tn = n // 4
    return pl.pallas_call(
        _mod_kernel,
        grid=(n // tn,),
        in_specs=[pl.BlockSpec((rows, D_MODEL), lambda j: (0, 0)),
                  pl.BlockSpec((D_MODEL, tn), lambda j: (0, j)),
                  pl.BlockSpec((1, tn), lambda j: (0, j))],
        out_specs=pl.BlockSpec((rows, 1, tn), lambda j: (0, 0, j)),
        out_shape=jax.ShapeDtypeStruct((rows, 1, n), F32),
        compiler_params=_params(1),
        name="adaln_mod",
    )(cc, w_ada, b_ada)


def _qkv_kernel(*refs, rope, emit_new_kv):
    x_ref, mod_ref, g1_ref, wk_ref, wt_ref, gsum_ref, kgain_ref, qgain_ref = refs[:8]
    refs = refs[8:]
    if rope:
        cos_ref, sa_ref, sb_ref, cost_ref, sint_ref = refs[:5]
        refs = refs[5:]
    qat_ref, qbt_ref, ka_ref, kb_ref, vat_ref, vbt_ref = refs[:6]
    if emit_new_kv:
        kat_ref, va_ref, kbt_ref = refs[6:]

    mod = mod_ref[...]
    gsum = gsum_ref[...]
    n_wide = A_W // CHUNK
    key_cols = [j * CHUNK for j in range(n_wide)] + [A_W]
    n_feat = N_T // CHUNK
    sub = min(QKV_SUBTILE, x_ref.shape[0])

    def normed(u):
        h = _modulated_norm(x_ref[u * sub:(u + 1) * sub, :], g1_ref[...], mod[:, D_MODEL:2 * D_MODEL],
                            mod[:, 0:D_MODEL])
        return h.astype(BF16), h.T.astype(BF16)

    def project(u, hb, ht):
        tok = slice(u * sub, (u + 1) * sub)

        def put_feat(ref, rows, val):
            width = ref.shape[2]
            step = min(sub, width)
            for off in range(0, sub, step):
                s, col = divmod(u * sub + off, width)
                ref[s, rows, col:col + step] = val[:, off:off + step].astype(ref.dtype)

        def key_project(j, _):
            return _dot(hb, wk_ref[:, key_cols[j]:key_cols[j] + CHUNK])

        def key_squares(j, p):
            return p, _dot((p * p).astype(BF16), gsum)

        def key_finish(j, st):
            p, ss = st
            row = 0 if j < n_wide else 1
            p = p * lax.rsqrt(ss * (1.0 / HEAD_DIM) + EPS) * kgain_ref[row:row + 1, :]
            if rope:
                cos, sa, sb = cos_ref[tok, :], sa_ref[tok, :], sb_ref[tok, :]
                halves = []
                for i in range(CHUNK // LANES):
                    xh = p[:, i * LANES:(i + 1) * LANES]
                    nxt = pltpu.roll(xh, LANES - HEAD_DIM // 4, 1)
                    prv = pltpu.roll(xh, HEAD_DIM // 4, 1)
                    halves.append(xh * cos + nxt * sa + prv * sb)
                p = jnp.concatenate(halves, axis=1)
            if j < n_wide:
                sl = slice(j * CHUNK, (j + 1) * CHUNK)
                ka_ref[tok, sl] = p.astype(BF16)
                if emit_new_kv:
                    put_feat(kat_ref, sl, p.T)
            else:
                kb_ref[tok, :] = p.astype(BF16)
                if emit_new_kv:
                    put_feat(kbt_ref, slice(None), p.T)

        _software_pipeline(len(key_cols), [key_project, key_squares, key_finish])

        def feat_project(c, _):
            return _dot(wt_ref[c * CHUNK:(c + 1) * CHUNK, :], ht)

        def feat_finish(c, p):
            kind, j = divmod(c, n_wide)
            sl = slice(j * CHUNK, (j + 1) * CHUNK)
            if kind == 2:
                put_feat(vat_ref, sl, p)
                if emit_new_kv:
                    va_ref[tok, sl] = p.T
                return
            if kind == 3:
                put_feat(vbt_ref, slice(None), p)
                return
            gain = qgain_ref[kind]
            q4 = HEAD_DIM // 4
            heads = []
            for r in range(CHUNK // HEAD_DIM):
                x = p[r * HEAD_DIM:(r + 1) * HEAD_DIM, :]
                x = x * lax.rsqrt(jnp.mean(x * x, axis=0, keepdims=True) + EPS) * gain
                if rope:
                    swapped = jnp.concatenate([x[q4:2 * q4], x[0:q4], x[3 * q4:4 * q4], x[2 * q4:3 * q4]],
                                              axis=0)
                    x = x * cost_ref[:, tok] + swapped * sint_ref[:, tok]
                heads.append(x)
            put_feat(qat_ref if kind == 0 else qbt_ref, sl, jnp.concatenate(heads, axis=0) * Q_SCALE)

        _software_pipeline(n_feat, [feat_project, feat_finish])

    n_sub = x_ref.shape[0] // sub
    nxt = normed(0)
    for u in range(n_sub):
        cur, nxt = nxt, (normed(u + 1) if u + 1 < n_sub else None)
        project(u, *cur)


def _qkv_rope(x3d, mod3, mod_row, g1, w_k, w_t, gsum, kgains, qgains, rope_tabs, tm):
    b, t, _ = x3d.shape
    rows = b * t
    tps = t // tm
    nat, featm = rope_tabs
    tok_wide = pl.BlockSpec((tm, A_W), lambda i: (i, 0))
    tok_narrow = pl.BlockSpec((tm, B_KVW), lambda i: (i, 0))
    feat_wide = pl.BlockSpec((1, A_W, tm), lambda i: (i // tps, 0, i % tps))
    feat_narrow = pl.BlockSpec((1, B_KVW, tm), lambda i: (i // tps, 0, i % tps))
    return pl.pallas_call(
        functools.partial(_qkv_kernel, rope=True, emit_new_kv=False),
        grid=(rows // tm,),
        in_specs=[pl.BlockSpec((tm, D_MODEL), lambda i: (i, 0)),
                  pl.BlockSpec((None, 1, N_MOD * D_MODEL), lambda i: (mod_row(i), 0, 0)),
                  _const_spec((1, D_MODEL)), _const_spec((D_MODEL, N_K)), _const_spec((N_T, D_MODEL)),
                  _const_spec((CHUNK, CHUNK)), _const_spec((2, CHUNK)), _const_spec((2, HEAD_DIM, 1))]
        + [pl.BlockSpec((tm, LANES), lambda i: (i % tps, 0))] * 3
        + [pl.BlockSpec((HEAD_DIM, tm), lambda i: (0, i % tps))] * 2,
        out_specs=[feat_wide, feat_wide, tok_wide, tok_narrow, feat_wide, feat_narrow],
        out_shape=[jax.ShapeDtypeStruct((b, A_W, t), BF16), jax.ShapeDtypeStruct((b, B_Q, t), BF16),
                   jax.ShapeDtypeStruct((rows, A_W), BF16), jax.ShapeDtypeStruct((rows, B_KVW), BF16),
                   jax.ShapeDtypeStruct((b, A_W, t), BF16), jax.ShapeDtypeStruct((b, B_KVW, t), BF16)],
        compiler_params=_params(1),
        name="qkv_rope",
    )(x3d.reshape(rows, D_MODEL), mod3, g1, w_k, w_t, gsum, kgains, qgains, *nat, *featm)


def _lambda(lam_ref):
    lv = lam_ref[...]
    t1 = jnp.sum(lv[0:1] * lv[1:2], axis=-1, keepdims=True)
    t2 = jnp.sum(lv[2:3] * lv[3:4], axis=-1, keepdims=True)
    return jnp.exp(t1) - jnp.exp(t2) + LAMBDA_INIT


def _with_ones(vt):
    return jnp.concatenate([vt, jnp.ones((ONES_ROWS, vt.shape[1]), vt.dtype)], axis=0)


def _column_softmax(scores, extra=None):
    m = functools.reduce(jnp.maximum, [jnp.max(s, axis=0, keepdims=True) for s in scores])
    if extra is not None:
        m = jnp.maximum(m, extra)
    return [jnp.exp2(s - m).astype(BF16) for s in scores], m


def _max_row_sq_norm(x):
    x = x.astype(F32)
    return jnp.max(jnp.sum(x * x, axis=1, keepdims=True), axis=0, keepdims=True)[0, 0]


def _scores_bounded(q_sq, k_sq, sink_abs=0.0):
    limit = SHIFT_FREE_LOG2
    return jnp.logical_and(q_sq * k_sq <= limit * limit, sink_abs <= limit)


def _diff_query_cols(qt):
    zero = jnp.zeros((HEAD_DIM, qt.shape[1]), qt.dtype)
    return jnp.concatenate([jnp.concatenate([qt[:HEAD_DIM], zero], axis=0),
                            jnp.concatenate([zero, qt[HEAD_DIM:]], axis=0)], axis=1)


def _diff_output(r, lam, subg_col):
    n = r.shape[1] // 2
    dv = 2 * HEAD_DIM
    tot = r[dv:dv + 1, :]
    o = r[:dv, :n] * (1.0 / tot[:, :n]) - r[:dv, n:] * (lam / tot[:, n:])
    o = o * lax.rsqrt(jnp.mean(o * o, axis=0, keepdims=True) + EPS)
    return (o * subg_col * (1.0 - LAMBDA_INIT)).T


def _group_output(r, m, sink_row):
    n = r.shape[1] // B_GROUP
    tot = r[HEAD_DIM:HEAD_DIM + 1, :] + jnp.exp2(sink_row - m)
    o = r[:HEAD_DIM, :] * (1.0 / tot)
    return jnp.concatenate([o[:, i * n:(i + 1) * n] for i in range(B_GROUP)], axis=0).T


def _sink_row(sink_ref, g, n):
    reps = n // LANES
    return jnp.concatenate([sink_ref[pl.ds(g * B_GROUP + r, 1), :] for r in range(B_GROUP)
                            for _ in range(reps)], axis=1)


def _ctx_attn_kernel(qat_ref, qbt_ref, ka_ref, kb_ref, vat_ref, vbt_ref, lam_ref, subg_ref, sink_ref,
                     bounds_ref, oa_ref, ob_ref):
    lam = _lambda(lam_ref)
    subg_col = subg_ref[...]
    n_seq, _, n = qat_ref.shape
    chains = A_HEADS + B_KV_HEADS

    def scores(item, _):
        s, c = divmod(item, chains)
        rows = slice(s * n, (s + 1) * n)
        if c < A_HEADS:
            sl = slice(c * LANES, (c + 1) * LANES)
            return _dot(ka_ref[rows, sl], _diff_query_cols(qat_ref[s, sl, :]))
        g = c - A_HEADS
        qs = jnp.concatenate([qbt_ref[s, (g * B_GROUP + r) * HEAD_DIM:(g * B_GROUP + r + 1) * HEAD_DIM, :]
                              for r in range(B_GROUP)], axis=1)
        zero = jnp.zeros_like(qs)
        w = jnp.concatenate([qs if i == g else zero for i in range(B_KV_HEADS)], axis=0)
        return _dot(kb_ref[rows, :], w)

    def probs(shift, item, sc):
        c = item % chains
        if not shift:
            return [jnp.exp2(sc).astype(BF16)], jnp.zeros((1, sc.shape[1]), F32)
        if c < A_HEADS:
            return _column_softmax([sc])
        return _column_softmax([sc], extra=_sink_row(sink_ref, c - A_HEADS, n))

    def outputs(item, st):
        s, c = divmod(item, chains)
        rows = slice(s * n, (s + 1) * n)
        es, m = st
        if c < A_HEADS:
            sl = slice(c * LANES, (c + 1) * LANES)
            r = _dot(_with_ones(vat_ref[s, sl, :]), es[0])
            oa_ref[rows, sl] = _diff_output(r, lam, subg_col).astype(BF16)
        else:
            g = c - A_HEADS
            vt = vbt_ref[s, g * HEAD_DIM:(g + 1) * HEAD_DIM, :].astype(BF16)
            r = _dot(_with_ones(vt), es[0])
            o = _group_output(r, m, _sink_row(sink_ref, g, n))
            ob_ref[rows, g * CHUNK:(g + 1) * CHUNK] = o.astype(BF16)

    bounded = jnp.logical_and(_scores_bounded(bounds_ref[0], bounds_ref[1]),
                              _scores_bounded(bounds_ref[2], bounds_ref[3], bounds_ref[4]))

    @pl.when(bounded)
    def _():
        _software_pipeline(n_seq * chains, [scores, functools.partial(probs, False), outputs])

    @pl.when(jnp.logical_not(bounded))
    def _():
        _software_pipeline(n_seq * chains, [scores, functools.partial(probs, True), outputs])


def _ctx_front_kernel(x_ref, mod_ref, g1_ref, wk_ref, wt_ref, gsum_ref, kgain_ref, qgain_ref,
                      lam_ref, subg_ref, sink_ref, bounds_ref,
                      oa_ref, ob_ref, vbt_ref, kat_ref, va_ref, kbt_ref,
                      qat_s, qbt_s, ka_s, kb_s, vat_s):
    _qkv_kernel(x_ref, mod_ref, g1_ref, wk_ref, wt_ref, gsum_ref, kgain_ref, qgain_ref,
                qat_s, qbt_s, ka_s, kb_s, vat_s, vbt_ref, kat_ref, va_ref, kbt_ref,
                rope=False, emit_new_kv=True)
    _ctx_attn_kernel(qat_s, qbt_s, ka_s, kb_s, vat_s, vbt_ref, lam_ref, subg_ref, sink_ref, bounds_ref,
                     oa_ref, ob_ref)


def _ctx_front(x3d, mod3, mod_row, g1, w_k, w_t, gsum, kgains, qgains, lam_vecs, subg_col, sink_rows, bounds,
               n_seq):
    b, t, _ = x3d.shape
    rows, tm = b * t, n_seq * t
    tok = pl.BlockSpec((tm, A_W), lambda i: (i, 0))
    feat_wide = pl.BlockSpec((n_seq, A_W, t), lambda i: (i, 0, 0))
    feat_narrow = pl.BlockSpec((n_seq, B_KVW, t), lambda i: (i, 0, 0))
    return pl.pallas_call(
        _ctx_front_kernel,
        grid=(b // n_seq,),
        in_specs=[pl.BlockSpec((tm, D_MODEL), lambda i: (i, 0)),
                  pl.BlockSpec((None, 1, N_MOD * D_MODEL), lambda i: (mod_row(i), 0, 0)),
                  _const_spec((1, D_MODEL)), _const_spec((D_MODEL, N_K)), _const_spec((N_T, D_MODEL)),
                  _const_spec((CHUNK, CHUNK)), _const_spec((2, CHUNK)), _const_spec((2, HEAD_DIM, 1)),
                  _const_spec((4, HEAD_DIM)), _const_spec((2 * HEAD_DIM, 1)),
                  _const_spec((B_HEADS, LANES)), pl.BlockSpec(memory_space=pltpu.SMEM)],
        out_specs=[tok, tok, feat_narrow, feat_wide, tok, feat_narrow],
        out_shape=[jax.ShapeDtypeStruct((rows, A_W), BF16), jax.ShapeDtypeStruct((rows, B_Q), BF16),
                   jax.ShapeDtypeStruct((b, B_KVW, t), F32), jax.ShapeDtypeStruct((b, A_W, t), F32),
                   jax.ShapeDtypeStruct((rows, A_W), F32), jax.ShapeDtypeStruct((b, B_KVW, t), F32)],
        scratch_shapes=[pltpu.VMEM((n_seq, A_W, t), BF16), pltpu.VMEM((n_seq, B_Q, t), BF16),
                        pltpu.VMEM((tm, A_W), BF16), pltpu.VMEM((tm, B_KVW), BF16),
                        pltpu.VMEM((n_seq, A_W, t), BF16)],
        compiler_params=_params(1),
        name="ctx_front",
    )(x3d.reshape(rows, D_MODEL), mod3, g1, w_k, w_t, gsum, kgains, qgains, lam_vecs, subg_col, sink_rows,
      bounds)


def _rider_specs(weights, n_steps, flat_step):
    specs = []
    for w in weights:
        rows, cols = w.shape
        bands = next(n for n in (n_steps, n_steps // 2, n_steps // 4)
                     if rows % n == 0 and (rows // n) % 16 == 0)
        repeat = n_steps // bands
        spec = pl.BlockSpec((rows // bands, cols), lambda *g, repeat=repeat: (flat_step(*g) // repeat, 0))
        specs.append((spec, jax.ShapeDtypeStruct(w.shape, BF16)))
    return specs


def _rider_pieces(rider_refs, n_items):
    n = len(rider_refs) // 2
    pieces = [(w_ref, o_ref, r) for w_ref, o_ref in zip(rider_refs[:n], rider_refs[n:])
              for r in range(0, w_ref.shape[0], 16)]

    def run(item):
        for w_ref, o_ref, r in pieces[item::n_items]:
            o_ref[r:r + 16, :] = w_ref[r:r + 16, :].astype(BF16)

    return run


def _diff_lat_kernel(qt_ref, k_ref, vt_ref, ckt_ref, cv_ref, lam_ref, subg_ref, bounds_ref, *rest, n_riders):
    o_ref = rest[n_riders]
    n_sub = qt_ref.shape[1] // DIFF_QUERIES
    cast_piece = _rider_pieces(rest[:n_riders] + rest[n_riders + 1:], n_sub)
    lam = _lambda(lam_ref)
    subg_col = subg_ref[...]
    k = k_ref[...]
    ck = ckt_ref[...].T.astype(BF16)
    vt = _with_ones(vt_ref[...])
    past = cv_ref.shape[0] // A_HEADS
    cv = cv_ref[pl.ds(pl.program_id(1), past, stride=A_HEADS), :]
    cvt = _with_ones(cv.T.astype(BF16))

    def scores(i, _):
        cast_piece(i)
        q = _diff_query_cols(qt_ref[:, i * DIFF_QUERIES:(i + 1) * DIFF_QUERIES])
        return [_dot(k, q), _dot(ck, q)]

    def outputs(shift, i, s):
        if shift:
            m = jnp.maximum(jnp.max(s[0], axis=0, keepdims=True), jnp.max(s[1], axis=0, keepdims=True))
        r = None
        for sc, v in ((s[0], vt), (s[1], cvt)):
            for c in range(0, sc.shape[0], KEY_CHUNK):
                piece = sc[c:c + KEY_CHUNK]
                part = _dot(v[:, c:c + KEY_CHUNK], jnp.exp2(piece - m if shift else piece).astype(BF16))
                r = part if r is None else r + part
        o_ref[i * DIFF_QUERIES:(i + 1) * DIFF_QUERIES, :] = _diff_output(r, lam, subg_col).astype(BF16)

    bounded = _scores_bounded(bounds_ref[0], jnp.maximum(bounds_ref[1], _max_row_sq_norm(ck)))

    @pl.when(bounded)
    def _():
        _software_pipeline(n_sub, [scores, functools.partial(outputs, False)])

    @pl.when(jnp.logical_not(bounded))
    def _():
        _software_pipeline(n_sub, [scores, functools.partial(outputs, True)])


def _diff_lat(qt, k, vt, ckt, cv, lam_vecs, subg_col, bounds, riders):
    b, _, t = qt.shape
    past = ckt.shape[2]
    rider_specs = _rider_specs(riders, b * A_HEADS, lambda i, h: i * A_HEADS + h)
    return pl.pallas_call(
        functools.partial(_diff_lat_kernel, n_riders=len(riders)),
        grid=(b, A_HEADS),
        in_specs=[pl.BlockSpec((None, LANES, t), lambda i, h: (i, h, 0)),
                  pl.BlockSpec((None, t, LANES), lambda i, h: (i, 0, h)),
                  pl.BlockSpec((None, LANES, t), lambda i, h: (i, h, 0)),
                  pl.BlockSpec((None, LANES, past), lambda i, h: (i, h, 0)),
                  pl.BlockSpec((None, past * A_HEADS, LANES), lambda i, h: (i, 0, 0)),
                  _const_spec((4, HEAD_DIM)), _const_spec((2 * HEAD_DIM, 1)),
                  pl.BlockSpec(memory_space=pltpu.SMEM)] + [spec for spec, _ in rider_specs],
        out_specs=[pl.BlockSpec((None, t, LANES), lambda i, h: (i, 0, h))] + [spec for spec, _ in rider_specs],
        out_shape=[jax.ShapeDtypeStruct((b, t, A_W), BF16)] + [shape for _, shape in rider_specs],
        compiler_params=_params(2),
        name="diff_lat",
    )(qt, k, vt, ckt, cv, lam_vecs, subg_col, bounds, *riders)


def _window_start(i, t):
    return min(max((i - 1) * BLOCK, 0), t - 3 * BLOCK)


def _band_offsets(t):
    return sorted({_window_start(i, t) - i * BLOCK for i in range(t // BLOCK)})


def _band_bias(t):
    key = np.arange(3 * BLOCK)[:, None]
    qry = np.arange(B_GROUP * BLOCK)[None, :] % BLOCK
    return jnp.asarray(np.stack([np.where(np.abs(key + off - qry) <= WINDOW, 0.0, NEG)
                                 for off in _band_offsets(t)]), dtype=F32)


def _win_lat_kernel(qt_ref, k_ref, vt_ref, ckt_ref, cvt_ref, sink_ref, bias_ref, bounds_ref, *rest, n_riders):
    o_ref = rest[n_riders]
    g = pl.program_id(1)
    t = k_ref.shape[0]
    span = 3 * BLOCK
    cast_piece = _rider_pieces(rest[:n_riders] + rest[n_riders + 1:], t // BLOCK)
    keep = lax.broadcasted_iota(jnp.int32, (1, CHUNK), 1) // HEAD_DIM == g
    k = jnp.where(keep, k_ref[...].astype(F32), 0.0).astype(BF16)
    ck = jnp.where(keep, ckt_ref[...].T, 0.0).astype(BF16)
    vt = vt_ref[...]
    cvt = _with_ones(cvt_ref[...].astype(BF16))
    sink_row = _sink_row(sink_ref, g, BLOCK)
    offsets = _band_offsets(t)

    def scores(i, _):
        cast_piece(i)
        w = _window_start(i, t)
        qs = jnp.concatenate([qt_ref[r * HEAD_DIM:(r + 1) * HEAD_DIM, i * BLOCK:(i + 1) * BLOCK]
                              for r in range(B_GROUP)], axis=1)
        wq = jnp.concatenate([qs] * B_KV_HEADS, axis=0)
        return [_dot(k[w:w + span], wq) + bias_ref[offsets.index(w - i * BLOCK)], _dot(ck, wq)]

    def probs(shift, i, s):
        if shift:
            return _column_softmax(s, extra=sink_row)
        return [jnp.exp2(piece).astype(BF16) for piece in s], jnp.zeros_like(sink_row)

    def outputs(i, st):
        es, m = st
        w = _window_start(i, t)
        r = _dot(_with_ones(vt[:, w:w + span]), es[0]) + _dot(cvt, es[1])
        o_ref[i * BLOCK:(i + 1) * BLOCK, :] = _group_output(r, m, sink_row).astype(BF16)

    bounded = _scores_bounded(bounds_ref[2], jnp.maximum(bounds_ref[3], _max_row_sq_norm(ck)), bounds_ref[4])

    @pl.when(bounded)
    def _():
        _software_pipeline(t // BLOCK, [scores, functools.partial(probs, False), outputs])

    @pl.when(jnp.logical_not(bounded))
    def _():
        _software_pipeline(t // BLOCK, [scores, functools.partial(probs, True), outputs])


def _win_lat(qt, k, vt, ckt, cvt, sink_rows, bounds, riders):
    b, _, t = qt.shape
    past = ckt.shape[2]
    bias = _band_bias(t)
    rider_specs = _rider_specs(riders, b * B_KV_HEADS, lambda i, g: i * B_KV_HEADS + g)
    return pl.pallas_call(
        functools.partial(_win_lat_kernel, n_riders=len(riders)),
        grid=(b, B_KV_HEADS),
        in_specs=[pl.BlockSpec((None, CHUNK, t), lambda i, g: (i, g, 0)),
                  pl.BlockSpec((None, t, B_KVW), lambda i, g: (i, 0, 0)),
                  pl.BlockSpec((None, HEAD_DIM, t), lambda i, g: (i, g, 0)),
                  pl.BlockSpec((None, B_KVW, past), lambda i, g: (i, 0, 0)),
                  pl.BlockSpec((None, HEAD_DIM, past), lambda i, g: (i, g, 0)),
                  _const_spec((B_HEADS, LANES)), _const_spec(bias.shape),
                  pl.BlockSpec(memory_space=pltpu.SMEM)] + [spec for spec, _ in rider_specs],
        out_specs=[pl.BlockSpec((None, t, CHUNK), lambda i, g: (i, 0, g))] + [spec for spec, _ in rider_specs],
        out_shape=[jax.ShapeDtypeStruct((b, t, B_Q), BF16)] + [shape for _, shape in rider_specs],
        compiler_params=_params(2),
        name="win_lat",
    )(qt, k, vt, ckt, cvt, sink_rows, bias, bounds, *riders)


def _merge_kernel(x1_ref, x2_ref, oa1_ref, oa2_ref, ob1_ref, ob2_ref, mod_ref, g1_ref, win_ref, woa_ref,
                  wob_ref, wout_ref, o1_ref, o2_ref, *, n_first):
    def tile(x_ref, oa_ref, ob_ref, o_ref):
        mod = mod_ref[...]
        rows = lambda i: slice(i * ROW_SUBTILE, (i + 1) * ROW_SUBTILE)

        def norm(i, _):
            return _modulated_norm(x_ref[rows(i), :], g1_ref[...], mod[:, D_MODEL:2 * D_MODEL],
                                   mod[:, 0:D_MODEL]).astype(BF16)

        def branches(i, h):
            ga = _dot(h, win_ref[:, GATE_COL:GATE_COL + D_MODEL])
            gb = _dot(h, win_ref[:, GATE_COL + D_MODEL:GATE_COL + 2 * D_MODEL])
            return (jax.nn.sigmoid(ga) * _dot(oa_ref[rows(i), :], woa_ref[...])
                    + jax.nn.sigmoid(gb) * _dot(ob_ref[rows(i), :], wob_ref[...])).astype(BF16)

        def project(i, merged):
            o_ref[rows(i), :] = x_ref[rows(i), :] + mod[:, 2 * D_MODEL:3 * D_MODEL] * _dot(merged, wout_ref[...])

        _software_pipeline(x_ref.shape[0] // ROW_SUBTILE, [norm, branches, project])

    _per_token_set(n_first, functools.partial(tile, x1_ref, oa1_ref, ob1_ref, o1_ref),
                   functools.partial(tile, x2_ref, oa2_ref, ob2_ref, o2_ref))


def _per_token_set(n_first, first_body, second_body):
    first = pl.program_id(0) < n_first
    pl.when(first)(first_body)
    pl.when(jnp.logical_not(first))(second_body)


def _split_tiles(n_first, tm, width):
    return (pl.BlockSpec((tm, width), lambda i: (jnp.minimum(i, n_first - 1), 0)),
            pl.BlockSpec((tm, width), lambda i: (jnp.maximum(i - n_first, 0), 0)))


def _merge(x1, x2, oa1, oa2, ob1, ob2, mod3, mod_row, g1, w_in, w_oa, w_ob, w_out, tm):
    n_first, n_second = x1.shape[0] // tm, x2.shape[0] // tm
    t1, t2 = _split_tiles(n_first, tm, D_MODEL)
    sq = _const_spec((D_MODEL, D_MODEL))
    return pl.pallas_call(
        functools.partial(_merge_kernel, n_first=n_first),
        grid=(n_first + n_second,),
        in_specs=[t1, t2, t1, t2, t1, t2,
                  pl.BlockSpec((None, 1, N_MOD * D_MODEL), lambda i: (mod_row(i), 0, 0)),
                  _const_spec((1, D_MODEL)), _const_spec(w_in.shape), sq, sq, sq],
        out_specs=[t1, t2],
        out_shape=[jax.ShapeDtypeStruct(x1.shape, F32), jax.ShapeDtypeStruct(x2.shape, F32)],
        compiler_params=_params(1),
        name="merge_out",
    )(x1, x2, oa1, oa2, ob1, ob2, mod3, g1, w_in, w_oa, w_ob, w_out)


def _ffn_kernel(x1_ref, x2_ref, mod_ref, g2_ref, wgate_ref, wup_ref, wdown_ref, o1_ref, o2_ref, *, n_first):
    def tile(x_ref, o_ref):
        mod = mod_ref[...]
        rows = lambda i: slice(i * FFN_SUBTILE, (i + 1) * FFN_SUBTILE)

        def norm(i, _):
            return _modulated_norm(x_ref[rows(i), :], g2_ref[...], mod[:, 4 * D_MODEL:5 * D_MODEL],
                                   mod[:, 3 * D_MODEL:4 * D_MODEL]).astype(BF16)

        def hidden(i, h):
            gate = _dot(h, wgate_ref[...])
            return (gate * jax.nn.sigmoid(gate) * _dot(h, wup_ref[...])).astype(BF16)

        def project(i, act):
            o_ref[rows(i), :] = x_ref[rows(i), :] + mod[:, 5 * D_MODEL:6 * D_MODEL] * _dot(act, wdown_ref[...])

        _software_pipeline(x_ref.shape[0] // FFN_SUBTILE, [norm, hidden, project])

    _per_token_set(n_first, functools.partial(tile, x1_ref, o1_ref), functools.partial(tile, x2_ref, o2_ref))


def _ffn(x1, x2, mod3, mod_row, g2, w_gate, w_up, w_down, tm):
    n_first, n_second = x1.shape[0] // tm, x2.shape[0] // tm
    t1, t2 = _split_tiles(n_first, tm, D_MODEL)
    return pl.pallas_call(
        functools.partial(_ffn_kernel, n_first=n_first),
        grid=(n_first + n_second,),
        in_specs=[t1, t2,
                  pl.BlockSpec((None, 1, N_MOD * D_MODEL), lambda i: (mod_row(i), 0, 0)),
                  _const_spec((1, D_MODEL)), _const_spec((D_MODEL, D_FF)),
                  _const_spec((D_MODEL, D_FF)), _const_spec((D_FF, D_MODEL))],
        out_specs=[t1, t2],
        out_shape=[jax.ShapeDtypeStruct(x1.shape, F32), jax.ShapeDtypeStruct(x2.shape, F32)],
        compiler_params=_params(1),
        name="swiglu_ffn",
    )(x1, x2, mod3, g2, w_gate, w_up, w_down)


def _rope_tables(seq_len):
    quarter = HEAD_DIM // 4
    t = np.arange(seq_len)
    row = (t // GRID_W).astype(np.float32)
    col = (t % GRID_W).astype(np.float32)
    freqs = (np.float32(ROPE_BASE) ** (-np.arange(quarter, dtype=np.float32) / np.float32(quarter))
             ).astype(np.float32)
    ang_r = row[:, None] * freqs
    ang_c = col[:, None] * freqs
    zeros = np.zeros_like(ang_r)
    cos = np.concatenate([np.cos(ang_r)] * 2 + [np.cos(ang_c)] * 2, axis=-1)
    sin_next = np.concatenate([-np.sin(ang_r), zeros, -np.sin(ang_c), zeros], axis=-1)
    sin_prev = np.concatenate([zeros, np.sin(ang_r), zeros, np.sin(ang_c)], axis=-1)
    rep = LANES // HEAD_DIM
    nat = tuple(jnp.asarray(np.tile(a, (1, rep)), dtype=F32) for a in (cos, sin_next, sin_prev))
    featm = (jnp.asarray(cos.T, dtype=F32), jnp.asarray((sin_next + sin_prev).T, dtype=F32))
    return nat, featm


def _group_sum_matrix():
    idx = np.arange(CHUNK) // HEAD_DIM
    return jnp.asarray(idx[:, None] == idx[None, :], dtype=BF16)


def kernel(x_prompt, x_sample, cache_diff_k, cache_diff_v, cache_win_k, cache_win_v, c, c_ctx,
           w_ada, b_ada, norm1_g, w_in, qn_a, kn_a, lambda_q1, lambda_k1, lambda_q2, lambda_k2,
           subln_g, qn_b, kn_b, sink, w_oa, w_ob, w_out, norm2_g, w_gate, w_up, w_down):
    batch, seq, _ = x_prompt.shape
    dec_batch, dec_seq, _ = x_sample.shape
    past = cache_diff_k.shape[2]
    l = 0

    cc = jnp.concatenate([c, c_ctx[None, :], jnp.zeros((8 - dec_batch - 1, D_MODEL), F32)], axis=0)
    mod3 = _modulation(cc, w_ada[l], b_ada[l][None, :])

    w_in_b = w_in[l].astype(BF16)
    o_ka, o_va, o_qb, o_kb, o_vb = A_W, 2 * A_W, 3 * A_W, 3 * A_W + B_Q, 3 * A_W + B_Q + B_KVW
    w_k = jnp.concatenate([w_in_b[:, o_ka:o_va], w_in_b[:, o_kb:o_vb]], axis=1)
    w_t = jnp.concatenate([w_in_b[:, :o_ka], w_in_b[:, o_qb:o_kb], w_in_b[:, o_va:o_qb],
                           w_in_b[:, o_vb:GATE_COL]], axis=1).T
    g1 = norm1_g[l][None, :]
    g2 = norm2_g[l][None, :]
    rep = CHUNK // HEAD_DIM
    kgains = jnp.stack([jnp.tile(kn_a[l], rep), jnp.tile(kn_b[l], rep)])
    qgains = jnp.stack([qn_a[l], qn_b[l]])[:, :, None]
    lam_vecs = jnp.stack([lambda_q1[l], lambda_k1[l], lambda_q2[l], lambda_k2[l]])
    subg_col = subln_g[l][:, None]
    sink_rows = jnp.broadcast_to((sink[l] * LOG2E)[:, None], (B_HEADS, LANES))
    head_gains = jnp.stack([qn_a[l], kn_a[l], qn_b[l], kn_b[l]])
    head_scale = jnp.asarray([Q_SCALE, 1.0, Q_SCALE, 1.0], F32)
    bounds = jnp.concatenate([1.02 * HEAD_DIM * jnp.max(jnp.square(head_gains), axis=1) * jnp.square(head_scale),
                              jnp.max(jnp.abs(sink[l]), keepdims=True) * LOG2E]).astype(F32)
    gsum = _group_sum_matrix()
    tm = 512

    xp = x_prompt.reshape(batch * seq, D_MODEL)
    ctx_row = lambda i: dec_batch
    oa_p, ob_p, vbt, kat_f, va_f, kbt_f = _ctx_front(
        x_prompt, mod3, ctx_row, g1, w_k, w_t, gsum, kgains, qgains, lam_vecs, subg_col, sink_rows, bounds,
        tm // seq)
    new_diff_k = kat_f.reshape(batch, A_HEADS, 2, HEAD_DIM, seq).transpose(0, 4, 1, 2, 3)[:, None]
    new_diff_v = va_f.reshape(batch, 1, seq, A_HEADS, 2 * HEAD_DIM)
    new_win_k = kbt_f.reshape(batch, B_KV_HEADS, HEAD_DIM, seq).transpose(0, 3, 1, 2)[:, None]
    new_win_v = vbt.reshape(batch, B_KV_HEADS, HEAD_DIM, seq).transpose(0, 3, 1, 2)[:, None]

    xs = x_sample.reshape(dec_batch * dec_seq, D_MODEL)
    lat_row = lambda i: i // (dec_seq // tm)
    qat, qbt, ka, kb, vat, vbt = _qkv_rope(
        x_sample, mod3, lat_row, g1, w_k, w_t, gsum, kgains, qgains, _rope_tables(dec_seq), tm)
    ckt_a = cache_diff_k[:, l].transpose(0, 2, 3, 4, 1).reshape(dec_batch, A_W, past)
    cv_a = cache_diff_v[:, l].reshape(dec_batch, past * A_HEADS, 2 * HEAD_DIM)
    ckt_b = cache_win_k[:, l].transpose(0, 2, 3, 1).reshape(dec_batch, B_KVW, past)
    cvt_b = cache_win_v[:, l].transpose(0, 2, 3, 1).reshape(dec_batch, B_KVW, past)
    oa_s, w_gate_b, w_up_b, w_down_b = _diff_lat(
        qat, ka.reshape(dec_batch, dec_seq, A_W), vat, ckt_a, cv_a, lam_vecs, subg_col, bounds,
        [w_gate[l], w_up[l], w_down[l]])
    ob_s, w_oa_b, w_ob_b, w_out_b = _win_lat(
        qbt, kb.reshape(dec_batch, dec_seq, B_KVW), vbt, ckt_b, cvt_b, sink_rows, bounds,
        [w_oa[l], w_ob[l], w_out[l]])

    n_ctx_tiles = batch * seq // tm
    both_row = lambda i: jnp.where(i < n_ctx_tiles, dec_batch, (i - n_ctx_tiles) // (dec_seq // tm))
    xp1, xs1 = _merge(xp, xs, oa_p, oa_s.reshape(-1, A_W), ob_p, ob_s.reshape(-1, B_Q), mod3, both_row, g1,
                      w_in_b, w_oa_b, w_ob_b, w_out_b, tm)
    y_prompt, y_sample = _ffn(xp1, xs1, mod3, both_row, g2, w_gate_b, w_up_b, w_down_b, tm)

    return (y_prompt.reshape(batch, seq, D_MODEL),
            y_sample.reshape(dec_batch, dec_seq, D_MODEL),
            new_diff_k, new_diff_v, new_win_k, new_win_v)
```

```python
import functools
import math

import jax
import jax.numpy as jnp
import numpy as np
from jax import lax
from jax.experimental import pallas as pl
from jax.experimental.pallas import tpu as pltpu

D_MODEL = 1024
HEAD_DIM = 64
A_HEADS = 8
B_HEADS = 16
B_KV_HEADS = 4
B_GROUP = B_HEADS // B_KV_HEADS
A_W = A_HEADS * 2 * HEAD_DIM
B_Q = B_HEADS * HEAD_DIM
B_KVW = B_KV_HEADS * HEAD_DIM
N_K = A_W + B_KVW
N_T = A_W + B_Q + A_W + B_KVW
GATE_COL = 3 * A_W + B_Q + 2 * B_KVW
WINDOW = 128
BLOCK = 128
GRID_W = 64
D_FF = 2816
N_MOD = 6
EPS = 1e-6
ROPE_BASE = 10000.0
NEG = -1e30
LAMBDA_INIT = 0.8 - 0.6 * math.exp(-0.3 * 0)
LOG2E = math.log2(math.e)
Q_SCALE = HEAD_DIM ** -0.5 * LOG2E

LANES = 128
CHUNK = 256
ONES_ROWS = 16
KEY_CHUNK = 512
QKV_SUBTILE = 512
ROW_SUBTILE = 256
FFN_SUBTILE = 128
DIFF_QUERIES = 128
SHIFT_FREE_LOG2 = 60.0
VMEM_LIMIT = 56 * 1024 * 1024

F32 = jnp.float32
BF16 = jnp.bfloat16


def _params(n_axes):
    return pltpu.CompilerParams(dimension_semantics=("arbitrary",) * n_axes,
                                vmem_limit_bytes=VMEM_LIMIT)


def _const_spec(shape):
    nd = len(shape)
    return pl.BlockSpec(shape, lambda *_: (0,) * nd, pipeline_mode=pl.Buffered(1))


def _dot(a, b):
    return jnp.dot(a, b, preferred_element_type=F32)


def _modulated_norm(x, gain, scale, shift):
    y = x * lax.rsqrt(jnp.mean(x * x, axis=-1, keepdims=True) + EPS) * gain
    return y * (1.0 + scale) + shift


def _software_pipeline(n, stages):
    depth = len(stages)
    state = [None] * n
    for step in range(n + depth - 1):
        for k, stage in enumerate(stages):
            i = step - k
            if 0 <= i < n:
                state[i] = stage(i, state[i])


def _mod_kernel(c_ref, w_ref, b_ref, o_ref):
    cc = c_ref[...]
    s = cc * jax.nn.sigmoid(cc)
    mod = _dot(s.astype(BF16), w_ref[...].astype(BF16)) + b_ref[...]
    for r in range(mod.shape[0]):
        o_ref[r] = mod[r:r + 1, :]


def _modulation(cc, w_ada, b_ada):
    rows, n = cc.shape[0], w_ada.shape[1]
    tn = n // 2
    return pl.pallas_call(
        _mod_kernel,
        grid=(n // tn,),
        in_specs=[pl.BlockSpec((rows, D_MODEL), lambda j: (0, 0)),
                  pl.BlockSpec((D_MODEL, tn), lambda j: (0, j)),
                  pl.BlockSpec((1, tn), lambda j: (0, j))],
        out_specs=pl.BlockSpec((rows, 1, tn), lambda j: (0, 0, j)),
        out_shape=jax.ShapeDtypeStruct((rows, 1, n), F32),
        compiler_params=_params(1),
        name="adaln_mod",
    )(cc, w_ada, b_ada)


def _qkv_kernel(*refs, rope, emit_new_kv):
    x_ref, mod_ref, g1_ref, wk_ref, wt_ref, gsum_ref, kgain_ref, qgain_ref = refs[:8]
    refs = refs[8:]
    if rope:
        cos_ref, sa_ref, sb_ref, cost_ref, sint_ref = refs[:5]
        refs = refs[5:]
    qat_ref, qbt_ref, ka_ref, kb_ref, vat_ref, vbt_ref = refs[:6]
    if emit_new_kv:
        kat_ref, va_ref, kbt_ref = refs[6:]

    mod = mod_ref[...]
    gsum = gsum_ref[...]
    if rope:
        pos = pl.program_id(0) % cos_ref.shape[0]
    n_wide = A_W // CHUNK
    key_cols = [j * CHUNK for j in range(n_wide)] + [A_W]
    n_feat = N_T // CHUNK
    sub = min(QKV_SUBTILE, x_ref.shape[0])

    def normed(u):
        h = _modulated_norm(x_ref[u * sub:(u + 1) * sub, :], g1_ref[...], mod[:, D_MODEL:2 * D_MODEL],
                            mod[:, 0:D_MODEL])
        return h.astype(BF16), h.T.astype(BF16)

    def project(u, hb, ht):
        tok = slice(u * sub, (u + 1) * sub)

        def put_feat(ref, rows, val):
            width = ref.shape[2]
            step = min(sub, width)
            for off in range(0, sub, step):
                s, col = divmod(u * sub + off, width)
                ref[s, rows, col:col + step] = val[:, off:off + step].astype(ref.dtype)

        def key_project(j, _):
            return _dot(hb, wk_ref[:, key_cols[j]:key_cols[j] + CHUNK])

        def key_squares(j, p):
            return p, _dot((p * p).astype(BF16), gsum)

        def key_finish(j, st):
            p, ss = st
            row = 0 if j < n_wide else 1
            p = p * lax.rsqrt(ss * (1.0 / HEAD_DIM) + EPS) * kgain_ref[row:row + 1, :]
            if rope:
                cos, sa, sb = cos_ref[pos, tok, :], sa_ref[pos, tok, :], sb_ref[pos, tok, :]
                halves = []
                for i in range(CHUNK // LANES):
                    xh = p[:, i * LANES:(i + 1) * LANES]
                    nxt = pltpu.roll(xh, LANES - HEAD_DIM // 4, 1)
                    prv = pltpu.roll(xh, HEAD_DIM // 4, 1)
                    halves.append(xh * cos + nxt * sa + prv * sb)
                p = jnp.concatenate(halves, axis=1)
            if j < n_wide:
                sl = slice(j * CHUNK, (j + 1) * CHUNK)
                ka_ref[tok, sl] = p.astype(BF16)
                if emit_new_kv:
                    put_feat(kat_ref, sl, p.T)
            else:
                kb_ref[tok, :] = p.astype(BF16)
                if emit_new_kv:
                    put_feat(kbt_ref, slice(None), p.T)

        _software_pipeline(len(key_cols), [key_project, key_squares, key_finish])

        def feat_project(c, _):
            return _dot(wt_ref[c * CHUNK:(c + 1) * CHUNK, :], ht)

        def feat_finish(c, p):
            kind, j = divmod(c, n_wide)
            sl = slice(j * CHUNK, (j + 1) * CHUNK)
            if kind == 2:
                put_feat(vat_ref, sl, p)
                if emit_new_kv:
                    va_ref[tok, sl] = p.T
                return
            if kind == 3:
                put_feat(vbt_ref, slice(None), p)
                return
            gain = qgain_ref[kind]
            q4 = HEAD_DIM // 4
            heads = []
            for r in range(CHUNK // HEAD_DIM):
                x = p[r * HEAD_DIM:(r + 1) * HEAD_DIM, :]
                x = x * lax.rsqrt(jnp.mean(x * x, axis=0, keepdims=True) + EPS) * gain
                if rope:
                    swapped = jnp.concatenate([x[q4:2 * q4], x[0:q4], x[3 * q4:4 * q4], x[2 * q4:3 * q4]],
                                              axis=0)
                    x = x * cost_ref[pos, :, tok] + swapped * sint_ref[pos, :, tok]
                heads.append(x)
            put_feat(qat_ref if kind == 0 else qbt_ref, sl, jnp.concatenate(heads, axis=0) * Q_SCALE)

        _software_pipeline(n_feat, [feat_project, feat_finish])

    n_sub = x_ref.shape[0] // sub
    nxt = normed(0)
    for u in range(n_sub):
        cur, nxt = nxt, (normed(u + 1) if u + 1 < n_sub else None)
        project(u, *cur)


def _qkv_rope(x3d, mod3, mod_row, g1, w_k, w_t, gsum, kgains, qgains, rope_tabs, tm):
    b, t, _ = x3d.shape
    rows = b * t
    tps = t // tm
    nat, featm = rope_tabs
    tok_wide = pl.BlockSpec((tm, A_W), lambda i: (i, 0))
    tok_narrow = pl.BlockSpec((tm, B_KVW), lambda i: (i, 0))
    feat_wide = pl.BlockSpec((1, A_W, tm), lambda i: (i // tps, 0, i % tps))
    feat_narrow = pl.BlockSpec((1, B_KVW, tm), lambda i: (i // tps, 0, i % tps))
    return pl.pallas_call(
        functools.partial(_qkv_kernel, rope=True, emit_new_kv=False),
        grid=(rows // tm,),
        in_specs=[pl.BlockSpec((tm, D_MODEL), lambda i: (i, 0)),
                  pl.BlockSpec((None, 1, N_MOD * D_MODEL), lambda i: (mod_row(i), 0, 0)),
                  _const_spec((1, D_MODEL)), _const_spec((D_MODEL, N_K)), _const_spec((N_T, D_MODEL)),
                  _const_spec((CHUNK, CHUNK)), _const_spec((2, CHUNK)), _const_spec((2, HEAD_DIM, 1))]
        + [_const_spec((tps, tm, LANES))] * 3 + [_const_spec((tps, HEAD_DIM, tm))] * 2,
        out_specs=[feat_wide, feat_wide, tok_wide, tok_narrow, feat_wide, feat_narrow],
        out_shape=[jax.ShapeDtypeStruct((b, A_W, t), BF16), jax.ShapeDtypeStruct((b, B_Q, t), BF16),
                   jax.ShapeDtypeStruct((rows, A_W), BF16), jax.ShapeDtypeStruct((rows, B_KVW), BF16),
                   jax.ShapeDtypeStruct((b, A_W, t), BF16), jax.ShapeDtypeStruct((b, B_KVW, t), BF16)],
        compiler_params=_params(1),
        name="qkv_rope",
    )(x3d.reshape(rows, D_MODEL), mod3, g1, w_k, w_t, gsum, kgains, qgains, *nat, *featm)


def _lambda(lam_ref):
    lv = lam_ref[...]
    t1 = jnp.sum(lv[0:1] * lv[1:2], axis=-1, keepdims=True)
    t2 = jnp.sum(lv[2:3] * lv[3:4], axis=-1, keepdims=True)
    return jnp.exp(t1) - jnp.exp(t2) + LAMBDA_INIT


def _with_ones(vt):
    return jnp.concatenate([vt, jnp.ones((ONES_ROWS, vt.shape[1]), vt.dtype)], axis=0)


def _column_softmax(scores, extra=None):
    m = functools.reduce(jnp.maximum, [jnp.max(s, axis=0, keepdims=True) for s in scores])
    if extra is not None:
        m = jnp.maximum(m, extra)
    return [jnp.exp2(s - m).astype(BF16) for s in scores], m


def _max_row_sq_norm(x):
    x = x.astype(F32)
    return jnp.max(jnp.sum(x * x, axis=1, keepdims=True), axis=0, keepdims=True)[0, 0]


def _scores_bounded(q_sq, k_sq, sink_abs=0.0):
    limit = SHIFT_FREE_LOG2
    return jnp.logical_and(q_sq * k_sq <= limit * limit, sink_abs <= limit)


def _diff_query_cols(qt):
    zero = jnp.zeros((HEAD_DIM, qt.shape[1]), qt.dtype)
    return jnp.concatenate([jnp.concatenate([qt[:HEAD_DIM], zero], axis=0),
                            jnp.concatenate([zero, qt[HEAD_DIM:]], axis=0)], axis=1)


def _diff_output(r, lam, subg_col):
    n = r.shape[1] // 2
    dv = 2 * HEAD_DIM
    tot = r[dv:dv + 1, :]
    o = r[:dv, :n] * (1.0 / tot[:, :n]) - r[:dv, n:] * (lam / tot[:, n:])
    o = o * lax.rsqrt(jnp.mean(o * o, axis=0, keepdims=True) + EPS)
    return (o * subg_col * (1.0 - LAMBDA_INIT)).T


def _group_output(r, m, sink_row):
    n = r.shape[1] // B_GROUP
    tot = r[HEAD_DIM:HEAD_DIM + 1, :] + jnp.exp2(sink_row - m)
    o = r[:HEAD_DIM, :] * (1.0 / tot)
    return jnp.concatenate([o[:, i * n:(i + 1) * n] for i in range(B_GROUP)], axis=0).T


def _sink_row(sink_ref, g, n):
    reps = n // LANES
    return jnp.concatenate([sink_ref[pl.ds(g * B_GROUP + r, 1), :] for r in range(B_GROUP)
                            for _ in range(reps)], axis=1)


def _ctx_attn_kernel(qat_ref, qbt_ref, ka_ref, kb_ref, vat_ref, vbt_ref, lam_ref, subg_ref, sink_ref,
                     bounds_ref, oa_ref, ob_ref):
    lam = _lambda(lam_ref)
    subg_col = subg_ref[...]
    n_seq, _, n = qat_ref.shape
    chains = A_HEADS + B_KV_HEADS

    def scores(item, _):
        s, c = divmod(item, chains)
        rows = slice(s * n, (s + 1) * n)
        if c < A_HEADS:
            sl = slice(c * LANES, (c + 1) * LANES)
            return _dot(ka_ref[rows, sl], _diff_query_cols(qat_ref[s, sl, :]))
        g = c - A_HEADS
        qs = jnp.concatenate([qbt_ref[s, (g * B_GROUP + r) * HEAD_DIM:(g * B_GROUP + r + 1) * HEAD_DIM, :]
                              for r in range(B_GROUP)], axis=1)
        zero = jnp.zeros_like(qs)
        w = jnp.concatenate([qs if i == g else zero for i in range(B_KV_HEADS)], axis=0)
        return _dot(kb_ref[rows, :], w)

    def probs(shift, item, sc):
        c = item % chains
        if not shift:
            return [jnp.exp2(sc).astype(BF16)], jnp.zeros((1, sc.shape[1]), F32)
        if c < A_HEADS:
            return _column_softmax([sc])
        return _column_softmax([sc], extra=_sink_row(sink_ref, c - A_HEADS, n))

    def outputs(item, st):
        s, c = divmod(item, chains)
        rows = slice(s * n, (s + 1) * n)
        es, m = st
        if c < A_HEADS:
            sl = slice(c * LANES, (c + 1) * LANES)
            r = _dot(_with_ones(vat_ref[s, sl, :]), es[0])
            oa_ref[rows, sl] = _diff_output(r, lam, subg_col).astype(BF16)
        else:
            g = c - A_HEADS
            vt = vbt_ref[s, g * HEAD_DIM:(g + 1) * HEAD_DIM, :].astype(BF16)
            r = _dot(_with_ones(vt), es[0])
            o = _group_output(r, m, _sink_row(sink_ref, g, n))
            ob_ref[rows, g * CHUNK:(g + 1) * CHUNK] = o.astype(BF16)

    bounded = jnp.logical_and(_scores_bounded(bounds_ref[0], bounds_ref[1]),
                              _scores_bounded(bounds_ref[2], bounds_ref[3], bounds_ref[4]))

    @pl.when(bounded)
    def _():
        _software_pipeline(n_seq * chains, [scores, functools.partial(probs, False), outputs])

    @pl.when(jnp.logical_not(bounded))
    def _():
        _software_pipeline(n_seq * chains, [scores, functools.partial(probs, True), outputs])


def _ctx_front_kernel(x_ref, mod_ref, g1_ref, wk_ref, wt_ref, gsum_ref, kgain_ref, qgain_ref,
                      lam_ref, subg_ref, sink_ref, bounds_ref,
                      oa_ref, ob_ref, vbt_ref, kat_ref, va_ref, kbt_ref,
                      qat_s, qbt_s, ka_s, kb_s, vat_s):
    _qkv_kernel(x_ref, mod_ref, g1_ref, wk_ref, wt_ref, gsum_ref, kgain_ref, qgain_ref,
                qat_s, qbt_s, ka_s, kb_s, vat_s, vbt_ref, kat_ref, va_ref, kbt_ref,
                rope=False, emit_new_kv=True)
    _ctx_attn_kernel(qat_s, qbt_s, ka_s, kb_s, vat_s, vbt_ref, lam_ref, subg_ref, sink_ref, bounds_ref,
                     oa_ref, ob_ref)


def _ctx_front(x3d, mod3, mod_row, g1, w_k, w_t, gsum, kgains, qgains, lam_vecs, subg_col, sink_rows, bounds,
               n_seq):
    b, t, _ = x3d.shape
    rows, tm = b * t, n_seq * t
    tok = pl.BlockSpec((tm, A_W), lambda i: (i, 0))
    feat_wide = pl.BlockSpec((n_seq, A_W, t), lambda i: (i, 0, 0))
    feat_narrow = pl.BlockSpec((n_seq, B_KVW, t), lambda i: (i, 0, 0))
    return pl.pallas_call(
        _ctx_front_kernel,
        grid=(b // n_seq,),
        in_specs=[pl.BlockSpec((tm, D_MODEL), lambda i: (i, 0)),
                  pl.BlockSpec((None, 1, N_MOD * D_MODEL), lambda i: (mod_row(i), 0, 0)),
                  _const_spec((1, D_MODEL)), _const_spec((D_MODEL, N_K)), _const_spec((N_T, D_MODEL)),
                  _const_spec((CHUNK, CHUNK)), _const_spec((2, CHUNK)), _const_spec((2, HEAD_DIM, 1)),
                  _const_spec((4, HEAD_DIM)), _const_spec((2 * HEAD_DIM, 1)),
                  _const_spec((B_HEADS, LANES)), pl.BlockSpec(memory_space=pltpu.SMEM)],
        out_specs=[tok, tok, feat_narrow, feat_wide, tok, feat_narrow],
        out_shape=[jax.ShapeDtypeStruct((rows, A_W), BF16), jax.ShapeDtypeStruct((rows, B_Q), BF16),
                   jax.ShapeDtypeStruct((b, B_KVW, t), F32), jax.ShapeDtypeStruct((b, A_W, t), F32),
                   jax.ShapeDtypeStruct((rows, A_W), F32), jax.ShapeDtypeStruct((b, B_KVW, t), F32)],
        scratch_shapes=[pltpu.VMEM((n_seq, A_W, t), BF16), pltpu.VMEM((n_seq, B_Q, t), BF16),
                        pltpu.VMEM((tm, A_W), BF16), pltpu.VMEM((tm, B_KVW), BF16),
                        pltpu.VMEM((n_seq, A_W, t), BF16)],
        compiler_params=_params(1),
        name="ctx_front",
    )(x3d.reshape(rows, D_MODEL), mod3, g1, w_k, w_t, gsum, kgains, qgains, lam_vecs, subg_col, sink_rows,
      bounds)


def _rider_specs(weights, n_steps, flat_step):
    specs = []
    for w in weights:
        rows, cols = w.shape
        bands = next(n for n in (n_steps, n_steps // 2, n_steps // 4)
                     if rows % n == 0 and (rows // n) % 16 == 0)
        repeat = n_steps // bands
        spec = pl.BlockSpec((rows // bands, cols), lambda *g, repeat=repeat: (flat_step(*g) // repeat, 0))
        specs.append((spec, jax.ShapeDtypeStruct(w.shape, BF16)))
    return specs


def _rider_pieces(rider_refs, n_items):
    n = len(rider_refs) // 2
    pieces = [(w_ref, o_ref, r) for w_ref, o_ref in zip(rider_refs[:n], rider_refs[n:])
              for r in range(0, w_ref.shape[0], 16)]

    def run(item):
        for w_ref, o_ref, r in pieces[item::n_items]:
            o_ref[r:r + 16, :] = w_ref[r:r + 16, :].astype(BF16)

    return run


def _diff_lat_kernel(qt_ref, k_ref, vt_ref, ckt_ref, cv_ref, lam_ref, subg_ref, bounds_ref, *rest, n_riders):
    o_ref = rest[n_riders]
    n_sub = qt_ref.shape[1] // DIFF_QUERIES
    cast_piece = _rider_pieces(rest[:n_riders] + rest[n_riders + 1:], n_sub)
    lam = _lambda(lam_ref)
    subg_col = subg_ref[...]
    k = k_ref[...]
    ck = ckt_ref[...].T.astype(BF16)
    vt = _with_ones(vt_ref[...])
    past = cv_ref.shape[0] // A_HEADS
    cv = cv_ref[pl.ds(pl.program_id(1), past, stride=A_HEADS), :]
    cvt = _with_ones(cv.T.astype(BF16))

    def scores(i, _):
        cast_piece(i)
        q = _diff_query_cols(qt_ref[:, i * DIFF_QUERIES:(i + 1) * DIFF_QUERIES])
        return [_dot(k, q), _dot(ck, q)]

    def outputs(shift, i, s):
        if shift:
            m = jnp.maximum(jnp.max(s[0], axis=0, keepdims=True), jnp.max(s[1], axis=0, keepdims=True))
        r = None
        for sc, v in ((s[0], vt), (s[1], cvt)):
            for c in range(0, sc.shape[0], KEY_CHUNK):
                piece = sc[c:c + KEY_CHUNK]
                part = _dot(v[:, c:c + KEY_CHUNK], jnp.exp2(piece - m if shift else piece).astype(BF16))
                r = part if r is None else r + part
        o_ref[i * DIFF_QUERIES:(i + 1) * DIFF_QUERIES, :] = _diff_output(r, lam, subg_col).astype(BF16)

    bounded = _scores_bounded(bounds_ref[0], jnp.maximum(bounds_ref[1], _max_row_sq_norm(ck)))

    @pl.when(bounded)
    def _():
        _software_pipeline(n_sub, [scores, functools.partial(outputs, False)])

    @pl.when(jnp.logical_not(bounded))
    def _():
        _software_pipeline(n_sub, [scores, functools.partial(outputs, True)])


def _diff_lat(qt, k, vt, ckt, cv, lam_vecs, subg_col, bounds, riders):
    b, _, t = qt.shape
    past = ckt.shape[2]
    rider_specs = _rider_specs(riders, b * A_HEADS, lambda i, h: i * A_HEADS + h)
    return pl.pallas_call(
        functools.partial(_diff_lat_kernel, n_riders=len(riders)),
        grid=(b, A_HEADS),
        in_specs=[pl.BlockSpec((None, LANES, t), lambda i, h: (i, h, 0)),
                  pl.BlockSpec((None, t, LANES), lambda i, h: (i, 0, h)),
                  pl.BlockSpec((None, LANES, t), lambda i, h: (i, h, 0)),
                  pl.BlockSpec((None, LANES, past), lambda i, h: (i, h, 0)),
                  pl.BlockSpec((None, past * A_HEADS, LANES), lambda i, h: (i, 0, 0)),
                  _const_spec((4, HEAD_DIM)), _const_spec((2 * HEAD_DIM, 1)),
                  pl.BlockSpec(memory_space=pltpu.SMEM)] + [spec for spec, _ in rider_specs],
        out_specs=[pl.BlockSpec((None, t, LANES), lambda i, h: (i, 0, h))] + [spec for spec, _ in rider_specs],
        out_shape=[jax.ShapeDtypeStruct((b, t, A_W), BF16)] + [shape for _, shape in rider_specs],
        compiler_params=_params(2),
        name="diff_lat",
    )(qt, k, vt, ckt, cv, lam_vecs, subg_col, bounds, *riders)


def _window_start(i, t):
    return min(max((i - 1) * BLOCK, 0), t - 3 * BLOCK)


def _band_offsets(t):
    return sorted({_window_start(i, t) - i * BLOCK for i in range(t // BLOCK)})


def _band_bias(t):
    key = np.arange(3 * BLOCK)[:, None]
    qry = np.arange(B_GROUP * BLOCK)[None, :] % BLOCK
    return jnp.asarray(np.stack([np.where(np.abs(key + off - qry) <= WINDOW, 0.0, NEG)
                                 for off in _band_offsets(t)]), dtype=F32)


def _win_lat_kernel(qt_ref, k_ref, vt_ref, ckt_ref, cvt_ref, sink_ref, bias_ref, bounds_ref, *rest, n_riders):
    o_ref = rest[n_riders]
    g = pl.program_id(1)
    t = k_ref.shape[0]
    span = 3 * BLOCK
    cast_piece = _rider_pieces(rest[:n_riders] + rest[n_riders + 1:], t // BLOCK)
    keep = lax.broadcasted_iota(jnp.int32, (1, CHUNK), 1) // HEAD_DIM == g
    k = jnp.where(keep, k_ref[...].astype(F32), 0.0).astype(BF16)
    ck = jnp.where(keep, ckt_ref[...].T, 0.0).astype(BF16)
    vt = vt_ref[...]
    cvt = _with_ones(cvt_ref[...].astype(BF16))
    sink_row = _sink_row(sink_ref, g, BLOCK)
    offsets = _band_offsets(t)

    def scores(i, _):
        cast_piece(i)
        w = _window_start(i, t)
        qs = jnp.concatenate([qt_ref[r * HEAD_DIM:(r + 1) * HEAD_DIM, i * BLOCK:(i + 1) * BLOCK]
                              for r in range(B_GROUP)], axis=1)
        wq = jnp.concatenate([qs] * B_KV_HEADS, axis=0)
        return [_dot(k[w:w + span], wq) + bias_ref[offsets.index(w - i * BLOCK)], _dot(ck, wq)]

    def probs(shift, i, s):
        if shift:
            return _column_softmax(s, extra=sink_row)
        return [jnp.exp2(piece).astype(BF16) for piece in s], jnp.zeros_like(sink_row)

    def outputs(i, st):
        es, m = st
        w = _window_start(i, t)
        r = _dot(_with_ones(vt[:, w:w + span]), es[0]) + _dot(cvt, es[1])
        o_ref[i * BLOCK:(i + 1) * BLOCK, :] = _group_output(r, m, sink_row).astype(BF16)

    bounded = _scores_bounded(bounds_ref[2], jnp.maximum(bounds_ref[3], _max_row_sq_norm(ck)), bounds_ref[4])

    @pl.when(bounded)
    def _():
        _software_pipeline(t // BLOCK, [scores, functools.partial(probs, False), outputs])

    @pl.when(jnp.logical_not(bounded))
    def _():
        _software_pipeline(t // BLOCK, [scores, functools.partial(probs, True), outputs])


def _win_lat(qt, k, vt, ckt, cvt, sink_rows, bounds, riders):
    b, _, t = qt.shape
    past = ckt.shape[2]
    bias = _band_bias(t)
    rider_specs = _rider_specs(riders, b * B_KV_HEADS, lambda i, g: i * B_KV_HEADS + g)
    return pl.pallas_call(
        functools.partial(_win_lat_kernel, n_riders=len(riders)),
        grid=(b, B_KV_HEADS),
        in_specs=[pl.BlockSpec((None, CHUNK, t), lambda i, g: (i, g, 0)),
                  pl.BlockSpec((None, t, B_KVW), lambda i, g: (i, 0, 0)),
                  pl.BlockSpec((None, HEAD_DIM, t), lambda i, g: (i, g, 0)),
                  pl.BlockSpec((None, B_KVW, past), lambda i, g: (i, 0, 0)),
                  pl.BlockSpec((None, HEAD_DIM, past), lambda i, g: (i, g, 0)),
                  _const_spec((B_HEADS, LANES)), _const_spec(bias.shape),
                  pl.BlockSpec(memory_space=pltpu.SMEM)] + [spec for spec, _ in rider_specs],
        out_specs=[pl.BlockSpec((None, t, CHUNK), lambda i, g: (i, 0, g))] + [spec for spec, _ in rider_specs],
        out_shape=[jax.ShapeDtypeStruct((b, t, B_Q), BF16)] + [shape for _, shape in rider_specs],
        compiler_params=_params(2),
        name="win_lat",
    )(qt, k, vt, ckt, cvt, sink_rows, bias, bounds, *riders)


def _merge_kernel(x1_ref, x2_ref, oa1_ref, oa2_ref, ob1_ref, ob2_ref, mod_ref, g1_ref, win_ref, woa_ref,
                  wob_ref, wout_ref, o1_ref, o2_ref, *, n_first):
    def tile(x_ref, oa_ref, ob_ref, o_ref):
        mod = mod_ref[...]
        rows = lambda i: slice(i * ROW_SUBTILE, (i + 1) * ROW_SUBTILE)

        def norm(i, _):
            return _modulated_norm(x_ref[rows(i), :], g1_ref[...], mod[:, D_MODEL:2 * D_MODEL],
                                   mod[:, 0:D_MODEL]).astype(BF16)

        def branches(i, h):
            ga = _dot(h, win_ref[:, GATE_COL:GATE_COL + D_MODEL])
            gb = _dot(h, win_ref[:, GATE_COL + D_MODEL:GATE_COL + 2 * D_MODEL])
            return (jax.nn.sigmoid(ga) * _dot(oa_ref[rows(i), :], woa_ref[...])
                    + jax.nn.sigmoid(gb) * _dot(ob_ref[rows(i), :], wob_ref[...])).astype(BF16)

        def project(i, merged):
            o_ref[rows(i), :] = x_ref[rows(i), :] + mod[:, 2 * D_MODEL:3 * D_MODEL] * _dot(merged, wout_ref[...])

        _software_pipeline(x_ref.shape[0] // ROW_SUBTILE, [norm, branches, project])

    _per_token_set(n_first, functools.partial(tile, x1_ref, oa1_ref, ob1_ref, o1_ref),
                   functools.partial(tile, x2_ref, oa2_ref, ob2_ref, o2_ref))


def _per_token_set(n_first, first_body, second_body):
    first = pl.program_id(0) < n_first
    pl.when(first)(first_body)
    pl.when(jnp.logical_not(first))(second_body)


def _split_tiles(n_first, tm, width):
    return (pl.BlockSpec((tm, width), lambda i: (jnp.minimum(i, n_first - 1), 0)),
            pl.BlockSpec((tm, width), lambda i: (jnp.maximum(i - n_first, 0), 0)))


def _merge(x1, x2, oa1, oa2, ob1, ob2, mod3, mod_row, g1, w_in, w_oa, w_ob, w_out, tm):
    n_first, n_second = x1.shape[0] // tm, x2.shape[0] // tm
    t1, t2 = _split_tiles(n_first, tm, D_MODEL)
    sq = _const_spec((D_MODEL, D_MODEL))
    return pl.pallas_call(
        functools.partial(_merge_kernel, n_first=n_first),
        grid=(n_first + n_second,),
        in_specs=[t1, t2, t1, t2, t1, t2,
                  pl.BlockSpec((None, 1, N_MOD * D_MODEL), lambda i: (mod_row(i), 0, 0)),
                  _const_spec((1, D_MODEL)), _const_spec(w_in.shape), sq, sq, sq],
        out_specs=[t1, t2],
        out_shape=[jax.ShapeDtypeStruct(x1.shape, F32), jax.ShapeDtypeStruct(x2.shape, F32)],
        compiler_params=_params(1),
        name="merge_out",
    )(x1, x2, oa1, oa2, ob1, ob2, mod3, g1, w_in, w_oa, w_ob, w_out)


def _ffn_kernel(x1_ref, x2_ref, mod_ref, g2_ref, wgate_ref, wup_ref, wdown_ref, o1_ref, o2_ref, *, n_first):
    def tile(x_ref, o_ref):
        mod = mod_ref[...]
        rows = lambda i: slice(i * FFN_SUBTILE, (i + 1) * FFN_SUBTILE)

        def norm(i, _):
            return _modulated_norm(x_ref[rows(i), :], g2_ref[...], mod[:, 4 * D_MODEL:5 * D_MODEL],
                                   mod[:, 3 * D_MODEL:4 * D_MODEL]).astype(BF16)

        def hidden(i, h):
            gate = _dot(h, wgate_ref[...])
            return (gate * jax.nn.sigmoid(gate) * _dot(h, wup_ref[...])).astype(BF16)

        def project(i, act):
            o_ref[rows(i), :] = x_ref[rows(i), :] + mod[:, 5 * D_MODEL:6 * D_MODEL] * _dot(act, wdown_ref[...])

        _software_pipeline(x_ref.shape[0] // FFN_SUBTILE, [norm, hidden, project])

    _per_token_set(n_first, functools.partial(tile, x1_ref, o1_ref), functools.partial(tile, x2_ref, o2_ref))


def _ffn(x1, x2, mod3, mod_row, g2, w_gate, w_up, w_down, tm):
    n_first, n_second = x1.shape[0] // tm, x2.shape[0] // tm
    t1, t2 = _split_tiles(n_first, tm, D_MODEL)
    return pl.pallas_call(
        functools.partial(_ffn_kernel, n_first=n_first),
        grid=(n_first + n_second,),
        in_specs=[t1, t2,
                  pl.BlockSpec((None, 1, N_MOD * D_MODEL), lambda i: (mod_row(i), 0, 0)),
                  _const_spec((1, D_MODEL)), _const_spec((D_MODEL, D_FF)),
                  _const_spec((D_MODEL, D_FF)), _const_spec((D_FF, D_MODEL))],
        out_specs=[t1, t2],
        out_shape=[jax.ShapeDtypeStruct(x1.shape, F32), jax.ShapeDtypeStruct(x2.shape, F32)],
        compiler_params=_params(1),
        name="swiglu_ffn",
    )(x1, x2, mod3, g2, w_gate, w_up, w_down)


def _rope_tables(seq_len, tm):
    quarter = HEAD_DIM // 4
    t = np.arange(seq_len)
    row = (t // GRID_W).astype(np.float32)
    col = (t % GRID_W).astype(np.float32)
    freqs = (np.float32(ROPE_BASE) ** (-np.arange(quarter, dtype=np.float32) / np.float32(quarter))
             ).astype(np.float32)
    ang_r = row[:, None] * freqs
    ang_c = col[:, None] * freqs
    zeros = np.zeros_like(ang_r)
    cos = np.concatenate([np.cos(ang_r)] * 2 + [np.cos(ang_c)] * 2, axis=-1)
    sin_next = np.concatenate([-np.sin(ang_r), zeros, -np.sin(ang_c), zeros], axis=-1)
    sin_prev = np.concatenate([zeros, np.sin(ang_r), zeros, np.sin(ang_c)], axis=-1)
    rep = LANES // HEAD_DIM
    tiles = seq_len // tm
    nat = tuple(jnp.asarray(np.tile(a, (1, rep)).reshape(tiles, tm, LANES), dtype=F32)
                for a in (cos, sin_next, sin_prev))
    featm = tuple(jnp.asarray(a.reshape(tiles, tm, HEAD_DIM).transpose(0, 2, 1), dtype=F32)
                  for a in (cos, sin_next + sin_prev))
    return nat, featm


def _group_sum_matrix():
    idx = np.arange(CHUNK) // HEAD_DIM
    return jnp.asarray(idx[:, None] == idx[None, :], dtype=BF16)


def kernel(x_prompt, x_sample, cache_diff_k, cache_diff_v, cache_win_k, cache_win_v, c, c_ctx,
           w_ada, b_ada, norm1_g, w_in, qn_a, kn_a, lambda_q1, lambda_k1, lambda_q2, lambda_k2,
           subln_g, qn_b, kn_b, sink, w_oa, w_ob, w_out, norm2_g, w_gate, w_up, w_down):
    batch, seq, _ = x_prompt.shape
    dec_batch, dec_seq, _ = x_sample.shape
    past = cache_diff_k.shape[2]
    l = 0

    cc = jnp.concatenate([c, c_ctx[None, :], jnp.zeros((8 - dec_batch - 1, D_MODEL), F32)], axis=0)
    mod3 = _modulation(cc, w_ada[l], b_ada[l][None, :])

    w_in_b = w_in[l].astype(BF16)
    o_ka, o_va, o_qb, o_kb, o_vb = A_W, 2 * A_W, 3 * A_W, 3 * A_W + B_Q, 3 * A_W + B_Q + B_KVW
    w_k = jnp.concatenate([w_in_b[:, o_ka:o_va], w_in_b[:, o_kb:o_vb]], axis=1)
    w_t = jnp.concatenate([w_in_b[:, :o_ka], w_in_b[:, o_qb:o_kb], w_in_b[:, o_va:o_qb],
                           w_in_b[:, o_vb:GATE_COL]], axis=1).T
    g1 = norm1_g[l][None, :]
    g2 = norm2_g[l][None, :]
    rep = CHUNK // HEAD_DIM
    kgains = jnp.stack([jnp.tile(kn_a[l], rep), jnp.tile(kn_b[l], rep)])
    qgains = jnp.stack([qn_a[l], qn_b[l]])[:, :, None]
    lam_vecs = jnp.stack([lambda_q1[l], lambda_k1[l], lambda_q2[l], lambda_k2[l]])
    subg_col = subln_g[l][:, None]
    sink_rows = jnp.broadcast_to((sink[l] * LOG2E)[:, None], (B_HEADS, LANES))
    head_gains = jnp.stack([qn_a[l], kn_a[l], qn_b[l], kn_b[l]])
    head_scale = jnp.asarray([Q_SCALE, 1.0, Q_SCALE, 1.0], F32)
    bounds = jnp.concatenate([1.02 * HEAD_DIM * jnp.max(jnp.square(head_gains), axis=1) * jnp.square(head_scale),
                              jnp.max(jnp.abs(sink[l]), keepdims=True) * LOG2E]).astype(F32)
    gsum = _group_sum_matrix()
    tm = 512

    xp = x_prompt.reshape(batch * seq, D_MODEL)
    ctx_row = lambda i: dec_batch
    oa_p, ob_p, vbt, kat_f, va_f, kbt_f = _ctx_front(
        x_prompt, mod3, ctx_row, g1, w_k, w_t, gsum, kgains, qgains, lam_vecs, subg_col, sink_rows, bounds,
        tm // seq)
    new_diff_k = kat_f.reshape(batch, A_HEADS, 2, HEAD_DIM, seq).transpose(0, 4, 1, 2, 3)[:, None]
    new_diff_v = va_f.reshape(batch, 1, seq, A_HEADS, 2 * HEAD_DIM)
    new_win_k = kbt_f.reshape(batch, B_KV_HEADS, HEAD_DIM, seq).transpose(0, 3, 1, 2)[:, None]
    new_win_v = vbt.reshape(batch, B_KV_HEADS, HEAD_DIM, seq).transpose(0, 3, 1, 2)[:, None]

    xs = x_sample.reshape(dec_batch * dec_seq, D_MODEL)
    lat_row = lambda i: i // (dec_seq // tm)
    qat, qbt, ka, kb, vat, vbt = _qkv_rope(
        x_sample, mod3, lat_row, g1, w_k, w_t, gsum, kgains, qgains, _rope_tables(dec_seq, tm), tm)
    ckt_a = cache_diff_k[:, l].transpose(0, 2, 3, 4, 1).reshape(dec_batch, A_W, past)
    cv_a = cache_diff_v[:, l].reshape(dec_batch, past * A_HEADS, 2 * HEAD_DIM)
    ckt_b = cache_win_k[:, l].transpose(0, 2, 3, 1).reshape(dec_batch, B_KVW, past)
    cvt_b = cache_win_v[:, l].transpose(0, 2, 3, 1).reshape(dec_batch, B_KVW, past)
    oa_s, w_gate_b, w_up_b, w_down_b = _diff_lat(
        qat, ka.reshape(dec_batch, dec_seq, A_W), vat, ckt_a, cv_a, lam_vecs, subg_col, bounds,
        [w_gate[l], w_up[l], w_down[l]])
    ob_s, w_oa_b, w_ob_b, w_out_b = _win_lat(
        qbt, kb.reshape(dec_batch, dec_seq, B_KVW), vbt, ckt_b, cvt_b, sink_rows, bounds,
        [w_oa[l], w_ob[l], w_out[l]])

    n_ctx_tiles = batch * seq // tm
    both_row = lambda i: jnp.where(i < n_ctx_tiles, dec_batch, (i - n_ctx_tiles) // (dec_seq // tm))
    xp1, xs1 = _merge(xp, xs, oa_p, oa_s.reshape(-1, A_W), ob_p, ob_s.reshape(-1, B_Q), mod3, both_row, g1,
                      w_in_b, w_oa_b, w_ob_b, w_out_b, tm)
    y_prompt, y_sample = _ffn(xp1, xs1, mod3, both_row, g2, w_gate_b, w_up_b, w_down_b, tm)

    return (y_prompt.reshape(batch, seq, D_MODEL),
            y_sample.reshape(dec_batch, dec_seq, D_MODEL),
            new_diff_k, new_diff_v, new_win_k, new_win_v)
```

```python
import functools
import math

import jax
import jax.numpy as jnp
import numpy as np
from jax import lax
from jax.experimental import pallas as pl
from jax.experimental.pallas import tpu as pltpu

D_MODEL = 1024
HEAD_DIM = 64
A_HEADS = 8
B_HEADS = 16
B_KV_HEADS = 4
B_GROUP = B_HEADS // B_KV_HEADS
A_W = A_HEADS * 2 * HEAD_DIM
B_Q = B_HEADS * HEAD_DIM
B_KVW = B_KV_HEADS * HEAD_DIM
N_K = A_W + B_KVW
N_T = A_W + B_Q + A_W + B_KVW
GATE_COL = 3 * A_W + B_Q + 2 * B_KVW
WINDOW = 128
BLOCK = 128
GRID_W = 64
D_FF = 2816
N_MOD = 6
EPS = 1e-6
ROPE_BASE = 10000.0
NEG = -1e30
LAMBDA_INIT = 0.8 - 0.6 * math.exp(-0.3 * 0)
LOG2E = math.log2(math.e)
Q_SCALE = HEAD_DIM ** -0.5 * LOG2E

LANES = 128
CHUNK = 256
ONES_ROWS = 16
KEY_CHUNK = 512
QKV_SUBTILE = 512
ROW_SUBTILE = 256
FFN_SUBTILE = 128
WIN_GROUPS_PER_STEP = 2
DIFF_QUERIES = 128
SHIFT_FREE_LOG2 = 60.0
VMEM_LIMIT = 56 * 1024 * 1024

F32 = jnp.float32
BF16 = jnp.bfloat16


def _params(n_axes):
    return pltpu.CompilerParams(dimension_semantics=("arbitrary",) * n_axes,
                                vmem_limit_bytes=VMEM_LIMIT)


def _const_spec(shape):
    nd = len(shape)
    return pl.BlockSpec(shape, lambda *_: (0,) * nd, pipeline_mode=pl.Buffered(1))


def _dot(a, b):
    return jnp.dot(a, b, preferred_element_type=F32)


def _modulated_norm(x, gain, scale, shift):
    y = x * lax.rsqrt(jnp.mean(x * x, axis=-1, keepdims=True) + EPS) * gain
    return y * (1.0 + scale) + shift


def _software_pipeline(n, stages):
    depth = len(stages)
    state = [None] * n
    for step in range(n + depth - 1):
        for k, stage in enumerate(stages):
            i = step - k
            if 0 <= i < n:
                state[i] = stage(i, state[i])


def _mod_kernel(c_ref, w_ref, b_ref, o_ref):
    cc = c_ref[...]
    s = cc * jax.nn.sigmoid(cc)
    mod = _dot(s.astype(BF16), w_ref[...].astype(BF16)) + b_ref[...]
    for r in range(mod.shape[0]):
        o_ref[r] = mod[r:r + 1, :]


def _modulation(cc, w_ada, b_ada):
    rows, n = cc.shape[0], w_ada.shape[1]
    tn = n // 4
    return pl.pallas_call(
        _mod_kernel,
        grid=(n // tn,),
        in_specs=[pl.BlockSpec((rows, D_MODEL), lambda j: (0, 0)),
                  pl.BlockSpec((D_MODEL, tn), lambda j: (0, j)),
                  pl.BlockSpec((1, tn), lambda j: (0, j))],
        out_specs=pl.BlockSpec((rows, 1, tn), lambda j: (0, 0, j)),
        out_shape=jax.ShapeDtypeStruct((rows, 1, n), F32),
        compiler_params=_params(1),
        name="adaln_mod",
    )(cc, w_ada, b_ada)


def _qkv_kernel(*refs, rope, emit_new_kv):
    x_ref, mod_ref, g1_ref, wk_ref, wt_ref, gsum_ref, kgain_ref, qgain_ref = refs[:8]
    refs = refs[8:]
    if rope:
        cos_ref, sa_ref, sb_ref, cost_ref, sint_ref = refs[:5]
        refs = refs[5:]
    qat_ref, qbt_ref, ka_ref, kb_ref, vat_ref, vbt_ref = refs[:6]
    if emit_new_kv:
        kat_ref, va_ref, kbt_ref = refs[6:]

    mod = mod_ref[...]
    gsum = gsum_ref[...]
    n_wide = A_W // CHUNK
    key_cols = [j * CHUNK for j in range(n_wide)] + [A_W]
    n_feat = N_T // CHUNK
    sub = min(QKV_SUBTILE, x_ref.shape[0])

    def normed(u):
        h = _modulated_norm(x_ref[u * sub:(u + 1) * sub, :], g1_ref[...], mod[:, D_MODEL:2 * D_MODEL],
                            mod[:, 0:D_MODEL])
        return h.astype(BF16), h.T.astype(BF16)

    def project(u, hb, ht):
        tok = slice(u * sub, (u + 1) * sub)

        def put_feat(ref, rows, val):
            width = ref.shape[2]
            step = min(sub, width)
            for off in range(0, sub, step):
                s, col = divmod(u * sub + off, width)
                ref[s, rows, col:col + step] = val[:, off:off + step].astype(ref.dtype)

        def key_project(j, _):
            return _dot(hb, wk_ref[:, key_cols[j]:key_cols[j] + CHUNK])

        def key_squares(j, p):
            return p, _dot((p * p).astype(BF16), gsum)

        def key_finish(j, st):
            p, ss = st
            row = 0 if j < n_wide else 1
            p = p * lax.rsqrt(ss * (1.0 / HEAD_DIM) + EPS) * kgain_ref[row:row + 1, :]
            if rope:
                cos, sa, sb = cos_ref[tok, :], sa_ref[tok, :], sb_ref[tok, :]
                halves = []
                for i in range(CHUNK // LANES):
                    xh = p[:, i * LANES:(i + 1) * LANES]
                    nxt = pltpu.roll(xh, LANES - HEAD_DIM // 4, 1)
                    prv = pltpu.roll(xh, HEAD_DIM // 4, 1)
                    halves.append(xh * cos + nxt * sa + prv * sb)
                p = jnp.concatenate(halves, axis=1)
            if j < n_wide:
                sl = slice(j * CHUNK, (j + 1) * CHUNK)
                ka_ref[tok, sl] = p.astype(BF16)
                if emit_new_kv:
                    put_feat(kat_ref, sl, p.T)
            else:
                kb_ref[tok, :] = p.astype(BF16)
                if emit_new_kv:
                    put_feat(kbt_ref, slice(None), p.T)

        _software_pipeline(len(key_cols), [key_project, key_squares, key_finish])

        def feat_project(c, _):
            return _dot(wt_ref[c * CHUNK:(c + 1) * CHUNK, :], ht)

        def feat_finish(c, p):
            kind, j = divmod(c, n_wide)
            sl = slice(j * CHUNK, (j + 1) * CHUNK)
            if kind == 2:
                put_feat(vat_ref, sl, p)
                if emit_new_kv:
                    va_ref[tok, sl] = p.T
                return
            if kind == 3:
                put_feat(vbt_ref, slice(None), p)
                return
            gain = qgain_ref[kind]
            q4 = HEAD_DIM // 4
            heads = []
            for r in range(CHUNK // HEAD_DIM):
                x = p[r * HEAD_DIM:(r + 1) * HEAD_DIM, :]
                x = x * lax.rsqrt(jnp.mean(x * x, axis=0, keepdims=True) + EPS) * gain
                if rope:
                    swapped = jnp.concatenate([x[q4:2 * q4], x[0:q4], x[3 * q4:4 * q4], x[2 * q4:3 * q4]],
                                              axis=0)
                    x = x * cost_ref[:, tok] + swapped * sint_ref[:, tok]
                heads.append(x)
            put_feat(qat_ref if kind == 0 else qbt_ref, sl, jnp.concatenate(heads, axis=0) * Q_SCALE)

        _software_pipeline(n_feat, [feat_project, feat_finish])

    n_sub = x_ref.shape[0] // sub
    nxt = normed(0)
    for u in range(n_sub):
        cur, nxt = nxt, (normed(u + 1) if u + 1 < n_sub else None)
        project(u, *cur)


def _qkv_rope(x3d, mod3, mod_row, g1, w_k, w_t, gsum, kgains, qgains, rope_tabs, tm):
    b, t, _ = x3d.shape
    rows = b * t
    tps = t // tm
    nat, featm = rope_tabs
    tok_wide = pl.BlockSpec((tm, A_W), lambda i: (i, 0))
    tok_narrow = pl.BlockSpec((tm, B_KVW), lambda i: (i, 0))
    feat_wide = pl.BlockSpec((1, A_W, tm), lambda i: (i // tps, 0, i % tps))
    feat_narrow = pl.BlockSpec((1, B_KVW, tm), lambda i: (i // tps, 0, i % tps))
    return pl.pallas_call(
        functools.partial(_qkv_kernel, rope=True, emit_new_kv=False),
        grid=(rows // tm,),
        in_specs=[pl.BlockSpec((tm, D_MODEL), lambda i: (i, 0)),
                  pl.BlockSpec((None, 1, N_MOD * D_MODEL), lambda i: (mod_row(i), 0, 0)),
                  _const_spec((1, D_MODEL)), _const_spec((D_MODEL, N_K)), _const_spec((N_T, D_MODEL)),
                  _const_spec((CHUNK, CHUNK)), _const_spec((2, CHUNK)), _const_spec((2, HEAD_DIM, 1))]
        + [pl.BlockSpec((tm, LANES), lambda i: (i % tps, 0))] * 3
        + [pl.BlockSpec((HEAD_DIM, tm), lambda i: (0, i % tps))] * 2,
        out_specs=[feat_wide, feat_wide, tok_wide, tok_narrow, feat_wide, feat_narrow],
        out_shape=[jax.ShapeDtypeStruct((b, A_W, t), BF16), jax.ShapeDtypeStruct((b, B_Q, t), BF16),
                   jax.ShapeDtypeStruct((rows, A_W), BF16), jax.ShapeDtypeStruct((rows, B_KVW), BF16),
                   jax.ShapeDtypeStruct((b, A_W, t), BF16), jax.ShapeDtypeStruct((b, B_KVW, t), BF16)],
        compiler_params=_params(1),
        name="qkv_rope",
    )(x3d.reshape(rows, D_MODEL), mod3, g1, w_k, w_t, gsum, kgains, qgains, *nat, *featm)


def _lambda(lam_ref):
    lv = lam_ref[...]
    t1 = jnp.sum(lv[0:1] * lv[1:2], axis=-1, keepdims=True)
    t2 = jnp.sum(lv[2:3] * lv[3:4], axis=-1, keepdims=True)
    return jnp.exp(t1) - jnp.exp(t2) + LAMBDA_INIT


def _with_ones(vt):
    return jnp.concatenate([vt, jnp.ones((ONES_ROWS, vt.shape[1]), vt.dtype)], axis=0)


def _column_softmax(scores, extra=None):
    m = functools.reduce(jnp.maximum, [jnp.max(s, axis=0, keepdims=True) for s in scores])
    if extra is not None:
        m = jnp.maximum(m, extra)
    return [jnp.exp2(s - m).astype(BF16) for s in scores], m


def _max_row_sq_norm(x):
    x = x.astype(F32)
    return jnp.max(jnp.sum(x * x, axis=1, keepdims=True), axis=0, keepdims=True)[0, 0]


def _scores_bounded(q_sq, k_sq, sink_abs=0.0):
    limit = SHIFT_FREE_LOG2
    return jnp.logical_and(q_sq * k_sq <= limit * limit, sink_abs <= limit)


def _diff_query_cols(qt):
    zero = jnp.zeros((HEAD_DIM, qt.shape[1]), qt.dtype)
    return jnp.concatenate([jnp.concatenate([qt[:HEAD_DIM], zero], axis=0),
                            jnp.concatenate([zero, qt[HEAD_DIM:]], axis=0)], axis=1)


def _diff_output(r, lam, subg_col):
    n = r.shape[1] // 2
    dv = 2 * HEAD_DIM
    tot = r[dv:dv + 1, :]
    o = r[:dv, :n] * (1.0 / tot[:, :n]) - r[:dv, n:] * (lam / tot[:, n:])
    o = o * lax.rsqrt(jnp.mean(o * o, axis=0, keepdims=True) + EPS)
    return (o * subg_col * (1.0 - LAMBDA_INIT)).T


def _group_output(r, m, sink_row):
    n = r.shape[1] // B_GROUP
    tot = r[HEAD_DIM:HEAD_DIM + 1, :] + jnp.exp2(sink_row - m)
    o = r[:HEAD_DIM, :] * (1.0 / tot)
    return jnp.concatenate([o[:, i * n:(i + 1) * n] for i in range(B_GROUP)], axis=0).T


def _sink_row(sink_ref, g, n):
    reps = n // LANES
    return jnp.concatenate([sink_ref[pl.ds(g * B_GROUP + r, 1), :] for r in range(B_GROUP)
                            for _ in range(reps)], axis=1)


def _ctx_attn_kernel(qat_ref, qbt_ref, ka_ref, kb_ref, vat_ref, vbt_ref, lam_ref, subg_ref, sink_ref,
                     bounds_ref, oa_ref, ob_ref):
    lam = _lambda(lam_ref)
    subg_col = subg_ref[...]
    n_seq, _, n = qat_ref.shape
    chains = A_HEADS + B_KV_HEADS

    def scores(item, _):
        s, c = divmod(item, chains)
        rows = slice(s * n, (s + 1) * n)
        if c < A_HEADS:
            sl = slice(c * LANES, (c + 1) * LANES)
            return _dot(ka_ref[rows, sl], _diff_query_cols(qat_ref[s, sl, :]))
        g = c - A_HEADS
        qs = jnp.concatenate([qbt_ref[s, (g * B_GROUP + r) * HEAD_DIM:(g * B_GROUP + r + 1) * HEAD_DIM, :]
                              for r in range(B_GROUP)], axis=1)
        zero = jnp.zeros_like(qs)
        w = jnp.concatenate([qs if i == g else zero for i in range(B_KV_HEADS)], axis=0)
        return _dot(kb_ref[rows, :], w)

    def probs(shift, item, sc):
        c = item % chains
        if not shift:
            return [jnp.exp2(sc).astype(BF16)], jnp.zeros((1, sc.shape[1]), F32)
        if c < A_HEADS:
            return _column_softmax([sc])
        return _column_softmax([sc], extra=_sink_row(sink_ref, c - A_HEADS, n))

    def outputs(item, st):
        s, c = divmod(item, chains)
        rows = slice(s * n, (s + 1) * n)
        es, m = st
        if c < A_HEADS:
            sl = slice(c * LANES, (c + 1) * LANES)
            r = _dot(_with_ones(vat_ref[s, sl, :]), es[0])
            oa_ref[rows, sl] = _diff_output(r, lam, subg_col).astype(BF16)
        else:
            g = c - A_HEADS
            vt = vbt_ref[s, g * HEAD_DIM:(g + 1) * HEAD_DIM, :].astype(BF16)
            r = _dot(_with_ones(vt), es[0])
            o = _group_output(r, m, _sink_row(sink_ref, g, n))
            ob_ref[rows, g * CHUNK:(g + 1) * CHUNK] = o.astype(BF16)

    bounded = jnp.logical_and(_scores_bounded(bounds_ref[0], bounds_ref[1]),
                              _scores_bounded(bounds_ref[2], bounds_ref[3], bounds_ref[4]))

    @pl.when(bounded)
    def _():
        _software_pipeline(n_seq * chains, [scores, functools.partial(probs, False), outputs])

    @pl.when(jnp.logical_not(bounded))
    def _():
        _software_pipeline(n_seq * chains, [scores, functools.partial(probs, True), outputs])


def _ctx_front_kernel(x_ref, mod_ref, g1_ref, wk_ref, wt_ref, gsum_ref, kgain_ref, qgain_ref,
                      lam_ref, subg_ref, sink_ref, bounds_ref,
                      oa_ref, ob_ref, vbt_ref, kat_ref, va_ref, kbt_ref,
                      qat_s, qbt_s, ka_s, kb_s, vat_s):
    _qkv_kernel(x_ref, mod_ref, g1_ref, wk_ref, wt_ref, gsum_ref, kgain_ref, qgain_ref,
                qat_s, qbt_s, ka_s, kb_s, vat_s, vbt_ref, kat_ref, va_ref, kbt_ref,
                rope=False, emit_new_kv=True)
    _ctx_attn_kernel(qat_s, qbt_s, ka_s, kb_s, vat_s, vbt_ref, lam_ref, subg_ref, sink_ref, bounds_ref,
                     oa_ref, ob_ref)


def _ctx_front(x3d, mod3, mod_row, g1, w_k, w_t, gsum, kgains, qgains, lam_vecs, subg_col, sink_rows, bounds,
               n_seq):
    b, t, _ = x3d.shape
    rows, tm = b * t, n_seq * t
    tok = pl.BlockSpec((tm, A_W), lambda i: (i, 0))
    feat_wide = pl.BlockSpec((n_seq, A_W, t), lambda i: (i, 0, 0))
    feat_narrow = pl.BlockSpec((n_seq, B_KVW, t), lambda i: (i, 0, 0))
    return pl.pallas_call(
        _ctx_front_kernel,
        grid=(b // n_seq,),
        in_specs=[pl.BlockSpec((tm, D_MODEL), lambda i: (i, 0)),
                  pl.BlockSpec((None, 1, N_MOD * D_MODEL), lambda i: (mod_row(i), 0, 0)),
                  _const_spec((1, D_MODEL)), _const_spec((D_MODEL, N_K)), _const_spec((N_T, D_MODEL)),
                  _const_spec((CHUNK, CHUNK)), _const_spec((2, CHUNK)), _const_spec((2, HEAD_DIM, 1)),
                  _const_spec((4, HEAD_DIM)), _const_spec((2 * HEAD_DIM, 1)),
                  _const_spec((B_HEADS, LANES)), pl.BlockSpec(memory_space=pltpu.SMEM)],
        out_specs=[tok, tok, feat_narrow, feat_wide, tok, feat_narrow],
        out_shape=[jax.ShapeDtypeStruct((rows, A_W), BF16), jax.ShapeDtypeStruct((rows, B_Q), BF16),
                   jax.ShapeDtypeStruct((b, B_KVW, t), F32), jax.ShapeDtypeStruct((b, A_W, t), F32),
                   jax.ShapeDtypeStruct((rows, A_W), F32), jax.ShapeDtypeStruct((b, B_KVW, t), F32)],
        scratch_shapes=[pltpu.VMEM((n_seq, A_W, t), BF16), pltpu.VMEM((n_seq, B_Q, t), BF16),
                        pltpu.VMEM((tm, A_W), BF16), pltpu.VMEM((tm, B_KVW), BF16),
                        pltpu.VMEM((n_seq, A_W, t), BF16)],
        compiler_params=_params(1),
        name="ctx_front",
    )(x3d.reshape(rows, D_MODEL), mod3, g1, w_k, w_t, gsum, kgains, qgains, lam_vecs, subg_col, sink_rows,
      bounds)


def _rider_specs(weights, n_steps, flat_step):
    specs = []
    for w in weights:
        rows, cols = w.shape
        bands = next(n for n in (n_steps, n_steps // 2, n_steps // 4)
                     if rows % n == 0 and (rows // n) % 16 == 0)
        repeat = n_steps // bands
        spec = pl.BlockSpec((rows // bands, cols), lambda *g, repeat=repeat: (flat_step(*g) // repeat, 0))
        specs.append((spec, jax.ShapeDtypeStruct(w.shape, BF16)))
    return specs


def _rider_pieces(rider_refs, n_items):
    n = len(rider_refs) // 2
    pieces = [(w_ref, o_ref, r) for w_ref, o_ref in zip(rider_refs[:n], rider_refs[n:])
              for r in range(0, w_ref.shape[0], 16)]

    def run(item):
        for w_ref, o_ref, r in pieces[item::n_items]:
            o_ref[r:r + 16, :] = w_ref[r:r + 16, :].astype(BF16)

    return run


def _diff_lat_kernel(qt_ref, k_ref, vt_ref, ckt_ref, cv_ref, lam_ref, subg_ref, bounds_ref, *rest, n_riders):
    o_ref = rest[n_riders]
    n_sub = qt_ref.shape[1] // DIFF_QUERIES
    cast_piece = _rider_pieces(rest[:n_riders] + rest[n_riders + 1:], n_sub)
    lam = _lambda(lam_ref)
    subg_col = subg_ref[...]
    k = k_ref[...]
    ck = ckt_ref[...].T.astype(BF16)
    vt = _with_ones(vt_ref[...])
    past = cv_ref.shape[0] // A_HEADS
    cv = cv_ref[pl.ds(pl.program_id(1), past, stride=A_HEADS), :]
    cvt = _with_ones(cv.T.astype(BF16))

    def scores(i, _):
        cast_piece(i)
        q = _diff_query_cols(qt_ref[:, i * DIFF_QUERIES:(i + 1) * DIFF_QUERIES])
        return [_dot(k, q), _dot(ck, q)]

    def outputs(shift, i, s):
        if shift:
            m = jnp.maximum(jnp.max(s[0], axis=0, keepdims=True), jnp.max(s[1], axis=0, keepdims=True))
        r = None
        for sc, v in ((s[0], vt), (s[1], cvt)):
            for c in range(0, sc.shape[0], KEY_CHUNK):
                piece = sc[c:c + KEY_CHUNK]
                part = _dot(v[:, c:c + KEY_CHUNK], jnp.exp2(piece - m if shift else piece).astype(BF16))
                r = part if r is None else r + part
        o_ref[i * DIFF_QUERIES:(i + 1) * DIFF_QUERIES, :] = _diff_output(r, lam, subg_col).astype(BF16)

    bounded = _scores_bounded(bounds_ref[0], jnp.maximum(bounds_ref[1], _max_row_sq_norm(ck)))

    @pl.when(bounded)
    def _():
        _software_pipeline(n_sub, [scores, functools.partial(outputs, False)])

    @pl.when(jnp.logical_not(bounded))
    def _():
        _software_pipeline(n_sub, [scores, functools.partial(outputs, True)])


def _diff_lat(qt, k, vt, ckt, cv, lam_vecs, subg_col, bounds, riders):
    b, _, t = qt.shape
    past = ckt.shape[2]
    rider_specs = _rider_specs(riders, b * A_HEADS, lambda i, h: i * A_HEADS + h)
    return pl.pallas_call(
        functools.partial(_diff_lat_kernel, n_riders=len(riders)),
        grid=(b, A_HEADS),
        in_specs=[pl.BlockSpec((None, LANES, t), lambda i, h: (i, h, 0)),
                  pl.BlockSpec((None, t, LANES), lambda i, h: (i, 0, h)),
                  pl.BlockSpec((None, LANES, t), lambda i, h: (i, h, 0)),
                  pl.BlockSpec((None, LANES, past), lambda i, h: (i, h, 0)),
                  pl.BlockSpec((None, past * A_HEADS, LANES), lambda i, h: (i, 0, 0)),
                  _const_spec((4, HEAD_DIM)), _const_spec((2 * HEAD_DIM, 1)),
                  pl.BlockSpec(memory_space=pltpu.SMEM)] + [spec for spec, _ in rider_specs],
        out_specs=[pl.BlockSpec((None, t, LANES), lambda i, h: (i, 0, h))] + [spec for spec, _ in rider_specs],
        out_shape=[jax.ShapeDtypeStruct((b, t, A_W), BF16)] + [shape for _, shape in rider_specs],
        compiler_params=_params(2),
        name="diff_lat",
    )(qt, k, vt, ckt, cv, lam_vecs, subg_col, bounds, *riders)


def _window_start(i, t):
    return min(max((i - 1) * BLOCK, 0), t - 3 * BLOCK)


def _band_offsets(t):
    return sorted({_window_start(i, t) - i * BLOCK for i in range(t // BLOCK)})


def _band_bias(t):
    key = np.arange(3 * BLOCK)[:, None]
    qry = np.arange(B_GROUP * BLOCK)[None, :] % BLOCK
    return jnp.asarray(np.stack([np.where(np.abs(key + off - qry) <= WINDOW, 0.0, NEG)
                                 for off in _band_offsets(t)]), dtype=F32)


def _win_lat_kernel(qt_ref, k_ref, vt_ref, ckt_ref, cvt_ref, sink_ref, bias_ref, bounds_ref, *rest, n_riders):
    o_ref = rest[n_riders]
    t = k_ref.shape[0]
    span = 3 * BLOCK
    n_blk = t // BLOCK
    cast_piece = _rider_pieces(rest[:n_riders] + rest[n_riders + 1:], WIN_GROUPS_PER_STEP * n_blk)
    offsets = _band_offsets(t)

    def group(gi):
        g = pl.program_id(1) * WIN_GROUPS_PER_STEP + gi
        q0, v0 = gi * CHUNK, gi * HEAD_DIM
        keep = lax.broadcasted_iota(jnp.int32, (1, CHUNK), 1) // HEAD_DIM == g
        k = jnp.where(keep, k_ref[...].astype(F32), 0.0).astype(BF16)
        ck = jnp.where(keep, ckt_ref[...].T, 0.0).astype(BF16)
        vt = vt_ref[v0:v0 + HEAD_DIM, :]
        cvt = _with_ones(cvt_ref[v0:v0 + HEAD_DIM, :].astype(BF16))
        sink_row = _sink_row(sink_ref, g, BLOCK)

        def scores(i, _):
            cast_piece(gi * n_blk + i)
            w = _window_start(i, t)
            qs = jnp.concatenate([qt_ref[q0 + r * HEAD_DIM:q0 + (r + 1) * HEAD_DIM, i * BLOCK:(i + 1) * BLOCK]
                                  for r in range(B_GROUP)], axis=1)
            wq = jnp.concatenate([qs] * B_KV_HEADS, axis=0)
            return [_dot(k[w:w + span], wq) + bias_ref[offsets.index(w - i * BLOCK)], _dot(ck, wq)]

        def probs(shift, i, s):
            if shift:
                return _column_softmax(s, extra=sink_row)
            return [jnp.exp2(piece).astype(BF16) for piece in s], jnp.zeros_like(sink_row)

        def outputs(i, st):
            es, m = st
            w = _window_start(i, t)
            r = _dot(_with_ones(vt[:, w:w + span]), es[0]) + _dot(cvt, es[1])
            o_ref[i * BLOCK:(i + 1) * BLOCK, q0:q0 + CHUNK] = _group_output(r, m, sink_row).astype(BF16)

        bounded = _scores_bounded(bounds_ref[2], jnp.maximum(bounds_ref[3], _max_row_sq_norm(ck)),
                                  bounds_ref[4])

        @pl.when(bounded)
        def _():
            _software_pipeline(n_blk, [scores, functools.partial(probs, False), outputs])

        @pl.when(jnp.logical_not(bounded))
        def _():
            _software_pipeline(n_blk, [scores, functools.partial(probs, True), outputs])

    for gi in range(WIN_GROUPS_PER_STEP):
        group(gi)


def _win_lat(qt, k, vt, ckt, cvt, sink_rows, bounds, riders):
    b, _, t = qt.shape
    past = ckt.shape[2]
    bias = _band_bias(t)
    gps = WIN_GROUPS_PER_STEP
    steps = B_KV_HEADS // gps
    rider_specs = _rider_specs(riders, b * steps, lambda i, g: i * steps + g)
    return pl.pallas_call(
        functools.partial(_win_lat_kernel, n_riders=len(riders)),
        grid=(b, steps),
        in_specs=[pl.BlockSpec((None, gps * CHUNK, t), lambda i, g: (i, g, 0)),
                  pl.BlockSpec((None, t, B_KVW), lambda i, g: (i, 0, 0)),
                  pl.BlockSpec((None, gps * HEAD_DIM, t), lambda i, g: (i, g, 0)),
                  pl.BlockSpec((None, B_KVW, past), lambda i, g: (i, 0, 0)),
                  pl.BlockSpec((None, gps * HEAD_DIM, past), lambda i, g: (i, g, 0)),
                  _const_spec((B_HEADS, LANES)), _const_spec(bias.shape),
                  pl.BlockSpec(memory_space=pltpu.SMEM)] + [spec for spec, _ in rider_specs],
        out_specs=[pl.BlockSpec((None, t, gps * CHUNK), lambda i, g: (i, 0, g))]
        + [spec for spec, _ in rider_specs],
        out_shape=[jax.ShapeDtypeStruct((b, t, B_Q), BF16)] + [shape for _, shape in rider_specs],
        compiler_params=_params(2),
        name="win_lat",
    )(qt, k, vt, ckt, cvt, sink_rows, bias, bounds, *riders)


def _merge_kernel(x1_ref, x2_ref, oa1_ref, oa2_ref, ob1_ref, ob2_ref, mod_ref, g1_ref, win_ref, woa_ref,
                  wob_ref, wout_ref, o1_ref, o2_ref, *, n_first):
    def tile(x_ref, oa_ref, ob_ref, o_ref):
        mod = mod_ref[...]
        rows = lambda i: slice(i * ROW_SUBTILE, (i + 1) * ROW_SUBTILE)

        def norm(i, _):
            return _modulated_norm(x_ref[rows(i), :], g1_ref[...], mod[:, D_MODEL:2 * D_MODEL],
                                   mod[:, 0:D_MODEL]).astype(BF16)

        def branches(i, h):
            ga = _dot(h, win_ref[:, GATE_COL:GATE_COL + D_MODEL])
            gb = _dot(h, win_ref[:, GATE_COL + D_MODEL:GATE_COL + 2 * D_MODEL])
            return (jax.nn.sigmoid(ga) * _dot(oa_ref[rows(i), :], woa_ref[...])
                    + jax.nn.sigmoid(gb) * _dot(ob_ref[rows(i), :], wob_ref[...])).astype(BF16)

        def project(i, merged):
            o_ref[rows(i), :] = x_ref[rows(i), :] + mod[:, 2 * D_MODEL:3 * D_MODEL] * _dot(merged, wout_ref[...])

        _software_pipeline(x_ref.shape[0] // ROW_SUBTILE, [norm, branches, project])

    _per_token_set(n_first, functools.partial(tile, x1_ref, oa1_ref, ob1_ref, o1_ref),
                   functools.partial(tile, x2_ref, oa2_ref, ob2_ref, o2_ref))


def _per_token_set(n_first, first_body, second_body):
    first = pl.program_id(0) < n_first
    pl.when(first)(first_body)
    pl.when(jnp.logical_not(first))(second_body)


def _split_tiles(n_first, tm, width):
    return (pl.BlockSpec((tm, width), lambda i: (jnp.minimum(i, n_first - 1), 0)),
            pl.BlockSpec((tm, width), lambda i: (jnp.maximum(i - n_first, 0), 0)))


def _merge(x1, x2, oa1, oa2, ob1, ob2, mod3, mod_row, g1, w_in, w_oa, w_ob, w_out, tm):
    n_first, n_second = x1.shape[0] // tm, x2.shape[0] // tm
    t1, t2 = _split_tiles(n_first, tm, D_MODEL)
    sq = _const_spec((D_MODEL, D_MODEL))
    return pl.pallas_call(
        functools.partial(_merge_kernel, n_first=n_first),
        grid=(n_first + n_second,),
        in_specs=[t1, t2, t1, t2, t1, t2,
                  pl.BlockSpec((None, 1, N_MOD * D_MODEL), lambda i: (mod_row(i), 0, 0)),
                  _const_spec((1, D_MODEL)), _const_spec(w_in.shape), sq, sq, sq],
        out_specs=[t1, t2],
        out_shape=[jax.ShapeDtypeStruct(x1.shape, F32), jax.ShapeDtypeStruct(x2.shape, F32)],
        compiler_params=_params(1),
        name="merge_out",
    )(x1, x2, oa1, oa2, ob1, ob2, mod3, g1, w_in, w_oa, w_ob, w_out)


def _ffn_kernel(x1_ref, x2_ref, mod_ref, g2_ref, wgate_ref, wup_ref, wdown_ref, o1_ref, o2_ref, *, n_first):
    def tile(x_ref, o_ref):
        mod = mod_ref[...]
        rows = lambda i: slice(i * FFN_SUBTILE, (i + 1) * FFN_SUBTILE)

        def norm(i, _):
            return _modulated_norm(x_ref[rows(i), :], g2_ref[...], mod[:, 4 * D_MODEL:5 * D_MODEL],
                                   mod[:, 3 * D_MODEL:4 * D_MODEL]).astype(BF16)

        def hidden(i, h):
            gate = _dot(h, wgate_ref[...])
            return (gate * jax.nn.sigmoid(gate) * _dot(h, wup_ref[...])).astype(BF16)

        def project(i, act):
            o_ref[rows(i), :] = x_ref[rows(i), :] + mod[:, 5 * D_MODEL:6 * D_MODEL] * _dot(act, wdown_ref[...])

        _software_pipeline(x_ref.shape[0] // FFN_SUBTILE, [norm, hidden, project])

    _per_token_set(n_first, functools.partial(tile, x1_ref, o1_ref), functools.partial(tile, x2_ref, o2_ref))


def _ffn(x1, x2, mod3, mod_row, g2, w_gate, w_up, w_down, tm):
    n_first, n_second = x1.shape[0] // tm, x2.shape[0] // tm
    t1, t2 = _split_tiles(n_first, tm, D_MODEL)
    return pl.pallas_call(
        functools.partial(_ffn_kernel, n_first=n_first),
        grid=(n_first + n_second,),
        in_specs=[t1, t2,
                  pl.BlockSpec((None, 1, N_MOD * D_MODEL), lambda i: (mod_row(i), 0, 0)),
                  _const_spec((1, D_MODEL)), _const_spec((D_MODEL, D_FF)),
                  _const_spec((D_MODEL, D_FF)), _const_spec((D_FF, D_MODEL))],
        out_specs=[t1, t2],
        out_shape=[jax.ShapeDtypeStruct(x1.shape, F32), jax.ShapeDtypeStruct(x2.shape, F32)],
        compiler_params=_params(1),
        name="swiglu_ffn",
    )(x1, x2, mod3, g2, w_gate, w_up, w_down)


def _rope_tables(seq_len):
    quarter = HEAD_DIM // 4
    t = np.arange(seq_len)
    row = (t // GRID_W).astype(np.float32)
    col = (t % GRID_W).astype(np.float32)
    freqs = (np.float32(ROPE_BASE) ** (-np.arange(quarter, dtype=np.float32) / np.float32(quarter))
             ).astype(np.float32)
    ang_r = row[:, None] * freqs
    ang_c = col[:, None] * freqs
    zeros = np.zeros_like(ang_r)
    cos = np.concatenate([np.cos(ang_r)] * 2 + [np.cos(ang_c)] * 2, axis=-1)
    sin_next = np.concatenate([-np.sin(ang_r), zeros, -np.sin(ang_c), zeros], axis=-1)
    sin_prev = np.concatenate([zeros, np.sin(ang_r), zeros, np.sin(ang_c)], axis=-1)
    rep = LANES // HEAD_DIM
    nat = tuple(jnp.asarray(np.tile(a, (1, rep)), dtype=F32) for a in (cos, sin_next, sin_prev))
    featm = (jnp.asarray(cos.T, dtype=F32), jnp.asarray((sin_next + sin_prev).T, dtype=F32))
    return nat, featm


def _group_sum_matrix():
    idx = np.arange(CHUNK) // HEAD_DIM
    return jnp.asarray(idx[:, None] == idx[None, :], dtype=BF16)


def kernel(x_prompt, x_sample, cache_diff_k, cache_diff_v, cache_win_k, cache_win_v, c, c_ctx,
           w_ada, b_ada, norm1_g, w_in, qn_a, kn_a, lambda_q1, lambda_k1, lambda_q2, lambda_k2,
           subln_g, qn_b, kn_b, sink, w_oa, w_ob, w_out, norm2_g, w_gate, w_up, w_down):
    batch, seq, _ = x_prompt.shape
    dec_batch, dec_seq, _ = x_sample.shape
    past = cache_diff_k.shape[2]
    l = 0

    cc = jnp.concatenate([c, c_ctx[None, :], jnp.zeros((8 - dec_batch - 1, D_MODEL), F32)], axis=0)
    mod3 = _modulation(cc, w_ada[l], b_ada[l][None, :])

    w_in_b = w_in[l].astype(BF16)
    o_ka, o_va, o_qb, o_kb, o_vb = A_W, 2 * A_W, 3 * A_W, 3 * A_W + B_Q, 3 * A_W + B_Q + B_KVW
    w_k = jnp.concatenate([w_in_b[:, o_ka:o_va], w_in_b[:, o_kb:o_vb]], axis=1)
    w_t = jnp.concatenate([w_in_b[:, :o_ka], w_in_b[:, o_qb:o_kb], w_in_b[:, o_va:o_qb],
                           w_in_b[:, o_vb:GATE_COL]], axis=1).T
    g1 = norm1_g[l][None, :]
    g2 = norm2_g[l][None, :]
    rep = CHUNK // HEAD_DIM
    kgains = jnp.stack([jnp.tile(kn_a[l], rep), jnp.tile(kn_b[l], rep)])
    qgains = jnp.stack([qn_a[l], qn_b[l]])[:, :, None]
    lam_vecs = jnp.stack([lambda_q1[l], lambda_k1[l], lambda_q2[l], lambda_k2[l]])
    subg_col = subln_g[l][:, None]
    sink_rows = jnp.broadcast_to((sink[l] * LOG2E)[:, None], (B_HEADS, LANES))
    head_gains = jnp.stack([qn_a[l], kn_a[l], qn_b[l], kn_b[l]])
    head_scale = jnp.asarray([Q_SCALE, 1.0, Q_SCALE, 1.0], F32)
    bounds = jnp.concatenate([1.02 * HEAD_DIM * jnp.max(jnp.square(head_gains), axis=1) * jnp.square(head_scale),
                              jnp.max(jnp.abs(sink[l]), keepdims=True) * LOG2E]).astype(F32)
    gsum = _group_sum_matrix()
    tm = 512

    xp = x_prompt.reshape(batch * seq, D_MODEL)
    ctx_row = lambda i: dec_batch
    oa_p, ob_p, vbt, kat_f, va_f, kbt_f = _ctx_front(
        x_prompt, mod3, ctx_row, g1, w_k, w_t, gsum, kgains, qgains, lam_vecs, subg_col, sink_rows, bounds,
        tm // seq)
    new_diff_k = kat_f.reshape(batch, A_HEADS, 2, HEAD_DIM, seq).transpose(0, 4, 1, 2, 3)[:, None]
    new_diff_v = va_f.reshape(batch, 1, seq, A_HEADS, 2 * HEAD_DIM)
    new_win_k = kbt_f.reshape(batch, B_KV_HEADS, HEAD_DIM, seq).transpose(0, 3, 1, 2)[:, None]
    new_win_v = vbt.reshape(batch, B_KV_HEADS, HEAD_DIM, seq).transpose(0, 3, 1, 2)[:, None]

    xs = x_sample.reshape(dec_batch * dec_seq, D_MODEL)
    lat_row = lambda i: i // (dec_seq // tm)
    qat, qbt, ka, kb, vat, vbt = _qkv_rope(
        x_sample, mod3, lat_row, g1, w_k, w_t, gsum, kgains, qgains, _rope_tables(dec_seq), tm)
    ckt_a = cache_diff_k[:, l].transpose(0, 2, 3, 4, 1).reshape(dec_batch, A_W, past)
    cv_a = cache_diff_v[:, l].reshape(dec_batch, past * A_HEADS, 2 * HEAD_DIM)
    ckt_b = cache_win_k[:, l].transpose(0, 2, 3, 1).reshape(dec_batch, B_KVW, past)
    cvt_b = cache_win_v[:, l].transpose(0, 2, 3, 1).reshape(dec_batch, B_KVW, past)
    oa_s, w_gate_b, w_up_b, w_down_b = _diff_lat(
        qat, ka.reshape(dec_batch, dec_seq, A_W), vat, ckt_a, cv_a, lam_vecs, subg_col, bounds,
        [w_gate[l], w_up[l], w_down[l]])
    ob_s, w_oa_b, w_ob_b, w_out_b = _win_lat(
        qbt, kb.reshape(dec_batch, dec_seq, B_KVW), vbt, ckt_b, cvt_b, sink_rows, bounds,
        [w_oa[l], w_ob[l], w_out[l]])

    n_ctx_tiles = batch * seq // tm
    both_row = lambda i: jnp.where(i < n_ctx_tiles, dec_batch, (i - n_ctx_tiles) // (dec_seq // tm))
    xp1, xs1 = _merge(xp, xs, oa_p, oa_s.reshape(-1, A_W), ob_p, ob_s.reshape(-1, B_Q), mod3, both_row, g1,
                      w_in_b, w_oa_b, w_ob_b, w_out_b, tm)
    y_prompt, y_sample = _ffn(xp1, xs1, mod3, both_row, g2, w_gate_b, w_up_b, w_down_b, tm)

    return (y_prompt.reshape(batch, seq, D_MODEL),
            y_sample.reshape(dec_batch, dec_seq, D_MODEL),
            new_diff_k, new_diff_v, new_win_k, new_win_v)
```

```python
import functools
import math

import jax
import jax.numpy as jnp
import numpy as np
from jax import lax
from jax.experimental import pallas as pl
from jax.experimental.pallas import tpu as pltpu

D_MODEL = 1024
HEAD_DIM = 64
A_HEADS = 8
B_HEADS = 16
B_KV_HEADS = 4
B_GROUP = B_HEADS // B_KV_HEADS
A_W = A_HEADS * 2 * HEAD_DIM
B_Q = B_HEADS * HEAD_DIM
B_KVW = B_KV_HEADS * HEAD_DIM
N_K = A_W + B_KVW
N_T = A_W + B_Q + A_W + B_KVW
GATE_COL = 3 * A_W + B_Q + 2 * B_KVW
GATE_BLOCK = 512
WINDOW = 128
BLOCK = 128
GRID_W = 64
D_FF = 2816
N_MOD = 6
EPS = 1e-6
ROPE_BASE = 10000.0
NEG = -1e30
LAMBDA_INIT = 0.8 - 0.6 * math.exp(-0.3 * 0)
LOG2E = math.log2(math.e)
Q_SCALE = HEAD_DIM ** -0.5 * LOG2E

LANES = 128
CHUNK = 256
ONES_ROWS = 16
KEY_CHUNK = 512
QKV_SUBTILE = 512
ROW_SUBTILE = 256
FFN_SUBTILE = 128
DIFF_QUERIES = 128
SHIFT_FREE_LOG2 = 60.0
VMEM_LIMIT = 56 * 1024 * 1024

F32 = jnp.float32
BF16 = jnp.bfloat16


def _params(n_axes):
    return pltpu.CompilerParams(dimension_semantics=("arbitrary",) * n_axes,
                                vmem_limit_bytes=VMEM_LIMIT)


def _const_spec(shape):
    nd = len(shape)
    return pl.BlockSpec(shape, lambda *_: (0,) * nd, pipeline_mode=pl.Buffered(1))


def _dot(a, b):
    return jnp.dot(a, b, preferred_element_type=F32)


def _modulated_norm(x, gain, scale, shift):
    y = x * lax.rsqrt(jnp.mean(x * x, axis=-1, keepdims=True) + EPS) * gain
    return y * (1.0 + scale) + shift


def _software_pipeline(n, stages):
    depth = len(stages)
    state = [None] * n
    for step in range(n + depth - 1):
        for k, stage in enumerate(stages):
            i = step - k
            if 0 <= i < n:
                state[i] = stage(i, state[i])


def _mod_kernel(c_ref, w_ref, b_ref, o_ref):
    cc = c_ref[...]
    s = cc * jax.nn.sigmoid(cc)
    mod = _dot(s.astype(BF16), w_ref[...].astype(BF16)) + b_ref[...]
    for r in range(mod.shape[0]):
        o_ref[r] = mod[r:r + 1, :]


def _modulation(cc, w_ada, b_ada):
    rows, n = cc.shape[0], w_ada.shape[1]
    tn = n // 4
    return pl.pallas_call(
        _mod_kernel,
        grid=(n // tn,),
        in_specs=[pl.BlockSpec((rows, D_MODEL), lambda j: (0, 0)),
                  pl.BlockSpec((D_MODEL, tn), lambda j: (0, j)),
                  pl.BlockSpec((1, tn), lambda j: (0, j))],
        out_specs=pl.BlockSpec((rows, 1, tn), lambda j: (0, 0, j)),
        out_shape=jax.ShapeDtypeStruct((rows, 1, n), F32),
        compiler_params=_params(1),
        name="adaln_mod",
    )(cc, w_ada, b_ada)


def _qkv_kernel(*refs, rope, emit_new_kv):
    x_ref, mod_ref, g1_ref, wk_ref, wt_ref, gsum_ref, kgain_ref, qgain_ref = refs[:8]
    refs = refs[8:]
    if rope:
        cos_ref, sa_ref, sb_ref, cost_ref, sint_ref = refs[:5]
        refs = refs[5:]
    qat_ref, qbt_ref, ka_ref, kb_ref, vat_ref, vbt_ref = refs[:6]
    if emit_new_kv:
        kat_ref, va_ref, kbt_ref = refs[6:]

    mod = mod_ref[...]
    gsum = gsum_ref[...]
    n_wide = A_W // CHUNK
    key_cols = [j * CHUNK for j in range(n_wide)] + [A_W]
    n_feat = N_T // CHUNK
    sub = min(QKV_SUBTILE, x_ref.shape[0])

    def normed(u):
        h = _modulated_norm(x_ref[u * sub:(u + 1) * sub, :], g1_ref[...], mod[:, D_MODEL:2 * D_MODEL],
                            mod[:, 0:D_MODEL])
        return h.astype(BF16), h.T.astype(BF16)

    def project(u, hb, ht):
        tok = slice(u * sub, (u + 1) * sub)

        def put_feat(ref, rows, val):
            width = ref.shape[2]
            step = min(sub, width)
            for off in range(0, sub, step):
                s, col = divmod(u * sub + off, width)
                ref[s, rows, col:col + step] = val[:, off:off + step].astype(ref.dtype)

        def key_project(j, _):
            return _dot(hb, wk_ref[:, key_cols[j]:key_cols[j] + CHUNK])

        def key_squares(j, p):
            return p, _dot((p * p).astype(BF16), gsum)

        def key_finish(j, st):
            p, ss = st
            row = 0 if j < n_wide else 1
            p = p * lax.rsqrt(ss * (1.0 / HEAD_DIM) + EPS) * kgain_ref[row:row + 1, :]
            if rope:
                cos, sa, sb = cos_ref[tok, :], sa_ref[tok, :], sb_ref[tok, :]
                halves = []
                for i in range(CHUNK // LANES):
                    xh = p[:, i * LANES:(i + 1) * LANES]
                    nxt = pltpu.roll(xh, LANES - HEAD_DIM // 4, 1)
                    prv = pltpu.roll(xh, HEAD_DIM // 4, 1)
                    halves.append(xh * cos + nxt * sa + prv * sb)
                p = jnp.concatenate(halves, axis=1)
            if j < n_wide:
                sl = slice(j * CHUNK, (j + 1) * CHUNK)
                ka_ref[tok, sl] = p.astype(BF16)
                if emit_new_kv:
                    put_feat(kat_ref, sl, p.T)
            else:
                kb_ref[tok, :] = p.astype(BF16)
                if emit_new_kv:
                    put_feat(kbt_ref, slice(None), p.T)

        _software_pipeline(len(key_cols), [key_project, key_squares, key_finish])

        def feat_project(c, _):
            return _dot(wt_ref[c * CHUNK:(c + 1) * CHUNK, :], ht)

        def feat_finish(c, p):
            kind, j = divmod(c, n_wide)
            sl = slice(j * CHUNK, (j + 1) * CHUNK)
            if kind == 2:
                put_feat(vat_ref, sl, p)
                if emit_new_kv:
                    va_ref[tok, sl] = p.T
                return
            if kind == 3:
                put_feat(vbt_ref, slice(None), p)
                return
            gain = qgain_ref[kind]
            q4 = HEAD_DIM // 4
            heads = []
            for r in range(CHUNK // HEAD_DIM):
                x = p[r * HEAD_DIM:(r + 1) * HEAD_DIM, :]
                x = x * lax.rsqrt(jnp.mean(x * x, axis=0, keepdims=True) + EPS) * gain
                if rope:
                    swapped = jnp.concatenate([x[q4:2 * q4], x[0:q4], x[3 * q4:4 * q4], x[2 * q4:3 * q4]],
                                              axis=0)
                    x = x * cost_ref[:, tok] + swapped * sint_ref[:, tok]
                heads.append(x)
            put_feat(qat_ref if kind == 0 else qbt_ref, sl, jnp.concatenate(heads, axis=0) * Q_SCALE)

        _software_pipeline(n_feat, [feat_project, feat_finish])

    n_sub = x_ref.shape[0] // sub
    nxt = normed(0)
    for u in range(n_sub):
        cur, nxt = nxt, (normed(u + 1) if u + 1 < n_sub else None)
        project(u, *cur)


def _qkv_rope(x3d, mod3, mod_row, g1, w_k, w_t, gsum, kgains, qgains, rope_tabs, tm):
    b, t, _ = x3d.shape
    rows = b * t
    tps = t // tm
    nat, featm = rope_tabs
    tok_wide = pl.BlockSpec((tm, A_W), lambda i: (i, 0))
    tok_narrow = pl.BlockSpec((tm, B_KVW), lambda i: (i, 0))
    feat_wide = pl.BlockSpec((1, A_W, tm), lambda i: (i // tps, 0, i % tps))
    feat_narrow = pl.BlockSpec((1, B_KVW, tm), lambda i: (i // tps, 0, i % tps))
    return pl.pallas_call(
        functools.partial(_qkv_kernel, rope=True, emit_new_kv=False),
        grid=(rows // tm,),
        in_specs=[pl.BlockSpec((tm, D_MODEL), lambda i: (i, 0)),
                  pl.BlockSpec((None, 1, N_MOD * D_MODEL), lambda i: (mod_row(i), 0, 0)),
                  _const_spec((1, D_MODEL)), _const_spec((D_MODEL, N_K)), _const_spec((N_T, D_MODEL)),
                  _const_spec((CHUNK, CHUNK)), _const_spec((2, CHUNK)), _const_spec((2, HEAD_DIM, 1))]
        + [pl.BlockSpec((tm, LANES), lambda i: (i % tps, 0))] * 3
        + [pl.BlockSpec((HEAD_DIM, tm), lambda i: (0, i % tps))] * 2,
        out_specs=[feat_wide, feat_wide, tok_wide, tok_narrow, feat_wide, feat_narrow],
        out_shape=[jax.ShapeDtypeStruct((b, A_W, t), BF16), jax.ShapeDtypeStruct((b, B_Q, t), BF16),
                   jax.ShapeDtypeStruct((rows, A_W), BF16), jax.ShapeDtypeStruct((rows, B_KVW), BF16),
                   jax.ShapeDtypeStruct((b, A_W, t), BF16), jax.ShapeDtypeStruct((b, B_KVW, t), BF16)],
        compiler_params=_params(1),
        name="qkv_rope",
    )(x3d.reshape(rows, D_MODEL), mod3, g1, w_k, w_t, gsum, kgains, qgains, *nat, *featm)


def _lambda(lam_ref):
    lv = lam_ref[...]
    t1 = jnp.sum(lv[0:1] * lv[1:2], axis=-1, keepdims=True)
    t2 = jnp.sum(lv[2:3] * lv[3:4], axis=-1, keepdims=True)
    return jnp.exp(t1) - jnp.exp(t2) + LAMBDA_INIT


def _with_ones(vt):
    return jnp.concatenate([vt, jnp.ones((ONES_ROWS, vt.shape[1]), vt.dtype)], axis=0)


def _column_softmax(scores, extra=None):
    m = functools.reduce(jnp.maximum, [jnp.max(s, axis=0, keepdims=True) for s in scores])
    if extra is not None:
        m = jnp.maximum(m, extra)
    return [jnp.exp2(s - m).astype(BF16) for s in scores], m


def _max_row_sq_norm(x):
    x = x.astype(F32)
    return jnp.max(jnp.sum(x * x, axis=1, keepdims=True), axis=0, keepdims=True)[0, 0]


def _scores_bounded(q_sq, k_sq, sink_abs=0.0):
    limit = SHIFT_FREE_LOG2
    return jnp.logical_and(q_sq * k_sq <= limit * limit, sink_abs <= limit)


def _diff_query_cols(qt):
    zero = jnp.zeros((HEAD_DIM, qt.shape[1]), qt.dtype)
    return jnp.concatenate([jnp.concatenate([qt[:HEAD_DIM], zero], axis=0),
                            jnp.concatenate([zero, qt[HEAD_DIM:]], axis=0)], axis=1)


def _diff_output(r, lam, subg_col):
    n = r.shape[1] // 2
    dv = 2 * HEAD_DIM
    tot = r[dv:dv + 1, :]
    o = r[:dv, :n] * (1.0 / tot[:, :n]) - r[:dv, n:] * (lam / tot[:, n:])
    o = o * lax.rsqrt(jnp.mean(o * o, axis=0, keepdims=True) + EPS)
    return (o * subg_col * (1.0 - LAMBDA_INIT)).T


def _group_output(r, m, sink_row):
    n = r.shape[1] // B_GROUP
    tot = r[HEAD_DIM:HEAD_DIM + 1, :] + jnp.exp2(sink_row - m)
    o = r[:HEAD_DIM, :] * (1.0 / tot)
    return jnp.concatenate([o[:, i * n:(i + 1) * n] for i in range(B_GROUP)], axis=0).T


def _sink_row(sink_ref, g, n):
    reps = n // LANES
    return jnp.concatenate([sink_ref[pl.ds(g * B_GROUP + r, 1), :] for r in range(B_GROUP)
                            for _ in range(reps)], axis=1)


def _ctx_attn_kernel(qat_ref, qbt_ref, ka_ref, kb_ref, vat_ref, vbt_ref, lam_ref, subg_ref, sink_ref,
                     bounds_ref, oa_ref, ob_ref):
    lam = _lambda(lam_ref)
    subg_col = subg_ref[...]
    n_seq, _, n = qat_ref.shape
    chains = A_HEADS + B_KV_HEADS

    def scores(item, _):
        s, c = divmod(item, chains)
        rows = slice(s * n, (s + 1) * n)
        if c < A_HEADS:
            sl = slice(c * LANES, (c + 1) * LANES)
            return _dot(ka_ref[rows, sl], _diff_query_cols(qat_ref[s, sl, :]))
        g = c - A_HEADS
        qs = jnp.concatenate([qbt_ref[s, (g * B_GROUP + r) * HEAD_DIM:(g * B_GROUP + r + 1) * HEAD_DIM, :]
                              for r in range(B_GROUP)], axis=1)
        zero = jnp.zeros_like(qs)
        w = jnp.concatenate([qs if i == g else zero for i in range(B_KV_HEADS)], axis=0)
        return _dot(kb_ref[rows, :], w)

    def probs(shift, item, sc):
        c = item % chains
        if not shift:
            return [jnp.exp2(sc).astype(BF16)], jnp.zeros((1, sc.shape[1]), F32)
        if c < A_HEADS:
            return _column_softmax([sc])
        return _column_softmax([sc], extra=_sink_row(sink_ref, c - A_HEADS, n))

    def outputs(item, st):
        s, c = divmod(item, chains)
        rows = slice(s * n, (s + 1) * n)
        es, m = st
        if c < A_HEADS:
            sl = slice(c * LANES, (c + 1) * LANES)
            r = _dot(_with_ones(vat_ref[s, sl, :]), es[0])
            oa_ref[rows, sl] = _diff_output(r, lam, subg_col).astype(BF16)
        else:
            g = c - A_HEADS
            vt = vbt_ref[s, g * HEAD_DIM:(g + 1) * HEAD_DIM, :].astype(BF16)
            r = _dot(_with_ones(vt), es[0])
            o = _group_output(r, m, _sink_row(sink_ref, g, n))
            ob_ref[rows, g * CHUNK:(g + 1) * CHUNK] = o.astype(BF16)

    bounded = jnp.logical_and(_scores_bounded(bounds_ref[0], bounds_ref[1]),
                              _scores_bounded(bounds_ref[2], bounds_ref[3], bounds_ref[4]))

    @pl.when(bounded)
    def _():
        _software_pipeline(n_seq * chains, [scores, functools.partial(probs, False), outputs])

    @pl.when(jnp.logical_not(bounded))
    def _():
        _software_pipeline(n_seq * chains, [scores, functools.partial(probs, True), outputs])


def _ctx_front_kernel(x_ref, mod_ref, g1_ref, wk_ref, wt_ref, gsum_ref, kgain_ref, qgain_ref,
                      lam_ref, subg_ref, sink_ref, bounds_ref,
                      oa_ref, ob_ref, vbt_ref, kat_ref, va_ref, kbt_ref,
                      qat_s, qbt_s, ka_s, kb_s, vat_s):
    _qkv_kernel(x_ref, mod_ref, g1_ref, wk_ref, wt_ref, gsum_ref, kgain_ref, qgain_ref,
                qat_s, qbt_s, ka_s, kb_s, vat_s, vbt_ref, kat_ref, va_ref, kbt_ref,
                rope=False, emit_new_kv=True)
    _ctx_attn_kernel(qat_s, qbt_s, ka_s, kb_s, vat_s, vbt_ref, lam_ref, subg_ref, sink_ref, bounds_ref,
                     oa_ref, ob_ref)


def _ctx_front(x3d, mod3, mod_row, g1, w_k, w_t, gsum, kgains, qgains, lam_vecs, subg_col, sink_rows, bounds,
               n_seq):
    b, t, _ = x3d.shape
    rows, tm = b * t, n_seq * t
    tok = pl.BlockSpec((tm, A_W), lambda i: (i, 0))
    feat_wide = pl.BlockSpec((n_seq, A_W, t), lambda i: (i, 0, 0))
    feat_narrow = pl.BlockSpec((n_seq, B_KVW, t), lambda i: (i, 0, 0))
    return pl.pallas_call(
        _ctx_front_kernel,
        grid=(b // n_seq,),
        in_specs=[pl.BlockSpec((tm, D_MODEL), lambda i: (i, 0)),
                  pl.BlockSpec((None, 1, N_MOD * D_MODEL), lambda i: (mod_row(i), 0, 0)),
                  _const_spec((1, D_MODEL)), _const_spec((D_MODEL, N_K)), _const_spec((N_T, D_MODEL)),
                  _const_spec((CHUNK, CHUNK)), _const_spec((2, CHUNK)), _const_spec((2, HEAD_DIM, 1)),
                  _const_spec((4, HEAD_DIM)), _const_spec((2 * HEAD_DIM, 1)),
                  _const_spec((B_HEADS, LANES)), pl.BlockSpec(memory_space=pltpu.SMEM)],
        out_specs=[tok, tok, feat_narrow, feat_wide, tok, feat_narrow],
        out_shape=[jax.ShapeDtypeStruct((rows, A_W), BF16), jax.ShapeDtypeStruct((rows, B_Q), BF16),
                   jax.ShapeDtypeStruct((b, B_KVW, t), F32), jax.ShapeDtypeStruct((b, A_W, t), F32),
                   jax.ShapeDtypeStruct((rows, A_W), F32), jax.ShapeDtypeStruct((b, B_KVW, t), F32)],
        scratch_shapes=[pltpu.VMEM((n_seq, A_W, t), BF16), pltpu.VMEM((n_seq, B_Q, t), BF16),
                        pltpu.VMEM((tm, A_W), BF16), pltpu.VMEM((tm, B_KVW), BF16),
                        pltpu.VMEM((n_seq, A_W, t), BF16)],
        compiler_params=_params(1),
        name="ctx_front",
    )(x3d.reshape(rows, D_MODEL), mod3, g1, w_k, w_t, gsum, kgains, qgains, lam_vecs, subg_col, sink_rows,
      bounds)


def _rider_specs(weights, n_steps, flat_step):
    specs = []
    for w in weights:
        rows, cols = w.shape
        bands = next(n for n in (n_steps, n_steps // 2, n_steps // 4)
                     if rows % n == 0 and (rows // n) % 16 == 0)
        repeat = n_steps // bands
        spec = pl.BlockSpec((rows // bands, cols), lambda *g, repeat=repeat: (flat_step(*g) // repeat, 0))
        specs.append((spec, jax.ShapeDtypeStruct(w.shape, BF16)))
    return specs


def _rider_pieces(rider_refs, n_items):
    n = len(rider_refs) // 2
    pieces = [(w_ref, o_ref, r) for w_ref, o_ref in zip(rider_refs[:n], rider_refs[n:])
              for r in range(0, w_ref.shape[0], 16)]

    def run(item):
        for w_ref, o_ref, r in pieces[item::n_items]:
            o_ref[r:r + 16, :] = w_ref[r:r + 16, :].astype(BF16)

    return run


def _diff_lat_kernel(qt_ref, k_ref, vt_ref, ckt_ref, cv_ref, lam_ref, subg_ref, bounds_ref, *rest, n_riders):
    o_ref = rest[n_riders]
    n_sub = qt_ref.shape[1] // DIFF_QUERIES
    cast_piece = _rider_pieces(rest[:n_riders] + rest[n_riders + 1:], n_sub)
    lam = _lambda(lam_ref)
    subg_col = subg_ref[...]
    k = k_ref[...]
    ck = ckt_ref[...].T.astype(BF16)
    vt = _with_ones(vt_ref[...])
    past = cv_ref.shape[0] // A_HEADS
    cv = cv_ref[pl.ds(pl.program_id(1), past, stride=A_HEADS), :]
    cvt = _with_ones(cv.T.astype(BF16))

    def scores(i, _):
        cast_piece(i)
        q = _diff_query_cols(qt_ref[:, i * DIFF_QUERIES:(i + 1) * DIFF_QUERIES])
        return [_dot(k, q), _dot(ck, q)]

    def outputs(shift, i, s):
        if shift:
            m = jnp.maximum(jnp.max(s[0], axis=0, keepdims=True), jnp.max(s[1], axis=0, keepdims=True))
        r = None
        for sc, v in ((s[0], vt), (s[1], cvt)):
            for c in range(0, sc.shape[0], KEY_CHUNK):
                piece = sc[c:c + KEY_CHUNK]
                part = _dot(v[:, c:c + KEY_CHUNK], jnp.exp2(piece - m if shift else piece).astype(BF16))
                r = part if r is None else r + part
        o_ref[i * DIFF_QUERIES:(i + 1) * DIFF_QUERIES, :] = _diff_output(r, lam, subg_col).astype(BF16)

    bounded = _scores_bounded(bounds_ref[0], jnp.maximum(bounds_ref[1], _max_row_sq_norm(ck)))

    @pl.when(bounded)
    def _():
        _software_pipeline(n_sub, [scores, functools.partial(outputs, False)])

    @pl.when(jnp.logical_not(bounded))
    def _():
        _software_pipeline(n_sub, [scores, functools.partial(outputs, True)])


def _diff_lat(qt, k, vt, ckt, cv, lam_vecs, subg_col, bounds, riders):
    b, _, t = qt.shape
    past = ckt.shape[2]
    rider_specs = _rider_specs(riders, b * A_HEADS, lambda i, h: i * A_HEADS + h)
    return pl.pallas_call(
        functools.partial(_diff_lat_kernel, n_riders=len(riders)),
        grid=(b, A_HEADS),
        in_specs=[pl.BlockSpec((None, LANES, t), lambda i, h: (i, h, 0)),
                  pl.BlockSpec((None, t, LANES), lambda i, h: (i, 0, h)),
                  pl.BlockSpec((None, LANES, t), lambda i, h: (i, h, 0)),
                  pl.BlockSpec((None, LANES, past), lambda i, h: (i, h, 0)),
                  pl.BlockSpec((None, past * A_HEADS, LANES), lambda i, h: (i, 0, 0)),
                  _const_spec((4, HEAD_DIM)), _const_spec((2 * HEAD_DIM, 1)),
                  pl.BlockSpec(memory_space=pltpu.SMEM)] + [spec for spec, _ in rider_specs],
        out_specs=[pl.BlockSpec((None, t, LANES), lambda i, h: (i, 0, h))] + [spec for spec, _ in rider_specs],
        out_shape=[jax.ShapeDtypeStruct((b, t, A_W), BF16)] + [shape for _, shape in rider_specs],
        compiler_params=_params(2),
        name="diff_lat",
    )(qt, k, vt, ckt, cv, lam_vecs, subg_col, bounds, *riders)


def _window_start(i, t):
    return min(max((i - 1) * BLOCK, 0), t - 3 * BLOCK)


def _band_offsets(t):
    return sorted({_window_start(i, t) - i * BLOCK for i in range(t // BLOCK)})


def _band_bias(t):
    key = np.arange(3 * BLOCK)[:, None]
    qry = np.arange(B_GROUP * BLOCK)[None, :] % BLOCK
    return jnp.asarray(np.stack([np.where(np.abs(key + off - qry) <= WINDOW, 0.0, NEG)
                                 for off in _band_offsets(t)]), dtype=F32)


def _win_lat_kernel(qt_ref, k_ref, vt_ref, ckt_ref, cvt_ref, sink_ref, bias_ref, bounds_ref, *rest, n_riders):
    o_ref = rest[n_riders]
    g = pl.program_id(1)
    t = k_ref.shape[0]
    span = 3 * BLOCK
    cast_piece = _rider_pieces(rest[:n_riders] + rest[n_riders + 1:], t // BLOCK)
    keep = lax.broadcasted_iota(jnp.int32, (1, CHUNK), 1) // HEAD_DIM == g
    k = jnp.where(keep, k_ref[...].astype(F32), 0.0).astype(BF16)
    ck = jnp.where(keep, ckt_ref[...].T, 0.0).astype(BF16)
    vt = vt_ref[...]
    cvt = _with_ones(cvt_ref[...].astype(BF16))
    sink_row = _sink_row(sink_ref, g, BLOCK)
    offsets = _band_offsets(t)

    def scores(i, _):
        cast_piece(i)
        w = _window_start(i, t)
        qs = jnp.concatenate([qt_ref[r * HEAD_DIM:(r + 1) * HEAD_DIM, i * BLOCK:(i + 1) * BLOCK]
                              for r in range(B_GROUP)], axis=1)
        wq = jnp.concatenate([qs] * B_KV_HEADS, axis=0)
        return [_dot(k[w:w + span], wq) + bias_ref[offsets.index(w - i * BLOCK)], _dot(ck, wq)]

    def probs(shift, i, s):
        if shift:
            return _column_softmax(s, extra=sink_row)
        return [jnp.exp2(piece).astype(BF16) for piece in s], jnp.zeros_like(sink_row)

    def outputs(i, st):
        es, m = st
        w = _window_start(i, t)
        r = _dot(_with_ones(vt[:, w:w + span]), es[0]) + _dot(cvt, es[1])
        o_ref[i * BLOCK:(i + 1) * BLOCK, :] = _group_output(r, m, sink_row).astype(BF16)

    bounded = _scores_bounded(bounds_ref[2], jnp.maximum(bounds_ref[3], _max_row_sq_norm(ck)), bounds_ref[4])

    @pl.when(bounded)
    def _():
        _software_pipeline(t // BLOCK, [scores, functools.partial(probs, False), outputs])

    @pl.when(jnp.logical_not(bounded))
    def _():
        _software_pipeline(t // BLOCK, [scores, functools.partial(probs, True), outputs])


def _win_lat(qt, k, vt, ckt, cvt, sink_rows, bounds, riders):
    b, _, t = qt.shape
    past = ckt.shape[2]
    bias = _band_bias(t)
    rider_specs = _rider_specs(riders, b * B_KV_HEADS, lambda i, g: i * B_KV_HEADS + g)
    return pl.pallas_call(
        functools.partial(_win_lat_kernel, n_riders=len(riders)),
        grid=(b, B_KV_HEADS),
        in_specs=[pl.BlockSpec((None, CHUNK, t), lambda i, g: (i, g, 0)),
                  pl.BlockSpec((None, t, B_KVW), lambda i, g: (i, 0, 0)),
                  pl.BlockSpec((None, HEAD_DIM, t), lambda i, g: (i, g, 0)),
                  pl.BlockSpec((None, B_KVW, past), lambda i, g: (i, 0, 0)),
                  pl.BlockSpec((None, HEAD_DIM, past), lambda i, g: (i, g, 0)),
                  _const_spec((B_HEADS, LANES)), _const_spec(bias.shape),
                  pl.BlockSpec(memory_space=pltpu.SMEM)] + [spec for spec, _ in rider_specs],
        out_specs=[pl.BlockSpec((None, t, CHUNK), lambda i, g: (i, 0, g))] + [spec for spec, _ in rider_specs],
        out_shape=[jax.ShapeDtypeStruct((b, t, B_Q), BF16)] + [shape for _, shape in rider_specs],
        compiler_params=_params(2),
        name="win_lat",
    )(qt, k, vt, ckt, cvt, sink_rows, bias, bounds, *riders)


def _merge_kernel(x1_ref, x2_ref, oa1_ref, oa2_ref, ob1_ref, ob2_ref, mod_ref, g1_ref, wga0_ref, wga1_ref,
                  wgb0_ref, wgb1_ref, woa_ref, wob_ref, wout_ref, o1_ref, o2_ref, *, n_first):
    def tile(x_ref, oa_ref, ob_ref, o_ref):
        mod = mod_ref[...]
        rows = lambda i: slice(i * ROW_SUBTILE, (i + 1) * ROW_SUBTILE)

        def norm(i, _):
            return _modulated_norm(x_ref[rows(i), :], g1_ref[...], mod[:, D_MODEL:2 * D_MODEL],
                                   mod[:, 0:D_MODEL]).astype(BF16)

        def branches(i, h):
            ga = jnp.concatenate([_dot(h, wga0_ref[...]), _dot(h, wga1_ref[...])], axis=1)
            gb = jnp.concatenate([_dot(h, wgb0_ref[...]), _dot(h, wgb1_ref[...])], axis=1)
            return (jax.nn.sigmoid(ga) * _dot(oa_ref[rows(i), :], woa_ref[...])
                    + jax.nn.sigmoid(gb) * _dot(ob_ref[rows(i), :], wob_ref[...])).astype(BF16)

        def project(i, merged):
            o_ref[rows(i), :] = x_ref[rows(i), :] + mod[:, 2 * D_MODEL:3 * D_MODEL] * _dot(merged, wout_ref[...])

        _software_pipeline(x_ref.shape[0] // ROW_SUBTILE, [norm, branches, project])

    _per_token_set(n_first, functools.partial(tile, x1_ref, oa1_ref, ob1_ref, o1_ref),
                   functools.partial(tile, x2_ref, oa2_ref, ob2_ref, o2_ref))


def _per_token_set(n_first, first_body, second_body):
    first = pl.program_id(0) < n_first
    pl.when(first)(first_body)
    pl.when(jnp.logical_not(first))(second_body)


def _split_tiles(n_first, tm, width):
    return (pl.BlockSpec((tm, width), lambda i: (jnp.minimum(i, n_first - 1), 0)),
            pl.BlockSpec((tm, width), lambda i: (jnp.maximum(i - n_first, 0), 0)))


def _merge(x1, x2, oa1, oa2, ob1, ob2, mod3, mod_row, g1, w_in, w_oa, w_ob, w_out, tm):
    n_first, n_second = x1.shape[0] // tm, x2.shape[0] // tm
    t1, t2 = _split_tiles(n_first, tm, D_MODEL)
    sq = _const_spec((D_MODEL, D_MODEL))
    return pl.pallas_call(
        functools.partial(_merge_kernel, n_first=n_first),
        grid=(n_first + n_second,),
        in_specs=[t1, t2, t1, t2, t1, t2,
                  pl.BlockSpec((None, 1, N_MOD * D_MODEL), lambda i: (mod_row(i), 0, 0)),
                  _const_spec((1, D_MODEL))]
        + [pl.BlockSpec((D_MODEL, GATE_BLOCK), lambda i, j=j: (0, GATE_COL // GATE_BLOCK + j),
                        pipeline_mode=pl.Buffered(1)) for j in range(2 * D_MODEL // GATE_BLOCK)]
        + [sq, sq, sq],
        out_specs=[t1, t2],
        out_shape=[jax.ShapeDtypeStruct(x1.shape, F32), jax.ShapeDtypeStruct(x2.shape, F32)],
        compiler_params=_params(1),
        name="merge_out",
    )(x1, x2, oa1, oa2, ob1, ob2, mod3, g1, w_in, w_in, w_in, w_in, w_oa, w_ob, w_out)


def _ffn_kernel(x1_ref, x2_ref, mod_ref, g2_ref, wgate_ref, wup_ref, wdown_ref, o1_ref, o2_ref, *, n_first):
    def tile(x_ref, o_ref):
        mod = mod_ref[...]
        rows = lambda i: slice(i * FFN_SUBTILE, (i + 1) * FFN_SUBTILE)

        def norm(i, _):
            return _modulated_norm(x_ref[rows(i), :], g2_ref[...], mod[:, 4 * D_MODEL:5 * D_MODEL],
                                   mod[:, 3 * D_MODEL:4 * D_MODEL]).astype(BF16)

        def hidden(i, h):
            gate = _dot(h, wgate_ref[...])
            return (gate * jax.nn.sigmoid(gate) * _dot(h, wup_ref[...])).astype(BF16)

        def project(i, act):
            o_ref[rows(i), :] = x_ref[rows(i), :] + mod[:, 5 * D_MODEL:6 * D_MODEL] * _dot(act, wdown_ref[...])

        _software_pipeline(x_ref.shape[0] // FFN_SUBTILE, [norm, hidden, project])

    _per_token_set(n_first, functools.partial(tile, x1_ref, o1_ref), functools.partial(tile, x2_ref, o2_ref))


def _ffn(x1, x2, mod3, mod_row, g2, w_gate, w_up, w_down, tm):
    n_first, n_second = x1.shape[0] // tm, x2.shape[0] // tm
    t1, t2 = _split_tiles(n_first, tm, D_MODEL)
    return pl.pallas_call(
        functools.partial(_ffn_kernel, n_first=n_first),
        grid=(n_first + n_second,),
        in_specs=[t1, t2,
                  pl.BlockSpec((None, 1, N_MOD * D_MODEL), lambda i: (mod_row(i), 0, 0)),
                  _const_spec((1, D_MODEL)), _const_spec((D_MODEL, D_FF)),
                  _const_spec((D_MODEL, D_FF)), _const_spec((D_FF, D_MODEL))],
        out_specs=[t1, t2],
        out_shape=[jax.ShapeDtypeStruct(x1.shape, F32), jax.ShapeDtypeStruct(x2.shape, F32)],
        compiler_params=_params(1),
        name="swiglu_ffn",
    )(x1, x2, mod3, g2, w_gate, w_up, w_down)


def _rope_tables(seq_len):
    quarter = HEAD_DIM // 4
    t = np.arange(seq_len)
    row = (t // GRID_W).astype(np.float32)
    col = (t % GRID_W).astype(np.float32)
    freqs = (np.float32(ROPE_BASE) ** (-np.arange(quarter, dtype=np.float32) / np.float32(quarter))
             ).astype(np.float32)
    ang_r = row[:, None] * freqs
    ang_c = col[:, None] * freqs
    zeros = np.zeros_like(ang_r)
    cos = np.concatenate([np.cos(ang_r)] * 2 + [np.cos(ang_c)] * 2, axis=-1)
    sin_next = np.concatenate([-np.sin(ang_r), zeros, -np.sin(ang_c), zeros], axis=-1)
    sin_prev = np.concatenate([zeros, np.sin(ang_r), zeros, np.sin(ang_c)], axis=-1)
    rep = LANES // HEAD_DIM
    nat = tuple(jnp.asarray(np.tile(a, (1, rep)), dtype=F32) for a in (cos, sin_next, sin_prev))
    featm = (jnp.asarray(cos.T, dtype=F32), jnp.asarray((sin_next + sin_prev).T, dtype=F32))
    return nat, featm


def _group_sum_matrix():
    idx = np.arange(CHUNK) // HEAD_DIM
    return jnp.asarray(idx[:, None] == idx[None, :], dtype=BF16)


def kernel(x_prompt, x_sample, cache_diff_k, cache_diff_v, cache_win_k, cache_win_v, c, c_ctx,
           w_ada, b_ada, norm1_g, w_in, qn_a, kn_a, lambda_q1, lambda_k1, lambda_q2, lambda_k2,
           subln_g, qn_b, kn_b, sink, w_oa, w_ob, w_out, norm2_g, w_gate, w_up, w_down):
    batch, seq, _ = x_prompt.shape
    dec_batch, dec_seq, _ = x_sample.shape
    past = cache_diff_k.shape[2]
    l = 0

    cc = jnp.concatenate([c, c_ctx[None, :], jnp.zeros((8 - dec_batch - 1, D_MODEL), F32)], axis=0)
    mod3 = _modulation(cc, w_ada[l], b_ada[l][None, :])

    w_in_b = w_in[l].astype(BF16)
    o_ka, o_va, o_qb, o_kb, o_vb = A_W, 2 * A_W, 3 * A_W, 3 * A_W + B_Q, 3 * A_W + B_Q + B_KVW
    w_k = jnp.concatenate([w_in_b[:, o_ka:o_va], w_in_b[:, o_kb:o_vb]], axis=1)
    w_t = jnp.concatenate([w_in_b[:, :o_ka], w_in_b[:, o_qb:o_kb], w_in_b[:, o_va:o_qb],
                           w_in_b[:, o_vb:GATE_COL]], axis=1).T
    g1 = norm1_g[l][None, :]
    g2 = norm2_g[l][None, :]
    rep = CHUNK // HEAD_DIM
    kgains = jnp.stack([jnp.tile(kn_a[l], rep), jnp.tile(kn_b[l], rep)])
    qgains = jnp.stack([qn_a[l], qn_b[l]])[:, :, None]
    lam_vecs = jnp.stack([lambda_q1[l], lambda_k1[l], lambda_q2[l], lambda_k2[l]])
    subg_col = subln_g[l][:, None]
    sink_rows = jnp.broadcast_to((sink[l] * LOG2E)[:, None], (B_HEADS, LANES))
    head_gains = jnp.stack([qn_a[l], kn_a[l], qn_b[l], kn_b[l]])
    head_scale = jnp.asarray([Q_SCALE, 1.0, Q_SCALE, 1.0], F32)
    bounds = jnp.concatenate([1.02 * HEAD_DIM * jnp.max(jnp.square(head_gains), axis=1) * jnp.square(head_scale),
                              jnp.max(jnp.abs(sink[l]), keepdims=True) * LOG2E]).astype(F32)
    gsum = _group_sum_matrix()
    tm = 512

    xp = x_prompt.reshape(batch * seq, D_MODEL)
    ctx_row = lambda i: dec_batch
    oa_p, ob_p, vbt, kat_f, va_f, kbt_f = _ctx_front(
        x_prompt, mod3, ctx_row, g1, w_k, w_t, gsum, kgains, qgains, lam_vecs, subg_col, sink_rows, bounds,
        tm // seq)
    new_diff_k = kat_f.reshape(batch, A_HEADS, 2, HEAD_DIM, seq).transpose(0, 4, 1, 2, 3)[:, None]
    new_diff_v = va_f.reshape(batch, 1, seq, A_HEADS, 2 * HEAD_DIM)
    new_win_k = kbt_f.reshape(batch, B_KV_HEADS, HEAD_DIM, seq).transpose(0, 3, 1, 2)[:, None]
    new_win_v = vbt.reshape(batch, B_KV_HEADS, HEAD_DIM, seq).transpose(0, 3, 1, 2)[:, None]

    xs = x_sample.reshape(dec_batch * dec_seq, D_MODEL)
    lat_row = lambda i: i // (dec_seq // tm)
    qat, qbt, ka, kb, vat, vbt = _qkv_rope(
        x_sample, mod3, lat_row, g1, w_k, w_t, gsum, kgains, qgains, _rope_tables(dec_seq), tm)
    ckt_a = cache_diff_k[:, l].transpose(0, 2, 3, 4, 1).reshape(dec_batch, A_W, past)
    cv_a = cache_diff_v[:, l].reshape(dec_batch, past * A_HEADS, 2 * HEAD_DIM)
    ckt_b = cache_win_k[:, l].transpose(0, 2, 3, 1).reshape(dec_batch, B_KVW, past)
    cvt_b = cache_win_v[:, l].transpose(0, 2, 3, 1).reshape(dec_batch, B_KVW, past)
    oa_s, w_gate_b, w_up_b, w_down_b = _diff_lat(
        qat, ka.reshape(dec_batch, dec_seq, A_W), vat, ckt_a, cv_a, lam_vecs, subg_col, bounds,
        [w_gate[l], w_up[l], w_down[l]])
    ob_s, w_oa_b, w_ob_b, w_out_b = _win_lat(
        qbt, kb.reshape(dec_batch, dec_seq, B_KVW), vbt, ckt_b, cvt_b, sink_rows, bounds,
        [w_oa[l], w_ob[l], w_out[l]])

    n_ctx_tiles = batch * seq // tm
    both_row = lambda i: jnp.where(i < n_ctx_tiles, dec_batch, (i - n_ctx_tiles) // (dec_seq // tm))
    xp1, xs1 = _merge(xp, xs, oa_p, oa_s.reshape(-1, A_W), ob_p, ob_s.reshape(-1, B_Q), mod3, both_row, g1,
                      w_in_b, w_oa_b, w_ob_b, w_out_b, tm)
    y_prompt, y_sample = _ffn(xp1, xs1, mod3, both_row, g2, w_gate_b, w_up_b, w_down_b, tm)

    return (y_prompt.reshape(batch, seq, D_MODEL),
            y_sample.reshape(dec_batch, dec_seq, D_MODEL),
            new_diff_k, new_diff_v, new_win_k, new_win_v)
```

```python
import functools
import math

import jax
import jax.numpy as jnp
import numpy as np
from jax import lax
from jax.experimental import pallas as pl
from jax.experimental.pallas import tpu as pltpu

D_MODEL = 1024
HEAD_DIM = 64
A_HEADS = 8
B_HEADS = 16
B_KV_HEADS = 4
B_GROUP = B_HEADS // B_KV_HEADS
A_W = A_HEADS * 2 * HEAD_DIM
B_Q = B_HEADS * HEAD_DIM
B_KVW = B_KV_HEADS * HEAD_DIM
N_K = A_W + B_KVW
N_T = A_W + B_Q + A_W + B_KVW
GATE_COL = 3 * A_W + B_Q + 2 * B_KVW
GATE_BLOCK = 512
WINDOW = 128
BLOCK = 128
GRID_W = 64
D_FF = 2816
N_MOD = 6
EPS = 1e-6
ROPE_BASE = 10000.0
NEG = -1e30
LAMBDA_INIT = 0.8 - 0.6 * math.exp(-0.3 * 0)
LOG2E = math.log2(math.e)
Q_SCALE = HEAD_DIM ** -0.5 * LOG2E

LANES = 128
CHUNK = 256
ONES_ROWS = 16
KEY_CHUNK = 512
QKV_SUBTILE = 512
ROW_SUBTILE = 256
FFN_SUBTILE = 128
DIFF_QUERIES = 128
SHIFT_FREE_LOG2 = 60.0
VMEM_LIMIT = 56 * 1024 * 1024

F32 = jnp.float32
BF16 = jnp.bfloat16


def _params(n_axes):
    return pltpu.CompilerParams(dimension_semantics=("arbitrary",) * n_axes,
                                vmem_limit_bytes=VMEM_LIMIT)


def _const_spec(shape):
    nd = len(shape)
    return pl.BlockSpec(shape, lambda *_: (0,) * nd, pipeline_mode=pl.Buffered(1))


def _dot(a, b):
    return jnp.dot(a, b, preferred_element_type=F32)


def _modulated_norm(x, gain, scale, shift):
    y = x * lax.rsqrt(jnp.mean(x * x, axis=-1, keepdims=True) + EPS) * gain
    return y * (1.0 + scale) + shift


def _software_pipeline(n, stages):
    depth = len(stages)
    state = [None] * n
    for step in range(n + depth - 1):
        for k, stage in enumerate(stages):
            i = step - k
            if 0 <= i < n:
                state[i] = stage(i, state[i])


def _mod_kernel(c_ref, w_ref, b_ref, o_ref):
    cc = c_ref[...]
    s = cc * jax.nn.sigmoid(cc)
    mod = _dot(s.astype(BF16), w_ref[...].astype(BF16)) + b_ref[...]
    for r in range(mod.shape[0]):
        o_ref[r] = mod[r:r + 1, :]


def _modulation(cc, w_ada, b_ada):
    rows, n = cc.shape[0], w_ada.shape[1]
    tn = n // 2
    return pl.pallas_call(
        _mod_kernel,
        grid=(n // tn,),
        in_specs=[pl.BlockSpec((rows, D_MODEL), lambda j: (0, 0)),
                  pl.BlockSpec((D_MODEL, tn), lambda j: (0, j)),
                  pl.BlockSpec((1, tn), lambda j: (0, j))],
        out_specs=pl.BlockSpec((rows, 1, tn), lambda j: (0, 0, j)),
        out_shape=jax.ShapeDtypeStruct((rows, 1, n), F32),
        compiler_params=_params(1),
        name="adaln_mod",
    )(cc, w_ada, b_ada)


def _qkv_kernel(*refs, rope, emit_new_kv):
    x_ref, mod_ref, g1_ref, wk_ref, wt_ref, gsum_ref, kgain_ref, qgain_ref = refs[:8]
    refs = refs[8:]
    if rope:
        cos_ref, sa_ref, sb_ref, cost_ref, sint_ref = refs[:5]
        refs = refs[5:]
    qat_ref, qbt_ref, ka_ref, kb_ref, vat_ref, vbt_ref = refs[:6]
    if emit_new_kv:
        kat_ref, va_ref, kbt_ref = refs[6:]

    mod = mod_ref[...]
    gsum = gsum_ref[...]
    n_wide = A_W // CHUNK
    key_cols = [j * CHUNK for j in range(n_wide)] + [A_W]
    n_feat = N_T // CHUNK
    sub = min(QKV_SUBTILE, x_ref.shape[0])

    def normed(u):
        h = _modulated_norm(x_ref[u * sub:(u + 1) * sub, :], g1_ref[...], mod[:, D_MODEL:2 * D_MODEL],
                            mod[:, 0:D_MODEL])
        return h.astype(BF16), h.T.astype(BF16)

    def project(u, hb, ht):
        tok = slice(u * sub, (u + 1) * sub)

        def put_feat(ref, rows, val):
            width = ref.shape[2]
            step = min(sub, width)
            for off in range(0, sub, step):
                s, col = divmod(u * sub + off, width)
                ref[s, rows, col:col + step] = val[:, off:off + step].astype(ref.dtype)

        def key_project(j, _):
            return _dot(hb, wk_ref[:, key_cols[j]:key_cols[j] + CHUNK])

        def key_squares(j, p):
            return p, _dot((p * p).astype(BF16), gsum)

        def key_finish(j, st):
            p, ss = st
            row = 0 if j < n_wide else 1
            p = p * lax.rsqrt(ss * (1.0 / HEAD_DIM) + EPS) * kgain_ref[row:row + 1, :]
            if rope:
                cos, sa, sb = cos_ref[tok, :], sa_ref[tok, :], sb_ref[tok, :]
                halves = []
                for i in range(CHUNK // LANES):
                    xh = p[:, i * LANES:(i + 1) * LANES]
                    nxt = pltpu.roll(xh, LANES - HEAD_DIM // 4, 1)
                    prv = pltpu.roll(xh, HEAD_DIM // 4, 1)
                    halves.append(xh * cos + nxt * sa + prv * sb)
                p = jnp.concatenate(halves, axis=1)
            if j < n_wide:
                sl = slice(j * CHUNK, (j + 1) * CHUNK)
                ka_ref[tok, sl] = p.astype(BF16)
                if emit_new_kv:
                    put_feat(kat_ref, sl, p.T)
            else:
                kb_ref[tok, :] = p.astype(BF16)
                if emit_new_kv:
                    put_feat(kbt_ref, slice(None), p.T)

        _software_pipeline(len(key_cols), [key_project, key_squares, key_finish])

        def feat_project(c, _):
            return _dot(wt_ref[c * CHUNK:(c + 1) * CHUNK, :], ht)

        def feat_finish(c, p):
            kind, j = divmod(c, n_wide)
            sl = slice(j * CHUNK, (j + 1) * CHUNK)
            if kind == 2:
                put_feat(vat_ref, sl, p)
                if emit_new_kv:
                    va_ref[tok, sl] = p.T
                return
            if kind == 3:
                put_feat(vbt_ref, slice(None), p)
                return
            gain = qgain_ref[kind]
            q4 = HEAD_DIM // 4
            heads = []
            for r in range(CHUNK // HEAD_DIM):
                x = p[r * HEAD_DIM:(r + 1) * HEAD_DIM, :]
                x = x * lax.rsqrt(jnp.mean(x * x, axis=0, keepdims=True) + EPS) * gain
                if rope:
                    swapped = jnp.concatenate([x[q4:2 * q4], x[0:q4], x[3 * q4:4 * q4], x[2 * q4:3 * q4]],
                                              axis=0)
                    x = x * cost_ref[:, tok] + swapped * sint_ref[:, tok]
                heads.append(x)
            put_feat(qat_ref if kind == 0 else qbt_ref, sl, jnp.concatenate(heads, axis=0) * Q_SCALE)

        _software_pipeline(n_feat, [feat_project, feat_finish])

    n_sub = x_ref.shape[0] // sub
    nxt = normed(0)
    for u in range(n_sub):
        cur, nxt = nxt, (normed(u + 1) if u + 1 < n_sub else None)
        project(u, *cur)


def _qkv_rope(x3d, mod3, mod_row, g1, w_k, w_t, gsum, kgains, qgains, rope_tabs, tm):
    b, t, _ = x3d.shape
    rows = b * t
    tps = t // tm
    nat, featm = rope_tabs
    tok_wide = pl.BlockSpec((tm, A_W), lambda i: (i, 0))
    tok_narrow = pl.BlockSpec((tm, B_KVW), lambda i: (i, 0))
    feat_wide = pl.BlockSpec((1, A_W, tm), lambda i: (i // tps, 0, i % tps))
    feat_narrow = pl.BlockSpec((1, B_KVW, tm), lambda i: (i // tps, 0, i % tps))
    return pl.pallas_call(
        functools.partial(_qkv_kernel, rope=True, emit_new_kv=False),
        grid=(rows // tm,),
        in_specs=[pl.BlockSpec((tm, D_MODEL), lambda i: (i, 0)),
                  pl.BlockSpec((None, 1, N_MOD * D_MODEL), lambda i: (mod_row(i), 0, 0)),
                  _const_spec((1, D_MODEL)), _const_spec((D_MODEL, N_K)), _const_spec((N_T, D_MODEL)),
                  _const_spec((CHUNK, CHUNK)), _const_spec((2, CHUNK)), _const_spec((2, HEAD_DIM, 1))]
        + [pl.BlockSpec((tm, LANES), lambda i: (i % tps, 0))] * 3
        + [pl.BlockSpec((HEAD_DIM, tm), lambda i: (0, i % tps))] * 2,
        out_specs=[feat_wide, feat_wide, tok_wide, tok_narrow, feat_wide, feat_narrow],
        out_shape=[jax.ShapeDtypeStruct((b, A_W, t), BF16), jax.ShapeDtypeStruct((b, B_Q, t), BF16),
                   jax.ShapeDtypeStruct((rows, A_W), BF16), jax.ShapeDtypeStruct((rows, B_KVW), BF16),
                   jax.ShapeDtypeStruct((b, A_W, t), BF16), jax.ShapeDtypeStruct((b, B_KVW, t), BF16)],
        compiler_params=_params(1),
        name="qkv_rope",
    )(x3d.reshape(rows, D_MODEL), mod3, g1, w_k, w_t, gsum, kgains, qgains, *nat, *featm)


def _lambda(lam_ref):
    lv = lam_ref[...]
    t1 = jnp.sum(lv[0:1] * lv[1:2], axis=-1, keepdims=True)
    t2 = jnp.sum(lv[2:3] * lv[3:4], axis=-1, keepdims=True)
    return jnp.exp(t1) - jnp.exp(t2) + LAMBDA_INIT


def _with_ones(vt):
    return jnp.concatenate([vt, jnp.ones((ONES_ROWS, vt.shape[1]), vt.dtype)], axis=0)


def _column_softmax(scores, extra=None):
    m = functools.reduce(jnp.maximum, [jnp.max(s, axis=0, keepdims=True) for s in scores])
    if extra is not None:
        m = jnp.maximum(m, extra)
    return [jnp.exp2(s - m).astype(BF16) for s in scores], m


def _max_row_sq_norm(x):
    x = x.astype(F32)
    return jnp.max(jnp.sum(x * x, axis=1, keepdims=True), axis=0, keepdims=True)[0, 0]


def _scores_bounded(q_sq, k_sq, sink_abs=0.0):
    limit = SHIFT_FREE_LOG2
    return jnp.logical_and(q_sq * k_sq <= limit * limit, sink_abs <= limit)


def _diff_query_cols(qt):
    zero = jnp.zeros((HEAD_DIM, qt.shape[1]), qt.dtype)
    return jnp.concatenate([jnp.concatenate([qt[:HEAD_DIM], zero], axis=0),
                            jnp.concatenate([zero, qt[HEAD_DIM:]], axis=0)], axis=1)


def _diff_output(r, lam, subg_col):
    n = r.shape[1] // 2
    dv = 2 * HEAD_DIM
    tot = r[dv:dv + 1, :]
    o = r[:dv, :n] * (1.0 / tot[:, :n]) - r[:dv, n:] * (lam / tot[:, n:])
    o = o * lax.rsqrt(jnp.mean(o * o, axis=0, keepdims=True) + EPS)
    return (o * subg_col * (1.0 - LAMBDA_INIT)).T


def _group_output(r, m, sink_row):
    n = r.shape[1] // B_GROUP
    tot = r[HEAD_DIM:HEAD_DIM + 1, :] + jnp.exp2(sink_row - m)
    o = r[:HEAD_DIM, :] * (1.0 / tot)
    return jnp.concatenate([o[:, i * n:(i + 1) * n] for i in range(B_GROUP)], axis=0).T


def _sink_row(sink_ref, g, n):
    reps = n // LANES
    return jnp.concatenate([sink_ref[pl.ds(g * B_GROUP + r, 1), :] for r in range(B_GROUP)
                            for _ in range(reps)], axis=1)


def _ctx_attn_kernel(qat_ref, qbt_ref, ka_ref, kb_ref, vat_ref, vbt_ref, lam_ref, subg_ref, sink_ref,
                     bounds_ref, oa_ref, ob_ref):
    lam = _lambda(lam_ref)
    subg_col = subg_ref[...]
    n_seq, _, n = qat_ref.shape
    chains = A_HEADS + B_KV_HEADS

    def scores(item, _):
        s, c = divmod(item, chains)
        rows = slice(s * n, (s + 1) * n)
        if c < A_HEADS:
            sl = slice(c * LANES, (c + 1) * LANES)
            return _dot(ka_ref[rows, sl], _diff_query_cols(qat_ref[s, sl, :]))
        g = c - A_HEADS
        qs = jnp.concatenate([qbt_ref[s, (g * B_GROUP + r) * HEAD_DIM:(g * B_GROUP + r + 1) * HEAD_DIM, :]
                              for r in range(B_GROUP)], axis=1)
        zero = jnp.zeros_like(qs)
        w = jnp.concatenate([qs if i == g else zero for i in range(B_KV_HEADS)], axis=0)
        return _dot(kb_ref[rows, :], w)

    def probs(shift, item, sc):
        c = item % chains
        if not shift:
            return [jnp.exp2(sc).astype(BF16)], jnp.zeros((1, sc.shape[1]), F32)
        if c < A_HEADS:
            return _column_softmax([sc])
        return _column_softmax([sc], extra=_sink_row(sink_ref, c - A_HEADS, n))

    def outputs(item, st):
        s, c = divmod(item, chains)
        rows = slice(s * n, (s + 1) * n)
        es, m = st
        if c < A_HEADS:
            sl = slice(c * LANES, (c + 1) * LANES)
            r = _dot(_with_ones(vat_ref[s, sl, :]), es[0])
            oa_ref[rows, sl] = _diff_output(r, lam, subg_col).astype(BF16)
        else:
            g = c - A_HEADS
            vt = vbt_ref[s, g * HEAD_DIM:(g + 1) * HEAD_DIM, :].astype(BF16)
            r = _dot(_with_ones(vt), es[0])
            o = _group_output(r, m, _sink_row(sink_ref, g, n))
            ob_ref[rows, g * CHUNK:(g + 1) * CHUNK] = o.astype(BF16)

    bounded = jnp.logical_and(_scores_bounded(bounds_ref[0], bounds_ref[1]),
                              _scores_bounded(bounds_ref[2], bounds_ref[3], bounds_ref[4]))

    @pl.when(bounded)
    def _():
        _software_pipeline(n_seq * chains, [scores, functools.partial(probs, False), outputs])

    @pl.when(jnp.logical_not(bounded))
    def _():
        _software_pipeline(n_seq * chains, [scores, functools.partial(probs, True), outputs])


def _ctx_front_kernel(x_ref, mod_ref, g1_ref, wk_ref, wt_ref, gsum_ref, kgain_ref, qgain_ref,
                      lam_ref, subg_ref, sink_ref, bounds_ref,
                      oa_ref, ob_ref, vbt_ref, kat_ref, va_ref, kbt_ref,
                      qat_s, qbt_s, ka_s, kb_s, vat_s):
    _qkv_kernel(x_ref, mod_ref, g1_ref, wk_ref, wt_ref, gsum_ref, kgain_ref, qgain_ref,
                qat_s, qbt_s, ka_s, kb_s, vat_s, vbt_ref, kat_ref, va_ref, kbt_ref,
                rope=False, emit_new_kv=True)
    _ctx_attn_kernel(qat_s, qbt_s, ka_s, kb_s, vat_s, vbt_ref, lam_ref, subg_ref, sink_ref, bounds_ref,
                     oa_ref, ob_ref)


def _ctx_front(x3d, mod3, mod_row, g1, w_k, w_t, gsum, kgains, qgains, lam_vecs, subg_col, sink_rows, bounds,
               n_seq):
    b, t, _ = x3d.shape
    rows, tm = b * t, n_seq * t
    tok = pl.BlockSpec((tm, A_W), lambda i: (i, 0))
    feat_wide = pl.BlockSpec((n_seq, A_W, t), lambda i: (i, 0, 0))
    feat_narrow = pl.BlockSpec((n_seq, B_KVW, t), lambda i: (i, 0, 0))
    return pl.pallas_call(
        _ctx_front_kernel,
        grid=(b // n_seq,),
        in_specs=[pl.BlockSpec((tm, D_MODEL), lambda i: (i, 0)),
                  pl.BlockSpec((None, 1, N_MOD * D_MODEL), lambda i: (mod_row(i), 0, 0)),
                  _const_spec((1, D_MODEL)), _const_spec((D_MODEL, N_K)), _const_spec((N_T, D_MODEL)),
                  _const_spec((CHUNK, CHUNK)), _const_spec((2, CHUNK)), _const_spec((2, HEAD_DIM, 1)),
                  _const_spec((4, HEAD_DIM)), _const_spec((2 * HEAD_DIM, 1)),
                  _const_spec((B_HEADS, LANES)), pl.BlockSpec(memory_space=pltpu.SMEM)],
        out_specs=[tok, tok, feat_narrow, feat_wide, tok, feat_narrow],
        out_shape=[jax.ShapeDtypeStruct((rows, A_W), BF16), jax.ShapeDtypeStruct((rows, B_Q), BF16),
                   jax.ShapeDtypeStruct((b, B_KVW, t), F32), jax.ShapeDtypeStruct((b, A_W, t), F32),
                   jax.ShapeDtypeStruct((rows, A_W), F32), jax.ShapeDtypeStruct((b, B_KVW, t), F32)],
        scratch_shapes=[pltpu.VMEM((n_seq, A_W, t), BF16), pltpu.VMEM((n_seq, B_Q, t), BF16),
                        pltpu.VMEM((tm, A_W), BF16), pltpu.VMEM((tm, B_KVW), BF16),
                        pltpu.VMEM((n_seq, A_W, t), BF16)],
        compiler_params=_params(1),
        name="ctx_front",
    )(x3d.reshape(rows, D_MODEL), mod3, g1, w_k, w_t, gsum, kgains, qgains, lam_vecs, subg_col, sink_rows,
      bounds)


def _rider_specs(weights, n_steps, flat_step):
    specs = []
    for w in weights:
        rows, cols = w.shape
        bands = next(n for n in (n_steps, n_steps // 2, n_steps // 4)
                     if rows % n == 0 and (rows // n) % 16 == 0)
        repeat = n_steps // bands
        spec = pl.BlockSpec((rows // bands, cols), lambda *g, repeat=repeat: (flat_step(*g) // repeat, 0))
        specs.append((spec, jax.ShapeDtypeStruct(w.shape, BF16)))
    return specs


def _rider_pieces(rider_refs, n_items):
    n = len(rider_refs) // 2
    pieces = [(w_ref, o_ref, r) for w_ref, o_ref in zip(rider_refs[:n], rider_refs[n:])
              for r in range(0, w_ref.shape[0], 16)]

    def run(item):
        for w_ref, o_ref, r in pieces[item::n_items]:
            o_ref[r:r + 16, :] = w_ref[r:r + 16, :].astype(BF16)

    return run


def _diff_lat_kernel(qt_ref, k_ref, vt_ref, ckt_ref, cv_ref, lam_ref, subg_ref, bounds_ref, *rest, n_riders):
    o_ref = rest[n_riders]
    n_sub = qt_ref.shape[1] // DIFF_QUERIES
    cast_piece = _rider_pieces(rest[:n_riders] + rest[n_riders + 1:], n_sub)
    lam = _lambda(lam_ref)
    subg_col = subg_ref[...]
    k = k_ref[...]
    ck = ckt_ref[...].T.astype(BF16)
    vt = _with_ones(vt_ref[...])
    past = cv_ref.shape[0] // A_HEADS
    cv = cv_ref[pl.ds(pl.program_id(1), past, stride=A_HEADS), :]
    cvt = _with_ones(cv.T.astype(BF16))

    def scores(i, _):
        cast_piece(i)
        q = _diff_query_cols(qt_ref[:, i * DIFF_QUERIES:(i + 1) * DIFF_QUERIES])
        return [_dot(k, q), _dot(ck, q)]

    def outputs(shift, i, s):
        if shift:
            m = jnp.maximum(jnp.max(s[0], axis=0, keepdims=True), jnp.max(s[1], axis=0, keepdims=True))
        r = None
        for sc, v in ((s[0], vt), (s[1], cvt)):
            for c in range(0, sc.shape[0], KEY_CHUNK):
                piece = sc[c:c + KEY_CHUNK]
                part = _dot(v[:, c:c + KEY_CHUNK], jnp.exp2(piece - m if shift else piece).astype(BF16))
                r = part if r is None else r + part
        o_ref[i * DIFF_QUERIES:(i + 1) * DIFF_QUERIES, :] = _diff_output(r, lam, subg_col).astype(BF16)

    bounded = _scores_bounded(bounds_ref[0], jnp.maximum(bounds_ref[1], _max_row_sq_norm(ck)))

    @pl.when(bounded)
    def _():
        _software_pipeline(n_sub, [scores, functools.partial(outputs, False)])

    @pl.when(jnp.logical_not(bounded))
    def _():
        _software_pipeline(n_sub, [scores, functools.partial(outputs, True)])


def _diff_lat(qt, k, vt, ckt, cv, lam_vecs, subg_col, bounds, riders):
    b, _, t = qt.shape
    past = ckt.shape[2]
    rider_specs = _rider_specs(riders, b * A_HEADS, lambda i, h: i * A_HEADS + h)
    return pl.pallas_call(
        functools.partial(_diff_lat_kernel, n_riders=len(riders)),
        grid=(b, A_HEADS),
        in_specs=[pl.BlockSpec((None, LANES, t), lambda i, h: (i, h, 0)),
                  pl.BlockSpec((None, t, LANES), lambda i, h: (i, 0, h)),
                  pl.BlockSpec((None, LANES, t), lambda i, h: (i, h, 0)),
                  pl.BlockSpec((None, LANES, past), lambda i, h: (i, h, 0)),
                  pl.BlockSpec((None, past * A_HEADS, LANES), lambda i, h: (i, 0, 0)),
                  _const_spec((4, HEAD_DIM)), _const_spec((2 * HEAD_DIM, 1)),
                  pl.BlockSpec(memory_space=pltpu.SMEM)] + [spec for spec, _ in rider_specs],
        out_specs=[pl.BlockSpec((None, t, LANES), lambda i, h: (i, 0, h))] + [spec for spec, _ in rider_specs],
        out_shape=[jax.ShapeDtypeStruct((b, t, A_W), BF16)] + [shape for _, shape in rider_specs],
        compiler_params=_params(2),
        name="diff_lat",
    )(qt, k, vt, ckt, cv, lam_vecs, subg_col, bounds, *riders)


def _window_start(i, t):
    return min(max((i - 1) * BLOCK, 0), t - 3 * BLOCK)


def _band_offsets(t):
    return sorted({_window_start(i, t) - i * BLOCK for i in range(t // BLOCK)})


def _band_bias(t):
    key = np.arange(3 * BLOCK)[:, None]
    qry = np.arange(B_GROUP * BLOCK)[None, :] % BLOCK
    return jnp.asarray(np.stack([np.where(np.abs(key + off - qry) <= WINDOW, 0.0, NEG)
                                 for off in _band_offsets(t)]), dtype=F32)


def _win_lat_kernel(qt_ref, k_ref, vt_ref, ckt_ref, cvt_ref, sink_ref, bias_ref, bounds_ref, *rest, n_riders):
    o_ref = rest[n_riders]
    g = pl.program_id(1)
    t = k_ref.shape[0]
    span = 3 * BLOCK
    cast_piece = _rider_pieces(rest[:n_riders] + rest[n_riders + 1:], t // BLOCK)
    keep = lax.broadcasted_iota(jnp.int32, (1, CHUNK), 1) // HEAD_DIM == g
    k = jnp.where(keep, k_ref[...].astype(F32), 0.0).astype(BF16)
    ck = jnp.where(keep, ckt_ref[...].T, 0.0).astype(BF16)
    vt = vt_ref[...]
    cvt = _with_ones(cvt_ref[...].astype(BF16))
    sink_row = _sink_row(sink_ref, g, BLOCK)
    offsets = _band_offsets(t)

    def scores(i, _):
        cast_piece(i)
        w = _window_start(i, t)
        qs = jnp.concatenate([qt_ref[r * HEAD_DIM:(r + 1) * HEAD_DIM, i * BLOCK:(i + 1) * BLOCK]
                              for r in range(B_GROUP)], axis=1)
        wq = jnp.concatenate([qs] * B_KV_HEADS, axis=0)
        return [_dot(k[w:w + span], wq) + bias_ref[offsets.index(w - i * BLOCK)], _dot(ck, wq)]

    def probs(shift, i, s):
        if shift:
            return _column_softmax(s, extra=sink_row)
        return [jnp.exp2(piece).astype(BF16) for piece in s], jnp.zeros_like(sink_row)

    def outputs(i, st):
        es, m = st
        w = _window_start(i, t)
        r = _dot(_with_ones(vt[:, w:w + span]), es[0]) + _dot(cvt, es[1])
        o_ref[i * BLOCK:(i + 1) * BLOCK, :] = _group_output(r, m, sink_row).astype(BF16)

    bounded = _scores_bounded(bounds_ref[2], jnp.maximum(bounds_ref[3], _max_row_sq_norm(ck)), bounds_ref[4])

    @pl.when(bounded)
    def _():
        _software_pipeline(t // BLOCK, [scores, functools.partial(probs, False), outputs])

    @pl.when(jnp.logical_not(bounded))
    def _():
        _software_pipeline(t // BLOCK, [scores, functools.partial(probs, True), outputs])


def _win_lat(qt, k, vt, ckt, cvt, sink_rows, bounds, riders):
    b, _, t = qt.shape
    past = ckt.shape[2]
    bias = _band_bias(t)
    rider_specs = _rider_specs(riders, b * B_KV_HEADS, lambda i, g: i * B_KV_HEADS + g)
    return pl.pallas_call(
        functools.partial(_win_lat_kernel, n_riders=len(riders)),
        grid=(b, B_KV_HEADS),
        in_specs=[pl.BlockSpec((None, CHUNK, t), lambda i, g: (i, g, 0)),
                  pl.BlockSpec((None, t, B_KVW), lambda i, g: (i, 0, 0)),
                  pl.BlockSpec((None, HEAD_DIM, t), lambda i, g: (i, g, 0)),
                  pl.BlockSpec((None, B_KVW, past), lambda i, g: (i, 0, 0)),
                  pl.BlockSpec((None, HEAD_DIM, past), lambda i, g: (i, g, 0)),
                  _const_spec((B_HEADS, LANES)), _const_spec(bias.shape),
                  pl.BlockSpec(memory_space=pltpu.SMEM)] + [spec for spec, _ in rider_specs],
        out_specs=[pl.BlockSpec((None, t, CHUNK), lambda i, g: (i, 0, g))] + [spec for spec, _ in rider_specs],
        out_shape=[jax.ShapeDtypeStruct((b, t, B_Q), BF16)] + [shape for _, shape in rider_specs],
        compiler_params=_params(2),
        name="win_lat",
    )(qt, k, vt, ckt, cvt, sink_rows, bias, bounds, *riders)


def _merge_kernel(x1_ref, x2_ref, oa1_ref, oa2_ref, ob1_ref, ob2_ref, mod_ref, g1_ref, wga0_ref, wga1_ref,
                  wgb0_ref, wgb1_ref, woa_ref, wob_ref, wout_ref, o1_ref, o2_ref, *, n_first):
    def tile(x_ref, oa_ref, ob_ref, o_ref):
        mod = mod_ref[...]
        rows = lambda i: slice(i * ROW_SUBTILE, (i + 1) * ROW_SUBTILE)

        def norm(i, _):
            return _modulated_norm(x_ref[rows(i), :], g1_ref[...], mod[:, D_MODEL:2 * D_MODEL],
                                   mod[:, 0:D_MODEL]).astype(BF16)

        def branches(i, h):
            ga = jnp.concatenate([_dot(h, wga0_ref[...]), _dot(h, wga1_ref[...])], axis=1)
            gb = jnp.concatenate([_dot(h, wgb0_ref[...]), _dot(h, wgb1_ref[...])], axis=1)
            return (jax.nn.sigmoid(ga) * _dot(oa_ref[rows(i), :], woa_ref[...])
                    + jax.nn.sigmoid(gb) * _dot(ob_ref[rows(i), :], wob_ref[...])).astype(BF16)

        def project(i, merged):
            o_ref[rows(i), :] = x_ref[rows(i), :] + mod[:, 2 * D_MODEL:3 * D_MODEL] * _dot(merged, wout_ref[...])

        _software_pipeline(x_ref.shape[0] // ROW_SUBTILE, [norm, branches, project])

    _per_token_set(n_first, functools.partial(tile, x1_ref, oa1_ref, ob1_ref, o1_ref),
                   functools.partial(tile, x2_ref, oa2_ref, ob2_ref, o2_ref))


def _per_token_set(n_first, first_body, second_body):
    first = pl.program_id(0) < n_first
    pl.when(first)(first_body)
    pl.when(jnp.logical_not(first))(second_body)


def _split_tiles(n_first, tm, width):
    return (pl.BlockSpec((tm, width), lambda i: (jnp.minimum(i, n_first - 1), 0)),
            pl.BlockSpec((tm, width), lambda i: (jnp.maximum(i - n_first, 0), 0)))


def _merge(x1, x2, oa1, oa2, ob1, ob2, mod3, mod_row, g1, w_in, w_oa, w_ob, w_out, tm):
    n_first, n_second = x1.shape[0] // tm, x2.shape[0] // tm
    t1, t2 = _split_tiles(n_first, tm, D_MODEL)
    sq = _const_spec((D_MODEL, D_MODEL))
    return pl.pallas_call(
        functools.partial(_merge_kernel, n_first=n_first),
        grid=(n_first + n_second,),
        in_specs=[t1, t2, t1, t2, t1, t2,
                  pl.BlockSpec((None, 1, N_MOD * D_MODEL), lambda i: (mod_row(i), 0, 0)),
                  _const_spec((1, D_MODEL))]
        + [pl.BlockSpec((D_MODEL, GATE_BLOCK), lambda i, j=j: (0, GATE_COL // GATE_BLOCK + j),
                        pipeline_mode=pl.Buffered(1)) for j in range(2 * D_MODEL // GATE_BLOCK)]
        + [sq, sq, sq],
        out_specs=[t1, t2],
        out_shape=[jax.ShapeDtypeStruct(x1.shape, F32), jax.ShapeDtypeStruct(x2.shape, F32)],
        compiler_params=_params(1),
        name="merge_out",
    )(x1, x2, oa1, oa2, ob1, ob2, mod3, g1, w_in, w_in, w_in, w_in, w_oa, w_ob, w_out)


def _ffn_kernel(x1_ref, x2_ref, mod_ref, g2_ref, wgate_ref, wup_ref, wdown_ref, o1_ref, o2_ref, *, n_first):
    def tile(x_ref, o_ref):
        mod = mod_ref[...]
        rows = lambda i: slice(i * FFN_SUBTILE, (i + 1) * FFN_SUBTILE)

        def norm(i, _):
            return _modulated_norm(x_ref[rows(i), :], g2_ref[...], mod[:, 4 * D_MODEL:5 * D_MODEL],
                                   mod[:, 3 * D_MODEL:4 * D_MODEL]).astype(BF16)

        def hidden(i, h):
            gate = _dot(h, wgate_ref[...])
            return (gate * jax.nn.sigmoid(gate) * _dot(h, wup_ref[...])).astype(BF16)

        def project(i, act):
            o_ref[rows(i), :] = x_ref[rows(i), :] + mod[:, 5 * D_MODEL:6 * D_MODEL] * _dot(act, wdown_ref[...])

        _software_pipeline(x_ref.shape[0] // FFN_SUBTILE, [norm, hidden, project])

    _per_token_set(n_first, functools.partial(tile, x1_ref, o1_ref), functools.partial(tile, x2_ref, o2_ref))


def _ffn(x1, x2, mod3, mod_row, g2, w_gate, w_up, w_down, tm):
    n_first, n_second = x1.shape[0] // tm, x2.shape[0] // tm
    t1, t2 = _split_tiles(n_first, tm, D_MODEL)
    return pl.pallas_call(
        functools.partial(_ffn_kernel, n_first=n_first),
        grid=(n_first + n_second,),
        in_specs=[t1, t2,
                  pl.BlockSpec((None, 1, N_MOD * D_MODEL), lambda i: (mod_row(i), 0, 0)),
                  _const_spec((1, D_MODEL)), _const_spec((D_MODEL, D_FF)),
                  _const_spec((D_MODEL, D_FF)), _const_spec((D_FF, D_MODEL))],
        out_specs=[t1, t2],
        out_shape=[jax.ShapeDtypeStruct(x1.shape, F32), jax.ShapeDtypeStruct(x2.shape, F32)],
        compiler_params=_params(1),
        name="swiglu_ffn",
    )(x1, x2, mod3, g2, w_gate, w_up, w_down)


def _rope_tables(seq_len):
    quarter = HEAD_DIM // 4
    t = np.arange(seq_len)
    row = (t // GRID_W).astype(np.float32)
    col = (t % GRID_W).astype(np.float32)
    freqs = (np.float32(ROPE_BASE) ** (-np.arange(quarter, dtype=np.float32) / np.float32(quarter))
             ).astype(np.float32)
    ang_r = row[:, None] * freqs
    ang_c = col[:, None] * freqs
    zeros = np.zeros_like(ang_r)
    cos = np.concatenate([np.cos(ang_r)] * 2 + [np.cos(ang_c)] * 2, axis=-1)
    sin_next = np.concatenate([-np.sin(ang_r), zeros, -np.sin(ang_c), zeros], axis=-1)
    sin_prev = np.concatenate([zeros, np.sin(ang_r), zeros, np.sin(ang_c)], axis=-1)
    rep = LANES // HEAD_DIM
    nat = tuple(jnp.asarray(np.tile(a, (1, rep)), dtype=F32) for a in (cos, sin_next, sin_prev))
    featm = (jnp.asarray(cos.T, dtype=F32), jnp.asarray((sin_next + sin_prev).T, dtype=F32))
    return nat, featm


def _group_sum_matrix():
    idx = np.arange(CHUNK) // HEAD_DIM
    return jnp.asarray(idx[:, None] == idx[None, :], dtype=BF16)


def kernel(x_prompt, x_sample, cache_diff_k, cache_diff_v, cache_win_k, cache_win_v, c, c_ctx,
           w_ada, b_ada, norm1_g, w_in, qn_a, kn_a, lambda_q1, lambda_k1, lambda_q2, lambda_k2,
           subln_g, qn_b, kn_b, sink, w_oa, w_ob, w_out, norm2_g, w_gate, w_up, w_down):
    batch, seq, _ = x_prompt.shape
    dec_batch, dec_seq, _ = x_sample.shape
    past = cache_diff_k.shape[2]
    l = 0

    cc = jnp.concatenate([c, c_ctx[None, :], jnp.zeros((8 - dec_batch - 1, D_MODEL), F32)], axis=0)
    mod3 = _modulation(cc, w_ada[l], b_ada[l][None, :])

    w_in_b = w_in[l].astype(BF16)
    o_ka, o_va, o_qb, o_kb, o_vb = A_W, 2 * A_W, 3 * A_W, 3 * A_W + B_Q, 3 * A_W + B_Q + B_KVW
    w_k = jnp.concatenate([w_in_b[:, o_ka:o_va], w_in_b[:, o_kb:o_vb]], axis=1)
    w_t = jnp.concatenate([w_in_b[:, :o_ka], w_in_b[:, o_qb:o_kb], w_in_b[:, o_va:o_qb],
                           w_in_b[:, o_vb:GATE_COL]], axis=1).T
    g1 = norm1_g[l][None, :]
    g2 = norm2_g[l][None, :]
    rep = CHUNK // HEAD_DIM
    kgains = jnp.stack([jnp.tile(kn_a[l], rep), jnp.tile(kn_b[l], rep)])
    qgains = jnp.stack([qn_a[l], qn_b[l]])[:, :, None]
    lam_vecs = jnp.stack([lambda_q1[l], lambda_k1[l], lambda_q2[l], lambda_k2[l]])
    subg_col = subln_g[l][:, None]
    sink_rows = jnp.broadcast_to((sink[l] * LOG2E)[:, None], (B_HEADS, LANES))
    head_gains = jnp.stack([qn_a[l], kn_a[l], qn_b[l], kn_b[l]])
    head_scale = jnp.asarray([Q_SCALE, 1.0, Q_SCALE, 1.0], F32)
    bounds = jnp.concatenate([1.02 * HEAD_DIM * jnp.max(jnp.square(head_gains), axis=1) * jnp.square(head_scale),
                              jnp.max(jnp.abs(sink[l]), keepdims=True) * LOG2E]).astype(F32)
    gsum = _group_sum_matrix()
    tm = 512

    xp = x_prompt.reshape(batch * seq, D_MODEL)
    ctx_row = lambda i: dec_batch
    oa_p, ob_p, vbt, kat_f, va_f, kbt_f = _ctx_front(
        x_prompt, mod3, ctx_row, g1, w_k, w_t, gsum, kgains, qgains, lam_vecs, subg_col, sink_rows, bounds,
        tm // seq)
    new_diff_k = kat_f.reshape(batch, A_HEADS, 2, HEAD_DIM, seq).transpose(0, 4, 1, 2, 3)[:, None]
    new_diff_v = va_f.reshape(batch, 1, seq, A_HEADS, 2 * HEAD_DIM)
    new_win_k = kbt_f.reshape(batch, B_KV_HEADS, HEAD_DIM, seq).transpose(0, 3, 1, 2)[:, None]
    new_win_v = vbt.reshape(batch, B_KV_HEADS, HEAD_DIM, seq).transpose(0, 3, 1, 2)[:, None]

    xs = x_sample.reshape(dec_batch * dec_seq, D_MODEL)
    lat_row = lambda i: i // (dec_seq // tm)
    qat, qbt, ka, kb, vat, vbt = _qkv_rope(
        x_sample, mod3, lat_row, g1, w_k, w_t, gsum, kgains, qgains, _rope_tables(dec_seq), tm)
    ckt_a = cache_diff_k[:, l].transpose(0, 2, 3, 4, 1).reshape(dec_batch, A_W, past)
    cv_a = cache_diff_v[:, l].reshape(dec_batch, past * A_HEADS, 2 * HEAD_DIM)
    ckt_b = cache_win_k[:, l].transpose(0, 2, 3, 1).reshape(dec_batch, B_KVW, past)
    cvt_b = cache_win_v[:, l].transpose(0, 2, 3, 1).reshape(dec_batch, B_KVW, past)
    oa_s, w_gate_b, w_up_b, w_down_b = _diff_lat(
        qat, ka.reshape(dec_batch, dec_seq, A_W), vat, ckt_a, cv_a, lam_vecs, subg_col, bounds,
        [w_gate[l], w_up[l], w_down[l]])
    ob_s, w_oa_b, w_ob_b, w_out_b = _win_lat(
        qbt, kb.reshape(dec_batch, dec_seq, B_KVW), vbt, ckt_b, cvt_b, sink_rows, bounds,
        [w_oa[l], w_ob[l], w_out[l]])

    n_ctx_tiles = batch * seq // tm
    both_row = lambda i: jnp.where(i < n_ctx_tiles, dec_batch, (i - n_ctx_tiles) // (dec_seq // tm))
    xp1, xs1 = _merge(xp, xs, oa_p, oa_s.reshape(-1, A_W), ob_p, ob_s.reshape(-1, B_Q), mod3, both_row, g1,
                      w_in_b, w_oa_b, w_ob_b, w_out_b, tm)
    y_prompt, y_sample = _ffn(xp1, xs1, mod3, both_row, g2, w_gate_b, w_up_b, w_down_b, tm)

    return (y_prompt.reshape(batch, seq, D_MODEL),
            y_sample.reshape(dec_batch, dec_seq, D_MODEL),
            new_diff_k, new_diff_v, new_win_k, new_win_v)
```
